```python
import math
import jax, jax.numpy as jnp
from jax import lax
import numpy as np

D_MODEL = 1024
BATCH = 16
SEQ = 2048
DEPTH = 2

PLE_DIM = 256
Q_BLOCK = 128
RNN_WIDTH = 512
RNN_BLOCKS = 8
CONV_WIDTH = 4
LRU_C = 8.0
DSA_HEADS = 8
DSA_HEAD_DIM = 64
IDX_HEADS = 4
IDX_DIM = 64
DSA_TOPK_MAX = 256
MLA_HEADS = 16
MLA_Q_LORA = 512
MLA_KV_LORA = 256
MLA_NOPE = 64
MLA_ROPE = 32
MLA_V = 64
ROPE_BASE = 10000.0
D_FF = 4 * D_MODEL
DN_ALPHA = (2 * DEPTH) ** 0.25
DN_BETA = (8 * DEPTH) ** -0.25
LN_EPS = 1e-5
RMS_EPS = 1e-6
N_EVEN = (DEPTH + 1) // 2
N_ODD = DEPTH // 2
EVEN_SPLITS = (RNN_WIDTH, RNN_WIDTH, DSA_HEADS * DSA_HEAD_DIM, DSA_HEAD_DIM, DSA_HEAD_DIM,
               IDX_HEADS * IDX_DIM, IDX_DIM, IDX_HEADS)
EVEN_IN = sum(EVEN_SPLITS)
EVEN_MIX = RNN_WIDTH + DSA_HEADS * DSA_HEAD_DIM
MLA_DOWN = MLA_Q_LORA + MLA_KV_LORA + MLA_ROPE
MLA_MIX = MLA_HEADS * MLA_V

kernel_name = "hybrid_rglru_dsa_mla_deepnorm"


def layer_norm(x, g, b):
    xf = x.astype(jnp.float32)
    mu = jnp.mean(xf, axis=-1, keepdims=True)
    var = jnp.mean(jnp.square(xf - mu), axis=-1, keepdims=True)
    y = (xf - mu) * lax.rsqrt(var + LN_EPS) * g.astype(jnp.float32) + b.astype(jnp.float32)
    return y.astype(x.dtype)


def rms_norm(x, g):
    xf = x.astype(jnp.float32)
    y = xf * lax.rsqrt(jnp.mean(jnp.square(xf), axis=-1, keepdims=True) + RMS_EPS)
    return (y * g.astype(jnp.float32)).astype(x.dtype)


def causal_depthwise_conv(x, w, b):
    c = x.shape[-1]
    y = lax.conv_general_dilated(x, w[:, None, :], window_strides=(1,), padding=[(CONV_WIDTH - 1, 0)],
                                 dimension_numbers=("NWC", "WIO", "NWC"), feature_group_count=c)
    return y + b


def rg_lru(x, w_a, b_a, w_x, b_x, lam):
    bsz, s, c = x.shape
    xb = x.reshape(bsz, s, RNN_BLOCKS, c // RNN_BLOCKS)
    r = jax.nn.sigmoid(jnp.einsum("bsnc,ncd->bsnd", xb, w_a).reshape(bsz, s, c) + b_a)
    i = jax.nn.sigmoid(jnp.einsum("bsnc,ncd->bsnd", xb, w_x).reshape(bsz, s, c) + b_x)
    log_a = -LRU_C * r.astype(jnp.float32) * jax.nn.softplus(-lam.astype(jnp.float32))
    a = jnp.exp(log_a)
    u = jnp.sqrt(-jnp.expm1(2.0 * log_a)) * (i * x).astype(jnp.float32)

    def combine(left, right):
        a1, b1 = left
        a2, b2 = right
        return a1 * a2, a2 * b1 + b2

    _, h = lax.associative_scan(combine, (a, u), axis=1)
    return h.astype(x.dtype)


def alibi_slopes(n):
    return jnp.exp2(-8.0 * (jnp.arange(n, dtype=jnp.float32) + 1.0) / n)


def rope(x, cos, sin):
    half = x.shape[-1] // 2
    x1, x2 = x[..., :half], x[..., half:]
    return jnp.concatenate([x1 * cos - x2 * sin, x2 * cos + x1 * sin], axis=-1)


def to_blocks(a, nb):
    a = a.reshape((a.shape[0], nb, Q_BLOCK) + a.shape[2:])
    return jnp.moveaxis(a, 1, 0)


def from_blocks(o):
    o = jnp.moveaxis(o, 0, 1)
    return o.reshape((o.shape[0], o.shape[1] * o.shape[2]) + o.shape[3:])


def gather_rows(src, idx):
    return jax.vmap(lambda sb, ib: sb[ib])(src, idx)


def dsa_attention(q, k, v, iq, ik, iw):
    bsz, s = q.shape[:2]
    nb = s // Q_BLOCK
    topk = min(DSA_TOPK_MAX, s // 4)
    slopes = alibi_slopes(DSA_HEADS)
    key_pos = jnp.arange(s, dtype=jnp.int32)
    iw = iw.astype(jnp.float32) * (IDX_HEADS ** -0.5 * IDX_DIM ** -0.5)

    def block(args):
        qb, iqb, iwb, t0 = args
        qpos = t0 + jnp.arange(Q_BLOCK, dtype=jnp.int32)
        dots = jnp.einsum("bthd,bsd->bths", iqb, ik).astype(jnp.float32)
        isc = jnp.einsum("bths,bth->bts", jax.nn.relu(dots), iwb)
        causal = key_pos[None, :] <= qpos[:, None]
        isc = jnp.where(causal[None], isc, -jnp.inf)
        _, idx = lax.top_k(isc, topk)
        ksel = gather_rows(k, idx)
        vsel = gather_rows(v, idx)
        logits = jnp.einsum("bthd,btkd->bthk", qb, ksel).astype(jnp.float32) * (DSA_HEAD_DIM ** -0.5)
        dist = qpos[None, :, None] - idx
        logits = logits - slopes[None, None, :, None] * dist.astype(jnp.float32)[:, :, None, :]
        logits = jnp.where((dist >= 0)[:, :, None, :], logits, -jnp.inf)
        probs = jax.nn.softmax(logits, axis=-1).astype(v.dtype)
        return jnp.einsum("bthk,btkd->bthd", probs, vsel)

    starts = jnp.arange(nb, dtype=jnp.int32) * Q_BLOCK
    out = lax.map(block, (to_blocks(q, nb), to_blocks(iq, nb), to_blocks(iw, nb), starts))
    return from_blocks(out)


def mla_attention(q, k, v):
    bsz, s = q.shape[:2]
    nb = s // Q_BLOCK
    scale = q.shape[-1] ** -0.5
    key_pos = jnp.arange(s, dtype=jnp.int32)

    def block(args):
        qb, t0 = args
        qpos = t0 + jnp.arange(Q_BLOCK, dtype=jnp.int32)
        logits = jnp.einsum("bthd,bshd->bhts", qb, k).astype(jnp.float32) * scale
        causal = key_pos[None, :] <= qpos[:, None]
        logits = jnp.where(causal[None, None], logits, -jnp.inf)
        probs = jax.nn.softmax(logits, axis=-1).astype(v.dtype)
        return jnp.einsum("bhts,bshd->bthd", probs, v)

    starts = jnp.arange(nb, dtype=jnp.int32) * Q_BLOCK
    out = lax.map(block, (to_blocks(q, nb), starts))
    return from_blocks(out)


def hybrid_mixer(x, w_in, conv_w, conv_b, ga_w, ga_b, gx_w, gx_b, lam, w_out):
    bsz, s, _ = x.shape
    proj = x @ w_in
    split_points = np.cumsum(EVEN_SPLITS)[:-1].tolist()
    xr, yr, q, k, v, iq, ik, iw = jnp.split(proj, split_points, axis=-1)
    xr = causal_depthwise_conv(xr, conv_w, conv_b)
    rec = rg_lru(xr, ga_w, ga_b, gx_w, gx_b, lam) * jax.nn.gelu(yr)
    att = dsa_attention(q.reshape(bsz, s, DSA_HEADS, DSA_HEAD_DIM), k, v,
                        iq.reshape(bsz, s, IDX_HEADS, IDX_DIM), ik, iw)
    mixed = jnp.concatenate([rec, att.reshape(bsz, s, DSA_HEADS * DSA_HEAD_DIM)], axis=-1)
    return mixed @ w_out


def mla_mixer(x, w_down, q_norm, kv_norm, w_uq, w_ukv, w_out):
    bsz, s, _ = x.shape
    down = x @ w_down
    cq, ckv, kr = jnp.split(down, [MLA_Q_LORA, MLA_Q_LORA + MLA_KV_LORA], axis=-1)
    q = (rms_norm(cq, q_norm) @ w_uq).reshape(bsz, s, MLA_HEADS, MLA_NOPE + MLA_ROPE)
    kv = (rms_norm(ckv, kv_norm) @ w_ukv).reshape(bsz, s, MLA_HEADS, MLA_NOPE + MLA_V)
    q_nope, q_rope = q[..., :MLA_NOPE], q[..., MLA_NOPE:]
    k_nope, v = kv[..., :MLA_NOPE], kv[..., MLA_NOPE:]
    pos = jnp.arange(s, dtype=jnp.float32)
    freq = ROPE_BASE ** (-jnp.arange(0, MLA_ROPE, 2, dtype=jnp.float32) / MLA_ROPE)
    ang = pos[:, None] * freq[None, :]
    cos = jnp.cos(ang).astype(x.dtype)
    sin = jnp.sin(ang).astype(x.dtype)
    q_rope = rope(q_rope, cos[:, None, :], sin[:, None, :])
    k_rope = rope(kr, cos, sin)
    qf = jnp.concatenate([q_nope, q_rope], axis=-1)
    kf = jnp.concatenate([k_nope, jnp.broadcast_to(k_rope[:, :, None, :], (bsz, s, MLA_HEADS, MLA_ROPE))], axis=-1)
    o = mla_attention(qf, kf, v)
    return o.reshape(bsz, s, MLA_MIX) @ w_out


def setup_inputs(seed: int = 0) -> dict:
    key = jax.random.key(seed)
    ks = jax.random.split(key, 32)

    def nrm(k, shape, scale):
        return jax.random.normal(k, shape, jnp.float32) * scale

    u = jax.random.uniform(ks[12], (N_EVEN, RNN_WIDTH), jnp.float32, 0.9, 0.999)
    sig = u ** (1.0 / LRU_C)
    lam = jnp.log(sig) - jnp.log1p(-sig)
    bw = RNN_WIDTH // RNN_BLOCKS
    return {
        "x": nrm(ks[0], (BATCH, SEQ, D_MODEL), 1.0),
        "p": nrm(ks[1], (DEPTH, BATCH, SEQ, PLE_DIM), 1.0),
        "ln1_g": 1.0 + nrm(ks[2], (DEPTH, D_MODEL), 0.02),
        "ln1_b": nrm(ks[3], (DEPTH, D_MODEL), 0.02),
        "ln2_g": 1.0 + nrm(ks[4], (DEPTH, D_MODEL), 0.02),
        "ln2_b": nrm(ks[5], (DEPTH, D_MODEL), 0.02),
        "mlp_w1": nrm(ks[6], (DEPTH, D_MODEL, D_FF), D_MODEL ** -0.5),
        "mlp_w2": nrm(ks[7], (DEPTH, D_FF, D_MODEL), DN_BETA * D_FF ** -0.5),
        "ple_w_proj": nrm(ks[8], (DEPTH, PLE_DIM, D_MODEL), DN_BETA * PLE_DIM ** -0.5),
        "ple_w_gate": nrm(ks[9], (DEPTH, D_MODEL, D_MODEL), D_MODEL ** -0.5),
        "hy_w_in": nrm(ks[10], (N_EVEN, D_MODEL, EVEN_IN), D_MODEL ** -0.5),
        "hy_conv_w": nrm(ks[11], (N_EVEN, CONV_WIDTH, RNN_WIDTH), CONV_WIDTH ** -0.5),
        "hy_conv_b": nrm(ks[13], (N_EVEN, RNN_WIDTH), 0.01),
        "hy_ga_w": nrm(ks[14], (N_EVEN, RNN_BLOCKS, bw, bw), bw ** -0.5),
        "hy_ga_b": nrm(ks[15], (N_EVEN, RNN_WIDTH), 0.01),
        "hy_gx_w": nrm(ks[16], (N_EVEN, RNN_BLOCKS, bw, bw), bw ** -0.5),
        "hy_gx_b": nrm(ks[17], (N_EVEN, RNN_WIDTH), 0.01),
        "hy_lambda": lam,
        "hy_w_out": nrm(ks[18], (N_EVEN, EVEN_MIX, D_MODEL), DN_BETA * EVEN_MIX ** -0.5),
        "mla_w_down": nrm(ks[19], (N_ODD, D_MODEL, MLA_DOWN), D_MODEL ** -0.5),
        "mla_q_norm": 1.0 + nrm(ks[20], (N_ODD, MLA_Q_LORA), 0.02),
        "mla_kv_norm": 1.0 + nrm(ks[21], (N_ODD, MLA_KV_LORA), 0.02),
        "mla_w_uq": nrm(ks[22], (N_ODD, MLA_Q_LORA, MLA_HEADS * (MLA_NOPE + MLA_ROPE)), MLA_Q_LORA ** -0.5),
        "mla_w_ukv": nrm(ks[23], (N_ODD, MLA_KV_LORA, MLA_HEADS * (MLA_NOPE + MLA_V)), MLA_KV_LORA ** -0.5),
        "mla_w_out": nrm(ks[24], (N_ODD, MLA_MIX, D_MODEL), DN_BETA * MLA_MIX ** -0.5),
    }


def reference(x, p, ln1_g, ln1_b, ln2_g, ln2_b, mlp_w1, mlp_w2, ple_w_proj, ple_w_gate,
              hy_w_in, hy_conv_w, hy_conv_b, hy_ga_w, hy_ga_b, hy_gx_w, hy_gx_b, hy_lambda, hy_w_out,
              mla_w_down, mla_q_norm, mla_kv_norm, mla_w_uq, mla_w_ukv, mla_w_out):
    for i in range(DEPTH):
        j = i // 2
        if i % 2 == 0:
            m = hybrid_mixer(x, hy_w_in[j], hy_conv_w[j], hy_conv_b[j], hy_ga_w[j], hy_ga_b[j],
                             hy_gx_w[j], hy_gx_b[j], hy_lambda[j], hy_w_out[j])
        else:
            m = mla_mixer(x, mla_w_down[j], mla_q_norm[j], mla_kv_norm[j], mla_w_uq[j],
                          mla_w_ukv[j], mla_w_out[j])
        h = layer_norm(DN_ALPHA * x + m, ln1_g[i], ln1_b[i])
        f = jnp.square(jax.nn.relu(h @ mlp_w1[i])) @ mlp_w2[i]
        e = jax.nn.sigmoid(h @ ple_w_gate[i]) * (p[i] @ ple_w_proj[i])
        x = layer_norm(DN_ALPHA * h + f + e, ln2_g[i], ln2_b[i])
    return x
```

```python
import functools
import math

import numpy as np
import jax
import jax.numpy as jnp
from jax import lax
from jax.experimental import pallas as pl
from jax.experimental.pallas import tpu as pltpu

F32 = jnp.float32
BF16 = jnp.bfloat16

RNN_WIDTH = 512
RNN_BLOCKS = 8
CONV_WIDTH = 4
LRU_C = 8.0
DSA_HEADS = 8
DSA_HEAD_DIM = 64
IDX_HEADS = 4
IDX_DIM = 64
DSA_TOPK_MAX = 256
MLA_HEADS = 16
MLA_Q_LORA = 512
MLA_KV_LORA = 256
MLA_NOPE = 64
MLA_ROPE = 32
MLA_V = 64
ROPE_BASE = 10000.0
DEPTH = 2
DN_ALPHA = (2 * DEPTH) ** 0.25
LN_EPS = 1e-5
RMS_EPS = 1e-6

LANES = 128
SUBLANES = 8
VMEM_LIMIT_BYTES = 56 * 1024 * 1024

MASK_NEG = -3.0e38
INT32_MIN = -(2 ** 31)


def _cparams(sem):
    return pltpu.CompilerParams(dimension_semantics=sem, vmem_limit_bytes=VMEM_LIMIT_BYTES)


def _dot(a, b):
    return jnp.dot(a, b, preferred_element_type=F32)


def _dot_nt(a, b):
    return lax.dot_general(a, b, (((1,), (1,)), ((), ())), preferred_element_type=F32)


def _layer_norm(y, g, b):
    mu = jnp.mean(y, axis=-1, keepdims=True)
    yc = y - mu
    var = jnp.mean(yc * yc, axis=-1, keepdims=True)
    return yc * lax.rsqrt(var + LN_EPS) * g + b


def _inproj_kernel(x_ref, w_ref, xy_ref, q_ref, iq_ref, misc_ref):
    xb = x_ref[...].astype(BF16)
    xy_ref[...] = _dot(xb, w_ref[:, 0:1024])
    q_ref[...] = (_dot(xb, w_ref[:, 1024:1536]) * (DSA_HEAD_DIM ** -0.5)).astype(BF16)
    iq_ref[...] = _dot(xb, w_ref[:, 1536:1792])
    misc_ref[...] = _dot(xb, w_ref[:, 1792:2048])


def _inproj(x2, w, tm):
    m, d = x2.shape
    n = w.shape[1]
    return pl.pallas_call(
        _inproj_kernel,
        grid=(m // tm,),
        in_specs=[pl.BlockSpec((tm, d), lambda i: (i, 0)),
                  pl.BlockSpec((d, n), lambda i: (0, 0))],
        out_specs=[pl.BlockSpec((tm, 1024), lambda i: (i, 0)),
                   pl.BlockSpec((tm, 512), lambda i: (i, 0)),
                   pl.BlockSpec((tm, 256), lambda i: (i, 0)),
                   pl.BlockSpec((tm, 256), lambda i: (i, 0))],
        out_shape=[jax.ShapeDtypeStruct((m, 1024), F32),
                   jax.ShapeDtypeStruct((m, 512), BF16),
                   jax.ShapeDtypeStruct((m, 256), F32),
                   jax.ShapeDtypeStruct((m, 256), F32)],
        compiler_params=_cparams(("parallel",)),
        name="hy_inproj",
    )(x2, w)


def _rglru_kernel(xr_ref, yr_ref, cw_ref, cb_ref, wa_ref, ba_ref, wx_ref, bx_ref, lam_ref,
                  o_ref, xpad_ref):
    s = xr_ref.shape[1]
    c = xr_ref.shape[2]
    xpad_ref[0:SUBLANES, :] = jnp.zeros((SUBLANES, c), F32)
    xpad_ref[SUBLANES:SUBLANES + s, :] = xr_ref[0]
    xc = None
    for k in range(CONV_WIDTH):
        start = SUBLANES - (CONV_WIDTH - 1) + k
        term = xpad_ref[start:start + s, :] * cw_ref[k:k + 1, :]
        xc = term if xc is None else xc + term
    xc = xc + cb_ref[...]
    xb = xc.astype(BF16)
    r = jax.nn.sigmoid(_dot(xb, wa_ref[0]) + ba_ref[...])
    gi = jax.nn.sigmoid(_dot(xb, wx_ref[0]) + bx_ref[...])
    nl = -lam_ref[...]
    softplus = jnp.maximum(nl, 0.0) + jnp.log1p(jnp.exp(-jnp.abs(nl)))
    log_a = (-LRU_C) * r * softplus
    a = jnp.exp(log_a)
    h = jnp.sqrt(-jnp.tanh(log_a) * (a * a + 1.0)) * (gi * xc)
    row = lax.broadcasted_iota(jnp.int32, (s, c), 0)
    sh = 1
    while sh < s:
        valid = row >= sh
        h_prev = jnp.where(valid, pltpu.roll(h, sh, axis=0), 0.0)
        h = a * h_prev + h
        if sh * 2 < s:
            a = a * jnp.where(valid, pltpu.roll(a, sh, axis=0), 1.0)
        sh *= 2
    o_ref[0] = (h * jax.nn.gelu(yr_ref[0])).astype(o_ref.dtype)


def _rglru(xy, conv_w, conv_b, wa, ba, wx, bx, lam):
    bsz, s, _ = xy.shape
    c = LANES
    ng = RNN_WIDTH // c
    vec = lambda: pl.BlockSpec((1, c), lambda b, g: (0, g))
    return pl.pallas_call(
        _rglru_kernel,
        grid=(bsz, ng),
        in_specs=[pl.BlockSpec((1, s, c), lambda b, g: (b, 0, g)),
                  pl.BlockSpec((1, s, c), lambda b, g: (b, 0, ng + g)),
                  pl.BlockSpec((CONV_WIDTH, c), lambda b, g: (0, g)),
                  vec(),
                  pl.BlockSpec((1, c, c), lambda b, g: (g, 0, 0)),
                  vec(),
                  pl.BlockSpec((1, c, c), lambda b, g: (g, 0, 0)),
                  vec(),
                  vec()],
        out_specs=pl.BlockSpec((1, s, c), lambda b, g: (b, 0, g)),
        out_shape=jax.ShapeDtypeStruct((bsz, s, RNN_WIDTH), BF16),
        scratch_shapes=[pltpu.VMEM((s + SUBLANES, c), F32)],
        compiler_params=_cparams(("parallel", "parallel")),
        name="hy_rglru",
    )(xy, xy, conv_w, conv_b, wa, ba, wx, bx, lam)


def _ordered_int_to_float(c):
    return lax.bitcast_convert_type(c ^ (lax.shift_right_arithmetic(c, 31) & jnp.int32(0x7FFFFFFF)), F32)


def _dsa_kernel(q_ref, iq_ref, mq_ref, mk_ref, o_ref, sc_ref, bias_ref, *, topk, row0):
    tq = q_ref.shape[1]
    ext = mk_ref.shape[1]
    kf = float(topk)
    qpos = row0 + pl.program_id(1) * tq + lax.broadcasted_iota(jnp.int32, (tq, 1), 0)
    kpos = lax.broadcasted_iota(jnp.int32, (1, ext), 1)
    causal = kpos <= qpos

    mk = mk_ref[0]
    v_b = mk[:, DSA_HEAD_DIM:2 * DSA_HEAD_DIM].astype(BF16)
    ik_b = mk[:, 2 * DSA_HEAD_DIM:2 * DSA_HEAD_DIM + IDX_DIM].astype(BF16)
    kcol = lax.broadcasted_iota(jnp.int32, (ext, 1), 0)
    lane = lax.broadcasted_iota(jnp.int32, (1, LANES), 1)
    pos_hi = (kcol & jnp.int32(-LANES)).astype(F32)
    pos_lo = (kcol & jnp.int32(LANES - 1)).astype(F32)
    k_aug = jnp.where(lane < DSA_HEAD_DIM, mk[:, 0:LANES],
                      jnp.where(lane == DSA_HEAD_DIM, pos_hi,
                                jnp.where(lane == DSA_HEAD_DIM + 1, pos_lo, 0.0))).astype(BF16)

    iw0 = 2 * DSA_HEAD_DIM + IDX_DIM
    iw = mq_ref[0][:, iw0:iw0 + IDX_HEADS] * (IDX_HEADS ** -0.5 * IDX_DIM ** -0.5)
    iq = iq_ref[0].astype(BF16)
    sc = None
    for h in range(IDX_HEADS):
        d = _dot_nt(iq[:, h * IDX_DIM:(h + 1) * IDX_DIM], ik_b)
        term = jnp.maximum(d, 0.0) * iw[:, h:h + 1]
        sc = term if sc is None else sc + term
    sc_ref[...] = jnp.where(causal, sc, MASK_NEG)

    def count_ge(cand):
        return jnp.sum(jnp.where(sc_ref[...] >= cand, 1.0, 0.0), axis=-1, keepdims=True)

    tau = jnp.where(count_ge(jnp.zeros((tq, 1), F32)) >= kf, jnp.int32(0), jnp.int32(INT32_MIN))

    def search(i, tau):
        cand = tau | lax.shift_left(jnp.int32(1), 30 - i)
        return jnp.where(count_ge(_ordered_int_to_float(cand)) >= kf, cand, tau)

    tau = _ordered_int_to_float(lax.fori_loop(0, 31, search, tau))

    sc = sc_ref[...]
    gt = sc > tau
    eq = sc == tau
    need = kf - jnp.sum(jnp.where(gt, 1.0, 0.0), axis=-1, keepdims=True)
    eqf = jnp.where(eq, 1.0, 0.0).astype(BF16)
    tri = (lax.broadcasted_iota(jnp.int32, (LANES, LANES), 0)
           <= lax.broadcasted_iota(jnp.int32, (LANES, LANES), 1)).astype(BF16)
    off = jnp.zeros((tq, 1), F32)
    for c in range(ext // LANES):
        sl = slice(c * LANES, (c + 1) * LANES)
        pc = _dot(eqf[:, sl], tri)
        take = gt[:, sl] | (eq[:, sl] & ((pc + off) <= need))
        bias_ref[:, sl] = jnp.where(take & causal[:, sl], 0.0, MASK_NEG)
        off = off + pc[:, LANES - 1:LANES]

    q = q_ref[0]
    tail_lane = lax.broadcasted_iota(jnp.int32, (tq, LANES - DSA_HEAD_DIM), 1)
    for h in range(DSA_HEADS):
        slope = 2.0 ** (-8.0 * (h + 1) / DSA_HEADS)
        tail = jnp.where(tail_lane < 2, slope, 0.0).astype(BF16)
        q_aug = jnp.concatenate([q[:, h * DSA_HEAD_DIM:(h + 1) * DSA_HEAD_DIM], tail], axis=1)
        lg = _dot_nt(q_aug, k_aug) + bias_ref[...]
        m = jnp.max(lg, axis=-1, keepdims=True)
        p = jnp.exp(lg - m)
        l = jnp.sum(p, axis=-1, keepdims=True)
        o = _dot(p.astype(BF16), v_b) / l
        o_ref[0, :, h * DSA_HEAD_DIM:(h + 1) * DSA_HEAD_DIM] = o.astype(o_ref.dtype)


def _dsa(q, iq, misc):
    bsz, s, _ = q.shape
    topk = min(DSA_TOPK_MAX, s // 4)
    cls = 512 if s % 512 == 0 else s
    tq = min(256, cls)
    hd = DSA_HEADS * DSA_HEAD_DIM
    outs = []
    for c in range(s // cls):
        ext = (c + 1) * cls
        jb = c * cls // tq
        outs.append(pl.pallas_call(
            functools.partial(_dsa_kernel, topk=topk, row0=c * cls),
            grid=(bsz, cls // tq),
            in_specs=[pl.BlockSpec((1, tq, hd), lambda b, j, jb=jb: (b, jb + j, 0)),
                      pl.BlockSpec((1, tq, IDX_HEADS * IDX_DIM), lambda b, j, jb=jb: (b, jb + j, 0)),
                      pl.BlockSpec((1, tq, 256), lambda b, j, jb=jb: (b, jb + j, 0)),
                      pl.BlockSpec((1, ext, 256), lambda b, j: (b, 0, 0))],
            out_specs=pl.BlockSpec((1, tq, hd), lambda b, j: (b, j, 0)),
            out_shape=jax.ShapeDtypeStruct((bsz, cls, hd), BF16),
            scratch_shapes=[pltpu.VMEM((tq, ext), F32), pltpu.VMEM((tq, ext), F32)],
            compiler_params=_cparams(("parallel", "parallel")),
            name=f"hy_dsa_c{c}",
        )(q, iq, misc, misc))
    return outs[0] if len(outs) == 1 else jnp.concatenate(outs, axis=1)


def _proj_ln_kernel(*refs, n_in):
    a_refs = refs[:n_in]
    w_refs = refs[n_in:2 * n_in]
    x_ref, g_ref, b_ref, o_ref = refs[2 * n_in:]
    m = None
    for a_ref, w_ref in zip(a_refs, w_refs):
        t = _dot(a_ref[...].astype(BF16), w_ref[...])
        m = t if m is None else m + t
    o_ref[...] = _layer_norm(DN_ALPHA * x_ref[...] + m, g_ref[...], b_ref[...])


def _proj_ln(acts, ws, x2, g, b, tm):
    m, d = x2.shape
    n_in = len(acts)
    in_specs = ([pl.BlockSpec((tm, a.shape[1]), lambda i: (i, 0)) for a in acts]
                + [pl.BlockSpec(w.shape, lambda i: (0, 0)) for w in ws]
                + [pl.BlockSpec((tm, d), lambda i: (i, 0)),
                   pl.BlockSpec((1, d), lambda i: (0, 0)),
                   pl.BlockSpec((1, d), lambda i: (0, 0))])
    return pl.pallas_call(
        functools.partial(_proj_ln_kernel, n_in=n_in),
        grid=(m // tm,),
        in_specs=in_specs,
        out_specs=pl.BlockSpec((tm, d), lambda i: (i, 0)),
        out_shape=jax.ShapeDtypeStruct((m, d), F32),
        compiler_params=_cparams(("parallel",)),
        name="proj_ln1",
    )(*acts, *ws, x2, g, b)


def _mlp_kernel(h_ref, p_ref, w1_ref, w2_ref, wg_ref, wp_ref, g_ref, b_ref, o_ref, hb_ref, acc_ref):
    f = pl.program_id(1)

    @pl.when(f == 0)
    def _():
        h = h_ref[...]
        hb = h.astype(BF16)
        hb_ref[...] = hb
        gate = jax.nn.sigmoid(_dot(hb, wg_ref[...]))
        e = gate * _dot(p_ref[...].astype(BF16), wp_ref[...])
        acc_ref[...] = DN_ALPHA * h + e

    a = jnp.maximum(_dot(hb_ref[...], w1_ref[...]), 0.0)
    acc_ref[...] += _dot((a * a).astype(BF16), w2_ref[...])

    @pl.when(f == pl.num_programs(1) - 1)
    def _():
        o_ref[...] = _layer_norm(acc_ref[...], g_ref[...], b_ref[...])


def _mlp(h2, p2, w1, w2, wg, wp, g, b, tm, tf):
    m, d = h2.shape
    dff = w1.shape[1]
    pd = p2.shape[1]
    return pl.pallas_call(
        _mlp_kernel,
        grid=(m // tm, dff // tf),
        in_specs=[pl.BlockSpec((tm, d), lambda i, f: (i, 0)),
                  pl.BlockSpec((tm, pd), lambda i, f: (i, 0)),
                  pl.BlockSpec((d, tf), lambda i, f: (0, f)),
                  pl.BlockSpec((tf, d), lambda i, f: (f, 0)),
                  pl.BlockSpec((d, d), lambda i, f: (0, 0)),
                  pl.BlockSpec((pd, d), lambda i, f: (0, 0)),
                  pl.BlockSpec((1, d), lambda i, f: (0, 0)),
                  pl.BlockSpec((1, d), lambda i, f: (0, 0))],
        out_specs=pl.BlockSpec((tm, d), lambda i, f: (i, 0)),
        out_shape=jax.ShapeDtypeStruct((m, d), F32),
        scratch_shapes=[pltpu.VMEM((tm, d), BF16), pltpu.VMEM((tm, d), F32)],
        compiler_params=_cparams(("parallel", "arbitrary")),
        name="mlp_ple_ln2",
    )(h2, p2, w1, w2, wg, wp, g, b)


MLA_Q_SCALE = (MLA_NOPE + MLA_ROPE) ** -0.5 * math.log2(math.e)


def _rope_group(t, cos_t, sin_lo, sin_hi):
    half = MLA_ROPE // 2
    return (t * cos_t + pltpu.roll(t, LANES - half, axis=1) * sin_lo
            + pltpu.roll(t, half, axis=1) * sin_hi)


def _mla_proj_kernel(x_ref, wd_ref, qg_ref, kvg_ref, wq_ref, wk_ref, wv_ref,
                     cos_ref, slo_ref, shi_ref, q_ref, k_ref, v_ref):
    xb = x_ref[...].astype(BF16)
    down = _dot(xb, wd_ref[...])
    cq = down[:, 0:MLA_Q_LORA]
    ckv = down[:, MLA_Q_LORA:MLA_Q_LORA + MLA_KV_LORA]
    krg = down[:, MLA_Q_LORA + MLA_KV_LORA:]
    cqn = (cq * lax.rsqrt(jnp.mean(cq * cq, axis=-1, keepdims=True) + RMS_EPS) * qg_ref[...]).astype(BF16)
    ckvn = (ckv * lax.rsqrt(jnp.mean(ckv * ckv, axis=-1, keepdims=True) + RMS_EPS) * kvg_ref[...]).astype(BF16)
    cos_t, sin_lo, sin_hi = cos_ref[...], slo_ref[...], shi_ref[...]
    kr = _rope_group(krg, cos_t, sin_lo, sin_hi)
    q = _dot(cqn, wq_ref[...])
    k = _dot(ckvn, wk_ref[...])
    for h in range(MLA_HEADS):
        sl = slice(h * LANES, (h + 1) * LANES)
        q_ref[:, sl] = (_rope_group(q[:, sl], cos_t, sin_lo, sin_hi) * MLA_Q_SCALE).astype(BF16)
        k_ref[:, sl] = (k[:, sl] + kr).astype(BF16)
    v_ref[...] = _dot(ckvn, wv_ref[...]).astype(BF16)


def _mla_proj(x2, wd, qg, kvg, wq, wk, wv, cos_t, sin_lo, sin_hi, tm, s):
    m, d = x2.shape
    nt = s // tm
    full = lambda a: pl.BlockSpec(a.shape, lambda i: (0, 0))
    tab = lambda: pl.BlockSpec((tm, LANES), lambda i: (i % nt, 0))
    hq = MLA_HEADS * LANES
    hv = MLA_HEADS * MLA_V
    return pl.pallas_call(
        _mla_proj_kernel,
        grid=(m // tm,),
        in_specs=[pl.BlockSpec((tm, d), lambda i: (i, 0)), full(wd), full(qg), full(kvg),
                  full(wq), full(wk), full(wv), tab(), tab(), tab()],
        out_specs=[pl.BlockSpec((tm, hq), lambda i: (i, 0)),
                   pl.BlockSpec((tm, hq), lambda i: (i, 0)),
                   pl.BlockSpec((tm, hv), lambda i: (i, 0))],
        out_shape=[jax.ShapeDtypeStruct((m, hq), BF16),
                   jax.ShapeDtypeStruct((m, hq), BF16),
                   jax.ShapeDtypeStruct((m, hv), BF16)],
        compiler_params=_cparams(("parallel",)),
        name="mla_proj",
    )(x2, wd, qg, kvg, wq, wk, wv, cos_t, sin_lo, sin_hi)


def _mla_attn_kernel(q_ref, k_ref, v_ref, o_ref, *, tq, heads_per_step):
    s = q_ref.shape[1]
    keep = (lax.broadcasted_iota(jnp.int32, (tq, tq), 1)
            <= lax.broadcasted_iota(jnp.int32, (tq, tq), 0))
    for j in range(s // tq):
        r0 = j * tq
        for hh in range(heads_per_step):
            ql = slice(hh * LANES, (hh + 1) * LANES)
            vl = slice(hh * MLA_V, (hh + 1) * MLA_V)
            qh = q_ref[0, r0:r0 + tq, ql]
            sd = jnp.where(keep, _dot_nt(qh, k_ref[0, r0:r0 + tq, ql]), MASK_NEG)
            m = jnp.max(sd, axis=-1, keepdims=True)
            if j > 0:
                so = _dot_nt(qh, k_ref[0, 0:r0, ql])
                m = jnp.maximum(m, jnp.max(so, axis=-1, keepdims=True))
            pd = jnp.exp2(sd - m)
            l = jnp.sum(pd, axis=-1, keepdims=True)
            acc = _dot(pd.astype(BF16), v_ref[0, r0:r0 + tq, vl])
            if j > 0:
                po = jnp.exp2(so - m)
                l = l + jnp.sum(po, axis=-1, keepdims=True)
                acc = acc + _dot(po.astype(BF16), v_ref[0, 0:r0, vl])
            o_ref[0, r0:r0 + tq, vl] = (acc / l).astype(o_ref.dtype)


def _mla_attn(q, k, v, tq):
    bsz, s, _ = q.shape
    hps = 2
    return pl.pallas_call(
        functools.partial(_mla_attn_kernel, tq=tq, heads_per_step=hps),
        grid=(bsz, MLA_HEADS // hps),
        in_specs=[pl.BlockSpec((1, s, hps * LANES), lambda b, h: (b, 0, h)),
                  pl.BlockSpec((1, s, hps * LANES), lambda b, h: (b, 0, h)),
                  pl.BlockSpec((1, s, hps * MLA_V), lambda b, h: (b, 0, h))],
        out_specs=pl.BlockSpec((1, s, hps * MLA_V), lambda b, h: (b, 0, h)),
        out_shape=jax.ShapeDtypeStruct((bsz, s, MLA_HEADS * MLA_V), BF16),
        compiler_params=_cparams(("parallel", "parallel")),
        name="mla_attn",
    )(q, k, v)


def _hy_in_weight(w_in):
    xr, yr, q, k, v, iq, ik, iw = jnp.split(
        w_in, np.cumsum([512, 512, 512, 64, 64, 256, 64, 4])[:-1].tolist(), axis=1)
    pad = jnp.zeros((w_in.shape[0], 256 - 64 * 3 - 4), w_in.dtype)
    return jnp.concatenate([xr, yr, q, iq, k, v, ik, iw, pad], axis=1).astype(BF16)


def _block_diag_groups(w):
    bw = w.shape[-1]
    per = LANES // bw
    ng = w.shape[0] // per
    out = jnp.zeros((ng, LANES, LANES), w.dtype)
    for g in range(ng):
        for t in range(per):
            out = out.at[g, t * bw:(t + 1) * bw, t * bw:(t + 1) * bw].set(w[g * per + t])
    return out.astype(BF16)


def _mla_weights(w_down, w_uq, w_ukv):
    d = w_down.shape[0]
    z = lambda n: jnp.zeros((d, n), w_down.dtype)
    wd = jnp.concatenate([w_down[:, :MLA_Q_LORA + MLA_KV_LORA], z(MLA_NOPE),
                          w_down[:, MLA_Q_LORA + MLA_KV_LORA:], z(LANES - MLA_NOPE - MLA_ROPE)], axis=1)
    wq = w_uq.reshape(MLA_Q_LORA, MLA_HEADS, MLA_NOPE + MLA_ROPE)
    wq = jnp.pad(wq, ((0, 0), (0, 0), (0, LANES - MLA_NOPE - MLA_ROPE))).reshape(MLA_Q_LORA, MLA_HEADS * LANES)
    wkv = w_ukv.reshape(MLA_KV_LORA, MLA_HEADS, MLA_NOPE + MLA_V)
    wk = jnp.pad(wkv[:, :, :MLA_NOPE], ((0, 0), (0, 0), (0, LANES - MLA_NOPE))).reshape(MLA_KV_LORA, MLA_HEADS * LANES)
    wv = wkv[:, :, MLA_NOPE:].reshape(MLA_KV_LORA, MLA_HEADS * MLA_V)
    return wd.astype(BF16), wq.astype(BF16), wk.astype(BF16), wv.astype(BF16)


def _rope_tables(s):
    half = MLA_ROPE // 2
    pos = jnp.arange(s, dtype=F32)
    freq = ROPE_BASE ** (-jnp.arange(0, MLA_ROPE, 2, dtype=F32) / MLA_ROPE)
    ang = pos[:, None] * freq[None, :]
    cos, sin = jnp.cos(ang), jnp.sin(ang)
    ones = jnp.ones((s, MLA_NOPE), F32)
    z = lambda n: jnp.zeros((s, n), F32)
    tail = LANES - MLA_NOPE - MLA_ROPE
    cos_t = jnp.concatenate([ones, cos, cos, jnp.ones((s, tail), F32)], axis=1)
    sin_lo = jnp.concatenate([z(MLA_NOPE), -sin, z(half), z(tail)], axis=1)
    sin_hi = jnp.concatenate([z(MLA_NOPE), z(half), sin, z(tail)], axis=1)
    return cos_t, sin_lo, sin_hi


def _tile_m(m):
    return 512 if m % 512 == 0 else m


def kernel(x, p, ln1_g, ln1_b, ln2_g, ln2_b, mlp_w1, mlp_w2, ple_w_proj, ple_w_gate, hy_w_in, hy_conv_w, hy_conv_b, hy_ga_w, hy_ga_b, hy_gx_w, hy_gx_b, hy_lambda, hy_w_out, mla_w_down, mla_q_norm, mla_kv_norm, mla_w_uq, mla_w_ukv, mla_w_out):
    bsz, s, d = x.shape
    m = bsz * s
    tm = _tile_m(m)
    tf = 1024
    row = lambda a: a.reshape(1, -1)
    x2 = x.reshape(m, d)

    def channel_mixer(h2, i):
        return _mlp(h2, p[i].reshape(m, -1), mlp_w1[i].astype(BF16), mlp_w2[i].astype(BF16),
                    ple_w_gate[i].astype(BF16), ple_w_proj[i].astype(BF16),
                    row(ln2_g[i]), row(ln2_b[i]), tm, tf)

    xy, q, iq, misc = _inproj(x2, _hy_in_weight(hy_w_in[0]), tm)
    rec = _rglru(xy.reshape(bsz, s, -1), hy_conv_w[0], row(hy_conv_b[0]),
                 _block_diag_groups(hy_ga_w[0]), row(hy_ga_b[0]),
                 _block_diag_groups(hy_gx_w[0]), row(hy_gx_b[0]), row(hy_lambda[0]))
    att = _dsa(q.reshape(bsz, s, -1), iq.reshape(bsz, s, -1), misc.reshape(bsz, s, -1))
    w_out = hy_w_out[0].astype(BF16)
    h2 = _proj_ln([rec.reshape(m, -1), att.reshape(m, -1)], [w_out[:RNN_WIDTH], w_out[RNN_WIDTH:]],
                  x2, row(ln1_g[0]), row(ln1_b[0]), tm)
    x2 = channel_mixer(h2, 0)

    wd, wq, wk, wv = _mla_weights(mla_w_down[0], mla_w_uq[0], mla_w_ukv[0])
    cos_t, sin_lo, sin_hi = _rope_tables(s)
    tmp = min(tm, s)
    qp, kp, vp = _mla_proj(x2, wd, row(mla_q_norm[0]), row(mla_kv_norm[0]), wq, wk, wv,
                           cos_t, sin_lo, sin_hi, tmp, s)
    o = _mla_attn(qp.reshape(bsz, s, -1), kp.reshape(bsz, s, -1), vp.reshape(bsz, s, -1), min(256, s))
    h2 = _proj_ln([o.reshape(m, -1)], [mla_w_out[0].astype(BF16)], x2, row(ln1_g[1]), row(ln1_b[1]), tm)
    x2 = channel_mixer(h2, 1)
    return x2.reshape(bsz, s, d)
```

```python
import functools
import math

import numpy as np
import jax
import jax.numpy as jnp
from jax import lax
from jax.experimental import pallas as pl
from jax.experimental.pallas import tpu as pltpu

F32 = jnp.float32
BF16 = jnp.bfloat16

RNN_WIDTH = 512
RNN_BLOCKS = 8
CONV_WIDTH = 4
LRU_C = 8.0
DSA_HEADS = 8
DSA_HEAD_DIM = 64
IDX_HEADS = 4
IDX_DIM = 64
DSA_TOPK_MAX = 256
MLA_HEADS = 16
MLA_Q_LORA = 512
MLA_KV_LORA = 256
MLA_NOPE = 64
MLA_ROPE = 32
MLA_V = 64
ROPE_BASE = 10000.0
DEPTH = 2
DN_ALPHA = (2 * DEPTH) ** 0.25
LN_EPS = 1e-5
RMS_EPS = 1e-6

LANES = 128
SUBLANES = 8
VMEM_LIMIT_BYTES = 56 * 1024 * 1024

MASK_NEG = -3.0e38
INT32_MIN = -(2 ** 31)


def _cparams(sem):
    return pltpu.CompilerParams(dimension_semantics=sem, vmem_limit_bytes=VMEM_LIMIT_BYTES)


def _dot(a, b):
    return jnp.dot(a, b, preferred_element_type=F32)


def _dot_nt(a, b):
    return lax.dot_general(a, b, (((1,), (1,)), ((), ())), preferred_element_type=F32)


def _layer_norm(y, g, b):
    mu = jnp.mean(y, axis=-1, keepdims=True)
    yc = y - mu
    var = jnp.mean(yc * yc, axis=-1, keepdims=True)
    return yc * lax.rsqrt(var + LN_EPS) * g + b


def _inproj_kernel(x_ref, w_ref, xy_ref, q_ref, iq_ref, misc_ref):
    xb = x_ref[...].astype(BF16)
    xy_ref[...] = _dot(xb, w_ref[:, 0:1024])
    q_ref[...] = (_dot(xb, w_ref[:, 1024:1536]) * (DSA_HEAD_DIM ** -0.5)).astype(BF16)
    iq_ref[...] = _dot(xb, w_ref[:, 1536:1792])
    misc_ref[...] = _dot(xb, w_ref[:, 1792:2048])


def _inproj(x2, w, tm):
    m, d = x2.shape
    n = w.shape[1]
    return pl.pallas_call(
        _inproj_kernel,
        grid=(m // tm,),
        in_specs=[pl.BlockSpec((tm, d), lambda i: (i, 0)),
                  pl.BlockSpec((d, n), lambda i: (0, 0))],
        out_specs=[pl.BlockSpec((tm, 1024), lambda i: (i, 0)),
                   pl.BlockSpec((tm, 512), lambda i: (i, 0)),
                   pl.BlockSpec((tm, 256), lambda i: (i, 0)),
                   pl.BlockSpec((tm, 256), lambda i: (i, 0))],
        out_shape=[jax.ShapeDtypeStruct((m, 1024), F32),
                   jax.ShapeDtypeStruct((m, 512), BF16),
                   jax.ShapeDtypeStruct((m, 256), F32),
                   jax.ShapeDtypeStruct((m, 256), F32)],
        compiler_params=_cparams(("parallel",)),
        name="hy_inproj",
    )(x2, w)


def _rglru_kernel(xr_ref, yr_ref, cw_ref, cb_ref, wa_ref, ba_ref, wx_ref, bx_ref, lam_ref,
                  o_ref, xpad_ref):
    s = xr_ref.shape[1]
    c = xr_ref.shape[2]
    xpad_ref[0:SUBLANES, :] = jnp.zeros((SUBLANES, c), F32)
    xpad_ref[SUBLANES:SUBLANES + s, :] = xr_ref[0]
    xc = None
    for k in range(CONV_WIDTH):
        start = SUBLANES - (CONV_WIDTH - 1) + k
        term = xpad_ref[start:start + s, :] * cw_ref[k:k + 1, :]
        xc = term if xc is None else xc + term
    xc = xc + cb_ref[...]
    xb = xc.astype(BF16)
    r = jax.nn.sigmoid(_dot(xb, wa_ref[0]) + ba_ref[...])
    gi = jax.nn.sigmoid(_dot(xb, wx_ref[0]) + bx_ref[...])
    nl = -lam_ref[...]
    softplus = jnp.maximum(nl, 0.0) + jnp.log1p(jnp.exp(-jnp.abs(nl)))
    log_a = (-LRU_C) * r * softplus
    a = jnp.exp(log_a)
    y = -jnp.tanh(log_a) * (a * a + 1.0)
    h = jnp.where(y > 0.0, y * lax.rsqrt(y), 0.0) * (gi * xc)
    row = lax.broadcasted_iota(jnp.int32, (s, c), 0)
    sh = 1
    while sh < s:
        valid = row >= sh
        h_prev = jnp.where(valid, pltpu.roll(h, sh, axis=0), 0.0)
        h = a * h_prev + h
        if sh * 2 < s:
            a = a * jnp.where(valid, pltpu.roll(a, sh, axis=0), 1.0)
        sh *= 2
    o_ref[0] = (h * jax.nn.gelu(yr_ref[0])).astype(o_ref.dtype)


def _rglru(xy, conv_w, conv_b, wa, ba, wx, bx, lam):
    bsz, s, _ = xy.shape
    c = LANES
    ng = RNN_WIDTH // c
    vec = lambda: pl.BlockSpec((1, c), lambda b, g: (0, g))
    return pl.pallas_call(
        _rglru_kernel,
        grid=(bsz, ng),
        in_specs=[pl.BlockSpec((1, s, c), lambda b, g: (b, 0, g)),
                  pl.BlockSpec((1, s, c), lambda b, g: (b, 0, ng + g)),
                  pl.BlockSpec((CONV_WIDTH, c), lambda b, g: (0, g)),
                  vec(),
                  pl.BlockSpec((1, c, c), lambda b, g: (g, 0, 0)),
                  vec(),
                  pl.BlockSpec((1, c, c), lambda b, g: (g, 0, 0)),
                  vec(),
                  vec()],
        out_specs=pl.BlockSpec((1, s, c), lambda b, g: (b, 0, g)),
        out_shape=jax.ShapeDtypeStruct((bsz, s, RNN_WIDTH), BF16),
        scratch_shapes=[pltpu.VMEM((s + SUBLANES, c), F32)],
        compiler_params=_cparams(("parallel", "parallel")),
        name="hy_rglru",
    )(xy, xy, conv_w, conv_b, wa, ba, wx, bx, lam)


def _ordered_int_to_float(c):
    return lax.bitcast_convert_type(c ^ (lax.shift_right_arithmetic(c, 31) & jnp.int32(0x7FFFFFFF)), F32)


def _dsa_kernel(q_ref, iq_ref, mq_ref, mk_ref, o_ref, sc_ref, bias_ref, *, topk, n_sub):
    tq = q_ref.shape[1]
    ext = mk_ref.shape[1]
    kf = float(topk)
    d0 = ext - tq
    keep = (lax.broadcasted_iota(jnp.int32, (tq, tq), 1)
            <= lax.broadcasted_iota(jnp.int32, (tq, tq), 0))

    mk = mk_ref[0]
    vlane = lax.broadcasted_iota(jnp.int32, (ext, DSA_HEAD_DIM), 1)
    v_aug = jnp.concatenate([mk[:, DSA_HEAD_DIM:2 * DSA_HEAD_DIM],
                             jnp.where(vlane == 0, 1.0, 0.0)], axis=1).astype(BF16)
    ik_b = mk[:, 2 * DSA_HEAD_DIM:2 * DSA_HEAD_DIM + IDX_DIM].astype(BF16)
    kcol = lax.broadcasted_iota(jnp.int32, (ext, 1), 0)
    lane = lax.broadcasted_iota(jnp.int32, (1, LANES), 1)
    pos_hi = (kcol & jnp.int32(-LANES)).astype(F32)
    pos_lo = (kcol & jnp.int32(LANES - 1)).astype(F32)
    k_aug = jnp.where(lane < DSA_HEAD_DIM, mk[:, 0:LANES],
                      jnp.where(lane == DSA_HEAD_DIM, pos_hi,
                                jnp.where(lane == DSA_HEAD_DIM + 1, pos_lo, 0.0))).astype(BF16)

    iw0 = 2 * DSA_HEAD_DIM + IDX_DIM
    iw = mq_ref[0][:, iw0:iw0 + IDX_HEADS] * (IDX_HEADS ** -0.5 * IDX_DIM ** -0.5)
    iq = iq_ref[0].astype(BF16)
    sc = None
    for h in range(IDX_HEADS):
        d = _dot_nt(iq[:, h * IDX_DIM:(h + 1) * IDX_DIM], ik_b)
        term = jnp.maximum(d, 0.0) * iw[:, h:h + 1]
        sc = term if sc is None else sc + term
    if d0 > 0:
        sc_ref[:, 0:d0] = sc[:, 0:d0]
    sc_ref[:, d0:ext] = jnp.where(keep, sc[:, d0:ext], MASK_NEG)

    rs = tq // n_sub

    def search(i, taus):
        bit = lax.shift_left(jnp.int32(1), 31 - i)
        out = []
        for g, tau_g in enumerate(taus):
            cand = tau_g + bit
            ge = sc_ref[g * rs:(g + 1) * rs, :] >= _ordered_int_to_float(cand)
            cnt = jnp.sum(jnp.where(ge, 1.0, 0.0), axis=-1, keepdims=True)
            out.append(jnp.where(cnt >= kf, cand, tau_g))
        return tuple(out)

    taus = lax.fori_loop(0, 32, search,
                         tuple(jnp.full((rs, 1), INT32_MIN, jnp.int32) for _ in range(n_sub)),
                         unroll=True)
    tau = _ordered_int_to_float(jnp.concatenate(taus, axis=0))

    sc = sc_ref[...]
    gt = sc > tau
    eq = sc == tau
    need = kf - jnp.sum(jnp.where(gt, 1.0, 0.0), axis=-1, keepdims=True)
    eqf = jnp.where(eq, 1.0, 0.0).astype(BF16)
    tri = (lax.broadcasted_iota(jnp.int32, (LANES, LANES), 0)
           <= lax.broadcasted_iota(jnp.int32, (LANES, LANES), 1)).astype(BF16)
    off = jnp.zeros((tq, 1), F32)
    for c in range(ext // LANES):
        sl = slice(c * LANES, (c + 1) * LANES)
        pc = _dot(eqf[:, sl], tri)
        take = gt[:, sl] | (eq[:, sl] & ((pc + off) <= need))
        if c * LANES >= d0:
            take = take & keep[:, c * LANES - d0:(c + 1) * LANES - d0]
        bias_ref[:, sl] = jnp.where(take, 0.0, MASK_NEG)
        off = off + pc[:, LANES - 1:LANES]

    q = q_ref[0]
    tail_lane = lax.broadcasted_iota(jnp.int32, (tq, LANES - DSA_HEAD_DIM), 1)
    for h in range(DSA_HEADS):
        slope = 2.0 ** (-8.0 * (h + 1) / DSA_HEADS)
        tail = jnp.where(tail_lane < 2, slope, 0.0).astype(BF16)
        q_aug = jnp.concatenate([q[:, h * DSA_HEAD_DIM:(h + 1) * DSA_HEAD_DIM], tail], axis=1)
        lg = _dot_nt(q_aug, k_aug) + bias_ref[...]
        p = jnp.exp(lg - jnp.max(lg, axis=-1, keepdims=True))
        o = _dot(p.astype(BF16), v_aug)
        o = o[:, 0:DSA_HEAD_DIM] / o[:, DSA_HEAD_DIM:DSA_HEAD_DIM + 1]
        o_ref[0, :, h * DSA_HEAD_DIM:(h + 1) * DSA_HEAD_DIM] = o.astype(o_ref.dtype)


def _dsa(q, iq, misc):
    bsz, s, _ = q.shape
    topk = min(DSA_TOPK_MAX, s // 4)
    tq = min(256, s)
    hd = DSA_HEADS * DSA_HEAD_DIM
    outs = []
    for c in range(s // tq):
        ext = (c + 1) * tq
        outs.append(pl.pallas_call(
            functools.partial(_dsa_kernel, topk=topk, n_sub=2),
            grid=(bsz,),
            in_specs=[pl.BlockSpec((1, tq, hd), lambda b, c=c: (b, c, 0)),
                      pl.BlockSpec((1, tq, IDX_HEADS * IDX_DIM), lambda b, c=c: (b, c, 0)),
                      pl.BlockSpec((1, tq, 256), lambda b, c=c: (b, c, 0)),
                      pl.BlockSpec((1, ext, 256), lambda b: (b, 0, 0))],
            out_specs=pl.BlockSpec((1, tq, hd), lambda b: (b, 0, 0)),
            out_shape=jax.ShapeDtypeStruct((bsz, tq, hd), BF16),
            scratch_shapes=[pltpu.VMEM((tq, ext), F32), pltpu.VMEM((tq, ext), F32)],
            compiler_params=_cparams(("parallel",)),
            name=f"hy_dsa_c{c}",
        )(q, iq, misc, misc))
    return outs[0] if len(outs) == 1 else jnp.concatenate(outs, axis=1)


def _proj_ln_kernel(*refs, n_in):
    a_refs = refs[:n_in]
    w_refs = refs[n_in:2 * n_in]
    x_ref, g_ref, b_ref, o_ref = refs[2 * n_in:]
    m = None
    for a_ref, w_ref in zip(a_refs, w_refs):
        t = _dot(a_ref[...].astype(BF16), w_ref[...])
        m = t if m is None else m + t
    o_ref[...] = _layer_norm(DN_ALPHA * x_ref[...] + m, g_ref[...], b_ref[...])


def _proj_ln(acts, ws, x2, g, b, tm):
    m, d = x2.shape
    n_in = len(acts)
    in_specs = ([pl.BlockSpec((tm, a.shape[1]), lambda i: (i, 0)) for a in acts]
                + [pl.BlockSpec(w.shape, lambda i: (0, 0)) for w in ws]
                + [pl.BlockSpec((tm, d), lambda i: (i, 0)),
                   pl.BlockSpec((1, d), lambda i: (0, 0)),
                   pl.BlockSpec((1, d), lambda i: (0, 0))])
    return pl.pallas_call(
        functools.partial(_proj_ln_kernel, n_in=n_in),
        grid=(m // tm,),
        in_specs=in_specs,
        out_specs=pl.BlockSpec((tm, d), lambda i: (i, 0)),
        out_shape=jax.ShapeDtypeStruct((m, d), F32),
        compiler_params=_cparams(("parallel",)),
        name="proj_ln1",
    )(*acts, *ws, x2, g, b)


def _mlp_kernel(h_ref, p_ref, w1_ref, w2_ref, wg_ref, wp_ref, g_ref, b_ref, o_ref, hb_ref, acc_ref):
    f = pl.program_id(1)

    @pl.when(f == 0)
    def _():
        h = h_ref[...]
        hb = h.astype(BF16)
        hb_ref[...] = hb
        gate = jax.nn.sigmoid(_dot(hb, wg_ref[...]))
        e = gate * _dot(p_ref[...].astype(BF16), wp_ref[...])
        acc_ref[...] = DN_ALPHA * h + e

    a = jnp.maximum(_dot(hb_ref[...], w1_ref[...]), 0.0)
    acc_ref[...] += _dot((a * a).astype(BF16), w2_ref[...])

    @pl.when(f == pl.num_programs(1) - 1)
    def _():
        o_ref[...] = _layer_norm(acc_ref[...], g_ref[...], b_ref[...])


def _mlp(h2, p2, w1, w2, wg, wp, g, b, tm, tf):
    m, d = h2.shape
    dff = w1.shape[1]
    pd = p2.shape[1]
    return pl.pallas_call(
        _mlp_kernel,
        grid=(m // tm, dff // tf),
        in_specs=[pl.BlockSpec((tm, d), lambda i, f: (i, 0)),
                  pl.BlockSpec((tm, pd), lambda i, f: (i, 0)),
                  pl.BlockSpec((d, tf), lambda i, f: (0, f)),
                  pl.BlockSpec((tf, d), lambda i, f: (f, 0)),
                  pl.BlockSpec((d, d), lambda i, f: (0, 0)),
                  pl.BlockSpec((pd, d), lambda i, f: (0, 0)),
                  pl.BlockSpec((1, d), lambda i, f: (0, 0)),
                  pl.BlockSpec((1, d), lambda i, f: (0, 0))],
        out_specs=pl.BlockSpec((tm, d), lambda i, f: (i, 0)),
        out_shape=jax.ShapeDtypeStruct((m, d), F32),
        scratch_shapes=[pltpu.VMEM((tm, d), BF16), pltpu.VMEM((tm, d), F32)],
        compiler_params=_cparams(("parallel", "arbitrary")),
        name="mlp_ple_ln2",
    )(h2, p2, w1, w2, wg, wp, g, b)


MLA_Q_SCALE = (MLA_NOPE + MLA_ROPE) ** -0.5 * math.log2(math.e)


def _rope_group(t, cos_t, sin_lo, sin_hi):
    half = MLA_ROPE // 2
    return (t * cos_t + pltpu.roll(t, LANES - half, axis=1) * sin_lo
            + pltpu.roll(t, half, axis=1) * sin_hi)


def _mla_proj_kernel(x_ref, wd_ref, qg_ref, kvg_ref, wq_ref, wk_ref, wv_ref,
                     cos_ref, slo_ref, shi_ref, q_ref, k_ref, v_ref):
    xb = x_ref[...].astype(BF16)
    down = _dot(xb, wd_ref[...])
    cq = down[:, 0:MLA_Q_LORA]
    ckv = down[:, MLA_Q_LORA:MLA_Q_LORA + MLA_KV_LORA]
    krg = down[:, MLA_Q_LORA + MLA_KV_LORA:]
    cqn = (cq * lax.rsqrt(jnp.mean(cq * cq, axis=-1, keepdims=True) + RMS_EPS) * qg_ref[...]).astype(BF16)
    ckvn = (ckv * lax.rsqrt(jnp.mean(ckv * ckv, axis=-1, keepdims=True) + RMS_EPS) * kvg_ref[...]).astype(BF16)
    cos_t, sin_lo, sin_hi = cos_ref[...], slo_ref[...], shi_ref[...]
    kr = _rope_group(krg, cos_t, sin_lo, sin_hi)
    q = _dot(cqn, wq_ref[...])
    k = _dot(ckvn, wk_ref[...])
    for h in range(MLA_HEADS):
        sl = slice(h * LANES, (h + 1) * LANES)
        q_ref[:, sl] = (_rope_group(q[:, sl], cos_t, sin_lo, sin_hi) * MLA_Q_SCALE).astype(BF16)
        k_ref[:, sl] = (k[:, sl] + kr).astype(BF16)
    v_ref[...] = _dot(ckvn, wv_ref[...]).astype(BF16)


def _mla_proj(x2, wd, qg, kvg, wq, wk, wv, cos_t, sin_lo, sin_hi, tm, s):
    m, d = x2.shape
    nt = s // tm
    full = lambda a: pl.BlockSpec(a.shape, lambda i: (0, 0))
    tab = lambda: pl.BlockSpec((tm, LANES), lambda i: (i % nt, 0))
    hq = MLA_HEADS * LANES
    hv = MLA_HEADS * MLA_V
    return pl.pallas_call(
        _mla_proj_kernel,
        grid=(m // tm,),
        in_specs=[pl.BlockSpec((tm, d), lambda i: (i, 0)), full(wd), full(qg), full(kvg),
                  full(wq), full(wk), full(wv), tab(), tab(), tab()],
        out_specs=[pl.BlockSpec((tm, hq), lambda i: (i, 0)),
                   pl.BlockSpec((tm, hq), lambda i: (i, 0)),
                   pl.BlockSpec((tm, hv), lambda i: (i, 0))],
        out_shape=[jax.ShapeDtypeStruct((m, hq), BF16),
                   jax.ShapeDtypeStruct((m, hq), BF16),
                   jax.ShapeDtypeStruct((m, hv), BF16)],
        compiler_params=_cparams(("parallel",)),
        name="mla_proj",
    )(x2, wd, qg, kvg, wq, wk, wv, cos_t, sin_lo, sin_hi)


def _mla_attn_kernel(q_ref, k_ref, v_ref, o_ref, *, tq, heads_per_step):
    s = q_ref.shape[1]
    keep = (lax.broadcasted_iota(jnp.int32, (tq, tq), 1)
            <= lax.broadcasted_iota(jnp.int32, (tq, tq), 0))
    for j in range(s // tq):
        r0 = j * tq
        for hh in range(heads_per_step):
            ql = slice(hh * LANES, (hh + 1) * LANES)
            vl = slice(hh * MLA_V, (hh + 1) * MLA_V)
            qh = q_ref[0, r0:r0 + tq, ql]
            sd = jnp.where(keep, _dot_nt(qh, k_ref[0, r0:r0 + tq, ql]), MASK_NEG)
            m = jnp.max(sd, axis=-1, keepdims=True)
            if j > 0:
                so = _dot_nt(qh, k_ref[0, 0:r0, ql])
                m = jnp.maximum(m, jnp.max(so, axis=-1, keepdims=True))
            pd = jnp.exp2(sd - m)
            l = jnp.sum(pd, axis=-1, keepdims=True)
            acc = _dot(pd.astype(BF16), v_ref[0, r0:r0 + tq, vl])
            if j > 0:
                po = jnp.exp2(so - m)
                l = l + jnp.sum(po, axis=-1, keepdims=True)
                acc = acc + _dot(po.astype(BF16), v_ref[0, 0:r0, vl])
            o_ref[0, r0:r0 + tq, vl] = (acc / l).astype(o_ref.dtype)


def _mla_attn(q, k, v, tq):
    bsz, s, _ = q.shape
    hps = 2
    return pl.pallas_call(
        functools.partial(_mla_attn_kernel, tq=tq, heads_per_step=hps),
        grid=(bsz, MLA_HEADS // hps),
        in_specs=[pl.BlockSpec((1, s, hps * LANES), lambda b, h: (b, 0, h)),
                  pl.BlockSpec((1, s, hps * LANES), lambda b, h: (b, 0, h)),
                  pl.BlockSpec((1, s, hps * MLA_V), lambda b, h: (b, 0, h))],
        out_specs=pl.BlockSpec((1, s, hps * MLA_V), lambda b, h: (b, 0, h)),
        out_shape=jax.ShapeDtypeStruct((bsz, s, MLA_HEADS * MLA_V), BF16),
        compiler_params=_cparams(("parallel", "parallel")),
        name="mla_attn",
    )(q, k, v)


def _hy_in_weight(w_in):
    xr, yr, q, k, v, iq, ik, iw = jnp.split(
        w_in, np.cumsum([512, 512, 512, 64, 64, 256, 64, 4])[:-1].tolist(), axis=1)
    pad = jnp.zeros((w_in.shape[0], 256 - 64 * 3 - 4), w_in.dtype)
    return jnp.concatenate([xr, yr, q, iq, k, v, ik, iw, pad], axis=1).astype(BF16)


def _block_diag_groups(w):
    bw = w.shape[-1]
    per = LANES // bw
    ng = w.shape[0] // per
    out = jnp.zeros((ng, LANES, LANES), w.dtype)
    for g in range(ng):
        for t in range(per):
            out = out.at[g, t * bw:(t + 1) * bw, t * bw:(t + 1) * bw].set(w[g * per + t])
    return out.astype(BF16)


def _mla_weights(w_down, w_uq, w_ukv):
    d = w_down.shape[0]
    z = lambda n: jnp.zeros((d, n), w_down.dtype)
    wd = jnp.concatenate([w_down[:, :MLA_Q_LORA + MLA_KV_LORA], z(MLA_NOPE),
                          w_down[:, MLA_Q_LORA + MLA_KV_LORA:], z(LANES - MLA_NOPE - MLA_ROPE)], axis=1)
    wq = w_uq.reshape(MLA_Q_LORA, MLA_HEADS, MLA_NOPE + MLA_ROPE)
    wq = jnp.pad(wq, ((0, 0), (0, 0), (0, LANES - MLA_NOPE - MLA_ROPE))).reshape(MLA_Q_LORA, MLA_HEADS * LANES)
    wkv = w_ukv.reshape(MLA_KV_LORA, MLA_HEADS, MLA_NOPE + MLA_V)
    wk = jnp.pad(wkv[:, :, :MLA_NOPE], ((0, 0), (0, 0), (0, LANES - MLA_NOPE))).reshape(MLA_KV_LORA, MLA_HEADS * LANES)
    wv = wkv[:, :, MLA_NOPE:].reshape(MLA_KV_LORA, MLA_HEADS * MLA_V)
    return wd.astype(BF16), wq.astype(BF16), wk.astype(BF16), wv.astype(BF16)


def _rope_tables(s):
    half = MLA_ROPE // 2
    pos = jnp.arange(s, dtype=F32)
    freq = ROPE_BASE ** (-jnp.arange(0, MLA_ROPE, 2, dtype=F32) / MLA_ROPE)
    ang = pos[:, None] * freq[None, :]
    cos, sin = jnp.cos(ang), jnp.sin(ang)
    ones = jnp.ones((s, MLA_NOPE), F32)
    z = lambda n: jnp.zeros((s, n), F32)
    tail = LANES - MLA_NOPE - MLA_ROPE
    cos_t = jnp.concatenate([ones, cos, cos, jnp.ones((s, tail), F32)], axis=1)
    sin_lo = jnp.concatenate([z(MLA_NOPE), -sin, z(half), z(tail)], axis=1)
    sin_hi = jnp.concatenate([z(MLA_NOPE), z(half), sin, z(tail)], axis=1)
    return cos_t, sin_lo, sin_hi


def _tile_m(m):
    return 512 if m % 512 == 0 else m


def kernel(x, p, ln1_g, ln1_b, ln2_g, ln2_b, mlp_w1, mlp_w2, ple_w_proj, ple_w_gate, hy_w_in, hy_conv_w, hy_conv_b, hy_ga_w, hy_ga_b, hy_gx_w, hy_gx_b, hy_lambda, hy_w_out, mla_w_down, mla_q_norm, mla_kv_norm, mla_w_uq, mla_w_ukv, mla_w_out):
    bsz, s, d = x.shape
    m = bsz * s
    tm = _tile_m(m)
    tf = 1024
    row = lambda a: a.reshape(1, -1)
    x2 = x.reshape(m, d)

    def channel_mixer(h2, i):
        return _mlp(h2, p[i].reshape(m, -1), mlp_w1[i].astype(BF16), mlp_w2[i].astype(BF16),
                    ple_w_gate[i].astype(BF16), ple_w_proj[i].astype(BF16),
                    row(ln2_g[i]), row(ln2_b[i]), tm, tf)

    xy, q, iq, misc = _inproj(x2, _hy_in_weight(hy_w_in[0]), tm)
    rec = _rglru(xy.reshape(bsz, s, -1), hy_conv_w[0], row(hy_conv_b[0]),
                 _block_diag_groups(hy_ga_w[0]), row(hy_ga_b[0]),
                 _block_diag_groups(hy_gx_w[0]), row(hy_gx_b[0]), row(hy_lambda[0]))
    att = _dsa(q.reshape(bsz, s, -1), iq.reshape(bsz, s, -1), misc.reshape(bsz, s, -1))
    w_out = hy_w_out[0].astype(BF16)
    h2 = _proj_ln([rec.reshape(m, -1), att.reshape(m, -1)], [w_out[:RNN_WIDTH], w_out[RNN_WIDTH:]],
                  x2, row(ln1_g[0]), row(ln1_b[0]), tm)
    x2 = channel_mixer(h2, 0)

    wd, wq, wk, wv = _mla_weights(mla_w_down[0], mla_w_uq[0], mla_w_ukv[0])
    cos_t, sin_lo, sin_hi = _rope_tables(s)
    tmp = min(tm, s)
    qp, kp, vp = _mla_proj(x2, wd, row(mla_q_norm[0]), row(mla_kv_norm[0]), wq, wk, wv,
                           cos_t, sin_lo, sin_hi, tmp, s)
    o = _mla_attn(qp.reshape(bsz, s, -1), kp.reshape(bsz, s, -1), vp.reshape(bsz, s, -1), min(256, s))
    h2 = _proj_ln([o.reshape(m, -1)], [mla_w_out[0].astype(BF16)], x2, row(ln1_g[1]), row(ln1_b[1]), tm)
    x2 = channel_mixer(h2, 1)
    return x2.reshape(bsz, s, d)
```

```python
import functools
import math

import numpy as np
import jax
import jax.numpy as jnp
from jax import lax
from jax.experimental import pallas as pl
from jax.experimental.pallas import tpu as pltpu

F32 = jnp.float32
BF16 = jnp.bfloat16

RNN_WIDTH = 512
RNN_BLOCKS = 8
CONV_WIDTH = 4
LRU_C = 8.0
DSA_HEADS = 8
DSA_HEAD_DIM = 64
IDX_HEADS = 4
IDX_DIM = 64
DSA_TOPK_MAX = 256
MLA_HEADS = 16
MLA_Q_LORA = 512
MLA_KV_LORA = 256
MLA_NOPE = 64
MLA_ROPE = 32
MLA_V = 64
ROPE_BASE = 10000.0
DEPTH = 2
DN_ALPHA = (2 * DEPTH) ** 0.25
LN_EPS = 1e-5
RMS_EPS = 1e-6

LANES = 128
SUBLANES = 8
VMEM_LIMIT_BYTES = 56 * 1024 * 1024

MASK_NEG = -3.0e38
INT32_MIN = -(2 ** 31)


def _cparams(sem):
    return pltpu.CompilerParams(dimension_semantics=sem, vmem_limit_bytes=VMEM_LIMIT_BYTES)


def _dot(a, b):
    return jnp.dot(a, b, preferred_element_type=F32)


def _dot_nt(a, b):
    return lax.dot_general(a, b, (((1,), (1,)), ((), ())), preferred_element_type=F32)


def _layer_norm(y, g, b):
    mu = jnp.mean(y, axis=-1, keepdims=True)
    yc = y - mu
    var = jnp.mean(yc * yc, axis=-1, keepdims=True)
    return yc * lax.rsqrt(var + LN_EPS) * g + b


def _inproj_kernel(x_ref, w_ref, xy_ref, q_ref, iq_ref, misc_ref):
    xb = x_ref[...].astype(BF16)
    xy_ref[...] = _dot(xb, w_ref[:, 0:1024])
    q_ref[...] = (_dot(xb, w_ref[:, 1024:1536]) * (DSA_HEAD_DIM ** -0.5)).astype(BF16)
    iq_ref[...] = _dot(xb, w_ref[:, 1536:1792])
    misc_ref[...] = _dot(xb, w_ref[:, 1792:2048])


def _inproj(x2, w, tm):
    m, d = x2.shape
    n = w.shape[1]
    return pl.pallas_call(
        _inproj_kernel,
        grid=(m // tm,),
        in_specs=[pl.BlockSpec((tm, d), lambda i: (i, 0)),
                  pl.BlockSpec((d, n), lambda i: (0, 0))],
        out_specs=[pl.BlockSpec((tm, 1024), lambda i: (i, 0)),
                   pl.BlockSpec((tm, 512), lambda i: (i, 0)),
                   pl.BlockSpec((tm, 256), lambda i: (i, 0)),
                   pl.BlockSpec((tm, 256), lambda i: (i, 0))],
        out_shape=[jax.ShapeDtypeStruct((m, 1024), F32),
                   jax.ShapeDtypeStruct((m, 512), BF16),
                   jax.ShapeDtypeStruct((m, 256), F32),
                   jax.ShapeDtypeStruct((m, 256), F32)],
        compiler_params=_cparams(("parallel",)),
        name="hy_inproj",
    )(x2, w)


def _rglru_kernel(xr_ref, yr_ref, cw_ref, cb_ref, wa_ref, ba_ref, wx_ref, bx_ref, lam_ref,
                  o_ref, xpad_ref):
    s = xr_ref.shape[1]
    c = xr_ref.shape[2]
    xpad_ref[0:SUBLANES, :] = jnp.zeros((SUBLANES, c), F32)
    xpad_ref[SUBLANES:SUBLANES + s, :] = xr_ref[0]
    xc = None
    for k in range(CONV_WIDTH):
        start = SUBLANES - (CONV_WIDTH - 1) + k
        term = xpad_ref[start:start + s, :] * cw_ref[k:k + 1, :]
        xc = term if xc is None else xc + term
    xc = xc + cb_ref[...]
    xb = xc.astype(BF16)
    r = jax.nn.sigmoid(_dot(xb, wa_ref[0]) + ba_ref[...])
    gi = jax.nn.sigmoid(_dot(xb, wx_ref[0]) + bx_ref[...])
    nl = -lam_ref[...]
    softplus = jnp.maximum(nl, 0.0) + jnp.log1p(jnp.exp(-jnp.abs(nl)))
    log_a = (-LRU_C) * r * softplus
    a = jnp.exp(log_a)
    y = -jnp.tanh(log_a) * (a * a + 1.0)
    h = jnp.where(y > 0.0, y * lax.rsqrt(y), 0.0) * (gi * xc)
    row = lax.broadcasted_iota(jnp.int32, (s, c), 0)
    sh = 1
    while sh < s:
        valid = row >= sh
        h_prev = jnp.where(valid, pltpu.roll(h, sh, axis=0), 0.0)
        h = a * h_prev + h
        if sh * 2 < s:
            a = a * jnp.where(valid, pltpu.roll(a, sh, axis=0), 1.0)
        sh *= 2
    o_ref[0] = (h * jax.nn.gelu(yr_ref[0])).astype(o_ref.dtype)


def _rglru(xy, conv_w, conv_b, wa, ba, wx, bx, lam):
    bsz, s, _ = xy.shape
    c = LANES
    ng = RNN_WIDTH // c
    vec = lambda: pl.BlockSpec((1, c), lambda b, g: (0, g))
    return pl.pallas_call(
        _rglru_kernel,
        grid=(bsz, ng),
        in_specs=[pl.BlockSpec((1, s, c), lambda b, g: (b, 0, g)),
                  pl.BlockSpec((1, s, c), lambda b, g: (b, 0, ng + g)),
                  pl.BlockSpec((CONV_WIDTH, c), lambda b, g: (0, g)),
                  vec(),
                  pl.BlockSpec((1, c, c), lambda b, g: (g, 0, 0)),
                  vec(),
                  pl.BlockSpec((1, c, c), lambda b, g: (g, 0, 0)),
                  vec(),
                  vec()],
        out_specs=pl.BlockSpec((1, s, c), lambda b, g: (b, 0, g)),
        out_shape=jax.ShapeDtypeStruct((bsz, s, RNN_WIDTH), BF16),
        scratch_shapes=[pltpu.VMEM((s + SUBLANES, c), F32)],
        compiler_params=_cparams(("parallel", "parallel")),
        name="hy_rglru",
    )(xy, xy, conv_w, conv_b, wa, ba, wx, bx, lam)


def _ordered_int_to_float(c):
    return lax.bitcast_convert_type(c ^ (lax.shift_right_arithmetic(c, 31) & jnp.int32(0x7FFFFFFF)), F32)


def _dsa_kernel(q_ref, iq_ref, mq_ref, mk_ref, o_ref, sc_ref, bias_ref, *, topk, n_sub):
    tq = q_ref.shape[1]
    ext = mk_ref.shape[1]
    kf = float(topk)
    d0 = ext - tq
    keep = (lax.broadcasted_iota(jnp.int32, (tq, tq), 1)
            <= lax.broadcasted_iota(jnp.int32, (tq, tq), 0))

    mk = mk_ref[0]
    vlane = lax.broadcasted_iota(jnp.int32, (ext, DSA_HEAD_DIM), 1)
    v_aug = jnp.concatenate([mk[:, DSA_HEAD_DIM:2 * DSA_HEAD_DIM],
                             jnp.where(vlane == 0, 1.0, 0.0)], axis=1).astype(BF16)
    ik_b = mk[:, 2 * DSA_HEAD_DIM:2 * DSA_HEAD_DIM + IDX_DIM].astype(BF16)
    kcol = lax.broadcasted_iota(jnp.int32, (ext, 1), 0)
    lane = lax.broadcasted_iota(jnp.int32, (1, LANES), 1)
    pos_hi = (kcol & jnp.int32(-LANES)).astype(F32)
    pos_lo = (kcol & jnp.int32(LANES - 1)).astype(F32)
    k_aug = jnp.where(lane < DSA_HEAD_DIM, mk[:, 0:LANES],
                      jnp.where(lane == DSA_HEAD_DIM, pos_hi,
                                jnp.where(lane == DSA_HEAD_DIM + 1, pos_lo, 0.0))).astype(BF16)

    iw0 = 2 * DSA_HEAD_DIM + IDX_DIM
    iw = mq_ref[0][:, iw0:iw0 + IDX_HEADS] * (IDX_HEADS ** -0.5 * IDX_DIM ** -0.5)
    iq = iq_ref[0].astype(BF16)
    sc = None
    for h in range(IDX_HEADS):
        d = _dot_nt(iq[:, h * IDX_DIM:(h + 1) * IDX_DIM], ik_b)
        term = jnp.maximum(d, 0.0) * iw[:, h:h + 1]
        sc = term if sc is None else sc + term
    if d0 > 0:
        sc_ref[:, 0:d0] = sc[:, 0:d0]
    sc_ref[:, d0:ext] = jnp.where(keep, sc[:, d0:ext], MASK_NEG)

    rs = tq // n_sub

    def search(i, taus):
        bit = lax.shift_left(jnp.int32(1), 31 - i)
        out = []
        for g, tau_g in enumerate(taus):
            cand = tau_g + bit
            ge = sc_ref[g * rs:(g + 1) * rs, :] >= _ordered_int_to_float(cand)
            cnt = jnp.sum(jnp.where(ge, 1.0, 0.0), axis=-1, keepdims=True)
            out.append(jnp.where(cnt >= kf, cand, tau_g))
        return tuple(out)

    taus = lax.fori_loop(0, 32, search,
                         tuple(jnp.full((rs, 1), INT32_MIN, jnp.int32) for _ in range(n_sub)),
                         unroll=True)
    tau = _ordered_int_to_float(jnp.concatenate(taus, axis=0))

    sc = sc_ref[...]
    gt = sc > tau
    eq = sc == tau
    need = kf - jnp.sum(jnp.where(gt, 1.0, 0.0), axis=-1, keepdims=True)
    eqf = jnp.where(eq, 1.0, 0.0).astype(BF16)
    tri = (lax.broadcasted_iota(jnp.int32, (LANES, LANES), 0)
           <= lax.broadcasted_iota(jnp.int32, (LANES, LANES), 1)).astype(BF16)
    off = jnp.zeros((tq, 1), F32)
    for c in range(ext // LANES):
        sl = slice(c * LANES, (c + 1) * LANES)
        pc = _dot(eqf[:, sl], tri)
        take = gt[:, sl] | (eq[:, sl] & ((pc + off) <= need))
        if c * LANES >= d0:
            take = take & keep[:, c * LANES - d0:(c + 1) * LANES - d0]
        bias_ref[:, sl] = jnp.where(take, 0.0, MASK_NEG)
        off = off + pc[:, LANES - 1:LANES]

    q = q_ref[0]
    tail_lane = lax.broadcasted_iota(jnp.int32, (tq, LANES - DSA_HEAD_DIM), 1)
    for h in range(DSA_HEADS):
        slope = 2.0 ** (-8.0 * (h + 1) / DSA_HEADS)
        tail = jnp.where(tail_lane < 2, slope, 0.0).astype(BF16)
        q_aug = jnp.concatenate([q[:, h * DSA_HEAD_DIM:(h + 1) * DSA_HEAD_DIM], tail], axis=1)
        lg = _dot_nt(q_aug, k_aug) + bias_ref[...]
        p = jnp.exp(lg - jnp.max(lg, axis=-1, keepdims=True))
        o = _dot(p.astype(BF16), v_aug)
        o = o[:, 0:DSA_HEAD_DIM] / o[:, DSA_HEAD_DIM:DSA_HEAD_DIM + 1]
        o_ref[0, :, h * DSA_HEAD_DIM:(h + 1) * DSA_HEAD_DIM] = o.astype(o_ref.dtype)


def _dsa(q, iq, misc):
    bsz, s, _ = q.shape
    topk = min(DSA_TOPK_MAX, s // 4)
    tq = min(256, s)
    hd = DSA_HEADS * DSA_HEAD_DIM
    outs = []
    for c in range(s // tq):
        ext = (c + 1) * tq
        outs.append(pl.pallas_call(
            functools.partial(_dsa_kernel, topk=topk, n_sub=2),
            grid=(bsz,),
            in_specs=[pl.BlockSpec((1, tq, hd), lambda b, c=c: (b, c, 0)),
                      pl.BlockSpec((1, tq, IDX_HEADS * IDX_DIM), lambda b, c=c: (b, c, 0)),
                      pl.BlockSpec((1, tq, 256), lambda b, c=c: (b, c, 0)),
                      pl.BlockSpec((1, ext, 256), lambda b: (b, 0, 0))],
            out_specs=pl.BlockSpec((1, tq, hd), lambda b: (b, 0, 0)),
            out_shape=jax.ShapeDtypeStruct((bsz, tq, hd), BF16),
            scratch_shapes=[pltpu.VMEM((tq, ext), F32), pltpu.VMEM((tq, ext), F32)],
            compiler_params=_cparams(("parallel",)),
            name=f"hy_dsa_c{c}",
        )(q, iq, misc, misc))
    return outs[0] if len(outs) == 1 else jnp.concatenate(outs, axis=1)


def _tail_kernel(*refs, n_in, tf):
    a_refs = refs[:n_in]
    wo_refs = refs[n_in:2 * n_in]
    (x_ref, p_ref, g1_ref, b1_ref, w1_ref, w2_ref, wg_ref, wp_ref, g2_ref, b2_ref,
     o_ref, acc_ref) = refs[2 * n_in:]
    m = None
    for a_ref, w_ref in zip(a_refs, wo_refs):
        t = _dot(a_ref[...].astype(BF16), w_ref[...])
        m = t if m is None else m + t
    h = _layer_norm(DN_ALPHA * x_ref[...] + m, g1_ref[...], b1_ref[...])
    hb = h.astype(BF16)
    gate = jax.nn.sigmoid(_dot(hb, wg_ref[...]))
    acc_ref[...] = DN_ALPHA * h + gate * _dot(p_ref[...].astype(BF16), wp_ref[...])
    for c in range(w1_ref.shape[1] // tf):
        a = jnp.maximum(_dot(hb, w1_ref[:, c * tf:(c + 1) * tf]), 0.0)
        acc_ref[...] += _dot((a * a).astype(BF16), w2_ref[c * tf:(c + 1) * tf, :])
    o_ref[...] = _layer_norm(acc_ref[...], g2_ref[...], b2_ref[...])


def _tail(acts, wos, x2, p2, g1, b1, w1, w2, wg, wp, g2, b2, tm, tf):
    m, d = x2.shape
    n_in = len(acts)
    rows = lambda a: pl.BlockSpec((tm, a.shape[1]), lambda i: (i, 0))
    resident = lambda a: pl.BlockSpec(a.shape, lambda i: (0, 0), pipeline_mode=pl.Buffered(1))
    params = [g1, b1, w1, w2, wg, wp, g2, b2]
    return pl.pallas_call(
        functools.partial(_tail_kernel, n_in=n_in, tf=tf),
        grid=(m // tm,),
        in_specs=([rows(a) for a in acts] + [resident(w) for w in wos] + [rows(x2), rows(p2)]
                  + [resident(a) for a in params]),
        out_specs=pl.BlockSpec((tm, d), lambda i: (i, 0)),
        out_shape=jax.ShapeDtypeStruct((m, d), F32),
        scratch_shapes=[pltpu.VMEM((tm, d), F32)],
        compiler_params=_cparams(("parallel",)),
        name="layer_tail",
    )(*acts, *wos, x2, p2, *params)


MLA_Q_SCALE = (MLA_NOPE + MLA_ROPE) ** -0.5 * math.log2(math.e)


def _rope_group(t, cos_t, sin_lo, sin_hi):
    half = MLA_ROPE // 2
    return (t * cos_t + pltpu.roll(t, LANES - half, axis=1) * sin_lo
            + pltpu.roll(t, half, axis=1) * sin_hi)


def _mla_proj_kernel(x_ref, wd_ref, qg_ref, kvg_ref, wq_ref, wk_ref, wv_ref,
                     cos_ref, slo_ref, shi_ref, q_ref, k_ref, v_ref):
    xb = x_ref[...].astype(BF16)
    down = _dot(xb, wd_ref[...])
    cq = down[:, 0:MLA_Q_LORA]
    ckv = down[:, MLA_Q_LORA:MLA_Q_LORA + MLA_KV_LORA]
    krg = down[:, MLA_Q_LORA + MLA_KV_LORA:]
    cqn = (cq * lax.rsqrt(jnp.mean(cq * cq, axis=-1, keepdims=True) + RMS_EPS) * qg_ref[...]).astype(BF16)
    ckvn = (ckv * lax.rsqrt(jnp.mean(ckv * ckv, axis=-1, keepdims=True) + RMS_EPS) * kvg_ref[...]).astype(BF16)
    cos_t, sin_lo, sin_hi = cos_ref[...], slo_ref[...], shi_ref[...]
    kr = _rope_group(krg, cos_t, sin_lo, sin_hi)
    q = _dot(cqn, wq_ref[...])
    k = _dot(ckvn, wk_ref[...])
    for h in range(MLA_HEADS):
        sl = slice(h * LANES, (h + 1) * LANES)
        q_ref[:, sl] = (_rope_group(q[:, sl], cos_t, sin_lo, sin_hi) * MLA_Q_SCALE).astype(BF16)
        k_ref[:, sl] = (k[:, sl] + kr).astype(BF16)
    v_ref[...] = _dot(ckvn, wv_ref[...]).astype(BF16)


def _mla_proj(x2, wd, qg, kvg, wq, wk, wv, cos_t, sin_lo, sin_hi, tm, s):
    m, d = x2.shape
    nt = s // tm
    full = lambda a: pl.BlockSpec(a.shape, lambda i: (0, 0))
    tab = lambda: pl.BlockSpec((tm, LANES), lambda i: (i % nt, 0))
    hq = MLA_HEADS * LANES
    hv = MLA_HEADS * MLA_V
    return pl.pallas_call(
        _mla_proj_kernel,
        grid=(m // tm,),
        in_specs=[pl.BlockSpec((tm, d), lambda i: (i, 0)), full(wd), full(qg), full(kvg),
                  full(wq), full(wk), full(wv), tab(), tab(), tab()],
        out_specs=[pl.BlockSpec((tm, hq), lambda i: (i, 0)),
                   pl.BlockSpec((tm, hq), lambda i: (i, 0)),
                   pl.BlockSpec((tm, hv), lambda i: (i, 0))],
        out_shape=[jax.ShapeDtypeStruct((m, hq), BF16),
                   jax.ShapeDtypeStruct((m, hq), BF16),
                   jax.ShapeDtypeStruct((m, hv), BF16)],
        compiler_params=_cparams(("parallel",)),
        name="mla_proj",
    )(x2, wd, qg, kvg, wq, wk, wv, cos_t, sin_lo, sin_hi)


def _mla_attn_kernel(q_ref, k_ref, v_ref, o_ref, *, tq, heads_per_step):
    s = q_ref.shape[1]
    keep = (lax.broadcasted_iota(jnp.int32, (tq, tq), 1)
            <= lax.broadcasted_iota(jnp.int32, (tq, tq), 0))
    units = [(j * tq, hh) for j in range(s // tq) for hh in range(heads_per_step)]

    def logits(r0, hh):
        ql = slice(hh * LANES, (hh + 1) * LANES)
        qh = q_ref[0, r0:r0 + tq, ql]
        sd = jnp.where(keep, _dot_nt(qh, k_ref[0, r0:r0 + tq, ql]), MASK_NEG)
        so = _dot_nt(qh, k_ref[0, 0:r0, ql]) if r0 > 0 else None
        return sd, so

    nxt = logits(*units[0])
    for u, (r0, hh) in enumerate(units):
        sd, so = nxt
        if u + 1 < len(units):
            nxt = logits(*units[u + 1])
        vl = slice(hh * MLA_V, (hh + 1) * MLA_V)
        m = jnp.max(sd, axis=-1, keepdims=True)
        if so is not None:
            m = jnp.maximum(m, jnp.max(so, axis=-1, keepdims=True))
        pd = jnp.exp2(sd - m)
        l = jnp.sum(pd, axis=-1, keepdims=True)
        acc = _dot(pd.astype(BF16), v_ref[0, r0:r0 + tq, vl])
        if so is not None:
            po = jnp.exp2(so - m)
            l = l + jnp.sum(po, axis=-1, keepdims=True)
            acc = acc + _dot(po.astype(BF16), v_ref[0, 0:r0, vl])
        o_ref[0, r0:r0 + tq, vl] = (acc / l).astype(o_ref.dtype)


def _mla_attn(q, k, v, tq):
    bsz, s, _ = q.shape
    hps = 2
    return pl.pallas_call(
        functools.partial(_mla_attn_kernel, tq=tq, heads_per_step=hps),
        grid=(bsz, MLA_HEADS // hps),
        in_specs=[pl.BlockSpec((1, s, hps * LANES), lambda b, h: (b, 0, h)),
                  pl.BlockSpec((1, s, hps * LANES), lambda b, h: (b, 0, h)),
                  pl.BlockSpec((1, s, hps * MLA_V), lambda b, h: (b, 0, h))],
        out_specs=pl.BlockSpec((1, s, hps * MLA_V), lambda b, h: (b, 0, h)),
        out_shape=jax.ShapeDtypeStruct((bsz, s, MLA_HEADS * MLA_V), BF16),
        compiler_params=_cparams(("parallel", "parallel")),
        name="mla_attn",
    )(q, k, v)


def _hy_in_weight(w_in):
    xr, yr, q, k, v, iq, ik, iw = jnp.split(
        w_in, np.cumsum([512, 512, 512, 64, 64, 256, 64, 4])[:-1].tolist(), axis=1)
    pad = jnp.zeros((w_in.shape[0], 256 - 64 * 3 - 4), w_in.dtype)
    return jnp.concatenate([xr, yr, q, iq, k, v, ik, iw, pad], axis=1).astype(BF16)


def _block_diag_groups(w):
    bw = w.shape[-1]
    per = LANES // bw
    ng = w.shape[0] // per
    out = jnp.zeros((ng, LANES, LANES), w.dtype)
    for g in range(ng):
        for t in range(per):
            out = out.at[g, t * bw:(t + 1) * bw, t * bw:(t + 1) * bw].set(w[g * per + t])
    return out.astype(BF16)


def _mla_weights(w_down, w_uq, w_ukv):
    d = w_down.shape[0]
    z = lambda n: jnp.zeros((d, n), w_down.dtype)
    wd = jnp.concatenate([w_down[:, :MLA_Q_LORA + MLA_KV_LORA], z(MLA_NOPE),
                          w_down[:, MLA_Q_LORA + MLA_KV_LORA:], z(LANES - MLA_NOPE - MLA_ROPE)], axis=1)
    wq = w_uq.reshape(MLA_Q_LORA, MLA_HEADS, MLA_NOPE + MLA_ROPE)
    wq = jnp.pad(wq, ((0, 0), (0, 0), (0, LANES - MLA_NOPE - MLA_ROPE))).reshape(MLA_Q_LORA, MLA_HEADS * LANES)
    wkv = w_ukv.reshape(MLA_KV_LORA, MLA_HEADS, MLA_NOPE + MLA_V)
    wk = jnp.pad(wkv[:, :, :MLA_NOPE], ((0, 0), (0, 0), (0, LANES - MLA_NOPE))).reshape(MLA_KV_LORA, MLA_HEADS * LANES)
    wv = wkv[:, :, MLA_NOPE:].reshape(MLA_KV_LORA, MLA_HEADS * MLA_V)
    return wd.astype(BF16), wq.astype(BF16), wk.astype(BF16), wv.astype(BF16)


def _rope_tables(s):
    half = MLA_ROPE // 2
    pos = jnp.arange(s, dtype=F32)
    freq = ROPE_BASE ** (-jnp.arange(0, MLA_ROPE, 2, dtype=F32) / MLA_ROPE)
    ang = pos[:, None] * freq[None, :]
    cos, sin = jnp.cos(ang), jnp.sin(ang)
    ones = jnp.ones((s, MLA_NOPE), F32)
    z = lambda n: jnp.zeros((s, n), F32)
    tail = LANES - MLA_NOPE - MLA_ROPE
    cos_t = jnp.concatenate([ones, cos, cos, jnp.ones((s, tail), F32)], axis=1)
    sin_lo = jnp.concatenate([z(MLA_NOPE), -sin, z(half), z(tail)], axis=1)
    sin_hi = jnp.concatenate([z(MLA_NOPE), z(half), sin, z(tail)], axis=1)
    return cos_t, sin_lo, sin_hi


def _tile_m(m):
    return 512 if m % 512 == 0 else m


def kernel(x, p, ln1_g, ln1_b, ln2_g, ln2_b, mlp_w1, mlp_w2, ple_w_proj, ple_w_gate, hy_w_in, hy_conv_w, hy_conv_b, hy_ga_w, hy_ga_b, hy_gx_w, hy_gx_b, hy_lambda, hy_w_out, mla_w_down, mla_q_norm, mla_kv_norm, mla_w_uq, mla_w_ukv, mla_w_out):
    bsz, s, d = x.shape
    m = bsz * s
    tm = _tile_m(m)
    tf = 1024
    row = lambda a: a.reshape(1, -1)
    x2 = x.reshape(m, d)

    def layer_tail(acts, wos, x2, i):
        return _tail(acts, wos, x2, p[i].reshape(m, -1), row(ln1_g[i]), row(ln1_b[i]),
                     mlp_w1[i].astype(BF16), mlp_w2[i].astype(BF16),
                     ple_w_gate[i].astype(BF16), ple_w_proj[i].astype(BF16),
                     row(ln2_g[i]), row(ln2_b[i]), tm, tf)

    xy, q, iq, misc = _inproj(x2, _hy_in_weight(hy_w_in[0]), tm)
    rec = _rglru(xy.reshape(bsz, s, -1), hy_conv_w[0], row(hy_conv_b[0]),
                 _block_diag_groups(hy_ga_w[0]), row(hy_ga_b[0]),
                 _block_diag_groups(hy_gx_w[0]), row(hy_gx_b[0]), row(hy_lambda[0]))
    att = _dsa(q.reshape(bsz, s, -1), iq.reshape(bsz, s, -1), misc.reshape(bsz, s, -1))
    w_out = hy_w_out[0].astype(BF16)
    x2 = layer_tail([rec.reshape(m, -1), att.reshape(m, -1)], [w_out[:RNN_WIDTH], w_out[RNN_WIDTH:]], x2, 0)

    wd, wq, wk, wv = _mla_weights(mla_w_down[0], mla_w_uq[0], mla_w_ukv[0])
    cos_t, sin_lo, sin_hi = _rope_tables(s)
    tmp = min(tm, s)
    qp, kp, vp = _mla_proj(x2, wd, row(mla_q_norm[0]), row(mla_kv_norm[0]), wq, wk, wv,
                           cos_t, sin_lo, sin_hi, tmp, s)
    o = _mla_attn(qp.reshape(bsz, s, -1), kp.reshape(bsz, s, -1), vp.reshape(bsz, s, -1), min(256, s))
    x2 = layer_tail([o.reshape(m, -1)], [mla_w_out[0].astype(BF16)], x2, 1)
    return x2.reshape(bsz, s, d)
```

```python
import functools
import math

import numpy as np
import jax
import jax.numpy as jnp
from jax import lax
from jax.experimental import pallas as pl
from jax.experimental.pallas import tpu as pltpu

F32 = jnp.float32
BF16 = jnp.bfloat16

RNN_WIDTH = 512
RNN_BLOCKS = 8
CONV_WIDTH = 4
LRU_C = 8.0
DSA_HEADS = 8
DSA_HEAD_DIM = 64
IDX_HEADS = 4
IDX_DIM = 64
DSA_TOPK_MAX = 256
MLA_HEADS = 16
MLA_Q_LORA = 512
MLA_KV_LORA = 256
MLA_NOPE = 64
MLA_ROPE = 32
MLA_V = 64
ROPE_BASE = 10000.0
DEPTH = 2
DN_ALPHA = (2 * DEPTH) ** 0.25
LN_EPS = 1e-5
RMS_EPS = 1e-6

LANES = 128
SUBLANES = 8
VMEM_LIMIT_BYTES = 56 * 1024 * 1024

MASK_NEG = -3.0e38
INT32_MIN = -(2 ** 31)


def _cparams(sem):
    return pltpu.CompilerParams(dimension_semantics=sem, vmem_limit_bytes=VMEM_LIMIT_BYTES)


def _dot(a, b):
    return jnp.dot(a, b, preferred_element_type=F32)


def _dot_nt(a, b):
    return lax.dot_general(a, b, (((1,), (1,)), ((), ())), preferred_element_type=F32)


def _layer_norm(y, g, b):
    mu = jnp.mean(y, axis=-1, keepdims=True)
    yc = y - mu
    var = jnp.mean(yc * yc, axis=-1, keepdims=True)
    return yc * lax.rsqrt(var + LN_EPS) * g + b


def _inproj_kernel(x_ref, w_ref, xy_ref, q_ref, iq_ref, misc_ref):
    xb = x_ref[...].astype(BF16)
    xy_ref[...] = _dot(xb, w_ref[:, 0:1024])
    q_ref[...] = (_dot(xb, w_ref[:, 1024:1536]) * (DSA_HEAD_DIM ** -0.5)).astype(BF16)
    iq_ref[...] = _dot(xb, w_ref[:, 1536:1792])
    misc_ref[...] = _dot(xb, w_ref[:, 1792:2048])


def _inproj(x2, w, tm):
    m, d = x2.shape
    n = w.shape[1]
    return pl.pallas_call(
        _inproj_kernel,
        grid=(m // tm,),
        in_specs=[pl.BlockSpec((tm, d), lambda i: (i, 0)),
                  pl.BlockSpec((d, n), lambda i: (0, 0))],
        out_specs=[pl.BlockSpec((tm, 1024), lambda i: (i, 0)),
                   pl.BlockSpec((tm, 512), lambda i: (i, 0)),
                   pl.BlockSpec((tm, 256), lambda i: (i, 0)),
                   pl.BlockSpec((tm, 256), lambda i: (i, 0))],
        out_shape=[jax.ShapeDtypeStruct((m, 1024), F32),
                   jax.ShapeDtypeStruct((m, 512), BF16),
                   jax.ShapeDtypeStruct((m, 256), F32),
                   jax.ShapeDtypeStruct((m, 256), F32)],
        compiler_params=_cparams(("parallel",)),
        name="hy_inproj",
    )(x2, w)


def _rglru_kernel(xr_ref, yr_ref, cw_ref, cb_ref, wa_ref, ba_ref, wx_ref, bx_ref, lam_ref,
                  o_ref, xpad_ref):
    s = xr_ref.shape[1]
    c = xr_ref.shape[2]
    xpad_ref[0:SUBLANES, :] = jnp.zeros((SUBLANES, c), F32)
    xpad_ref[SUBLANES:SUBLANES + s, :] = xr_ref[0]
    xc = None
    for k in range(CONV_WIDTH):
        start = SUBLANES - (CONV_WIDTH - 1) + k
        term = xpad_ref[start:start + s, :] * cw_ref[k:k + 1, :]
        xc = term if xc is None else xc + term
    xc = xc + cb_ref[...]
    xb = xc.astype(BF16)
    r = jax.nn.sigmoid(_dot(xb, wa_ref[0]) + ba_ref[...])
    gi = jax.nn.sigmoid(_dot(xb, wx_ref[0]) + bx_ref[...])
    nl = -lam_ref[...]
    softplus = jnp.maximum(nl, 0.0) + jnp.log1p(jnp.exp(-jnp.abs(nl)))
    log_a = (-LRU_C) * r * softplus
    a = jnp.exp(log_a)
    y = -jnp.tanh(log_a) * (a * a + 1.0)
    h = jnp.where(y > 0.0, y * lax.rsqrt(y), 0.0) * (gi * xc)
    row = lax.broadcasted_iota(jnp.int32, (s, c), 0)
    sh = 1
    while sh < s:
        valid = row >= sh
        h_prev = jnp.where(valid, pltpu.roll(h, sh, axis=0), 0.0)
        h = a * h_prev + h
        if sh * 2 < s:
            a = a * jnp.where(valid, pltpu.roll(a, sh, axis=0), 1.0)
        sh *= 2
    o_ref[0] = (h * jax.nn.gelu(yr_ref[0])).astype(o_ref.dtype)


def _rglru(xy, conv_w, conv_b, wa, ba, wx, bx, lam):
    bsz, s, _ = xy.shape
    c = LANES
    ng = RNN_WIDTH // c
    vec = lambda: pl.BlockSpec((1, c), lambda b, g: (0, g))
    return pl.pallas_call(
        _rglru_kernel,
        grid=(bsz, ng),
        in_specs=[pl.BlockSpec((1, s, c), lambda b, g: (b, 0, g)),
                  pl.BlockSpec((1, s, c), lambda b, g: (b, 0, ng + g)),
                  pl.BlockSpec((CONV_WIDTH, c), lambda b, g: (0, g)),
                  vec(),
                  pl.BlockSpec((1, c, c), lambda b, g: (g, 0, 0)),
                  vec(),
                  pl.BlockSpec((1, c, c), lambda b, g: (g, 0, 0)),
                  vec(),
                  vec()],
        out_specs=pl.BlockSpec((1, s, c), lambda b, g: (b, 0, g)),
        out_shape=jax.ShapeDtypeStruct((bsz, s, RNN_WIDTH), BF16),
        scratch_shapes=[pltpu.VMEM((s + SUBLANES, c), F32)],
        compiler_params=_cparams(("parallel", "parallel")),
        name="hy_rglru",
    )(xy, xy, conv_w, conv_b, wa, ba, wx, bx, lam)


def _ordered_int_to_float(c):
    return lax.bitcast_convert_type(c ^ (lax.shift_right_arithmetic(c, 31) & jnp.int32(0x7FFFFFFF)), F32)


def _dsa_kernel(q_ref, iq_ref, mq_ref, mk_ref, o_ref, sc_ref, bias_ref, *, topk, n_sub):
    tq = q_ref.shape[1]
    ext = mk_ref.shape[1]
    kf = float(topk)
    d0 = ext - tq
    keep = (lax.broadcasted_iota(jnp.int32, (tq, tq), 1)
            <= lax.broadcasted_iota(jnp.int32, (tq, tq), 0))

    mk = mk_ref[0]
    vlane = lax.broadcasted_iota(jnp.int32, (ext, DSA_HEAD_DIM), 1)
    v_aug = jnp.concatenate([mk[:, DSA_HEAD_DIM:2 * DSA_HEAD_DIM],
                             jnp.where(vlane == 0, 1.0, 0.0)], axis=1).astype(BF16)
    ik_b = mk[:, 2 * DSA_HEAD_DIM:2 * DSA_HEAD_DIM + IDX_DIM].astype(BF16)
    kcol = lax.broadcasted_iota(jnp.int32, (ext, 1), 0)
    lane = lax.broadcasted_iota(jnp.int32, (1, LANES), 1)
    pos_hi = (kcol & jnp.int32(-LANES)).astype(F32)
    pos_lo = (kcol & jnp.int32(LANES - 1)).astype(F32)
    k_aug = jnp.where(lane < DSA_HEAD_DIM, mk[:, 0:LANES],
                      jnp.where(lane == DSA_HEAD_DIM, pos_hi,
                                jnp.where(lane == DSA_HEAD_DIM + 1, pos_lo, 0.0))).astype(BF16)

    iw0 = 2 * DSA_HEAD_DIM + IDX_DIM
    iw = mq_ref[0][:, iw0:iw0 + IDX_HEADS] * (IDX_HEADS ** -0.5 * IDX_DIM ** -0.5)
    iq = iq_ref[0].astype(BF16)
    sc = None
    for h in range(IDX_HEADS):
        d = _dot_nt(iq[:, h * IDX_DIM:(h + 1) * IDX_DIM], ik_b)
        term = jnp.maximum(d, 0.0) * iw[:, h:h + 1]
        sc = term if sc is None else sc + term
    if d0 > 0:
        sc_ref[:, 0:d0] = sc[:, 0:d0]
    sc_ref[:, d0:ext] = jnp.where(keep, sc[:, d0:ext], MASK_NEG)

    rs = tq // n_sub

    def search(i, taus):
        bit = lax.shift_left(jnp.int32(1), 31 - i)
        out = []
        for g, tau_g in enumerate(taus):
            cand = tau_g + bit
            ge = sc_ref[g * rs:(g + 1) * rs, :] >= _ordered_int_to_float(cand)
            cnt = jnp.sum(jnp.where(ge, 1.0, 0.0), axis=-1, keepdims=True)
            out.append(jnp.where(cnt >= kf, cand, tau_g))
        return tuple(out)

    taus = lax.fori_loop(0, 32, search,
                         tuple(jnp.full((rs, 1), INT32_MIN, jnp.int32) for _ in range(n_sub)),
                         unroll=True)
    tau = _ordered_int_to_float(jnp.concatenate(taus, axis=0))

    sc = sc_ref[...]
    gt = sc > tau
    eq = sc == tau
    need = kf - jnp.sum(jnp.where(gt, 1.0, 0.0), axis=-1, keepdims=True)
    eqf = jnp.where(eq, 1.0, 0.0).astype(BF16)
    tri = (lax.broadcasted_iota(jnp.int32, (LANES, LANES), 0)
           <= lax.broadcasted_iota(jnp.int32, (LANES, LANES), 1)).astype(BF16)
    off = jnp.zeros((tq, 1), F32)
    for c in range(ext // LANES):
        sl = slice(c * LANES, (c + 1) * LANES)
        pc = _dot(eqf[:, sl], tri)
        take = gt[:, sl] | (eq[:, sl] & ((pc + off) <= need))
        if c * LANES >= d0:
            take = take & keep[:, c * LANES - d0:(c + 1) * LANES - d0]
        bias_ref[:, sl] = jnp.where(take, 0.0, MASK_NEG)
        off = off + pc[:, LANES - 1:LANES]

    q = q_ref[0]
    tail_lane = lax.broadcasted_iota(jnp.int32, (tq, LANES - DSA_HEAD_DIM), 1)
    for h in range(DSA_HEADS):
        slope = 2.0 ** (-8.0 * (h + 1) / DSA_HEADS)
        tail = jnp.where(tail_lane < 2, slope, 0.0).astype(BF16)
        q_aug = jnp.concatenate([q[:, h * DSA_HEAD_DIM:(h + 1) * DSA_HEAD_DIM], tail], axis=1)
        lg = _dot_nt(q_aug, k_aug) + bias_ref[...]
        p = jnp.exp(lg - jnp.max(lg, axis=-1, keepdims=True))
        o = _dot(p.astype(BF16), v_aug)
        o = o[:, 0:DSA_HEAD_DIM] / o[:, DSA_HEAD_DIM:DSA_HEAD_DIM + 1]
        o_ref[0, :, h * DSA_HEAD_DIM:(h + 1) * DSA_HEAD_DIM] = o.astype(o_ref.dtype)


def _dsa(q, iq, misc):
    bsz, s, _ = q.shape
    topk = min(DSA_TOPK_MAX, s // 4)
    tq = min(256, s)
    hd = DSA_HEADS * DSA_HEAD_DIM
    outs = []
    for c in range(s // tq):
        ext = (c + 1) * tq
        outs.append(pl.pallas_call(
            functools.partial(_dsa_kernel, topk=topk, n_sub=2),
            grid=(bsz,),
            in_specs=[pl.BlockSpec((1, tq, hd), lambda b, c=c: (b, c, 0)),
                      pl.BlockSpec((1, tq, IDX_HEADS * IDX_DIM), lambda b, c=c: (b, c, 0)),
                      pl.BlockSpec((1, tq, 256), lambda b, c=c: (b, c, 0)),
                      pl.BlockSpec((1, ext, 256), lambda b: (b, 0, 0))],
            out_specs=pl.BlockSpec((1, tq, hd), lambda b: (b, 0, 0)),
            out_shape=jax.ShapeDtypeStruct((bsz, tq, hd), BF16),
            scratch_shapes=[pltpu.VMEM((tq, ext), F32), pltpu.VMEM((tq, ext), F32)],
            compiler_params=_cparams(("parallel",)),
            name=f"hy_dsa_c{c}",
        )(q, iq, misc, misc))
    return outs[0] if len(outs) == 1 else jnp.concatenate(outs, axis=1)


def _tail_kernel(*refs, n_in, tf):
    a_refs = refs[:n_in]
    wo_refs = refs[n_in:2 * n_in]
    (x_ref, p_ref, g1_ref, b1_ref, w1_ref, w2_ref, wg_ref, wp_ref, g2_ref, b2_ref,
     o_ref, acc_ref) = refs[2 * n_in:]
    m = None
    for a_ref, w_ref in zip(a_refs, wo_refs):
        t = _dot(a_ref[...].astype(BF16), w_ref[...])
        m = t if m is None else m + t
    h = _layer_norm(DN_ALPHA * x_ref[...] + m, g1_ref[...], b1_ref[...])
    hb = h.astype(BF16)
    gate = jax.nn.sigmoid(_dot(hb, wg_ref[...]))
    acc_ref[...] = DN_ALPHA * h + gate * _dot(p_ref[...].astype(BF16), wp_ref[...])
    for c in range(w1_ref.shape[1] // tf):
        a = jnp.maximum(_dot(hb, w1_ref[:, c * tf:(c + 1) * tf]), 0.0)
        acc_ref[...] += _dot((a * a).astype(BF16), w2_ref[c * tf:(c + 1) * tf, :])
    o_ref[...] = _layer_norm(acc_ref[...], g2_ref[...], b2_ref[...])


def _tail(acts, wos, x2, p2, g1, b1, w1, w2, wg, wp, g2, b2, tm, tf):
    m, d = x2.shape
    n_in = len(acts)
    rows = lambda a: pl.BlockSpec((tm, a.shape[1]), lambda i: (i, 0))
    resident = lambda a: pl.BlockSpec(a.shape, lambda i: (0, 0), pipeline_mode=pl.Buffered(1))
    params = [g1, b1, w1, w2, wg, wp, g2, b2]
    return pl.pallas_call(
        functools.partial(_tail_kernel, n_in=n_in, tf=tf),
        grid=(m // tm,),
        in_specs=([rows(a) for a in acts] + [resident(w) for w in wos] + [rows(x2), rows(p2)]
                  + [resident(a) for a in params]),
        out_specs=pl.BlockSpec((tm, d), lambda i: (i, 0)),
        out_shape=jax.ShapeDtypeStruct((m, d), F32),
        scratch_shapes=[pltpu.VMEM((tm, d), F32)],
        compiler_params=_cparams(("parallel",)),
        name="layer_tail",
    )(*acts, *wos, x2, p2, *params)


MLA_Q_SCALE = (MLA_NOPE + MLA_ROPE) ** -0.5 * math.log2(math.e)
ROPE_HALF = MLA_ROPE // 2
NOPE_LO = LANES // 2 - ROPE_HALF


def _rope_group(t, cos_t, sin_t):
    return t * cos_t + pltpu.roll(t, LANES // 2, axis=1) * sin_t


def _mla_proj_kernel(x_ref, wd_ref, qg_ref, kvg_ref, wq_ref, wk_ref, wv_ref,
                     cosq_ref, sinq_ref, cosk_ref, sink_ref, q_ref, k_ref, v_ref):
    xb = x_ref[...].astype(BF16)
    down = _dot(xb, wd_ref[...])
    cq = down[:, 0:MLA_Q_LORA]
    ckv = down[:, MLA_Q_LORA:MLA_Q_LORA + MLA_KV_LORA]
    krg = down[:, MLA_Q_LORA + MLA_KV_LORA:]
    cqn = (cq * lax.rsqrt(jnp.mean(cq * cq, axis=-1, keepdims=True) + RMS_EPS) * qg_ref[...]).astype(BF16)
    ckvn = (ckv * lax.rsqrt(jnp.mean(ckv * ckv, axis=-1, keepdims=True) + RMS_EPS) * kvg_ref[...]).astype(BF16)
    cos_q, sin_q = cosq_ref[...], sinq_ref[...]
    kr = _rope_group(krg, cosk_ref[...], sink_ref[...])
    q = _dot(cqn, wq_ref[...])
    k = _dot(ckvn, wk_ref[...])
    for h in range(MLA_HEADS):
        sl = slice(h * LANES, (h + 1) * LANES)
        q_ref[:, sl] = _rope_group(q[:, sl], cos_q, sin_q).astype(BF16)
        k_ref[:, sl] = (k[:, sl] + kr).astype(BF16)
    v_ref[...] = _dot(ckvn, wv_ref[...]).astype(BF16)


def _mla_proj(x2, wd, qg, kvg, wq, wk, wv, tables, tm, s):
    m, d = x2.shape
    nt = s // tm
    full = lambda a: pl.BlockSpec(a.shape, lambda i: (0, 0))
    tab = lambda: pl.BlockSpec((tm, LANES), lambda i: (i % nt, 0))
    hq = MLA_HEADS * LANES
    hv = MLA_HEADS * MLA_V
    return pl.pallas_call(
        _mla_proj_kernel,
        grid=(m // tm,),
        in_specs=[pl.BlockSpec((tm, d), lambda i: (i, 0)), full(wd), full(qg), full(kvg),
                  full(wq), full(wk), full(wv)] + [tab() for _ in tables],
        out_specs=[pl.BlockSpec((tm, hq), lambda i: (i, 0)),
                   pl.BlockSpec((tm, hq), lambda i: (i, 0)),
                   pl.BlockSpec((tm, hv), lambda i: (i, 0))],
        out_shape=[jax.ShapeDtypeStruct((m, hq), BF16),
                   jax.ShapeDtypeStruct((m, hq), BF16),
                   jax.ShapeDtypeStruct((m, hv), BF16)],
        compiler_params=_cparams(("parallel",)),
        name="mla_proj",
    )(x2, wd, qg, kvg, wq, wk, wv, *tables)


def _mla_attn_kernel(q_ref, k_ref, v_ref, o_ref, ot_ref, *, tq, heads_per_step):
    s = q_ref.shape[1]
    keep_t = (lax.broadcasted_iota(jnp.int32, (tq, tq), 0)
              <= lax.broadcasted_iota(jnp.int32, (tq, tq), 1))
    v_t = jnp.transpose(v_ref[0].astype(F32)).astype(BF16)
    ones_rows = jnp.ones((2 * SUBLANES, s), BF16)
    lhs = [jnp.concatenate([v_t[hh * MLA_V:(hh + 1) * MLA_V, :], ones_rows], axis=0)
           for hh in range(heads_per_step)]
    units = [(j * tq, hh) for j in range(s // tq) for hh in range(heads_per_step)]

    def logits(r0, hh):
        ql = slice(hh * LANES, (hh + 1) * LANES)
        qh = q_ref[0, r0:r0 + tq, ql]
        sd = jnp.where(keep_t, _dot_nt(k_ref[0, r0:r0 + tq, ql], qh), MASK_NEG)
        so = _dot_nt(k_ref[0, 0:r0, ql], qh) if r0 > 0 else None
        return sd, so

    def probs(sd, so):
        m = jnp.max(sd, axis=0, keepdims=True)
        if so is None:
            return jnp.exp2(sd - m).astype(BF16), None
        m = jnp.maximum(m, jnp.max(so, axis=0, keepdims=True))
        return jnp.exp2(sd - m).astype(BF16), jnp.exp2(so - m).astype(BF16)

    def values(r0, hh, pd, po):
        acc = _dot(lhs[hh][:, r0:r0 + tq], pd)
        if po is not None:
            acc = acc + _dot(lhs[hh][:, 0:r0], po)
        ot_ref[hh * MLA_V:(hh + 1) * MLA_V, r0:r0 + tq] = acc[0:MLA_V, :] / acc[MLA_V:MLA_V + 1, :]

    n = len(units)
    s_cur = logits(*units[0])
    p_prev = None
    for u in range(n + 1):
        s_nxt = logits(*units[u + 1]) if u + 1 < n else None
        p_cur = probs(*s_cur) if u < n else None
        if p_prev is not None:
            values(*units[u - 1], *p_prev)
        s_cur, p_prev = s_nxt, p_cur
    o_ref[0] = jnp.transpose(ot_ref[...]).astype(o_ref.dtype)


def _mla_attn(q, k, v, tq):
    bsz, s, _ = q.shape
    hps = 2
    return pl.pallas_call(
        functools.partial(_mla_attn_kernel, tq=tq, heads_per_step=hps),
        grid=(bsz, MLA_HEADS // hps),
        in_specs=[pl.BlockSpec((1, s, hps * LANES), lambda b, h: (b, 0, h)),
                  pl.BlockSpec((1, s, hps * LANES), lambda b, h: (b, 0, h)),
                  pl.BlockSpec((1, s, hps * MLA_V), lambda b, h: (b, 0, h))],
        out_specs=pl.BlockSpec((1, s, hps * MLA_V), lambda b, h: (b, 0, h)),
        out_shape=jax.ShapeDtypeStruct((bsz, s, MLA_HEADS * MLA_V), BF16),
        scratch_shapes=[pltpu.VMEM((hps * MLA_V, s), F32)],
        compiler_params=_cparams(("parallel", "parallel")),
        name="mla_attn",
    )(q, k, v)


def _hy_in_weight(w_in):
    xr, yr, q, k, v, iq, ik, iw = jnp.split(
        w_in, np.cumsum([512, 512, 512, 64, 64, 256, 64, 4])[:-1].tolist(), axis=1)
    pad = jnp.zeros((w_in.shape[0], 256 - 64 * 3 - 4), w_in.dtype)
    return jnp.concatenate([xr, yr, q, iq, k, v, ik, iw, pad], axis=1).astype(BF16)


def _block_diag_groups(w):
    bw = w.shape[-1]
    per = LANES // bw
    ng = w.shape[0] // per
    out = jnp.zeros((ng, LANES, LANES), w.dtype)
    for g in range(ng):
        for t in range(per):
            out = out.at[g, t * bw:(t + 1) * bw, t * bw:(t + 1) * bw].set(w[g * per + t])
    return out.astype(BF16)


def _head_lanes(nope, rope):
    lead = nope.shape[:-1] if nope is not None else rope.shape[:-1]
    dt = nope.dtype if nope is not None else rope.dtype
    z = lambda n: jnp.zeros(lead + (n,), dt)
    r1, r2 = (rope[..., :ROPE_HALF], rope[..., ROPE_HALF:]) if rope is not None else (z(ROPE_HALF), z(ROPE_HALF))
    n1, n2 = (nope[..., :NOPE_LO], nope[..., NOPE_LO:]) if nope is not None else (z(NOPE_LO), z(MLA_NOPE - NOPE_LO))
    return jnp.concatenate([r1, n1, r2, n2, z(LANES - MLA_NOPE - MLA_ROPE)], axis=-1)


def _mla_weights(w_down, w_uq, w_ukv):
    lat = MLA_Q_LORA + MLA_KV_LORA
    wd = jnp.concatenate([w_down[:, :lat], _head_lanes(None, w_down[:, lat:])], axis=1)
    wq = w_uq.reshape(MLA_Q_LORA, MLA_HEADS, MLA_NOPE + MLA_ROPE)
    wq = _head_lanes(wq[:, :, :MLA_NOPE], wq[:, :, MLA_NOPE:]).reshape(MLA_Q_LORA, MLA_HEADS * LANES)
    wkv = w_ukv.reshape(MLA_KV_LORA, MLA_HEADS, MLA_NOPE + MLA_V)
    wk = _head_lanes(wkv[:, :, :MLA_NOPE], None).reshape(MLA_KV_LORA, MLA_HEADS * LANES)
    wv = wkv[:, :, MLA_NOPE:].reshape(MLA_KV_LORA, MLA_HEADS * MLA_V)
    return wd.astype(BF16), wq.astype(BF16), wk.astype(BF16), wv.astype(BF16)


def _rope_tables(s):
    pos = jnp.arange(s, dtype=F32)
    freq = ROPE_BASE ** (-jnp.arange(0, MLA_ROPE, 2, dtype=F32) / MLA_ROPE)
    ang = pos[:, None] * freq[None, :]
    cos, sin = jnp.cos(ang), jnp.sin(ang)
    cos_t = _head_lanes(jnp.ones((s, MLA_NOPE), F32), jnp.concatenate([cos, cos], axis=1))
    cos_t = cos_t.at[:, MLA_NOPE + MLA_ROPE:].set(1.0)
    sin_t = _head_lanes(None, jnp.concatenate([-sin, sin], axis=1))
    return cos_t * MLA_Q_SCALE, sin_t * MLA_Q_SCALE, cos_t, sin_t


def _tile_m(m):
    return 512 if m % 512 == 0 else m


def kernel(x, p, ln1_g, ln1_b, ln2_g, ln2_b, mlp_w1, mlp_w2, ple_w_proj, ple_w_gate, hy_w_in, hy_conv_w, hy_conv_b, hy_ga_w, hy_ga_b, hy_gx_w, hy_gx_b, hy_lambda, hy_w_out, mla_w_down, mla_q_norm, mla_kv_norm, mla_w_uq, mla_w_ukv, mla_w_out):
    bsz, s, d = x.shape
    m = bsz * s
    tm = _tile_m(m)
    tf = 1024
    row = lambda a: a.reshape(1, -1)
    x2 = x.reshape(m, d)

    def layer_tail(acts, wos, x2, i):
        return _tail(acts, wos, x2, p[i].reshape(m, -1), row(ln1_g[i]), row(ln1_b[i]),
                     mlp_w1[i].astype(BF16), mlp_w2[i].astype(BF16),
                     ple_w_gate[i].astype(BF16), ple_w_proj[i].astype(BF16),
                     row(ln2_g[i]), row(ln2_b[i]), tm, tf)

    xy, q, iq, misc = _inproj(x2, _hy_in_weight(hy_w_in[0]), tm)
    rec = _rglru(xy.reshape(bsz, s, -1), hy_conv_w[0], row(hy_conv_b[0]),
                 _block_diag_groups(hy_ga_w[0]), row(hy_ga_b[0]),
                 _block_diag_groups(hy_gx_w[0]), row(hy_gx_b[0]), row(hy_lambda[0]))
    att = _dsa(q.reshape(bsz, s, -1), iq.reshape(bsz, s, -1), misc.reshape(bsz, s, -1))
    w_out = hy_w_out[0].astype(BF16)
    x2 = layer_tail([rec.reshape(m, -1), att.reshape(m, -1)], [w_out[:RNN_WIDTH], w_out[RNN_WIDTH:]], x2, 0)

    wd, wq, wk, wv = _mla_weights(mla_w_down[0], mla_w_uq[0], mla_w_ukv[0])
    qp, kp, vp = _mla_proj(x2, wd, row(mla_q_norm[0]), row(mla_kv_norm[0]), wq, wk, wv,
                           _rope_tables(s), min(tm, s), s)
    o = _mla_attn(qp.reshape(bsz, s, -1), kp.reshape(bsz, s, -1), vp.reshape(bsz, s, -1), min(256, s))
    x2 = layer_tail([o.reshape(m, -1)], [mla_w_out[0].astype(BF16)], x2, 1)
    return x2.reshape(bsz, s, d)
```

```python
import functools
import math

import numpy as np
import jax
import jax.numpy as jnp
from jax import lax
from jax.experimental import pallas as pl
from jax.experimental.pallas import tpu as pltpu

F32 = jnp.float32
BF16 = jnp.bfloat16

RNN_WIDTH = 512
RNN_BLOCKS = 8
CONV_WIDTH = 4
LRU_C = 8.0
DSA_HEADS = 8
DSA_HEAD_DIM = 64
IDX_HEADS = 4
IDX_DIM = 64
DSA_TOPK_MAX = 256
MLA_HEADS = 16
MLA_Q_LORA = 512
MLA_KV_LORA = 256
MLA_NOPE = 64
MLA_ROPE = 32
MLA_V = 64
ROPE_BASE = 10000.0
DEPTH = 2
DN_ALPHA = (2 * DEPTH) ** 0.25
LN_EPS = 1e-5
RMS_EPS = 1e-6

LANES = 128
SUBLANES = 8
VMEM_LIMIT_BYTES = 56 * 1024 * 1024

MASK_NEG = -3.0e38
INT32_MIN = -(2 ** 31)


def _cparams(sem):
    return pltpu.CompilerParams(dimension_semantics=sem, vmem_limit_bytes=VMEM_LIMIT_BYTES)


def _dot(a, b):
    return jnp.dot(a, b, preferred_element_type=F32)


def _dot_nt(a, b):
    return lax.dot_general(a, b, (((1,), (1,)), ((), ())), preferred_element_type=F32)


def _layer_norm(y, g, b):
    mu = jnp.mean(y, axis=-1, keepdims=True)
    yc = y - mu
    var = jnp.mean(yc * yc, axis=-1, keepdims=True)
    return yc * lax.rsqrt(var + LN_EPS) * g + b


def _inproj_kernel(x_ref, w_ref, xy_ref, q_ref, iq_ref, misc_ref):
    xb = x_ref[...].astype(BF16)
    xy_ref[...] = _dot(xb, w_ref[:, 0:1024])
    q_ref[...] = (_dot(xb, w_ref[:, 1024:1536]) * (DSA_HEAD_DIM ** -0.5)).astype(BF16)
    iq_ref[...] = _dot(xb, w_ref[:, 1536:1792])
    misc_ref[...] = _dot(xb, w_ref[:, 1792:2048])


def _inproj(x2, w, tm):
    m, d = x2.shape
    n = w.shape[1]
    return pl.pallas_call(
        _inproj_kernel,
        grid=(m // tm,),
        in_specs=[pl.BlockSpec((tm, d), lambda i: (i, 0)),
                  pl.BlockSpec((d, n), lambda i: (0, 0))],
        out_specs=[pl.BlockSpec((tm, 1024), lambda i: (i, 0)),
                   pl.BlockSpec((tm, 512), lambda i: (i, 0)),
                   pl.BlockSpec((tm, 256), lambda i: (i, 0)),
                   pl.BlockSpec((tm, 256), lambda i: (i, 0))],
        out_shape=[jax.ShapeDtypeStruct((m, 1024), F32),
                   jax.ShapeDtypeStruct((m, 512), BF16),
                   jax.ShapeDtypeStruct((m, 256), F32),
                   jax.ShapeDtypeStruct((m, 256), F32)],
        compiler_params=_cparams(("parallel",)),
        name="hy_inproj",
    )(x2, w)


def _rglru_kernel(xr_ref, yr_ref, cw_ref, cb_ref, wa_ref, ba_ref, wx_ref, bx_ref, lam_ref,
                  o_ref, xpad_ref):
    s = xr_ref.shape[1]
    c = xr_ref.shape[2]
    xpad_ref[0:SUBLANES, :] = jnp.zeros((SUBLANES, c), F32)
    xpad_ref[SUBLANES:SUBLANES + s, :] = xr_ref[0]
    xc = None
    for k in range(CONV_WIDTH):
        start = SUBLANES - (CONV_WIDTH - 1) + k
        term = xpad_ref[start:start + s, :] * cw_ref[k:k + 1, :]
        xc = term if xc is None else xc + term
    xc = xc + cb_ref[...]
    xb = xc.astype(BF16)
    r = jax.nn.sigmoid(_dot(xb, wa_ref[0]) + ba_ref[...])
    gi = jax.nn.sigmoid(_dot(xb, wx_ref[0]) + bx_ref[...])
    nl = -lam_ref[...]
    softplus = jnp.maximum(nl, 0.0) + jnp.log1p(jnp.exp(-jnp.abs(nl)))
    log_a = (-LRU_C) * r * softplus
    a = jnp.exp(log_a)
    y = -jnp.tanh(log_a) * (a * a + 1.0)
    h = jnp.where(y > 0.0, y * lax.rsqrt(y), 0.0) * (gi * xc)
    row = lax.broadcasted_iota(jnp.int32, (s, c), 0)
    sh = 1
    while sh < s:
        if sh < SUBLANES:
            valid = row >= sh
            h = a * jnp.where(valid, pltpu.roll(h, sh, axis=0), 0.0) + h
            a = a * jnp.where(valid, pltpu.roll(a, sh, axis=0), 1.0)
        else:
            h = jnp.concatenate([h[:sh], a[sh:] * h[:s - sh] + h[sh:]], axis=0)
            if sh * 2 < s:
                a = jnp.concatenate([a[:sh], a[sh:] * a[:s - sh]], axis=0)
        sh *= 2
    o_ref[0] = (h * jax.nn.gelu(yr_ref[0])).astype(o_ref.dtype)


def _rglru(xy, conv_w, conv_b, wa, ba, wx, bx, lam):
    bsz, s, _ = xy.shape
    c = LANES
    ng = RNN_WIDTH // c
    vec = lambda: pl.BlockSpec((1, c), lambda b, g: (0, g))
    return pl.pallas_call(
        _rglru_kernel,
        grid=(bsz, ng),
        in_specs=[pl.BlockSpec((1, s, c), lambda b, g: (b, 0, g)),
                  pl.BlockSpec((1, s, c), lambda b, g: (b, 0, ng + g)),
                  pl.BlockSpec((CONV_WIDTH, c), lambda b, g: (0, g)),
                  vec(),
                  pl.BlockSpec((1, c, c), lambda b, g: (g, 0, 0)),
                  vec(),
                  pl.BlockSpec((1, c, c), lambda b, g: (g, 0, 0)),
                  vec(),
                  vec()],
        out_specs=pl.BlockSpec((1, s, c), lambda b, g: (b, 0, g)),
        out_shape=jax.ShapeDtypeStruct((bsz, s, RNN_WIDTH), BF16),
        scratch_shapes=[pltpu.VMEM((s + SUBLANES, c), F32)],
        compiler_params=_cparams(("parallel", "parallel")),
        name="hy_rglru",
    )(xy, xy, conv_w, conv_b, wa, ba, wx, bx, lam)


def _ordered_int_to_float(c):
    return lax.bitcast_convert_type(c ^ (lax.shift_right_arithmetic(c, 31) & jnp.int32(0x7FFFFFFF)), F32)


def _reduce_keys(x, reduce_fn, combine_fn, groups=8):
    rows = x.shape[0]
    step = max(rows // groups, SUBLANES)
    parts = [reduce_fn(x[r:r + step, :], axis=0, keepdims=True) for r in range(0, rows, step)]
    while len(parts) > 1:
        parts = [combine_fn(parts[i], parts[i + 1]) if i + 1 < len(parts) else parts[i]
                 for i in range(0, len(parts), 2)]
    return parts[0]


def _dsa_prepare(q_ref, mk_ref):
    tq = q_ref.shape[1]
    ext = mk_ref.shape[1]
    d0 = ext - tq
    keep = (lax.broadcasted_iota(jnp.int32, (tq, tq), 1)
            <= lax.broadcasted_iota(jnp.int32, (tq, tq), 0))

    mk = mk_ref[0]
    vlane = lax.broadcasted_iota(jnp.int32, (ext, DSA_HEAD_DIM), 1)
    v_aug = jnp.concatenate([mk[:, DSA_HEAD_DIM:2 * DSA_HEAD_DIM],
                             jnp.where(vlane == 0, 1.0, 0.0)], axis=1).astype(BF16)
    ik_b = mk[:, 2 * DSA_HEAD_DIM:2 * DSA_HEAD_DIM + IDX_DIM].astype(BF16)
    kcol = lax.broadcasted_iota(jnp.int32, (ext, 1), 0)
    lane = lax.broadcasted_iota(jnp.int32, (1, LANES), 1)
    pos_hi = (kcol & jnp.int32(-LANES)).astype(F32)
    pos_lo = (kcol & jnp.int32(LANES - 1)).astype(F32)
    k_aug = jnp.where(lane < DSA_HEAD_DIM, mk[:, 0:LANES],
                      jnp.where(lane == DSA_HEAD_DIM, pos_hi,
                                jnp.where(lane == DSA_HEAD_DIM + 1, pos_lo, 0.0))).astype(BF16)
    return d0, keep, ik_b, k_aug, v_aug


def _dsa_select(iq_ref, mq_ref, sc_ref, bias_ref, ik_b, keep, *, topk):
    tq, ext = sc_ref.shape
    d0 = ext - tq
    kf = float(topk)
    iw0 = 2 * DSA_HEAD_DIM + IDX_DIM
    iw = mq_ref[0][:, iw0:iw0 + IDX_HEADS] * (IDX_HEADS ** -0.5 * IDX_DIM ** -0.5)
    iq = iq_ref[0].astype(BF16)
    sc = None
    for h in range(IDX_HEADS):
        d = _dot_nt(iq[:, h * IDX_DIM:(h + 1) * IDX_DIM], ik_b)
        term = jnp.maximum(d, 0.0) * iw[:, h:h + 1]
        sc = term if sc is None else sc + term
    if d0 > 0:
        sc_ref[:, 0:d0] = sc[:, 0:d0]
    sc_ref[:, d0:ext] = jnp.where(keep, sc[:, d0:ext], MASK_NEG)

    def search(i, tau):
        cand = tau + lax.shift_left(jnp.int32(1), 31 - i)
        ge = sc_ref[...] >= _ordered_int_to_float(cand)
        cnt = jnp.sum(jnp.where(ge, 1.0, 0.0), axis=-1, keepdims=True)
        return jnp.where(cnt >= kf, cand, tau)

    tau = lax.fori_loop(0, 32, search, jnp.full((tq, 1), INT32_MIN, jnp.int32), unroll=True)
    tau = _ordered_int_to_float(tau)

    sc = sc_ref[...]
    gt = sc > tau
    eq = sc == tau
    need = kf - jnp.sum(jnp.where(gt, 1.0, 0.0), axis=-1, keepdims=True)
    eqf = jnp.where(eq, 1.0, 0.0).astype(BF16)
    tri = (lax.broadcasted_iota(jnp.int32, (LANES, LANES), 0)
           <= lax.broadcasted_iota(jnp.int32, (LANES, LANES), 1)).astype(BF16)
    off = jnp.zeros((tq, 1), F32)
    for c in range(ext // LANES):
        sl = slice(c * LANES, (c + 1) * LANES)
        pc = _dot(eqf[:, sl], tri)
        take = gt[:, sl] | (eq[:, sl] & ((pc + off) <= need))
        if c * LANES >= d0:
            take = take & keep[:, c * LANES - d0:(c + 1) * LANES - d0]
        bias_ref[:, sl] = jnp.where(take, 0.0, MASK_NEG)
        off = off + pc[:, LANES - 1:LANES]


def _dsa_attend(q_ref, o_ref, bias_ref, k_aug, v_aug):
    tq = q_ref.shape[1]
    q = q_ref[0]
    tail_lane = lax.broadcasted_iota(jnp.int32, (tq, LANES - DSA_HEAD_DIM), 1)
    for h in range(DSA_HEADS):
        slope = 2.0 ** (-8.0 * (h + 1) / DSA_HEADS)
        tail = jnp.where(tail_lane < 2, slope, 0.0).astype(BF16)
        q_aug = jnp.concatenate([q[:, h * DSA_HEAD_DIM:(h + 1) * DSA_HEAD_DIM], tail], axis=1)
        lg = _dot_nt(q_aug, k_aug) + bias_ref[...]
        p = jnp.exp(lg - jnp.max(lg, axis=-1, keepdims=True))
        o = _dot(p.astype(BF16), v_aug)
        o = o[:, 0:DSA_HEAD_DIM] / o[:, DSA_HEAD_DIM:DSA_HEAD_DIM + 1]
        o_ref[0, :, h * DSA_HEAD_DIM:(h + 1) * DSA_HEAD_DIM] = o.astype(o_ref.dtype)


def _dsa_kernel(q_ref, iq_ref, mq_ref, mk_ref, o_ref, sc_ref, bias_ref, *, topk):
    d0, keep, ik_b, k_aug, v_aug = _dsa_prepare(q_ref, mk_ref)
    if mk_ref.shape[1] > topk:
        _dsa_select(iq_ref, mq_ref, sc_ref, bias_ref, ik_b, keep, topk=topk)
    else:
        if d0 > 0:
            bias_ref[:, 0:d0] = jnp.zeros((keep.shape[0], d0), F32)
        bias_ref[:, d0:] = jnp.where(keep, 0.0, MASK_NEG)
    _dsa_attend(q_ref, o_ref, bias_ref, k_aug, v_aug)


def _dsa(q, iq, misc):
    bsz, s, _ = q.shape
    topk = min(DSA_TOPK_MAX, s // 4)
    tq = min(256, s)
    hd = DSA_HEADS * DSA_HEAD_DIM
    outs = []
    for c in range(s // tq):
        ext = (c + 1) * tq
        outs.append(pl.pallas_call(
            functools.partial(_dsa_kernel, topk=topk),
            grid=(bsz,),
            in_specs=[pl.BlockSpec((1, tq, hd), lambda b, c=c: (b, c, 0)),
                      pl.BlockSpec((1, tq, IDX_HEADS * IDX_DIM), lambda b, c=c: (b, c, 0)),
                      pl.BlockSpec((1, tq, 256), lambda b, c=c: (b, c, 0)),
                      pl.BlockSpec((1, ext, 256), lambda b: (b, 0, 0))],
            out_specs=pl.BlockSpec((1, tq, hd), lambda b: (b, 0, 0)),
            out_shape=jax.ShapeDtypeStruct((bsz, tq, hd), BF16),
            scratch_shapes=[pltpu.VMEM((tq, ext), F32), pltpu.VMEM((tq, ext), F32)],
            compiler_params=_cparams(("parallel",)),
            name=f"hy_dsa_c{c}",
        )(q, iq, misc, misc))
    return outs[0] if len(outs) == 1 else jnp.concatenate(outs, axis=1)


def _tail_kernel(*refs, n_in, tf):
    a_refs = refs[:n_in]
    wo_refs = refs[n_in:2 * n_in]
    (x_ref, p_ref, g1_ref, b1_ref, w1_ref, w2_ref, wg_ref, wp_ref, g2_ref, b2_ref,
     o_ref, acc_ref) = refs[2 * n_in:]
    m = None
    for a_ref, w_ref in zip(a_refs, wo_refs):
        t = _dot(a_ref[...].astype(BF16), w_ref[...])
        m = t if m is None else m + t
    h = _layer_norm(DN_ALPHA * x_ref[...] + m, g1_ref[...], b1_ref[...])
    hb = h.astype(BF16)
    gate = jax.nn.sigmoid(_dot(hb, wg_ref[...]))
    acc_ref[...] = DN_ALPHA * h + gate * _dot(p_ref[...].astype(BF16), wp_ref[...])
    for c in range(w1_ref.shape[1] // tf):
        a = jnp.maximum(_dot(hb, w1_ref[:, c * tf:(c + 1) * tf]), 0.0)
        acc_ref[...] += _dot((a * a).astype(BF16), w2_ref[c * tf:(c + 1) * tf, :])
    o_ref[...] = _layer_norm(acc_ref[...], g2_ref[...], b2_ref[...])


def _tail(acts, wos, x2, p2, g1, b1, w1, w2, wg, wp, g2, b2, tm, tf):
    m, d = x2.shape
    n_in = len(acts)
    rows = lambda a: pl.BlockSpec((tm, a.shape[1]), lambda i: (i, 0))
    resident = lambda a: pl.BlockSpec(a.shape, lambda i: (0, 0), pipeline_mode=pl.Buffered(1))
    params = [g1, b1, w1, w2, wg, wp, g2, b2]
    return pl.pallas_call(
        functools.partial(_tail_kernel, n_in=n_in, tf=tf),
        grid=(m // tm,),
        in_specs=([rows(a) for a in acts] + [resident(w) for w in wos] + [rows(x2), rows(p2)]
                  + [resident(a) for a in params]),
        out_specs=pl.BlockSpec((tm, d), lambda i: (i, 0)),
        out_shape=jax.ShapeDtypeStruct((m, d), F32),
        scratch_shapes=[pltpu.VMEM((tm, d), F32)],
        compiler_params=_cparams(("parallel",)),
        name="layer_tail",
    )(*acts, *wos, x2, p2, *params)


MLA_Q_SCALE = (MLA_NOPE + MLA_ROPE) ** -0.5 * math.log2(math.e)
ROPE_HALF = MLA_ROPE // 2
NOPE_LO = LANES // 2 - ROPE_HALF


def _rope_group(t, cos_t, sin_t):
    return t * cos_t + pltpu.roll(t, LANES // 2, axis=1) * sin_t


def _mla_proj_kernel(x_ref, wd_ref, qg_ref, kvg_ref, wq_ref, wk_ref, wv_ref,
                     cosq_ref, sinq_ref, cosk_ref, sink_ref, q_ref, k_ref, v_ref):
    xb = x_ref[...].astype(BF16)
    down = _dot(xb, wd_ref[...])
    cq = down[:, 0:MLA_Q_LORA]
    ckv = down[:, MLA_Q_LORA:MLA_Q_LORA + MLA_KV_LORA]
    krg = down[:, MLA_Q_LORA + MLA_KV_LORA:]
    cqn = (cq * lax.rsqrt(jnp.mean(cq * cq, axis=-1, keepdims=True) + RMS_EPS) * qg_ref[...]).astype(BF16)
    ckvn = (ckv * lax.rsqrt(jnp.mean(ckv * ckv, axis=-1, keepdims=True) + RMS_EPS) * kvg_ref[...]).astype(BF16)
    cos_q, sin_q = cosq_ref[...], sinq_ref[...]
    kr = _rope_group(krg, cosk_ref[...], sink_ref[...])
    q = _dot(cqn, wq_ref[...])
    k = _dot(ckvn, wk_ref[...])
    for h in range(MLA_HEADS):
        sl = slice(h * LANES, (h + 1) * LANES)
        q_ref[:, sl] = _rope_group(q[:, sl], cos_q, sin_q).astype(BF16)
        k_ref[:, sl] = (k[:, sl] + kr).astype(BF16)
    v_ref[...] = _dot(ckvn, wv_ref[...]).astype(BF16)


def _mla_proj(x2, wd, qg, kvg, wq, wk, wv, tables, tm, s):
    m, d = x2.shape
    nt = s // tm
    full = lambda a: pl.BlockSpec(a.shape, lambda i: (0, 0))
    tab = lambda: pl.BlockSpec((tm, LANES), lambda i: (i % nt, 0))
    hq = MLA_HEADS * LANES
    hv = MLA_HEADS * MLA_V
    return pl.pallas_call(
        _mla_proj_kernel,
        grid=(m // tm,),
        in_specs=[pl.BlockSpec((tm, d), lambda i: (i, 0)), full(wd), full(qg), full(kvg),
                  full(wq), full(wk), full(wv)] + [tab() for _ in tables],
        out_specs=[pl.BlockSpec((tm, hq), lambda i: (i, 0)),
                   pl.BlockSpec((tm, hq), lambda i: (i, 0)),
                   pl.BlockSpec((tm, hv), lambda i: (i, 0))],
        out_shape=[jax.ShapeDtypeStruct((m, hq), BF16),
                   jax.ShapeDtypeStruct((m, hq), BF16),
                   jax.ShapeDtypeStruct((m, hv), BF16)],
        compiler_params=_cparams(("parallel",)),
        name="mla_proj",
    )(x2, wd, qg, kvg, wq, wk, wv, *tables)


def _mla_attn_kernel(q_ref, k_ref, v_ref, o_ref, ot_ref, *, tq, heads_per_step):
    s = q_ref.shape[1]
    keep_t = (lax.broadcasted_iota(jnp.int32, (tq, tq), 0)
              <= lax.broadcasted_iota(jnp.int32, (tq, tq), 1))
    v_t = jnp.transpose(v_ref[0].astype(F32)).astype(BF16)
    ones_rows = jnp.ones((2 * SUBLANES, s), BF16)
    lhs = [jnp.concatenate([v_t[hh * MLA_V:(hh + 1) * MLA_V, :], ones_rows], axis=0)
           for hh in range(heads_per_step)]
    units = [(j * tq, hh) for j in range(s // tq) for hh in range(heads_per_step)]

    def logits(r0, hh):
        ql = slice(hh * LANES, (hh + 1) * LANES)
        qh = q_ref[0, r0:r0 + tq, ql]
        sd = jnp.where(keep_t, _dot_nt(k_ref[0, r0:r0 + tq, ql], qh), MASK_NEG)
        so = _dot_nt(k_ref[0, 0:r0, ql], qh) if r0 > 0 else None
        return sd, so

    def probs(sd, so):
        m = _reduce_keys(sd, jnp.max, jnp.maximum)
        if so is None:
            return jnp.exp2(sd - m).astype(BF16), None
        m = jnp.maximum(m, _reduce_keys(so, jnp.max, jnp.maximum))
        return jnp.exp2(sd - m).astype(BF16), jnp.exp2(so - m).astype(BF16)

    def values(r0, hh, pd, po):
        acc = _dot(lhs[hh][:, r0:r0 + tq], pd)
        if po is not None:
            acc = acc + _dot(lhs[hh][:, 0:r0], po)
        ot_ref[hh * MLA_V:(hh + 1) * MLA_V, r0:r0 + tq] = acc[0:MLA_V, :] / acc[MLA_V:MLA_V + 1, :]

    n = len(units)
    s_cur = logits(*units[0])
    p_prev = None
    for u in range(n + 1):
        s_nxt = logits(*units[u + 1]) if u + 1 < n else None
        p_cur = probs(*s_cur) if u < n else None
        if p_prev is not None:
            values(*units[u - 1], *p_prev)
        s_cur, p_prev = s_nxt, p_cur
    o_ref[0] = jnp.transpose(ot_ref[...]).astype(o_ref.dtype)


def _mla_attn(q, k, v, tq):
    bsz, s, _ = q.shape
    hps = 2
    return pl.pallas_call(
        functools.partial(_mla_attn_kernel, tq=tq, heads_per_step=hps),
        grid=(bsz, MLA_HEADS // hps),
        in_specs=[pl.BlockSpec((1, s, hps * LANES), lambda b, h: (b, 0, h)),
                  pl.BlockSpec((1, s, hps * LANES), lambda b, h: (b, 0, h)),
                  pl.BlockSpec((1, s, hps * MLA_V), lambda b, h: (b, 0, h))],
        out_specs=pl.BlockSpec((1, s, hps * MLA_V), lambda b, h: (b, 0, h)),
        out_shape=jax.ShapeDtypeStruct((bsz, s, MLA_HEADS * MLA_V), BF16),
        scratch_shapes=[pltpu.VMEM((hps * MLA_V, s), F32)],
        compiler_params=_cparams(("parallel", "parallel")),
        name="mla_attn",
    )(q, k, v)


def _hy_in_weight(w_in):
    xr, yr, q, k, v, iq, ik, iw = jnp.split(
        w_in, np.cumsum([512, 512, 512, 64, 64, 256, 64, 4])[:-1].tolist(), axis=1)
    pad = jnp.zeros((w_in.shape[0], 256 - 64 * 3 - 4), w_in.dtype)
    return jnp.concatenate([xr, yr, q, iq, k, v, ik, iw, pad], axis=1).astype(BF16)


def _block_diag_groups(w):
    bw = w.shape[-1]
    per = LANES // bw
    ng = w.shape[0] // per
    out = jnp.zeros((ng, LANES, LANES), w.dtype)
    for g in range(ng):
        for t in range(per):
            out = out.at[g, t * bw:(t + 1) * bw, t * bw:(t + 1) * bw].set(w[g * per + t])
    return out.astype(BF16)


def _head_lanes(nope, rope):
    lead = nope.shape[:-1] if nope is not None else rope.shape[:-1]
    dt = nope.dtype if nope is not None else rope.dtype
    z = lambda n: jnp.zeros(lead + (n,), dt)
    r1, r2 = (rope[..., :ROPE_HALF], rope[..., ROPE_HALF:]) if rope is not None else (z(ROPE_HALF), z(ROPE_HALF))
    n1, n2 = (nope[..., :NOPE_LO], nope[..., NOPE_LO:]) if nope is not None else (z(NOPE_LO), z(MLA_NOPE - NOPE_LO))
    return jnp.concatenate([r1, n1, r2, n2, z(LANES - MLA_NOPE - MLA_ROPE)], axis=-1)


def _mla_weights(w_down, w_uq, w_ukv):
    lat = MLA_Q_LORA + MLA_KV_LORA
    wd = jnp.concatenate([w_down[:, :lat], _head_lanes(None, w_down[:, lat:])], axis=1)
    wq = w_uq.reshape(MLA_Q_LORA, MLA_HEADS, MLA_NOPE + MLA_ROPE)
    wq = _head_lanes(wq[:, :, :MLA_NOPE], wq[:, :, MLA_NOPE:]).reshape(MLA_Q_LORA, MLA_HEADS * LANES)
    wkv = w_ukv.reshape(MLA_KV_LORA, MLA_HEADS, MLA_NOPE + MLA_V)
    wk = _head_lanes(wkv[:, :, :MLA_NOPE], None).reshape(MLA_KV_LORA, MLA_HEADS * LANES)
    wv = wkv[:, :, MLA_NOPE:].reshape(MLA_KV_LORA, MLA_HEADS * MLA_V)
    return wd.astype(BF16), wq.astype(BF16), wk.astype(BF16), wv.astype(BF16)


def _rope_tables(s):
    pos = jnp.arange(s, dtype=F32)
    freq = ROPE_BASE ** (-jnp.arange(0, MLA_ROPE, 2, dtype=F32) / MLA_ROPE)
    ang = pos[:, None] * freq[None, :]
    cos, sin = jnp.cos(ang), jnp.sin(ang)
    cos_t = _head_lanes(jnp.ones((s, MLA_NOPE), F32), jnp.concatenate([cos, cos], axis=1))
    cos_t = cos_t.at[:, MLA_NOPE + MLA_ROPE:].set(1.0)
    sin_t = _head_lanes(None, jnp.concatenate([-sin, sin], axis=1))
    return cos_t * MLA_Q_SCALE, sin_t * MLA_Q_SCALE, cos_t, sin_t


def _tile_m(m):
    return 512 if m % 512 == 0 else m


def kernel(x, p, ln1_g, ln1_b, ln2_g, ln2_b, mlp_w1, mlp_w2, ple_w_proj, ple_w_gate, hy_w_in, hy_conv_w, hy_conv_b, hy_ga_w, hy_ga_b, hy_gx_w, hy_gx_b, hy_lambda, hy_w_out, mla_w_down, mla_q_norm, mla_kv_norm, mla_w_uq, mla_w_ukv, mla_w_out):
    bsz, s, d = x.shape
    m = bsz * s
    tm = _tile_m(m)
    tf = 1024
    row = lambda a: a.reshape(1, -1)
    x2 = x.reshape(m, d)

    def layer_tail(acts, wos, x2, i):
        return _tail(acts, wos, x2, p[i].reshape(m, -1), row(ln1_g[i]), row(ln1_b[i]),
                     mlp_w1[i].astype(BF16), mlp_w2[i].astype(BF16),
                     ple_w_gate[i].astype(BF16), ple_w_proj[i].astype(BF16),
                     row(ln2_g[i]), row(ln2_b[i]), tm, tf)

    xy, q, iq, misc = _inproj(x2, _hy_in_weight(hy_w_in[0]), tm)
    rec = _rglru(xy.reshape(bsz, s, -1), hy_conv_w[0], row(hy_conv_b[0]),
                 _block_diag_groups(hy_ga_w[0]), row(hy_ga_b[0]),
                 _block_diag_groups(hy_gx_w[0]), row(hy_gx_b[0]), row(hy_lambda[0]))
    att = _dsa(q.reshape(bsz, s, -1), iq.reshape(bsz, s, -1), misc.reshape(bsz, s, -1))
    w_out = hy_w_out[0].astype(BF16)
    x2 = layer_tail([rec.reshape(m, -1), att.reshape(m, -1)], [w_out[:RNN_WIDTH], w_out[RNN_WIDTH:]], x2, 0)

    wd, wq, wk, wv = _mla_weights(mla_w_down[0], mla_w_uq[0], mla_w_ukv[0])
    qp, kp, vp = _mla_proj(x2, wd, row(mla_q_norm[0]), row(mla_kv_norm[0]), wq, wk, wv,
                           _rope_tables(s), min(tm, s), s)
    o = _mla_attn(qp.reshape(bsz, s, -1), kp.reshape(bsz, s, -1), vp.reshape(bsz, s, -1), min(256, s))
    x2 = layer_tail([o.reshape(m, -1)], [mla_w_out[0].astype(BF16)], x2, 1)
    return x2.reshape(bsz, s, d)
```

```python
import functools
import math

import numpy as np
import jax
import jax.numpy as jnp
from jax import lax
from jax.experimental import pallas as pl
from jax.experimental.pallas import tpu as pltpu

F32 = jnp.float32
BF16 = jnp.bfloat16

RNN_WIDTH = 512
RNN_BLOCKS = 8
CONV_WIDTH = 4
LRU_C = 8.0
DSA_HEADS = 8
DSA_HEAD_DIM = 64
IDX_HEADS = 4
IDX_DIM = 64
DSA_TOPK_MAX = 256
MLA_HEADS = 16
MLA_Q_LORA = 512
MLA_KV_LORA = 256
MLA_NOPE = 64
MLA_ROPE = 32
MLA_V = 64
ROPE_BASE = 10000.0
DEPTH = 2
DN_ALPHA = (2 * DEPTH) ** 0.25
LN_EPS = 1e-5
RMS_EPS = 1e-6

LANES = 128
SUBLANES = 8
VMEM_LIMIT_BYTES = 56 * 1024 * 1024

MASK_NEG = -3.0e38
INT32_MIN = -(2 ** 31)
DSA_Q_SCALE = DSA_HEAD_DIM ** -0.5


def _cparams(sem):
    return pltpu.CompilerParams(dimension_semantics=sem, vmem_limit_bytes=VMEM_LIMIT_BYTES)


def _dot(a, b):
    return jnp.dot(a, b, preferred_element_type=F32)


def _dot_nt(a, b):
    return lax.dot_general(a, b, (((1,), (1,)), ((), ())), preferred_element_type=F32)


def _layer_norm(y, g, b):
    mu = jnp.mean(y, axis=-1, keepdims=True)
    yc = y - mu
    var = jnp.mean(yc * yc, axis=-1, keepdims=True)
    return yc * lax.rsqrt(var + LN_EPS) * g + b


IN_XY = 2 * RNN_WIDTH
IN_Q = DSA_HEADS * DSA_HEAD_DIM
IN_IQ = IDX_HEADS * IDX_DIM
IN_MISC = 2 * LANES
IN_OFFSETS = tuple(int(v) for v in np.cumsum([0, IN_XY, IN_Q, IN_IQ, IN_MISC]))


def _inproj_kernel(x_ref, w_ref, xy_ref, q_ref, iq_ref, misc_ref):
    xb = x_ref[...].astype(BF16)
    cols = [w_ref[:, a:b] for a, b in zip(IN_OFFSETS[:-1], IN_OFFSETS[1:])]
    xy_ref[...] = _dot(xb, cols[0])
    q_ref[...] = (_dot(xb, cols[1]) * DSA_Q_SCALE).astype(BF16)
    iq_ref[...] = _dot(xb, cols[2])
    misc_ref[...] = _dot(xb, cols[3])


def _inproj(x2, w, tm):
    m, d = x2.shape
    n = w.shape[1]
    widths = (IN_XY, IN_Q, IN_IQ, IN_MISC)
    dtypes = (F32, BF16, F32, F32)
    return pl.pallas_call(
        _inproj_kernel,
        grid=(m // tm,),
        in_specs=[pl.BlockSpec((tm, d), lambda i: (i, 0)),
                  pl.BlockSpec((d, n), lambda i: (0, 0))],
        out_specs=[pl.BlockSpec((tm, wd), lambda i: (i, 0)) for wd in widths],
        out_shape=[jax.ShapeDtypeStruct((m, wd), dt) for wd, dt in zip(widths, dtypes)],
        compiler_params=_cparams(("parallel",)),
        name="hy_inproj",
    )(x2, w)


def _rglru_kernel(xr_ref, yr_ref, cw_ref, cb_ref, wa_ref, ba_ref, wx_ref, bx_ref, lam_ref,
                  o_ref, xpad_ref):
    s = xr_ref.shape[1]
    c = xr_ref.shape[2]
    xpad_ref[0:SUBLANES, :] = jnp.zeros((SUBLANES, c), F32)
    xpad_ref[SUBLANES:SUBLANES + s, :] = xr_ref[0]
    xc = None
    for k in range(CONV_WIDTH):
        start = SUBLANES - (CONV_WIDTH - 1) + k
        term = xpad_ref[start:start + s, :] * cw_ref[k:k + 1, :]
        xc = term if xc is None else xc + term
    xc = xc + cb_ref[...]
    xb = xc.astype(BF16)
    r = jax.nn.sigmoid(_dot(xb, wa_ref[0]) + ba_ref[...])
    gi = jax.nn.sigmoid(_dot(xb, wx_ref[0]) + bx_ref[...])
    nl = -lam_ref[...]
    softplus = jnp.maximum(nl, 0.0) + jnp.log1p(jnp.exp(-jnp.abs(nl)))
    log_a = (-LRU_C) * r * softplus
    a = jnp.exp(log_a)
    y = -jnp.tanh(log_a) * (a * a + 1.0)
    h = jnp.where(y > 0.0, y * lax.rsqrt(y), 0.0) * (gi * xc)
    row = lax.broadcasted_iota(jnp.int32, (s, c), 0)
    sh = 1
    while sh < s:
        if sh < SUBLANES:
            valid = row >= sh
            h = a * jnp.where(valid, pltpu.roll(h, sh, axis=0), 0.0) + h
            a = a * jnp.where(valid, pltpu.roll(a, sh, axis=0), 1.0)
        else:
            h = jnp.concatenate([h[:sh], a[sh:] * h[:s - sh] + h[sh:]], axis=0)
            if sh * 2 < s:
                a = jnp.concatenate([a[:sh], a[sh:] * a[:s - sh]], axis=0)
        sh *= 2
    o_ref[0] = (h * jax.nn.gelu(yr_ref[0])).astype(o_ref.dtype)


def _rglru(xy, conv_w, conv_b, wa, ba, wx, bx, lam):
    bsz, s, _ = xy.shape
    c = LANES
    ng = RNN_WIDTH // c
    vec = lambda: pl.BlockSpec((1, c), lambda b, g: (0, g))
    return pl.pallas_call(
        _rglru_kernel,
        grid=(bsz, ng),
        in_specs=[pl.BlockSpec((1, s, c), lambda b, g: (b, 0, g)),
                  pl.BlockSpec((1, s, c), lambda b, g: (b, 0, ng + g)),
                  pl.BlockSpec((CONV_WIDTH, c), lambda b, g: (0, g)),
                  vec(),
                  pl.BlockSpec((1, c, c), lambda b, g: (g, 0, 0)),
                  vec(),
                  pl.BlockSpec((1, c, c), lambda b, g: (g, 0, 0)),
                  vec(),
                  vec()],
        out_specs=pl.BlockSpec((1, s, c), lambda b, g: (b, 0, g)),
        out_shape=jax.ShapeDtypeStruct((bsz, s, RNN_WIDTH), BF16),
        scratch_shapes=[pltpu.VMEM((s + SUBLANES, c), F32)],
        compiler_params=_cparams(("parallel", "parallel")),
        name="hy_rglru",
    )(xy, xy, conv_w, conv_b, wa, ba, wx, bx, lam)


def _ordered_int_to_float(c):
    return lax.bitcast_convert_type(c ^ (lax.shift_right_arithmetic(c, 31) & jnp.int32(0x7FFFFFFF)), F32)


def _reduce_keys(x, reduce_fn, combine_fn, groups=8):
    rows = x.shape[0]
    step = max(rows // groups, SUBLANES)
    parts = [reduce_fn(x[r:r + step, :], axis=0, keepdims=True) for r in range(0, rows, step)]
    while len(parts) > 1:
        parts = [combine_fn(parts[i], parts[i + 1]) if i + 1 < len(parts) else parts[i]
                 for i in range(0, len(parts), 2)]
    return parts[0]


def _dsa_prepare(q_ref, mk_ref):
    tq = q_ref.shape[1]
    ext = mk_ref.shape[1]
    d0 = ext - tq
    keep = (lax.broadcasted_iota(jnp.int32, (tq, tq), 1)
            <= lax.broadcasted_iota(jnp.int32, (tq, tq), 0))

    mk = mk_ref[0]
    vlane = lax.broadcasted_iota(jnp.int32, (ext, DSA_HEAD_DIM), 1)
    v_aug = jnp.concatenate([mk[:, DSA_HEAD_DIM:2 * DSA_HEAD_DIM],
                             jnp.where(vlane == 0, 1.0, 0.0)], axis=1).astype(BF16)
    ik_b = mk[:, 2 * DSA_HEAD_DIM:2 * DSA_HEAD_DIM + IDX_DIM].astype(BF16)
    kcol = lax.broadcasted_iota(jnp.int32, (ext, 1), 0)
    lane = lax.broadcasted_iota(jnp.int32, (1, LANES), 1)
    pos_hi = (kcol & jnp.int32(-LANES)).astype(F32)
    pos_lo = (kcol & jnp.int32(LANES - 1)).astype(F32)
    k_aug = jnp.where(lane < DSA_HEAD_DIM, mk[:, 0:LANES],
                      jnp.where(lane == DSA_HEAD_DIM, pos_hi,
                                jnp.where(lane == DSA_HEAD_DIM + 1, pos_lo, 0.0))).astype(BF16)
    return d0, keep, ik_b, k_aug, v_aug


def _dsa_select(iq_ref, mq_ref, sc_ref, bias_ref, ik_b, keep, *, topk):
    tq, ext = sc_ref.shape
    d0 = ext - tq
    kf = float(topk)
    iw0 = 2 * DSA_HEAD_DIM + IDX_DIM
    iw = mq_ref[0][:, iw0:iw0 + IDX_HEADS] * (IDX_HEADS ** -0.5 * IDX_DIM ** -0.5)
    iq = iq_ref[0].astype(BF16)
    sc = None
    for h in range(IDX_HEADS):
        d = _dot_nt(iq[:, h * IDX_DIM:(h + 1) * IDX_DIM], ik_b)
        term = jnp.maximum(d, 0.0) * iw[:, h:h + 1]
        sc = term if sc is None else sc + term
    if d0 > 0:
        sc_ref[:, 0:d0] = sc[:, 0:d0]
    sc_ref[:, d0:ext] = jnp.where(keep, sc[:, d0:ext], MASK_NEG)

    def search(i, tau):
        cand = tau + lax.shift_left(jnp.int32(1), 31 - i)
        ge = sc_ref[...] >= _ordered_int_to_float(cand)
        cnt = jnp.sum(jnp.where(ge, 1.0, 0.0), axis=-1, keepdims=True)
        return jnp.where(cnt >= kf, cand, tau)

    tau = lax.fori_loop(0, 32, search, jnp.full((tq, 1), INT32_MIN, jnp.int32), unroll=True)
    tau = _ordered_int_to_float(tau)

    sc = sc_ref[...]
    gt = sc > tau
    eq = sc == tau
    need = kf - jnp.sum(jnp.where(gt, 1.0, 0.0), axis=-1, keepdims=True)
    eqf = jnp.where(eq, 1.0, 0.0).astype(BF16)
    tri = (lax.broadcasted_iota(jnp.int32, (LANES, LANES), 0)
           <= lax.broadcasted_iota(jnp.int32, (LANES, LANES), 1)).astype(BF16)
    off = jnp.zeros((tq, 1), F32)
    for c in range(ext // LANES):
        sl = slice(c * LANES, (c + 1) * LANES)
        pc = _dot(eqf[:, sl], tri)
        take = gt[:, sl] | (eq[:, sl] & ((pc + off) <= need))
        if c * LANES >= d0:
            take = take & keep[:, c * LANES - d0:(c + 1) * LANES - d0]
        bias_ref[:, sl] = jnp.where(take, 0.0, MASK_NEG)
        off = off + pc[:, LANES - 1:LANES]


def _dsa_attend(q_ref, o_ref, bias_ref, k_aug, v_aug):
    tq = q_ref.shape[1]
    q = q_ref[0]
    tail_lane = lax.broadcasted_iota(jnp.int32, (tq, LANES - DSA_HEAD_DIM), 1)
    for h in range(DSA_HEADS):
        slope = 2.0 ** (-8.0 * (h + 1) / DSA_HEADS)
        tail = jnp.where(tail_lane < 2, slope, 0.0).astype(BF16)
        q_aug = jnp.concatenate([q[:, h * DSA_HEAD_DIM:(h + 1) * DSA_HEAD_DIM], tail], axis=1)
        lg = _dot_nt(q_aug, k_aug) + bias_ref[...]
        p = jnp.exp(lg - jnp.max(lg, axis=-1, keepdims=True))
        o = _dot(p.astype(BF16), v_aug)
        o = o[:, 0:DSA_HEAD_DIM] / o[:, DSA_HEAD_DIM:DSA_HEAD_DIM + 1]
        o_ref[0, :, h * DSA_HEAD_DIM:(h + 1) * DSA_HEAD_DIM] = o.astype(o_ref.dtype)


def _dsa_kernel(q_ref, iq_ref, mq_ref, mk_ref, o_ref, sc_ref, bias_ref, *, topk):
    d0, keep, ik_b, k_aug, v_aug = _dsa_prepare(q_ref, mk_ref)
    if mk_ref.shape[1] > topk:
        _dsa_select(iq_ref, mq_ref, sc_ref, bias_ref, ik_b, keep, topk=topk)
    else:
        if d0 > 0:
            bias_ref[:, 0:d0] = jnp.zeros((keep.shape[0], d0), F32)
        bias_ref[:, d0:] = jnp.where(keep, 0.0, MASK_NEG)
    _dsa_attend(q_ref, o_ref, bias_ref, k_aug, v_aug)


def _dsa(q, iq, misc):
    bsz, s, _ = q.shape
    topk = min(DSA_TOPK_MAX, s // 4)
    tq = min(256, s)
    hd = DSA_HEADS * DSA_HEAD_DIM
    outs = []
    for c in range(s // tq):
        ext = (c + 1) * tq
        outs.append(pl.pallas_call(
            functools.partial(_dsa_kernel, topk=topk),
            grid=(bsz,),
            in_specs=[pl.BlockSpec((1, tq, hd), lambda b, c=c: (b, c, 0)),
                      pl.BlockSpec((1, tq, IDX_HEADS * IDX_DIM), lambda b, c=c: (b, c, 0)),
                      pl.BlockSpec((1, tq, 256), lambda b, c=c: (b, c, 0)),
                      pl.BlockSpec((1, ext, 256), lambda b: (b, 0, 0))],
            out_specs=pl.BlockSpec((1, tq, hd), lambda b: (b, 0, 0)),
            out_shape=jax.ShapeDtypeStruct((bsz, tq, hd), BF16),
            scratch_shapes=[pltpu.VMEM((tq, ext), F32), pltpu.VMEM((tq, ext), F32)],
            compiler_params=_cparams(("parallel",)),
            name=f"hy_dsa_c{c}",
        )(q, iq, misc, misc))
    return outs[0] if len(outs) == 1 else jnp.concatenate(outs, axis=1)


def _tail_kernel(*refs, n_in, tf):
    a_refs = refs[:n_in]
    wo_refs = refs[n_in:2 * n_in]
    (x_ref, p_ref, g1_ref, b1_ref, w1_ref, w2_ref, wg_ref, wp_ref, g2_ref, b2_ref,
     o_ref, acc_ref) = refs[2 * n_in:]
    m = None
    for a_ref, w_ref in zip(a_refs, wo_refs):
        t = _dot(a_ref[...].astype(BF16), w_ref[...])
        m = t if m is None else m + t
    h = _layer_norm(DN_ALPHA * x_ref[...] + m, g1_ref[...], b1_ref[...])
    hb = h.astype(BF16)
    gate = jax.nn.sigmoid(_dot(hb, wg_ref[...]))
    acc_ref[...] = DN_ALPHA * h + gate * _dot(p_ref[...].astype(BF16), wp_ref[...])
    for c in range(w1_ref.shape[1] // tf):
        a = jnp.maximum(_dot(hb, w1_ref[:, c * tf:(c + 1) * tf]), 0.0)
        acc_ref[...] += _dot((a * a).astype(BF16), w2_ref[c * tf:(c + 1) * tf, :])
    o_ref[...] = _layer_norm(acc_ref[...], g2_ref[...], b2_ref[...])


def _tail(acts, wos, x2, p3, layer, g1, b1, w1, w2, wg, wp, g2, b2, tm, tf):
    m, d = x2.shape
    n_in = len(acts)
    rows = lambda a: pl.BlockSpec((tm, a.shape[1]), lambda i: (i, 0))
    resident = lambda a: pl.BlockSpec(a.shape, lambda i: (0, 0), pipeline_mode=pl.Buffered(1))
    of_layer = lambda a: pl.BlockSpec((None,) + a.shape[1:], lambda i: (layer, 0, 0),
                                      pipeline_mode=pl.Buffered(1))
    params = [g1, b1, w1, w2, wg, wp, g2, b2]
    return pl.pallas_call(
        functools.partial(_tail_kernel, n_in=n_in, tf=tf),
        grid=(m // tm,),
        in_specs=([rows(a) for a in acts] + [resident(w) for w in wos]
                  + [rows(x2), pl.BlockSpec((None, tm, p3.shape[2]), lambda i: (layer, i, 0))]
                  + [of_layer(a) for a in params]),
        out_specs=pl.BlockSpec((tm, d), lambda i: (i, 0)),
        out_shape=jax.ShapeDtypeStruct((m, d), F32),
        scratch_shapes=[pltpu.VMEM((tm, d), F32)],
        compiler_params=_cparams(("parallel",)),
        name="layer_tail",
    )(*acts, *wos, x2, p3, *params)


MLA_Q_SCALE = (MLA_NOPE + MLA_ROPE) ** -0.5 * math.log2(math.e)
ROPE_HALF = MLA_ROPE // 2
NOPE_LO = LANES // 2 - ROPE_HALF


def _rope_group(t, cos_t, sin_t):
    return t * cos_t + pltpu.roll(t, LANES // 2, axis=1) * sin_t


def _mla_proj_kernel(x_ref, wd_ref, qg_ref, kvg_ref, wq_ref, wk_ref, wv_ref,
                     cosq_ref, sinq_ref, cosk_ref, sink_ref, q_ref, k_ref, v_ref):
    xb = x_ref[...].astype(BF16)
    down = _dot(xb, wd_ref[...])
    cq = down[:, 0:MLA_Q_LORA]
    ckv = down[:, MLA_Q_LORA:MLA_Q_LORA + MLA_KV_LORA]
    krg = down[:, MLA_Q_LORA + MLA_KV_LORA:]
    cqn = (cq * lax.rsqrt(jnp.mean(cq * cq, axis=-1, keepdims=True) + RMS_EPS) * qg_ref[...]).astype(BF16)
    ckvn = (ckv * lax.rsqrt(jnp.mean(ckv * ckv, axis=-1, keepdims=True) + RMS_EPS) * kvg_ref[...]).astype(BF16)
    cos_q, sin_q = cosq_ref[...], sinq_ref[...]
    kr = _rope_group(krg, cosk_ref[...], sink_ref[...])
    q = _dot(cqn, wq_ref[...])
    k = _dot(ckvn, wk_ref[...])
    for h in range(MLA_HEADS):
        sl = slice(h * LANES, (h + 1) * LANES)
        q_ref[:, sl] = _rope_group(q[:, sl], cos_q, sin_q).astype(BF16)
        k_ref[:, sl] = (k[:, sl] + kr).astype(BF16)
    v_ref[...] = _dot(ckvn, wv_ref[...]).astype(BF16)


def _mla_proj(x2, wd, qg, kvg, wq, wk, wv, tables, tm, s):
    m, d = x2.shape
    nt = s // tm
    full = lambda a: pl.BlockSpec(a.shape, lambda i: (0, 0))
    tab = lambda: pl.BlockSpec((tm, LANES), lambda i: (i % nt, 0))
    hq = MLA_HEADS * LANES
    hv = MLA_HEADS * MLA_V
    return pl.pallas_call(
        _mla_proj_kernel,
        grid=(m // tm,),
        in_specs=[pl.BlockSpec((tm, d), lambda i: (i, 0)), full(wd), full(qg), full(kvg),
                  full(wq), full(wk), full(wv)] + [tab() for _ in tables],
        out_specs=[pl.BlockSpec((tm, hq), lambda i: (i, 0)),
                   pl.BlockSpec((tm, hq), lambda i: (i, 0)),
                   pl.BlockSpec((tm, hv), lambda i: (i, 0))],
        out_shape=[jax.ShapeDtypeStruct((m, hq), BF16),
                   jax.ShapeDtypeStruct((m, hq), BF16),
                   jax.ShapeDtypeStruct((m, hv), BF16)],
        compiler_params=_cparams(("parallel",)),
        name="mla_proj",
    )(x2, wd, qg, kvg, wq, wk, wv, *tables)


def _mla_attn_kernel(q_ref, k_ref, v_ref, o_ref, ot_ref, *, tq, heads_per_step):
    s = q_ref.shape[1]
    keep_t = (lax.broadcasted_iota(jnp.int32, (tq, tq), 0)
              <= lax.broadcasted_iota(jnp.int32, (tq, tq), 1))
    v_t = jnp.transpose(v_ref[0].astype(F32)).astype(BF16)
    ones_rows = jnp.ones((2 * SUBLANES, s), BF16)
    lhs = [jnp.concatenate([v_t[hh * MLA_V:(hh + 1) * MLA_V, :], ones_rows], axis=0)
           for hh in range(heads_per_step)]
    units = [(j * tq, hh) for j in range(s // tq) for hh in range(heads_per_step)]

    def logits(r0, hh):
        ql = slice(hh * LANES, (hh + 1) * LANES)
        qh = q_ref[0, r0:r0 + tq, ql]
        sd = jnp.where(keep_t, _dot_nt(k_ref[0, r0:r0 + tq, ql], qh), MASK_NEG)
        so = _dot_nt(k_ref[0, 0:r0, ql], qh) if r0 > 0 else None
        return sd, so

    def probs(sd, so):
        m = _reduce_keys(sd, jnp.max, jnp.maximum)
        if so is None:
            return jnp.exp2(sd - m).astype(BF16), None
        m = jnp.maximum(m, _reduce_keys(so, jnp.max, jnp.maximum))
        return jnp.exp2(sd - m).astype(BF16), jnp.exp2(so - m).astype(BF16)

    def values(r0, hh, pd, po):
        acc = _dot(lhs[hh][:, r0:r0 + tq], pd)
        if po is not None:
            acc = acc + _dot(lhs[hh][:, 0:r0], po)
        ot_ref[hh * MLA_V:(hh + 1) * MLA_V, r0:r0 + tq] = acc[0:MLA_V, :] / acc[MLA_V:MLA_V + 1, :]

    n = len(units)
    s_cur = logits(*units[0])
    p_prev = None
    for u in range(n + 1):
        s_nxt = logits(*units[u + 1]) if u + 1 < n else None
        p_cur = probs(*s_cur) if u < n else None
        if p_prev is not None:
            values(*units[u - 1], *p_prev)
        s_cur, p_prev = s_nxt, p_cur
    o_ref[0] = jnp.transpose(ot_ref[...]).astype(o_ref.dtype)


def _mla_attn(q, k, v, tq):
    bsz, s, _ = q.shape
    hps = 2
    return pl.pallas_call(
        functools.partial(_mla_attn_kernel, tq=tq, heads_per_step=hps),
        grid=(bsz, MLA_HEADS // hps),
        in_specs=[pl.BlockSpec((1, s, hps * LANES), lambda b, h: (b, 0, h)),
                  pl.BlockSpec((1, s, hps * LANES), lambda b, h: (b, 0, h)),
                  pl.BlockSpec((1, s, hps * MLA_V), lambda b, h: (b, 0, h))],
        out_specs=pl.BlockSpec((1, s, hps * MLA_V), lambda b, h: (b, 0, h)),
        out_shape=jax.ShapeDtypeStruct((bsz, s, MLA_HEADS * MLA_V), BF16),
        scratch_shapes=[pltpu.VMEM((hps * MLA_V, s), F32)],
        compiler_params=_cparams(("parallel", "parallel")),
        name="mla_attn",
    )(q, k, v)


def _hy_in_weight(w_in):
    xr, yr, q, k, v, iq, ik, iw = jnp.split(
        w_in, np.cumsum([512, 512, 512, 64, 64, 256, 64, 4])[:-1].tolist(), axis=1)
    pad = jnp.zeros((w_in.shape[0], 256 - 64 * 3 - 4), w_in.dtype)
    return jnp.concatenate([xr, yr, q, iq, k, v, ik, iw, pad], axis=1).astype(BF16)


def _block_diag_groups(w):
    bw = w.shape[-1]
    per = LANES // bw
    ng = w.shape[0] // per
    on_diag = jnp.eye(per, dtype=bool)[None, :, None, :, None]
    out = jnp.where(on_diag, w.reshape(ng, per, bw, 1, bw), 0.0)
    return out.reshape(ng, LANES, LANES).astype(BF16)


def _head_lanes(nope, rope):
    lead = nope.shape[:-1] if nope is not None else rope.shape[:-1]
    dt = nope.dtype if nope is not None else rope.dtype
    z = lambda n: jnp.zeros(lead + (n,), dt)
    r1, r2 = (rope[..., :ROPE_HALF], rope[..., ROPE_HALF:]) if rope is not None else (z(ROPE_HALF), z(ROPE_HALF))
    n1, n2 = (nope[..., :NOPE_LO], nope[..., NOPE_LO:]) if nope is not None else (z(NOPE_LO), z(MLA_NOPE - NOPE_LO))
    return jnp.concatenate([r1, n1, r2, n2, z(LANES - MLA_NOPE - MLA_ROPE)], axis=-1)


def _mla_weights(w_down, w_uq, w_ukv):
    lat = MLA_Q_LORA + MLA_KV_LORA
    wd = jnp.concatenate([w_down[:, :lat], _head_lanes(None, w_down[:, lat:])], axis=1)
    wq = w_uq.reshape(MLA_Q_LORA, MLA_HEADS, MLA_NOPE + MLA_ROPE)
    wq = _head_lanes(wq[:, :, :MLA_NOPE], wq[:, :, MLA_NOPE:]).reshape(MLA_Q_LORA, MLA_HEADS * LANES)
    wkv = w_ukv.reshape(MLA_KV_LORA, MLA_HEADS, MLA_NOPE + MLA_V)
    wk = _head_lanes(wkv[:, :, :MLA_NOPE], None).reshape(MLA_KV_LORA, MLA_HEADS * LANES)
    wv = wkv[:, :, MLA_NOPE:].reshape(MLA_KV_LORA, MLA_HEADS * MLA_V)
    return wd.astype(BF16), wq.astype(BF16), wk.astype(BF16), wv.astype(BF16)


def _rope_tables(s):
    pos = np.arange(s, dtype=np.float64)
    freq = ROPE_BASE ** (-np.arange(0, MLA_ROPE, 2, dtype=np.float64) / MLA_ROPE)
    ang = pos[:, None] * freq[None, :]
    lo = slice(0, ROPE_HALF)
    hi = slice(LANES // 2, LANES // 2 + ROPE_HALF)
    cos_t = np.ones((s, LANES))
    sin_t = np.zeros((s, LANES))
    cos_t[:, lo] = cos_t[:, hi] = np.cos(ang)
    sin_t[:, lo] = -np.sin(ang)
    sin_t[:, hi] = np.sin(ang)
    f32 = lambda a: jnp.asarray(a.astype(np.float32))
    return f32(cos_t * MLA_Q_SCALE), f32(sin_t * MLA_Q_SCALE), f32(cos_t), f32(sin_t)


def _tile_m(m):
    return 512 if m % 512 == 0 else m


def kernel(x, p, ln1_g, ln1_b, ln2_g, ln2_b, mlp_w1, mlp_w2, ple_w_proj, ple_w_gate, hy_w_in, hy_conv_w, hy_conv_b, hy_ga_w, hy_ga_b, hy_gx_w, hy_gx_b, hy_lambda, hy_w_out, mla_w_down, mla_q_norm, mla_kv_norm, mla_w_uq, mla_w_ukv, mla_w_out):
    bsz, s, d = x.shape
    m = bsz * s
    tm = _tile_m(m)
    tf = 1024
    row = lambda a: a.reshape(1, -1)
    x2 = x.reshape(m, d)

    rows3 = lambda a: a.reshape(a.shape[0], 1, -1)
    tail_params = (rows3(ln1_g), rows3(ln1_b), mlp_w1.astype(BF16), mlp_w2.astype(BF16),
                   ple_w_gate.astype(BF16), ple_w_proj.astype(BF16), rows3(ln2_g), rows3(ln2_b))
    p3 = p.reshape(p.shape[0], m, -1)

    def layer_tail(acts, wos, x2, i):
        return _tail(acts, wos, x2, p3, i, *tail_params, tm, tf)

    xy, q, iq, misc = _inproj(x2, _hy_in_weight(hy_w_in[0]), tm)
    rec = _rglru(xy.reshape(bsz, s, -1), hy_conv_w[0], row(hy_conv_b[0]),
                 _block_diag_groups(hy_ga_w[0]), row(hy_ga_b[0]),
                 _block_diag_groups(hy_gx_w[0]), row(hy_gx_b[0]), row(hy_lambda[0]))
    att = _dsa(q.reshape(bsz, s, -1), iq.reshape(bsz, s, -1), misc.reshape(bsz, s, -1))
    w_out = hy_w_out[0].astype(BF16)
    x2 = layer_tail([rec.reshape(m, -1), att.reshape(m, -1)], [w_out[:RNN_WIDTH], w_out[RNN_WIDTH:]], x2, 0)

    wd, wq, wk, wv = _mla_weights(mla_w_down[0], mla_w_uq[0], mla_w_ukv[0])
    qp, kp, vp = _mla_proj(x2, wd, row(mla_q_norm[0]), row(mla_kv_norm[0]), wq, wk, wv,
                           _rope_tables(s), min(tm, s), s)
    o = _mla_attn(qp.reshape(bsz, s, -1), kp.reshape(bsz, s, -1), vp.reshape(bsz, s, -1), min(256, s))
    x2 = layer_tail([o.reshape(m, -1)], [mla_w_out[0].astype(BF16)], x2, 1)
    return x2.reshape(bsz, s, d)
```

```python
import functools
import math

import numpy as np
import jax
import jax.numpy as jnp
from jax import lax
from jax.experimental import pallas as pl
from jax.experimental.pallas import tpu as pltpu

F32 = jnp.float32
BF16 = jnp.bfloat16

RNN_WIDTH = 512
RNN_BLOCKS = 8
CONV_WIDTH = 4
LRU_C = 8.0
DSA_HEADS = 8
DSA_HEAD_DIM = 64
IDX_HEADS = 4
IDX_DIM = 64
DSA_TOPK_MAX = 256
MLA_HEADS = 16
MLA_Q_LORA = 512
MLA_KV_LORA = 256
MLA_NOPE = 64
MLA_ROPE = 32
MLA_V = 64
ROPE_BASE = 10000.0
DEPTH = 2
DN_ALPHA = (2 * DEPTH) ** 0.25
LN_EPS = 1e-5
RMS_EPS = 1e-6

LANES = 128
SUBLANES = 8
VMEM_LIMIT_BYTES = 56 * 1024 * 1024

MASK_NEG = -3.0e38
INT32_MIN = -(2 ** 31)
DSA_Q_SCALE = DSA_HEAD_DIM ** -0.5


def _cparams(sem):
    return pltpu.CompilerParams(dimension_semantics=sem, vmem_limit_bytes=VMEM_LIMIT_BYTES)


def _dot(a, b):
    return jnp.dot(a, b, preferred_element_type=F32)


def _dot_nt(a, b):
    return lax.dot_general(a, b, (((1,), (1,)), ((), ())), preferred_element_type=F32)


def _layer_norm(y, g, b):
    mu = jnp.mean(y, axis=-1, keepdims=True)
    yc = y - mu
    var = jnp.mean(yc * yc, axis=-1, keepdims=True)
    return yc * lax.rsqrt(var + LN_EPS) * g + b


IN_XY = 2 * RNN_WIDTH
IN_Q = DSA_HEADS * DSA_HEAD_DIM
IN_IQ = IDX_HEADS * IDX_DIM
IN_MISC = 2 * LANES
IN_OFFSETS = tuple(int(v) for v in np.cumsum([0, IN_XY, IN_Q, IN_IQ, IN_MISC]))


def _inproj_kernel(x_ref, w_ref, xy_ref, q_ref, iq_ref, misc_ref):
    xb = x_ref[...].astype(BF16)
    cols = [w_ref[:, a:b] for a, b in zip(IN_OFFSETS[:-1], IN_OFFSETS[1:])]
    xy_ref[...] = _dot(xb, cols[0])
    q_ref[...] = (_dot(xb, cols[1]) * DSA_Q_SCALE).astype(BF16)
    iq_ref[...] = _dot(xb, cols[2])
    misc_ref[...] = _dot(xb, cols[3])


def _inproj(x2, w, tm):
    m, d = x2.shape
    n = w.shape[1]
    widths = (IN_XY, IN_Q, IN_IQ, IN_MISC)
    dtypes = (F32, BF16, F32, F32)
    return pl.pallas_call(
        _inproj_kernel,
        grid=(m // tm,),
        in_specs=[pl.BlockSpec((tm, d), lambda i: (i, 0)),
                  pl.BlockSpec((d, n), lambda i: (0, 0))],
        out_specs=[pl.BlockSpec((tm, wd), lambda i: (i, 0)) for wd in widths],
        out_shape=[jax.ShapeDtypeStruct((m, wd), dt) for wd, dt in zip(widths, dtypes)],
        compiler_params=_cparams(("parallel",)),
        name="hy_inproj",
    )(x2, w)


def _rglru_kernel(xr_ref, yr_ref, cw_ref, cb_ref, wa_ref, ba_ref, wx_ref, bx_ref, lam_ref,
                  o_ref, xpad_ref):
    s = xr_ref.shape[1]
    c = xr_ref.shape[2]
    xpad_ref[0:SUBLANES, :] = jnp.zeros((SUBLANES, c), F32)
    xpad_ref[SUBLANES:SUBLANES + s, :] = xr_ref[0]
    xc = None
    for k in range(CONV_WIDTH):
        start = SUBLANES - (CONV_WIDTH - 1) + k
        term = xpad_ref[start:start + s, :] * cw_ref[k:k + 1, :]
        xc = term if xc is None else xc + term
    xc = xc + cb_ref[...]
    xb = xc.astype(BF16)
    r = jax.nn.sigmoid(_dot(xb, wa_ref[0]) + ba_ref[...])
    gi = jax.nn.sigmoid(_dot(xb, wx_ref[0]) + bx_ref[...])
    nl = -lam_ref[...]
    softplus = jnp.maximum(nl, 0.0) + jnp.log1p(jnp.exp(-jnp.abs(nl)))
    log_a = (-LRU_C) * r * softplus
    a = jnp.exp(log_a)
    y = -jnp.tanh(log_a) * (a * a + 1.0)
    h = jnp.where(y > 0.0, y * lax.rsqrt(y), 0.0) * (gi * xc)
    row = lax.broadcasted_iota(jnp.int32, (s, c), 0)
    sh = 1
    while sh < s:
        if sh < SUBLANES:
            valid = row >= sh
            h = a * jnp.where(valid, pltpu.roll(h, sh, axis=0), 0.0) + h
            a = a * jnp.where(valid, pltpu.roll(a, sh, axis=0), 1.0)
        else:
            h = jnp.concatenate([h[:sh], a[sh:] * h[:s - sh] + h[sh:]], axis=0)
            if sh * 2 < s:
                a = jnp.concatenate([a[:sh], a[sh:] * a[:s - sh]], axis=0)
        sh *= 2
    o_ref[0] = (h * jax.nn.gelu(yr_ref[0])).astype(o_ref.dtype)


def _rglru(xy, conv_w, conv_b, wa, ba, wx, bx, lam):
    bsz, s, _ = xy.shape
    c = LANES
    ng = RNN_WIDTH // c
    vec = lambda: pl.BlockSpec((1, c), lambda b, g: (0, g))
    return pl.pallas_call(
        _rglru_kernel,
        grid=(bsz, ng),
        in_specs=[pl.BlockSpec((1, s, c), lambda b, g: (b, 0, g)),
                  pl.BlockSpec((1, s, c), lambda b, g: (b, 0, ng + g)),
                  pl.BlockSpec((CONV_WIDTH, c), lambda b, g: (0, g)),
                  vec(),
                  pl.BlockSpec((1, c, c), lambda b, g: (g, 0, 0)),
                  vec(),
                  pl.BlockSpec((1, c, c), lambda b, g: (g, 0, 0)),
                  vec(),
                  vec()],
        out_specs=pl.BlockSpec((1, s, c), lambda b, g: (b, 0, g)),
        out_shape=jax.ShapeDtypeStruct((bsz, s, RNN_WIDTH), BF16),
        scratch_shapes=[pltpu.VMEM((s + SUBLANES, c), F32)],
        compiler_params=_cparams(("parallel", "parallel")),
        name="hy_rglru",
    )(xy, xy, conv_w, conv_b, wa, ba, wx, bx, lam)


def _ordered_int_to_float(c):
    return lax.bitcast_convert_type(c ^ (lax.shift_right_arithmetic(c, 31) & jnp.int32(0x7FFFFFFF)), F32)


def _reduce_keys(x, reduce_fn, combine_fn, groups=8):
    rows = x.shape[0]
    step = max(rows // groups, SUBLANES)
    parts = [reduce_fn(x[r:r + step, :], axis=0, keepdims=True) for r in range(0, rows, step)]
    while len(parts) > 1:
        parts = [combine_fn(parts[i], parts[i + 1]) if i + 1 < len(parts) else parts[i]
                 for i in range(0, len(parts), 2)]
    return parts[0]


def _dsa_prepare(q_ref, mk_ref):
    tq = q_ref.shape[1]
    ext = mk_ref.shape[1]
    d0 = ext - tq
    keep = (lax.broadcasted_iota(jnp.int32, (tq, tq), 1)
            <= lax.broadcasted_iota(jnp.int32, (tq, tq), 0))

    mk = mk_ref[0]
    vlane = lax.broadcasted_iota(jnp.int32, (ext, DSA_HEAD_DIM), 1)
    v_aug = jnp.concatenate([mk[:, DSA_HEAD_DIM:2 * DSA_HEAD_DIM],
                             jnp.where(vlane == 0, 1.0, 0.0)], axis=1).astype(BF16)
    ik_b = mk[:, 2 * DSA_HEAD_DIM:2 * DSA_HEAD_DIM + IDX_DIM].astype(BF16)
    kcol = lax.broadcasted_iota(jnp.int32, (ext, 1), 0)
    lane = lax.broadcasted_iota(jnp.int32, (1, LANES), 1)
    pos_hi = (kcol & jnp.int32(-LANES)).astype(F32)
    pos_lo = (kcol & jnp.int32(LANES - 1)).astype(F32)
    k_aug = jnp.where(lane < DSA_HEAD_DIM, mk[:, 0:LANES],
                      jnp.where(lane == DSA_HEAD_DIM, pos_hi,
                                jnp.where(lane == DSA_HEAD_DIM + 1, pos_lo, 0.0))).astype(BF16)
    return d0, keep, ik_b, k_aug, v_aug


def _dsa_select(iq_ref, mq_ref, sc_ref, hb_ref, bias_ref, ik_b, keep, *, topk):
    tq, ext = sc_ref.shape
    d0 = ext - tq
    kf = float(topk)
    iw0 = 2 * DSA_HEAD_DIM + IDX_DIM
    iw = mq_ref[0][:, iw0:iw0 + IDX_HEADS] * (IDX_HEADS ** -0.5 * IDX_DIM ** -0.5)
    iq = iq_ref[0].astype(BF16)
    sc = None
    for h in range(IDX_HEADS):
        d = _dot_nt(iq[:, h * IDX_DIM:(h + 1) * IDX_DIM], ik_b)
        term = jnp.maximum(d, 0.0) * iw[:, h:h + 1]
        sc = term if sc is None else sc + term
    if d0 > 0:
        sc_ref[:, 0:d0] = sc[:, 0:d0]
    sc_ref[:, d0:ext] = jnp.where(keep, sc[:, d0:ext], MASK_NEG)

    hb_ref[...] = sc_ref[...].astype(BF16)
    one_b, zero_b = jnp.ones((), BF16), jnp.zeros((), BF16)

    def grid_point(hi16):
        o = lax.shift_left(hi16, 16) | (lax.shift_right_arithmetic(hi16, 31) & jnp.int32(0xFFFF))
        return o

    def coarse(i, hi16):
        cand = hi16 + lax.shift_left(jnp.int32(1), 15 - i)
        cand_b = _ordered_int_to_float(grid_point(cand)).astype(BF16)
        acc = None
        for c in range(ext // LANES):
            hit = jnp.where(hb_ref[:, c * LANES:(c + 1) * LANES] >= cand_b, one_b, zero_b)
            acc = hit if acc is None else acc + hit
        cnt = jnp.sum(acc.astype(F32), axis=-1, keepdims=True)
        return jnp.where(cnt >= kf, cand, hi16)

    hi16 = lax.fori_loop(0, 16, coarse, jnp.full((tq, 1), -(2 ** 15), jnp.int32), unroll=True)
    base = grid_point(hi16) - jnp.int32(0x8000)

    def fine(i, off):
        cand = off + lax.shift_left(jnp.int32(1), 16 - i)
        ge = sc_ref[...] >= _ordered_int_to_float(base + cand)
        cnt = jnp.sum(jnp.where(ge, 1.0, 0.0), axis=-1, keepdims=True)
        return jnp.where(cnt >= kf, cand, off)

    off = lax.fori_loop(0, 17, fine, jnp.zeros((tq, 1), jnp.int32), unroll=True)
    tau = _ordered_int_to_float(base + off)

    sc = sc_ref[...]
    gt = sc > tau
    eq = sc == tau
    need = kf - jnp.sum(jnp.where(gt, 1.0, 0.0), axis=-1, keepdims=True)
    eqf = jnp.where(eq, 1.0, 0.0).astype(BF16)
    tri = (lax.broadcasted_iota(jnp.int32, (LANES, LANES), 0)
           <= lax.broadcasted_iota(jnp.int32, (LANES, LANES), 1)).astype(BF16)
    off = jnp.zeros((tq, 1), F32)
    for c in range(ext // LANES):
        sl = slice(c * LANES, (c + 1) * LANES)
        pc = _dot(eqf[:, sl], tri)
        take = gt[:, sl] | (eq[:, sl] & ((pc + off) <= need))
        if c * LANES >= d0:
            take = take & keep[:, c * LANES - d0:(c + 1) * LANES - d0]
        bias_ref[:, sl] = jnp.where(take, 0.0, MASK_NEG)
        off = off + pc[:, LANES - 1:LANES]


def _dsa_attend(q_ref, o_ref, bias_ref, k_aug, v_aug):
    tq = q_ref.shape[1]
    q = q_ref[0]
    tail_lane = lax.broadcasted_iota(jnp.int32, (tq, LANES - DSA_HEAD_DIM), 1)
    for h in range(DSA_HEADS):
        slope = 2.0 ** (-8.0 * (h + 1) / DSA_HEADS)
        tail = jnp.where(tail_lane < 2, slope, 0.0).astype(BF16)
        q_aug = jnp.concatenate([q[:, h * DSA_HEAD_DIM:(h + 1) * DSA_HEAD_DIM], tail], axis=1)
        lg = _dot_nt(q_aug, k_aug) + bias_ref[...]
        p = jnp.exp(lg - jnp.max(lg, axis=-1, keepdims=True))
        o = _dot(p.astype(BF16), v_aug)
        o = o[:, 0:DSA_HEAD_DIM] / o[:, DSA_HEAD_DIM:DSA_HEAD_DIM + 1]
        o_ref[0, :, h * DSA_HEAD_DIM:(h + 1) * DSA_HEAD_DIM] = o.astype(o_ref.dtype)


def _dsa_kernel(q_ref, iq_ref, mq_ref, mk_ref, o_ref, sc_ref, hb_ref, bias_ref, *, topk):
    d0, keep, ik_b, k_aug, v_aug = _dsa_prepare(q_ref, mk_ref)
    if mk_ref.shape[1] > topk:
        _dsa_select(iq_ref, mq_ref, sc_ref, hb_ref, bias_ref, ik_b, keep, topk=topk)
    else:
        if d0 > 0:
            bias_ref[:, 0:d0] = jnp.zeros((keep.shape[0], d0), F32)
        bias_ref[:, d0:] = jnp.where(keep, 0.0, MASK_NEG)
    _dsa_attend(q_ref, o_ref, bias_ref, k_aug, v_aug)


def _dsa(q, iq, misc):
    bsz, s, _ = q.shape
    topk = min(DSA_TOPK_MAX, s // 4)
    tq = min(256, s)
    hd = DSA_HEADS * DSA_HEAD_DIM
    outs = []
    for c in range(s // tq):
        ext = (c + 1) * tq
        outs.append(pl.pallas_call(
            functools.partial(_dsa_kernel, topk=topk),
            grid=(bsz,),
            in_specs=[pl.BlockSpec((1, tq, hd), lambda b, c=c: (b, c, 0)),
                      pl.BlockSpec((1, tq, IDX_HEADS * IDX_DIM), lambda b, c=c: (b, c, 0)),
                      pl.BlockSpec((1, tq, 256), lambda b, c=c: (b, c, 0)),
                      pl.BlockSpec((1, ext, 256), lambda b: (b, 0, 0))],
            out_specs=pl.BlockSpec((1, tq, hd), lambda b: (b, 0, 0)),
            out_shape=jax.ShapeDtypeStruct((bsz, tq, hd), BF16),
            scratch_shapes=[pltpu.VMEM((tq, ext), F32), pltpu.VMEM((tq, ext), BF16),
                            pltpu.VMEM((tq, ext), F32)],
            compiler_params=_cparams(("parallel",)),
            name=f"hy_dsa_c{c}",
        )(q, iq, misc, misc))
    return outs[0] if len(outs) == 1 else jnp.concatenate(outs, axis=1)


def _tail_kernel(*refs, n_in, tf):
    a_refs = refs[:n_in]
    wo_refs = refs[n_in:2 * n_in]
    (x_ref, p_ref, g1_ref, b1_ref, w1_ref, w2_ref, wg_ref, wp_ref, g2_ref, b2_ref,
     o_ref, acc_ref) = refs[2 * n_in:]
    m = None
    for a_ref, w_ref in zip(a_refs, wo_refs):
        t = _dot(a_ref[...].astype(BF16), w_ref[...])
        m = t if m is None else m + t
    h = _layer_norm(DN_ALPHA * x_ref[...] + m, g1_ref[...], b1_ref[...])
    hb = h.astype(BF16)
    gate = jax.nn.sigmoid(_dot(hb, wg_ref[...]))
    acc_ref[...] = DN_ALPHA * h + gate * _dot(p_ref[...].astype(BF16), wp_ref[...])
    for c in range(w1_ref.shape[1] // tf):
        a = jnp.maximum(_dot(hb, w1_ref[:, c * tf:(c + 1) * tf]), 0.0)
        acc_ref[...] += _dot((a * a).astype(BF16), w2_ref[c * tf:(c + 1) * tf, :])
    o_ref[...] = _layer_norm(acc_ref[...], g2_ref[...], b2_ref[...])


def _tail(acts, wos, x2, p3, layer, g1, b1, w1, w2, wg, wp, g2, b2, tm, tf):
    m, d = x2.shape
    n_in = len(acts)
    rows = lambda a: pl.BlockSpec((tm, a.shape[1]), lambda i: (i, 0))
    resident = lambda a: pl.BlockSpec(a.shape, lambda i: (0, 0), pipeline_mode=pl.Buffered(1))
    of_layer = lambda a: pl.BlockSpec((None,) + a.shape[1:], lambda i: (layer, 0, 0),
                                      pipeline_mode=pl.Buffered(1))
    params = [g1, b1, w1, w2, wg, wp, g2, b2]
    return pl.pallas_call(
        functools.partial(_tail_kernel, n_in=n_in, tf=tf),
        grid=(m // tm,),
        in_specs=([rows(a) for a in acts] + [resident(w) for w in wos]
                  + [rows(x2), pl.BlockSpec((None, tm, p3.shape[2]), lambda i: (layer, i, 0))]
                  + [of_layer(a) for a in params]),
        out_specs=pl.BlockSpec((tm, d), lambda i: (i, 0)),
        out_shape=jax.ShapeDtypeStruct((m, d), F32),
        scratch_shapes=[pltpu.VMEM((tm, d), F32)],
        compiler_params=_cparams(("parallel",)),
        name="layer_tail",
    )(*acts, *wos, x2, p3, *params)


MLA_Q_SCALE = (MLA_NOPE + MLA_ROPE) ** -0.5 * math.log2(math.e)
ROPE_HALF = MLA_ROPE // 2
NOPE_LO = LANES // 2 - ROPE_HALF


def _rope_group(t, cos_t, sin_t):
    return t * cos_t + pltpu.roll(t, LANES // 2, axis=1) * sin_t


def _mla_proj_kernel(x_ref, wd_ref, qg_ref, kvg_ref, wq_ref, wk_ref, wv_ref,
                     cosq_ref, sinq_ref, cosk_ref, sink_ref, q_ref, k_ref, v_ref):
    xb = x_ref[...].astype(BF16)
    down = _dot(xb, wd_ref[...])
    cq = down[:, 0:MLA_Q_LORA]
    ckv = down[:, MLA_Q_LORA:MLA_Q_LORA + MLA_KV_LORA]
    krg = down[:, MLA_Q_LORA + MLA_KV_LORA:]
    cqn = (cq * lax.rsqrt(jnp.mean(cq * cq, axis=-1, keepdims=True) + RMS_EPS) * qg_ref[...]).astype(BF16)
    ckvn = (ckv * lax.rsqrt(jnp.mean(ckv * ckv, axis=-1, keepdims=True) + RMS_EPS) * kvg_ref[...]).astype(BF16)
    cos_q, sin_q = cosq_ref[...], sinq_ref[...]
    kr = _rope_group(krg, cosk_ref[...], sink_ref[...])
    q = _dot(cqn, wq_ref[...])
    k = _dot(ckvn, wk_ref[...])
    for h in range(MLA_HEADS):
        sl = slice(h * LANES, (h + 1) * LANES)
        q_ref[:, sl] = _rope_group(q[:, sl], cos_q, sin_q).astype(BF16)
        k_ref[:, sl] = (k[:, sl] + kr).astype(BF16)
    v_ref[...] = _dot(ckvn, wv_ref[...]).astype(BF16)


def _mla_proj(x2, wd, qg, kvg, wq, wk, wv, tables, tm, s):
    m, d = x2.shape
    nt = s // tm
    full = lambda a: pl.BlockSpec(a.shape, lambda i: (0, 0))
    tab = lambda: pl.BlockSpec((tm, LANES), lambda i: (i % nt, 0))
    hq = MLA_HEADS * LANES
    hv = MLA_HEADS * MLA_V
    return pl.pallas_call(
        _mla_proj_kernel,
        grid=(m // tm,),
        in_specs=[pl.BlockSpec((tm, d), lambda i: (i, 0)), full(wd), full(qg), full(kvg),
                  full(wq), full(wk), full(wv)] + [tab() for _ in tables],
        out_specs=[pl.BlockSpec((tm, hq), lambda i: (i, 0)),
                   pl.BlockSpec((tm, hq), lambda i: (i, 0)),
                   pl.BlockSpec((tm, hv), lambda i: (i, 0))],
        out_shape=[jax.ShapeDtypeStruct((m, hq), BF16),
                   jax.ShapeDtypeStruct((m, hq), BF16),
                   jax.ShapeDtypeStruct((m, hv), BF16)],
        compiler_params=_cparams(("parallel",)),
        name="mla_proj",
    )(x2, wd, qg, kvg, wq, wk, wv, *tables)


def _mla_attn_kernel(q_ref, k_ref, v_ref, o_ref, ot_ref, *, tq, heads_per_step, key_chunk):
    s = q_ref.shape[1]
    keep_t = (lax.broadcasted_iota(jnp.int32, (tq, tq), 0)
              <= lax.broadcasted_iota(jnp.int32, (tq, tq), 1))
    v_t = jnp.transpose(v_ref[0].astype(F32)).astype(BF16)
    ones_rows = jnp.ones((2 * SUBLANES, s), BF16)
    lhs = [jnp.concatenate([v_t[hh * MLA_V:(hh + 1) * MLA_V, :], ones_rows], axis=0)
           for hh in range(heads_per_step)]
    units = []
    for j in range(s // tq):
        ext = (j + 1) * tq
        starts = list(range(0, ext - tq, key_chunk)) or [0]
        bounds = [(a, b) for a, b in zip(starts, starts[1:] + [ext])]
        for hh in range(heads_per_step):
            units += [(j * tq, hh, k0, k1) for k0, k1 in bounds]

    def logits(r0, hh, k0, k1):
        ql = slice(hh * LANES, (hh + 1) * LANES)
        sc = _dot_nt(k_ref[0, k0:k1, ql], q_ref[0, r0:r0 + tq, ql])
        if k1 == r0 + tq:
            diag = jnp.where(keep_t, sc[k1 - k0 - tq:, :], MASK_NEG)
            sc = diag if k1 - k0 == tq else jnp.concatenate([sc[:k1 - k0 - tq, :], diag], axis=0)
        return sc

    def probs(sc):
        m = _reduce_keys(sc, jnp.max, jnp.maximum)
        return jnp.exp2(sc - m).astype(BF16), m

    running = {}

    def values(r0, hh, k0, k1, p, m):
        acc = _dot(lhs[hh][:, k0:k1], p)
        if k0 > 0:
            m_run, acc_run = running.pop((r0, hh))
            m_new = jnp.maximum(m_run, m)
            acc = acc_run * jnp.exp2(m_run - m_new) + acc * jnp.exp2(m - m_new)
            m = m_new
        if k1 == r0 + tq:
            ot_ref[hh * MLA_V:(hh + 1) * MLA_V, r0:r0 + tq] = acc[0:MLA_V, :] / acc[MLA_V:MLA_V + 1, :]
        else:
            running[(r0, hh)] = (m, acc)

    n = len(units)
    s_cur = logits(*units[0])
    p_prev = None
    for u in range(n + 1):
        s_nxt = logits(*units[u + 1]) if u + 1 < n else None
        p_cur = probs(s_cur) if u < n else None
        if p_prev is not None:
            values(*units[u - 1], *p_prev)
        s_cur, p_prev = s_nxt, p_cur
    o_ref[0] = jnp.transpose(ot_ref[...]).astype(o_ref.dtype)


def _mla_attn(q, k, v, tq):
    bsz, s, _ = q.shape
    hps = 2
    return pl.pallas_call(
        functools.partial(_mla_attn_kernel, tq=tq, heads_per_step=hps, key_chunk=4 * tq),
        grid=(bsz, MLA_HEADS // hps),
        in_specs=[pl.BlockSpec((1, s, hps * LANES), lambda b, h: (b, 0, h)),
                  pl.BlockSpec((1, s, hps * LANES), lambda b, h: (b, 0, h)),
                  pl.BlockSpec((1, s, hps * MLA_V), lambda b, h: (b, 0, h))],
        out_specs=pl.BlockSpec((1, s, hps * MLA_V), lambda b, h: (b, 0, h)),
        out_shape=jax.ShapeDtypeStruct((bsz, s, MLA_HEADS * MLA_V), BF16),
        scratch_shapes=[pltpu.VMEM((hps * MLA_V, s), F32)],
        compiler_params=_cparams(("parallel", "parallel")),
        name="mla_attn",
    )(q, k, v)


def _hy_in_weight(w_in):
    xr, yr, q, k, v, iq, ik, iw = jnp.split(
        w_in, np.cumsum([512, 512, 512, 64, 64, 256, 64, 4])[:-1].tolist(), axis=1)
    pad = jnp.zeros((w_in.shape[0], 256 - 64 * 3 - 4), w_in.dtype)
    return jnp.concatenate([xr, yr, q, iq, k, v, ik, iw, pad], axis=1).astype(BF16)


def _block_diag_groups(w):
    bw = w.shape[-1]
    per = LANES // bw
    ng = w.shape[0] // per
    on_diag = jnp.eye(per, dtype=bool)[None, :, None, :, None]
    out = jnp.where(on_diag, w.reshape(ng, per, bw, 1, bw), 0.0)
    return out.reshape(ng, LANES, LANES).astype(BF16)


def _head_lanes(nope, rope):
    lead = nope.shape[:-1] if nope is not None else rope.shape[:-1]
    dt = nope.dtype if nope is not None else rope.dtype
    z = lambda n: jnp.zeros(lead + (n,), dt)
    r1, r2 = (rope[..., :ROPE_HALF], rope[..., ROPE_HALF:]) if rope is not None else (z(ROPE_HALF), z(ROPE_HALF))
    n1, n2 = (nope[..., :NOPE_LO], nope[..., NOPE_LO:]) if nope is not None else (z(NOPE_LO), z(MLA_NOPE - NOPE_LO))
    return jnp.concatenate([r1, n1, r2, n2, z(LANES - MLA_NOPE - MLA_ROPE)], axis=-1)


def _mla_weights(w_down, w_uq, w_ukv):
    lat = MLA_Q_LORA + MLA_KV_LORA
    wd = jnp.concatenate([w_down[:, :lat], _head_lanes(None, w_down[:, lat:])], axis=1)
    wq = w_uq.reshape(MLA_Q_LORA, MLA_HEADS, MLA_NOPE + MLA_ROPE)
    wq = _head_lanes(wq[:, :, :MLA_NOPE], wq[:, :, MLA_NOPE:]).reshape(MLA_Q_LORA, MLA_HEADS * LANES)
    wkv = w_ukv.reshape(MLA_KV_LORA, MLA_HEADS, MLA_NOPE + MLA_V)
    wk = _head_lanes(wkv[:, :, :MLA_NOPE], None).reshape(MLA_KV_LORA, MLA_HEADS * LANES)
    wv = wkv[:, :, MLA_NOPE:].reshape(MLA_KV_LORA, MLA_HEADS * MLA_V)
    return wd.astype(BF16), wq.astype(BF16), wk.astype(BF16), wv.astype(BF16)


def _rope_tables(s):
    pos = np.arange(s, dtype=np.float64)
    freq = ROPE_BASE ** (-np.arange(0, MLA_ROPE, 2, dtype=np.float64) / MLA_ROPE)
    ang = pos[:, None] * freq[None, :]
    lo = slice(0, ROPE_HALF)
    hi = slice(LANES // 2, LANES // 2 + ROPE_HALF)
    cos_t = np.ones((s, LANES))
    sin_t = np.zeros((s, LANES))
    cos_t[:, lo] = cos_t[:, hi] = np.cos(ang)
    sin_t[:, lo] = -np.sin(ang)
    sin_t[:, hi] = np.sin(ang)
    f32 = lambda a: jnp.asarray(a.astype(np.float32))
    return f32(cos_t * MLA_Q_SCALE), f32(sin_t * MLA_Q_SCALE), f32(cos_t), f32(sin_t)


def _tile_m(m):
    return 512 if m % 512 == 0 else m


def kernel(x, p, ln1_g, ln1_b, ln2_g, ln2_b, mlp_w1, mlp_w2, ple_w_proj, ple_w_gate, hy_w_in, hy_conv_w, hy_conv_b, hy_ga_w, hy_ga_b, hy_gx_w, hy_gx_b, hy_lambda, hy_w_out, mla_w_down, mla_q_norm, mla_kv_norm, mla_w_uq, mla_w_ukv, mla_w_out):
    bsz, s, d = x.shape
    m = bsz * s
    tm = _tile_m(m)
    tf = 1024
    row = lambda a: a.reshape(1, -1)
    x2 = x.reshape(m, d)

    rows3 = lambda a: a.reshape(a.shape[0], 1, -1)
    tail_params = (rows3(ln1_g), rows3(ln1_b), mlp_w1.astype(BF16), mlp_w2.astype(BF16),
                   ple_w_gate.astype(BF16), ple_w_proj.astype(BF16), rows3(ln2_g), rows3(ln2_b))
    p3 = p.reshape(p.shape[0], m, -1)

    def layer_tail(acts, wos, x2, i):
        return _tail(acts, wos, x2, p3, i, *tail_params, tm, tf)

    xy, q, iq, misc = _inproj(x2, _hy_in_weight(hy_w_in[0]), tm)
    rec = _rglru(xy.reshape(bsz, s, -1), hy_conv_w[0], row(hy_conv_b[0]),
                 _block_diag_groups(hy_ga_w[0]), row(hy_ga_b[0]),
                 _block_diag_groups(hy_gx_w[0]), row(hy_gx_b[0]), row(hy_lambda[0]))
    att = _dsa(q.reshape(bsz, s, -1), iq.reshape(bsz, s, -1), misc.reshape(bsz, s, -1))
    w_out = hy_w_out[0].astype(BF16)
    x2 = layer_tail([rec.reshape(m, -1), att.reshape(m, -1)], [w_out[:RNN_WIDTH], w_out[RNN_WIDTH:]], x2, 0)

    wd, wq, wk, wv = _mla_weights(mla_w_down[0], mla_w_uq[0], mla_w_ukv[0])
    qp, kp, vp = _mla_proj(x2, wd, row(mla_q_norm[0]), row(mla_kv_norm[0]), wq, wk, wv,
                           _rope_tables(s), min(tm, s), s)
    o = _mla_attn(qp.reshape(bsz, s, -1), kp.reshape(bsz, s, -1), vp.reshape(bsz, s, -1), min(256, s))
    x2 = layer_tail([o.reshape(m, -1)], [mla_w_out[0].astype(BF16)], x2, 1)
    return x2.reshape(bsz, s, d)
```

```python
import functools
import math

import numpy as np
import jax
import jax.numpy as jnp
from jax import lax
from jax.experimental import pallas as pl
from jax.experimental.pallas import tpu as pltpu

F32 = jnp.float32
BF16 = jnp.bfloat16

RNN_WIDTH = 512
RNN_BLOCKS = 8
CONV_WIDTH = 4
LRU_C = 8.0
DSA_HEADS = 8
DSA_HEAD_DIM = 64
IDX_HEADS = 4
IDX_DIM = 64
DSA_TOPK_MAX = 256
MLA_HEADS = 16
MLA_Q_LORA = 512
MLA_KV_LORA = 256
MLA_NOPE = 64
MLA_ROPE = 32
MLA_V = 64
ROPE_BASE = 10000.0
DEPTH = 2
DN_ALPHA = (2 * DEPTH) ** 0.25
LN_EPS = 1e-5
RMS_EPS = 1e-6

LANES = 128
SUBLANES = 8
VMEM_LIMIT_BYTES = 56 * 1024 * 1024

MASK_NEG = -3.0e38
INT32_MIN = -(2 ** 31)
DSA_Q_SCALE = DSA_HEAD_DIM ** -0.5


def _cparams(sem):
    return pltpu.CompilerParams(dimension_semantics=sem, vmem_limit_bytes=VMEM_LIMIT_BYTES)


def _dot(a, b):
    return jnp.dot(a, b, preferred_element_type=F32)


def _dot_nt(a, b):
    return lax.dot_general(a, b, (((1,), (1,)), ((), ())), preferred_element_type=F32)


def _layer_norm(y, g, b):
    mu = jnp.mean(y, axis=-1, keepdims=True)
    yc = y - mu
    var = jnp.mean(yc * yc, axis=-1, keepdims=True)
    return yc * lax.rsqrt(var + LN_EPS) * g + b


IN_XY = 2 * RNN_WIDTH
IN_Q = DSA_HEADS * DSA_HEAD_DIM
IN_IQ = IDX_HEADS * IDX_DIM
IN_MISC = 2 * LANES
IN_OFFSETS = tuple(int(v) for v in np.cumsum([0, IN_XY, IN_Q, IN_IQ, IN_MISC]))


def _inproj_kernel(x_ref, w_ref, xy_ref, q_ref, iq_ref, misc_ref):
    xb = x_ref[...].astype(BF16)
    cols = [w_ref[:, a:b] for a, b in zip(IN_OFFSETS[:-1], IN_OFFSETS[1:])]
    xy_ref[...] = _dot(xb, cols[0])
    q_ref[...] = (_dot(xb, cols[1]) * DSA_Q_SCALE).astype(BF16)
    iq_ref[...] = _dot(xb, cols[2])
    misc_ref[...] = _dot(xb, cols[3])


def _inproj(x2, w, tm):
    m, d = x2.shape
    n = w.shape[1]
    widths = (IN_XY, IN_Q, IN_IQ, IN_MISC)
    dtypes = (F32, BF16, F32, F32)
    return pl.pallas_call(
        _inproj_kernel,
        grid=(m // tm,),
        in_specs=[pl.BlockSpec((tm, d), lambda i: (i, 0)),
                  pl.BlockSpec((d, n), lambda i: (0, 0))],
        out_specs=[pl.BlockSpec((tm, wd), lambda i: (i, 0)) for wd in widths],
        out_shape=[jax.ShapeDtypeStruct((m, wd), dt) for wd, dt in zip(widths, dtypes)],
        compiler_params=_cparams(("parallel",)),
        name="hy_inproj",
    )(x2, w)


def _rglru_kernel(xr_ref, yr_ref, cw_ref, cb_ref, wa_ref, ba_ref, wx_ref, bx_ref, lam_ref,
                  o_ref, xpad_ref):
    s = xr_ref.shape[1]
    c = xr_ref.shape[2]
    xpad_ref[0:SUBLANES, :] = jnp.zeros((SUBLANES, c), F32)
    xpad_ref[SUBLANES:SUBLANES + s, :] = xr_ref[0]
    xc = None
    for k in range(CONV_WIDTH):
        start = SUBLANES - (CONV_WIDTH - 1) + k
        term = xpad_ref[start:start + s, :] * cw_ref[k:k + 1, :]
        xc = term if xc is None else xc + term
    xc = xc + cb_ref[...]
    xb = xc.astype(BF16)
    r = jax.nn.sigmoid(_dot(xb, wa_ref[0]) + ba_ref[...])
    gi = jax.nn.sigmoid(_dot(xb, wx_ref[0]) + bx_ref[...])
    nl = -lam_ref[...]
    softplus = jnp.maximum(nl, 0.0) + jnp.log1p(jnp.exp(-jnp.abs(nl)))
    log_a = (-LRU_C) * r * softplus
    a = jnp.exp(log_a)
    y = -jnp.tanh(log_a) * (a * a + 1.0)
    h = jnp.where(y > 0.0, y * lax.rsqrt(y), 0.0) * (gi * xc)
    row = lax.broadcasted_iota(jnp.int32, (s, c), 0)
    sh = 1
    while sh < s:
        if sh < SUBLANES:
            valid = row >= sh
            h = a * jnp.where(valid, pltpu.roll(h, sh, axis=0), 0.0) + h
            a = a * jnp.where(valid, pltpu.roll(a, sh, axis=0), 1.0)
        else:
            h = jnp.concatenate([h[:sh], a[sh:] * h[:s - sh] + h[sh:]], axis=0)
            if sh * 2 < s:
                a = jnp.concatenate([a[:sh], a[sh:] * a[:s - sh]], axis=0)
        sh *= 2
    o_ref[0] = (h * jax.nn.gelu(yr_ref[0])).astype(o_ref.dtype)


def _rglru(xy, conv_w, conv_b, wa, ba, wx, bx, lam):
    bsz, s, _ = xy.shape
    c = LANES
    ng = RNN_WIDTH // c
    vec = lambda: pl.BlockSpec((1, c), lambda b, g: (0, g))
    return pl.pallas_call(
        _rglru_kernel,
        grid=(bsz, ng),
        in_specs=[pl.BlockSpec((1, s, c), lambda b, g: (b, 0, g)),
                  pl.BlockSpec((1, s, c), lambda b, g: (b, 0, ng + g)),
                  pl.BlockSpec((CONV_WIDTH, c), lambda b, g: (0, g)),
                  vec(),
                  pl.BlockSpec((1, c, c), lambda b, g: (g, 0, 0)),
                  vec(),
                  pl.BlockSpec((1, c, c), lambda b, g: (g, 0, 0)),
                  vec(),
                  vec()],
        out_specs=pl.BlockSpec((1, s, c), lambda b, g: (b, 0, g)),
        out_shape=jax.ShapeDtypeStruct((bsz, s, RNN_WIDTH), BF16),
        scratch_shapes=[pltpu.VMEM((s + SUBLANES, c), F32)],
        compiler_params=_cparams(("parallel", "parallel")),
        name="hy_rglru",
    )(xy, xy, conv_w, conv_b, wa, ba, wx, bx, lam)


def _ordered_int_to_float(c):
    return lax.bitcast_convert_type(c ^ (lax.shift_right_arithmetic(c, 31) & jnp.int32(0x7FFFFFFF)), F32)


def _reduce_keys(x, reduce_fn, combine_fn, groups=8):
    rows = x.shape[0]
    step = max(rows // groups, SUBLANES)
    parts = [reduce_fn(x[r:r + step, :], axis=0, keepdims=True) for r in range(0, rows, step)]
    while len(parts) > 1:
        parts = [combine_fn(parts[i], parts[i + 1]) if i + 1 < len(parts) else parts[i]
                 for i in range(0, len(parts), 2)]
    return parts[0]


def _dsa_prepare(q_ref, mk_ref):
    tq = q_ref.shape[1]
    ext = mk_ref.shape[1]
    d0 = ext - tq
    keep = (lax.broadcasted_iota(jnp.int32, (tq, tq), 1)
            <= lax.broadcasted_iota(jnp.int32, (tq, tq), 0))

    mk = mk_ref[0]
    vlane = lax.broadcasted_iota(jnp.int32, (ext, DSA_HEAD_DIM), 1)
    v_aug = jnp.concatenate([mk[:, DSA_HEAD_DIM:2 * DSA_HEAD_DIM],
                             jnp.where(vlane == 0, 1.0, 0.0)], axis=1).astype(BF16)
    ik_b = mk[:, 2 * DSA_HEAD_DIM:2 * DSA_HEAD_DIM + IDX_DIM].astype(BF16)
    kcol = lax.broadcasted_iota(jnp.int32, (ext, 1), 0)
    lane = lax.broadcasted_iota(jnp.int32, (1, LANES), 1)
    pos_hi = (kcol & jnp.int32(-LANES)).astype(F32)
    pos_lo = (kcol & jnp.int32(LANES - 1)).astype(F32)
    k_aug = jnp.where(lane < DSA_HEAD_DIM, mk[:, 0:LANES],
                      jnp.where(lane == DSA_HEAD_DIM, pos_hi,
                                jnp.where(lane == DSA_HEAD_DIM + 1, pos_lo, 0.0))).astype(BF16)
    return d0, keep, ik_b, k_aug, v_aug


def _dsa_select(iq_ref, mq_ref, sc_ref, bias_ref, ik_b, keep, *, topk):
    tq, ext = sc_ref.shape
    d0 = ext - tq
    kf = float(topk)
    iw0 = 2 * DSA_HEAD_DIM + IDX_DIM
    iw = mq_ref[0][:, iw0:iw0 + IDX_HEADS] * (IDX_HEADS ** -0.5 * IDX_DIM ** -0.5)
    iq = iq_ref[0].astype(BF16)
    sc = None
    for h in range(IDX_HEADS):
        d = _dot_nt(iq[:, h * IDX_DIM:(h + 1) * IDX_DIM], ik_b)
        term = jnp.maximum(d, 0.0) * iw[:, h:h + 1]
        sc = term if sc is None else sc + term
    if d0 > 0:
        sc_ref[:, 0:d0] = sc[:, 0:d0]
    sc_ref[:, d0:ext] = jnp.where(keep, sc[:, d0:ext], MASK_NEG)

    def search(i, tau):
        cand = tau + lax.shift_left(jnp.int32(1), 31 - i)
        ge = sc_ref[...] >= _ordered_int_to_float(cand)
        cnt = jnp.sum(jnp.where(ge, 1.0, 0.0), axis=-1, keepdims=True)
        return jnp.where(cnt >= kf, cand, tau)

    tau = lax.fori_loop(0, 32, search, jnp.full((tq, 1), INT32_MIN, jnp.int32), unroll=True)
    tau = _ordered_int_to_float(tau)

    sc = sc_ref[...]
    gt = sc > tau
    eq = sc == tau
    need = kf - jnp.sum(jnp.where(gt, 1.0, 0.0), axis=-1, keepdims=True)
    eqf = jnp.where(eq, 1.0, 0.0).astype(BF16)
    tri = (lax.broadcasted_iota(jnp.int32, (LANES, LANES), 0)
           <= lax.broadcasted_iota(jnp.int32, (LANES, LANES), 1)).astype(BF16)
    off = jnp.zeros((tq, 1), F32)
    for c in range(ext // LANES):
        sl = slice(c * LANES, (c + 1) * LANES)
        pc = _dot(eqf[:, sl], tri)
        take = gt[:, sl] | (eq[:, sl] & ((pc + off) <= need))
        if c * LANES >= d0:
            take = take & keep[:, c * LANES - d0:(c + 1) * LANES - d0]
        bias_ref[:, sl] = jnp.where(take, 0.0, MASK_NEG)
        off = off + pc[:, LANES - 1:LANES]


def _dsa_attend(q_ref, o_ref, bias_ref, k_aug, v_aug):
    tq = q_ref.shape[1]
    q = q_ref[0]
    tail_lane = lax.broadcasted_iota(jnp.int32, (tq, LANES - DSA_HEAD_DIM), 1)
    for h in range(DSA_HEADS):
        slope = 2.0 ** (-8.0 * (h + 1) / DSA_HEADS)
        tail = jnp.where(tail_lane < 2, slope, 0.0).astype(BF16)
        q_aug = jnp.concatenate([q[:, h * DSA_HEAD_DIM:(h + 1) * DSA_HEAD_DIM], tail], axis=1)
        lg = _dot_nt(q_aug, k_aug) + bias_ref[...]
        p = jnp.exp(lg - jnp.max(lg, axis=-1, keepdims=True))
        o = _dot(p.astype(BF16), v_aug)
        o = o[:, 0:DSA_HEAD_DIM] / o[:, DSA_HEAD_DIM:DSA_HEAD_DIM + 1]
        o_ref[0, :, h * DSA_HEAD_DIM:(h + 1) * DSA_HEAD_DIM] = o.astype(o_ref.dtype)


def _dsa_kernel(q_ref, iq_ref, mq_ref, mk_ref, o_ref, sc_ref, bias_ref, *, topk):
    d0, keep, ik_b, k_aug, v_aug = _dsa_prepare(q_ref, mk_ref)
    if mk_ref.shape[1] > topk:
        _dsa_select(iq_ref, mq_ref, sc_ref, bias_ref, ik_b, keep, topk=topk)
    else:
        if d0 > 0:
            bias_ref[:, 0:d0] = jnp.zeros((keep.shape[0], d0), F32)
        bias_ref[:, d0:] = jnp.where(keep, 0.0, MASK_NEG)
    _dsa_attend(q_ref, o_ref, bias_ref, k_aug, v_aug)


def _dsa(q, iq, misc):
    bsz, s, _ = q.shape
    topk = min(DSA_TOPK_MAX, s // 4)
    tq = min(256, s)
    hd = DSA_HEADS * DSA_HEAD_DIM
    outs = []
    for c in range(s // tq):
        ext = (c + 1) * tq
        outs.append(pl.pallas_call(
            functools.partial(_dsa_kernel, topk=topk),
            grid=(bsz,),
            in_specs=[pl.BlockSpec((1, tq, hd), lambda b, c=c: (b, c, 0)),
                      pl.BlockSpec((1, tq, IDX_HEADS * IDX_DIM), lambda b, c=c: (b, c, 0)),
                      pl.BlockSpec((1, tq, 256), lambda b, c=c: (b, c, 0)),
                      pl.BlockSpec((1, ext, 256), lambda b: (b, 0, 0))],
            out_specs=pl.BlockSpec((1, tq, hd), lambda b: (b, 0, 0)),
            out_shape=jax.ShapeDtypeStruct((bsz, tq, hd), BF16),
            scratch_shapes=[pltpu.VMEM((tq, ext), F32), pltpu.VMEM((tq, ext), F32)],
            compiler_params=_cparams(("parallel",)),
            name=f"hy_dsa_c{c}",
        )(q, iq, misc, misc))
    return outs[0] if len(outs) == 1 else jnp.concatenate(outs, axis=1)


def _tail_kernel(*refs, n_in, tf):
    a_refs = refs[:n_in]
    wo_refs = refs[n_in:2 * n_in]
    (x_ref, p_ref, g1_ref, b1_ref, w1_ref, w2_ref, wg_ref, wp_ref, g2_ref, b2_ref,
     o_ref, acc_ref) = refs[2 * n_in:]
    m = None
    for a_ref, w_ref in zip(a_refs, wo_refs):
        t = _dot(a_ref[...].astype(BF16), w_ref[...])
        m = t if m is None else m + t
    h = _layer_norm(DN_ALPHA * x_ref[...] + m, g1_ref[...], b1_ref[...])
    hb = h.astype(BF16)
    gate = jax.nn.sigmoid(_dot(hb, wg_ref[...]))
    acc_ref[...] = DN_ALPHA * h + gate * _dot(p_ref[...].astype(BF16), wp_ref[...])
    for c in range(w1_ref.shape[1] // tf):
        a = jnp.maximum(_dot(hb, w1_ref[:, c * tf:(c + 1) * tf]), 0.0)
        acc_ref[...] += _dot((a * a).astype(BF16), w2_ref[c * tf:(c + 1) * tf, :])
    o_ref[...] = _layer_norm(acc_ref[...], g2_ref[...], b2_ref[...])


def _tail(acts, wos, x2, p3, layer, g1, b1, w1, w2, wg, wp, g2, b2, tm, tf):
    m, d = x2.shape
    n_in = len(acts)
    rows = lambda a: pl.BlockSpec((tm, a.shape[1]), lambda i: (i, 0))
    resident = lambda a: pl.BlockSpec(a.shape, lambda i: (0, 0), pipeline_mode=pl.Buffered(1))
    of_layer = lambda a: pl.BlockSpec((None,) + a.shape[1:], lambda i: (layer, 0, 0),
                                      pipeline_mode=pl.Buffered(1))
    params = [g1, b1, w1, w2, wg, wp, g2, b2]
    return pl.pallas_call(
        functools.partial(_tail_kernel, n_in=n_in, tf=tf),
        grid=(m // tm,),
        in_specs=([rows(a) for a in acts] + [resident(w) for w in wos]
                  + [rows(x2), pl.BlockSpec((None, tm, p3.shape[2]), lambda i: (layer, i, 0))]
                  + [of_layer(a) for a in params]),
        out_specs=pl.BlockSpec((tm, d), lambda i: (i, 0)),
        out_shape=jax.ShapeDtypeStruct((m, d), F32),
        scratch_shapes=[pltpu.VMEM((tm, d), F32)],
        compiler_params=_cparams(("parallel",)),
        name="layer_tail",
    )(*acts, *wos, x2, p3, *params)


MLA_Q_SCALE = (MLA_NOPE + MLA_ROPE) ** -0.5 * math.log2(math.e)
ROPE_HALF = MLA_ROPE // 2
NOPE_LO = LANES // 2 - ROPE_HALF


def _rope_group(t, cos_t, sin_t):
    return t * cos_t + pltpu.roll(t, LANES // 2, axis=1) * sin_t


def _mla_proj_kernel(x_ref, wd_ref, qg_ref, kvg_ref, wq_ref, wk_ref, wv_ref,
                     cosq_ref, sinq_ref, cosk_ref, sink_ref, q_ref, k_ref, v_ref):
    xb = x_ref[...].astype(BF16)
    down = _dot(xb, wd_ref[...])
    cq = down[:, 0:MLA_Q_LORA]
    ckv = down[:, MLA_Q_LORA:MLA_Q_LORA + MLA_KV_LORA]
    krg = down[:, MLA_Q_LORA + MLA_KV_LORA:]
    cqn = (cq * lax.rsqrt(jnp.mean(cq * cq, axis=-1, keepdims=True) + RMS_EPS) * qg_ref[...]).astype(BF16)
    ckvn = (ckv * lax.rsqrt(jnp.mean(ckv * ckv, axis=-1, keepdims=True) + RMS_EPS) * kvg_ref[...]).astype(BF16)
    cos_q, sin_q = cosq_ref[...], sinq_ref[...]
    kr = _rope_group(krg, cosk_ref[...], sink_ref[...])
    q = _dot(cqn, wq_ref[...])
    k = _dot(ckvn, wk_ref[...])
    for h in range(MLA_HEADS):
        sl = slice(h * LANES, (h + 1) * LANES)
        q_ref[:, sl] = _rope_group(q[:, sl], cos_q, sin_q).astype(BF16)
        k_ref[:, sl] = (k[:, sl] + kr).astype(BF16)
    v_ref[...] = _dot(ckvn, wv_ref[...]).astype(BF16)


def _mla_proj(x2, wd, qg, kvg, wq, wk, wv, tables, tm, s):
    m, d = x2.shape
    nt = s // tm
    full = lambda a: pl.BlockSpec(a.shape, lambda i: (0, 0))
    tab = lambda: pl.BlockSpec((tm, LANES), lambda i: (i % nt, 0))
    hq = MLA_HEADS * LANES
    hv = MLA_HEADS * MLA_V
    return pl.pallas_call(
        _mla_proj_kernel,
        grid=(m // tm,),
        in_specs=[pl.BlockSpec((tm, d), lambda i: (i, 0)), full(wd), full(qg), full(kvg),
                  full(wq), full(wk), full(wv)] + [tab() for _ in tables],
        out_specs=[pl.BlockSpec((tm, hq), lambda i: (i, 0)),
                   pl.BlockSpec((tm, hq), lambda i: (i, 0)),
                   pl.BlockSpec((tm, hv), lambda i: (i, 0))],
        out_shape=[jax.ShapeDtypeStruct((m, hq), BF16),
                   jax.ShapeDtypeStruct((m, hq), BF16),
                   jax.ShapeDtypeStruct((m, hv), BF16)],
        compiler_params=_cparams(("parallel",)),
        name="mla_proj",
    )(x2, wd, qg, kvg, wq, wk, wv, *tables)


def _mla_attn_kernel(q_ref, k_ref, v_ref, o_ref, ot_ref, *, tq, heads_per_step, key_chunk):
    s = q_ref.shape[1]
    keep_t = (lax.broadcasted_iota(jnp.int32, (tq, tq), 0)
              <= lax.broadcasted_iota(jnp.int32, (tq, tq), 1))
    v_t = jnp.transpose(v_ref[0].astype(F32)).astype(BF16)
    ones_rows = jnp.ones((2 * SUBLANES, s), BF16)
    lhs = [jnp.concatenate([v_t[hh * MLA_V:(hh + 1) * MLA_V, :], ones_rows], axis=0)
           for hh in range(heads_per_step)]
    units = []
    for j in range(s // tq):
        ext = (j + 1) * tq
        starts = list(range(0, ext - tq, key_chunk)) or [0]
        bounds = [(a, b) for a, b in zip(starts, starts[1:] + [ext])]
        for hh in range(heads_per_step):
            units += [(j * tq, hh, k0, k1) for k0, k1 in bounds]

    def logits(r0, hh, k0, k1):
        ql = slice(hh * LANES, (hh + 1) * LANES)
        sc = _dot_nt(k_ref[0, k0:k1, ql], q_ref[0, r0:r0 + tq, ql])
        if k1 == r0 + tq:
            diag = jnp.where(keep_t, sc[k1 - k0 - tq:, :], MASK_NEG)
            sc = diag if k1 - k0 == tq else jnp.concatenate([sc[:k1 - k0 - tq, :], diag], axis=0)
        return sc

    def probs(sc):
        m = _reduce_keys(sc, jnp.max, jnp.maximum)
        return jnp.exp2(sc - m).astype(BF16), m

    running = {}

    def values(r0, hh, k0, k1, p, m):
        acc = _dot(lhs[hh][:, k0:k1], p)
        if k0 > 0:
            m_run, acc_run = running.pop((r0, hh))
            m_new = jnp.maximum(m_run, m)
            acc = acc_run * jnp.exp2(m_run - m_new) + acc * jnp.exp2(m - m_new)
            m = m_new
        if k1 == r0 + tq:
            ot_ref[hh * MLA_V:(hh + 1) * MLA_V, r0:r0 + tq] = acc[0:MLA_V, :] / acc[MLA_V:MLA_V + 1, :]
        else:
            running[(r0, hh)] = (m, acc)

    n = len(units)
    s_cur = logits(*units[0])
    p_prev = None
    for u in range(n + 1):
        s_nxt = logits(*units[u + 1]) if u + 1 < n else None
        p_cur = probs(s_cur) if u < n else None
        if p_prev is not None:
            values(*units[u - 1], *p_prev)
        s_cur, p_prev = s_nxt, p_cur
    o_ref[0] = jnp.transpose(ot_ref[...]).astype(o_ref.dtype)


def _mla_attn(q, k, v, tq):
    bsz, s, _ = q.shape
    hps = 2
    return pl.pallas_call(
        functools.partial(_mla_attn_kernel, tq=tq, heads_per_step=hps, key_chunk=4 * tq),
        grid=(bsz, MLA_HEADS // hps),
        in_specs=[pl.BlockSpec((1, s, hps * LANES), lambda b, h: (b, 0, h)),
                  pl.BlockSpec((1, s, hps * LANES), lambda b, h: (b, 0, h)),
                  pl.BlockSpec((1, s, hps * MLA_V), lambda b, h: (b, 0, h))],
        out_specs=pl.BlockSpec((1, s, hps * MLA_V), lambda b, h: (b, 0, h)),
        out_shape=jax.ShapeDtypeStruct((bsz, s, MLA_HEADS * MLA_V), BF16),
        scratch_shapes=[pltpu.VMEM((hps * MLA_V, s), F32)],
        compiler_params=_cparams(("parallel", "parallel")),
        name="mla_attn",
    )(q, k, v)


def _hy_in_weight(w_in):
    xr, yr, q, k, v, iq, ik, iw = jnp.split(
        w_in, np.cumsum([512, 512, 512, 64, 64, 256, 64, 4])[:-1].tolist(), axis=1)
    pad = jnp.zeros((w_in.shape[0], 256 - 64 * 3 - 4), w_in.dtype)
    return jnp.concatenate([xr, yr, q, iq, k, v, ik, iw, pad], axis=1).astype(BF16)


def _block_diag_groups(w):
    bw = w.shape[-1]
    per = LANES // bw
    ng = w.shape[0] // per
    on_diag = jnp.eye(per, dtype=bool)[None, :, None, :, None]
    out = jnp.where(on_diag, w.reshape(ng, per, bw, 1, bw), 0.0)
    return out.reshape(ng, LANES, LANES).astype(BF16)


def _head_lanes(nope, rope):
    lead = nope.shape[:-1] if nope is not None else rope.shape[:-1]
    dt = nope.dtype if nope is not None else rope.dtype
    z = lambda n: jnp.zeros(lead + (n,), dt)
    r1, r2 = (rope[..., :ROPE_HALF], rope[..., ROPE_HALF:]) if rope is not None else (z(ROPE_HALF), z(ROPE_HALF))
    n1, n2 = (nope[..., :NOPE_LO], nope[..., NOPE_LO:]) if nope is not None else (z(NOPE_LO), z(MLA_NOPE - NOPE_LO))
    return jnp.concatenate([r1, n1, r2, n2, z(LANES - MLA_NOPE - MLA_ROPE)], axis=-1)


def _mla_weights(w_down, w_uq, w_ukv):
    lat = MLA_Q_LORA + MLA_KV_LORA
    wd = jnp.concatenate([w_down[:, :lat], _head_lanes(None, w_down[:, lat:])], axis=1)
    wq = w_uq.reshape(MLA_Q_LORA, MLA_HEADS, MLA_NOPE + MLA_ROPE)
    wq = _head_lanes(wq[:, :, :MLA_NOPE], wq[:, :, MLA_NOPE:]).reshape(MLA_Q_LORA, MLA_HEADS * LANES)
    wkv = w_ukv.reshape(MLA_KV_LORA, MLA_HEADS, MLA_NOPE + MLA_V)
    wk = _head_lanes(wkv[:, :, :MLA_NOPE], None).reshape(MLA_KV_LORA, MLA_HEADS * LANES)
    wv = wkv[:, :, MLA_NOPE:].reshape(MLA_KV_LORA, MLA_HEADS * MLA_V)
    return wd.astype(BF16), wq.astype(BF16), wk.astype(BF16), wv.astype(BF16)


def _rope_tables(s):
    pos = np.arange(s, dtype=np.float64)
    freq = ROPE_BASE ** (-np.arange(0, MLA_ROPE, 2, dtype=np.float64) / MLA_ROPE)
    ang = pos[:, None] * freq[None, :]
    lo = slice(0, ROPE_HALF)
    hi = slice(LANES // 2, LANES // 2 + ROPE_HALF)
    cos_t = np.ones((s, LANES))
    sin_t = np.zeros((s, LANES))
    cos_t[:, lo] = cos_t[:, hi] = np.cos(ang)
    sin_t[:, lo] = -np.sin(ang)
    sin_t[:, hi] = np.sin(ang)
    f32 = lambda a: jnp.asarray(a.astype(np.float32))
    return f32(cos_t * MLA_Q_SCALE), f32(sin_t * MLA_Q_SCALE), f32(cos_t), f32(sin_t)


def _tile_m(m):
    return 512 if m % 512 == 0 else m


def kernel(x, p, ln1_g, ln1_b, ln2_g, ln2_b, mlp_w1, mlp_w2, ple_w_proj, ple_w_gate, hy_w_in, hy_conv_w, hy_conv_b, hy_ga_w, hy_ga_b, hy_gx_w, hy_gx_b, hy_lambda, hy_w_out, mla_w_down, mla_q_norm, mla_kv_norm, mla_w_uq, mla_w_ukv, mla_w_out):
    bsz, s, d = x.shape
    m = bsz * s
    tm = _tile_m(m)
    tf = 1024
    row = lambda a: a.reshape(1, -1)
    x2 = x.reshape(m, d)

    rows3 = lambda a: a.reshape(a.shape[0], 1, -1)
    tail_params = (rows3(ln1_g), rows3(ln1_b), mlp_w1.astype(BF16), mlp_w2.astype(BF16),
                   ple_w_gate.astype(BF16), ple_w_proj.astype(BF16), rows3(ln2_g), rows3(ln2_b))
    p3 = p.reshape(p.shape[0], m, -1)

    def layer_tail(acts, wos, x2, i):
        return _tail(acts, wos, x2, p3, i, *tail_params, tm, tf)

    xy, q, iq, misc = _inproj(x2, _hy_in_weight(hy_w_in[0]), tm)
    rec = _rglru(xy.reshape(bsz, s, -1), hy_conv_w[0], row(hy_conv_b[0]),
                 _block_diag_groups(hy_ga_w[0]), row(hy_ga_b[0]),
                 _block_diag_groups(hy_gx_w[0]), row(hy_gx_b[0]), row(hy_lambda[0]))
    att = _dsa(q.reshape(bsz, s, -1), iq.reshape(bsz, s, -1), misc.reshape(bsz, s, -1))
    w_out = hy_w_out[0].astype(BF16)
    x2 = layer_tail([rec.reshape(m, -1), att.reshape(m, -1)], [w_out[:RNN_WIDTH], w_out[RNN_WIDTH:]], x2, 0)

    wd, wq, wk, wv = _mla_weights(mla_w_down[0], mla_w_uq[0], mla_w_ukv[0])
    qp, kp, vp = _mla_proj(x2, wd, row(mla_q_norm[0]), row(mla_kv_norm[0]), wq, wk, wv,
                           _rope_tables(s), min(tm, s), s)
    o = _mla_attn(qp.reshape(bsz, s, -1), kp.reshape(bsz, s, -1), vp.reshape(bsz, s, -1), min(256, s))
    x2 = layer_tail([o.reshape(m, -1)], [mla_w_out[0].astype(BF16)], x2, 1)
    return x2.reshape(bsz, s, d)
```

```python
import functools
import math

import numpy as np
import jax
import jax.numpy as jnp
from jax import lax
from jax.experimental import pallas as pl
from jax.experimental.pallas import tpu as pltpu

F32 = jnp.float32
BF16 = jnp.bfloat16

RNN_WIDTH = 512
RNN_BLOCKS = 8
CONV_WIDTH = 4
LRU_C = 8.0
DSA_HEADS = 8
DSA_HEAD_DIM = 64
IDX_HEADS = 4
IDX_DIM = 64
DSA_TOPK_MAX = 256
MLA_HEADS = 16
MLA_Q_LORA = 512
MLA_KV_LORA = 256
MLA_NOPE = 64
MLA_ROPE = 32
MLA_V = 64
ROPE_BASE = 10000.0
DEPTH = 2
DN_ALPHA = (2 * DEPTH) ** 0.25
LN_EPS = 1e-5
RMS_EPS = 1e-6

LANES = 128
SUBLANES = 8
VMEM_LIMIT_BYTES = 56 * 1024 * 1024

MASK_NEG = -3.0e38
INT32_MIN = -(2 ** 31)
DSA_Q_SCALE = DSA_HEAD_DIM ** -0.5


def _cparams(sem):
    return pltpu.CompilerParams(dimension_semantics=sem, vmem_limit_bytes=VMEM_LIMIT_BYTES)


def _dot(a, b):
    return jnp.dot(a, b, preferred_element_type=F32)


def _dot_nt(a, b):
    return lax.dot_general(a, b, (((1,), (1,)), ((), ())), preferred_element_type=F32)


def _layer_norm(y, g, b):
    mu = jnp.mean(y, axis=-1, keepdims=True)
    yc = y - mu
    var = jnp.mean(yc * yc, axis=-1, keepdims=True)
    return yc * lax.rsqrt(var + LN_EPS) * g + b


IN_XY = 2 * RNN_WIDTH
IN_Q = DSA_HEADS * DSA_HEAD_DIM
IN_IQ = IDX_HEADS * IDX_DIM
IN_MISC = 2 * LANES
IN_OFFSETS = tuple(int(v) for v in np.cumsum([0, IN_XY, IN_Q, IN_IQ, IN_MISC]))


def _inproj_kernel(x_ref, w_ref, xy_ref, q_ref, iq_ref, misc_ref):
    xb = x_ref[...].astype(BF16)
    cols = [w_ref[:, a:b] for a, b in zip(IN_OFFSETS[:-1], IN_OFFSETS[1:])]
    xy_ref[...] = _dot(xb, cols[0])
    q_ref[...] = (_dot(xb, cols[1]) * DSA_Q_SCALE).astype(BF16)
    iq_ref[...] = _dot(xb, cols[2])
    misc_ref[...] = _dot(xb, cols[3])


def _inproj(x2, w, tm):
    m, d = x2.shape
    n = w.shape[1]
    widths = (IN_XY, IN_Q, IN_IQ, IN_MISC)
    dtypes = (F32, BF16, F32, F32)
    return pl.pallas_call(
        _inproj_kernel,
        grid=(m // tm,),
        in_specs=[pl.BlockSpec((tm, d), lambda i: (i, 0)),
                  pl.BlockSpec((d, n), lambda i: (0, 0))],
        out_specs=[pl.BlockSpec((tm, wd), lambda i: (i, 0)) for wd in widths],
        out_shape=[jax.ShapeDtypeStruct((m, wd), dt) for wd, dt in zip(widths, dtypes)],
        compiler_params=_cparams(("parallel",)),
        name="hy_inproj",
    )(x2, w)


def _rglru_kernel(xr_ref, yr_ref, cw_ref, cb_ref, wa_ref, ba_ref, wx_ref, bx_ref, lam_ref,
                  o_ref, xpad_ref):
    s = xr_ref.shape[1]
    c = xr_ref.shape[2]
    xpad_ref[0:SUBLANES, :] = jnp.zeros((SUBLANES, c), F32)
    xpad_ref[SUBLANES:SUBLANES + s, :] = xr_ref[0]
    xc = None
    for k in range(CONV_WIDTH):
        start = SUBLANES - (CONV_WIDTH - 1) + k
        term = xpad_ref[start:start + s, :] * cw_ref[k:k + 1, :]
        xc = term if xc is None else xc + term
    xc = xc + cb_ref[...]
    xb = xc.astype(BF16)
    r = jax.nn.sigmoid(_dot(xb, wa_ref[0]) + ba_ref[...])
    gi = jax.nn.sigmoid(_dot(xb, wx_ref[0]) + bx_ref[...])
    nl = -lam_ref[...]
    softplus = jnp.maximum(nl, 0.0) + jnp.log1p(jnp.exp(-jnp.abs(nl)))
    log_a = (-LRU_C) * r * softplus
    a = jnp.exp(log_a)
    y = -jnp.tanh(log_a) * (a * a + 1.0)
    h = jnp.where(y > 0.0, y * lax.rsqrt(y), 0.0) * (gi * xc)
    row = lax.broadcasted_iota(jnp.int32, (s, c), 0)
    sh = 1
    while sh < s:
        if sh < SUBLANES:
            valid = row >= sh
            h = a * jnp.where(valid, pltpu.roll(h, sh, axis=0), 0.0) + h
            a = a * jnp.where(valid, pltpu.roll(a, sh, axis=0), 1.0)
        else:
            h = jnp.concatenate([h[:sh], a[sh:] * h[:s - sh] + h[sh:]], axis=0)
            if sh * 2 < s:
                a = jnp.concatenate([a[:sh], a[sh:] * a[:s - sh]], axis=0)
        sh *= 2
    o_ref[0] = (h * jax.nn.gelu(yr_ref[0])).astype(o_ref.dtype)


def _rglru(xy, conv_w, conv_b, wa, ba, wx, bx, lam):
    bsz, s, _ = xy.shape
    c = LANES
    ng = RNN_WIDTH // c
    vec = lambda: pl.BlockSpec((1, c), lambda b, g: (0, g))
    return pl.pallas_call(
        _rglru_kernel,
        grid=(bsz, ng),
        in_specs=[pl.BlockSpec((1, s, c), lambda b, g: (b, 0, g)),
                  pl.BlockSpec((1, s, c), lambda b, g: (b, 0, ng + g)),
                  pl.BlockSpec((CONV_WIDTH, c), lambda b, g: (0, g)),
                  vec(),
                  pl.BlockSpec((1, c, c), lambda b, g: (g, 0, 0)),
                  vec(),
                  pl.BlockSpec((1, c, c), lambda b, g: (g, 0, 0)),
                  vec(),
                  vec()],
        out_specs=pl.BlockSpec((1, s, c), lambda b, g: (b, 0, g)),
        out_shape=jax.ShapeDtypeStruct((bsz, s, RNN_WIDTH), BF16),
        scratch_shapes=[pltpu.VMEM((s + SUBLANES, c), F32)],
        compiler_params=_cparams(("parallel", "parallel")),
        name="hy_rglru",
    )(xy, xy, conv_w, conv_b, wa, ba, wx, bx, lam)


def _ordered_int_to_float(c):
    return lax.bitcast_convert_type(c ^ (lax.shift_right_arithmetic(c, 31) & jnp.int32(0x7FFFFFFF)), F32)


def _reduce_keys(x, reduce_fn, combine_fn, groups=8):
    rows = x.shape[0]
    step = max(rows // groups, SUBLANES)
    parts = [reduce_fn(x[r:r + step, :], axis=0, keepdims=True) for r in range(0, rows, step)]
    while len(parts) > 1:
        parts = [combine_fn(parts[i], parts[i + 1]) if i + 1 < len(parts) else parts[i]
                 for i in range(0, len(parts), 2)]
    return parts[0]


def _dsa_prepare(q_ref, mk_ref):
    tq = q_ref.shape[1]
    ext = mk_ref.shape[1]
    d0 = ext - tq
    keep = (lax.broadcasted_iota(jnp.int32, (tq, tq), 1)
            <= lax.broadcasted_iota(jnp.int32, (tq, tq), 0))

    mk = mk_ref[0]
    vlane = lax.broadcasted_iota(jnp.int32, (ext, DSA_HEAD_DIM), 1)
    v_aug = jnp.concatenate([mk[:, DSA_HEAD_DIM:2 * DSA_HEAD_DIM],
                             jnp.where(vlane == 0, 1.0, 0.0)], axis=1).astype(BF16)
    ik_b = mk[:, 2 * DSA_HEAD_DIM:2 * DSA_HEAD_DIM + IDX_DIM].astype(BF16)
    kcol = lax.broadcasted_iota(jnp.int32, (ext, 1), 0)
    lane = lax.broadcasted_iota(jnp.int32, (1, LANES), 1)
    pos_hi = (kcol & jnp.int32(-LANES)).astype(F32)
    pos_lo = (kcol & jnp.int32(LANES - 1)).astype(F32)
    k_aug = jnp.where(lane < DSA_HEAD_DIM, mk[:, 0:LANES],
                      jnp.where(lane == DSA_HEAD_DIM, pos_hi,
                                jnp.where(lane == DSA_HEAD_DIM + 1, pos_lo, 0.0))).astype(BF16)
    return d0, keep, ik_b, k_aug, v_aug


def _dsa_select(iq_ref, mq_ref, sc_ref, bias_ref, ik_b, keep, *, topk, between_passes=None):
    tq, ext = sc_ref.shape
    d0 = ext - tq
    kf = float(topk)
    iw0 = 2 * DSA_HEAD_DIM + IDX_DIM
    iw = mq_ref[0][:, iw0:iw0 + IDX_HEADS] * (IDX_HEADS ** -0.5 * IDX_DIM ** -0.5)
    iq = iq_ref[0].astype(BF16)
    sc = None
    for h in range(IDX_HEADS):
        d = _dot_nt(iq[:, h * IDX_DIM:(h + 1) * IDX_DIM], ik_b)
        term = jnp.maximum(d, 0.0) * iw[:, h:h + 1]
        sc = term if sc is None else sc + term
    if d0 > 0:
        sc_ref[:, 0:d0] = sc[:, 0:d0]
    sc_ref[:, d0:ext] = jnp.where(keep, sc[:, d0:ext], MASK_NEG)

    tau = jnp.full((tq, 1), INT32_MIN, jnp.int32)
    for i in range(32):
        cand = tau + jnp.int32(1 << (31 - i) if i else INT32_MIN)
        ge = sc_ref[...] >= _ordered_int_to_float(cand)
        cnt = jnp.sum(jnp.where(ge, 1.0, 0.0), axis=-1, keepdims=True)
        tau = jnp.where(cnt >= kf, cand, tau)
        if between_passes is not None:
            between_passes(i)
    tau = _ordered_int_to_float(tau)

    sc = sc_ref[...]
    gt = sc > tau
    eq = sc == tau
    need = kf - jnp.sum(jnp.where(gt, 1.0, 0.0), axis=-1, keepdims=True)
    eqf = jnp.where(eq, 1.0, 0.0).astype(BF16)
    tri = (lax.broadcasted_iota(jnp.int32, (LANES, LANES), 0)
           <= lax.broadcasted_iota(jnp.int32, (LANES, LANES), 1)).astype(BF16)
    off = jnp.zeros((tq, 1), F32)
    for c in range(ext // LANES):
        sl = slice(c * LANES, (c + 1) * LANES)
        pc = _dot(eqf[:, sl], tri)
        take = gt[:, sl] | (eq[:, sl] & ((pc + off) <= need))
        if c * LANES >= d0:
            take = take & keep[:, c * LANES - d0:(c + 1) * LANES - d0]
        bias_ref[:, sl] = jnp.where(take, 0.0, MASK_NEG)
        off = off + pc[:, LANES - 1:LANES]


def _dsa_attend(q_ref, o_ref, bias_ref, k_aug, v_aug):
    tq = q_ref.shape[1]
    q = q_ref[0]
    tail_lane = lax.broadcasted_iota(jnp.int32, (tq, LANES - DSA_HEAD_DIM), 1)
    for h in range(DSA_HEADS):
        slope = 2.0 ** (-8.0 * (h + 1) / DSA_HEADS)
        tail = jnp.where(tail_lane < 2, slope, 0.0).astype(BF16)
        q_aug = jnp.concatenate([q[:, h * DSA_HEAD_DIM:(h + 1) * DSA_HEAD_DIM], tail], axis=1)
        lg = _dot_nt(q_aug, k_aug) + bias_ref[...]
        p = jnp.exp(lg - jnp.max(lg, axis=-1, keepdims=True))
        o = _dot(p.astype(BF16), v_aug)
        o = o[:, 0:DSA_HEAD_DIM] / o[:, DSA_HEAD_DIM:DSA_HEAD_DIM + 1]
        o_ref[0, :, h * DSA_HEAD_DIM:(h + 1) * DSA_HEAD_DIM] = o.astype(o_ref.dtype)


def _dsa_kernel(*refs, topk, n_tail_in, tf, aliased):
    q_ref, iq_ref, mq_ref, mk_ref = refs[:4]
    n_tail = (2 * n_tail_in + 2 + N_TAIL_PARAMS) if n_tail_in else 0
    tail_refs = refs[4:4 + n_tail]
    rest = refs[4 + n_tail + int(aliased):]
    if n_tail_in:
        o_ref, x_out_ref, sc_ref, bias_ref, acc_ref = rest
        steps = _tail_stages(tail_refs, x_out_ref, acc_ref, n_in=n_tail_in, tf=tf)
    else:
        o_ref, sc_ref, bias_ref = rest
        steps = []
    n_pass = 32

    def between_passes(i):
        while steps and (len(steps_all) - len(steps) + 1) * n_pass <= (i + 1) * len(steps_all):
            steps.pop(0)()

    steps_all = list(steps)
    d0, keep, ik_b, k_aug, v_aug = _dsa_prepare(q_ref, mk_ref)
    if mk_ref.shape[1] > topk:
        _dsa_select(iq_ref, mq_ref, sc_ref, bias_ref, ik_b, keep, topk=topk,
                    between_passes=between_passes)
    else:
        if d0 > 0:
            bias_ref[:, 0:d0] = jnp.zeros((keep.shape[0], d0), F32)
        bias_ref[:, d0:] = jnp.where(keep, 0.0, MASK_NEG)
    while steps:
        steps.pop(0)()
    _dsa_attend(q_ref, o_ref, bias_ref, k_aug, v_aug)


def _dsa_layer(q, iq, misc, rec2, x2, p3, layer, wos, params, tf):
    bsz, s, _ = q.shape
    m, d = x2.shape
    topk = min(DSA_TOPK_MAX, s // 4)
    tq = min(256, s)
    nb = s // tq
    hd = DSA_HEADS * DSA_HEAD_DIM
    att = x_next = None
    for c in range(nb):
        ext = (c + 1) * tq
        fused = c > 0
        dsa_specs = [pl.BlockSpec((1, tq, hd), lambda b, c=c: (b, c, 0)),
                     pl.BlockSpec((1, tq, IDX_HEADS * IDX_DIM), lambda b, c=c: (b, c, 0)),
                     pl.BlockSpec((1, tq, 256), lambda b, c=c: (b, c, 0)),
                     pl.BlockSpec((1, ext, 256), lambda b: (b, 0, 0))]
        args = [q, iq, misc, misc]
        out_specs = [pl.BlockSpec((1, tq, hd), lambda b: (b, 0, 0))]
        out_shape = [jax.ShapeDtypeStruct((bsz, tq, hd), BF16)]
        scratch = [pltpu.VMEM((tq, ext), F32), pltpu.VMEM((tq, ext), F32)]
        aliases = {}
        if fused:
            prev = lambda b, c=c: (b * nb + c - 1, 0)
            acts, act_maps = [rec2, att.reshape(bsz * tq, hd)], [prev, lambda b: (b, 0)]
            dsa_specs += _tail_specs(acts, act_maps, wos, x2, p3, layer, params, prev, tq)
            args += [*acts, *wos, x2, p3, *params]
            if x_next is not None:
                dsa_specs.append(pl.BlockSpec(memory_space=pl.ANY))
                aliases = {len(args): 1}
                args.append(x_next)
            out_specs.append(pl.BlockSpec((tq, d), prev))
            out_shape.append(jax.ShapeDtypeStruct((m, d), F32))
            scratch.append(pltpu.VMEM((tq, d), F32))
        outs = pl.pallas_call(
            functools.partial(_dsa_kernel, topk=topk, n_tail_in=2 if fused else 0, tf=tf,
                              aliased=bool(aliases)),
            grid=(bsz,),
            in_specs=dsa_specs,
            out_specs=out_specs,
            out_shape=out_shape,
            scratch_shapes=scratch,
            input_output_aliases=aliases,
            compiler_params=_cparams(("parallel",)),
            name=f"hy_dsa_c{c}",
        )(*args)
        att = outs[0]
        if fused:
            x_next = outs[1]
    last = lambda b: (b * nb + nb - 1, 0)
    return _tail([rec2, att.reshape(bsz * tq, hd)], [last, lambda b: (b, 0)], wos, x2, p3, layer,
                 params, tq, tf, (bsz,), last, into=x_next)


N_TAIL_PARAMS = 8


def _tail_stages(refs, o_ref, acc_ref, *, n_in, tf):
    a_refs = refs[:n_in]
    wo_refs = refs[n_in:2 * n_in]
    x_ref, p_ref, g1_ref, b1_ref, w1_ref, w2_ref, wg_ref, wp_ref, g2_ref, b2_ref = refs[2 * n_in:]
    state = {}

    def head():
        m = None
        for a_ref, w_ref in zip(a_refs, wo_refs):
            t = _dot(a_ref[...].astype(BF16), w_ref[...])
            m = t if m is None else m + t
        h = _layer_norm(DN_ALPHA * x_ref[...] + m, g1_ref[...], b1_ref[...])
        state["hb"] = h.astype(BF16)
        gate = jax.nn.sigmoid(_dot(state["hb"], wg_ref[...]))
        acc_ref[...] = DN_ALPHA * h + gate * _dot(p_ref[...].astype(BF16), wp_ref[...])

    def mlp_chunk(c):
        a = jnp.maximum(_dot(state["hb"], w1_ref[:, c * tf:(c + 1) * tf]), 0.0)
        acc_ref[...] += _dot((a * a).astype(BF16), w2_ref[c * tf:(c + 1) * tf, :])

    def finish():
        o_ref[...] = _layer_norm(acc_ref[...], g2_ref[...], b2_ref[...])

    chunks = [functools.partial(mlp_chunk, c) for c in range(w1_ref.shape[1] // tf)]
    return [head] + chunks + [finish]


def _tail_kernel(*refs, n_in, tf, aliased):
    n_inputs = 2 * n_in + 2 + N_TAIL_PARAMS
    o_ref, acc_ref = refs[n_inputs + int(aliased):]
    for step in _tail_stages(refs[:n_inputs], o_ref, acc_ref, n_in=n_in, tf=tf):
        step()


def _tail_specs(acts, act_maps, wos, x2, p3, layer, params, row_map, tm):
    resident = lambda a: pl.BlockSpec(a.shape, lambda *i: (0, 0), pipeline_mode=pl.Buffered(1))
    of_layer = lambda a: pl.BlockSpec((None,) + a.shape[1:], lambda *i: (layer, 0, 0),
                                      pipeline_mode=pl.Buffered(1))
    return ([pl.BlockSpec((tm, a.shape[1]), mp) for a, mp in zip(acts, act_maps)]
            + [resident(w) for w in wos]
            + [pl.BlockSpec((tm, x2.shape[1]), row_map),
               pl.BlockSpec((None, tm, p3.shape[2]), lambda *i: (layer,) + tuple(row_map(*i)))]
            + [of_layer(a) for a in params])


def _tail(acts, act_maps, wos, x2, p3, layer, params, tm, tf, grid, row_map, into=None):
    m, d = x2.shape
    in_specs = _tail_specs(acts, act_maps, wos, x2, p3, layer, params, row_map, tm)
    args = [*acts, *wos, x2, p3, *params]
    aliases = {}
    if into is not None:
        in_specs.append(pl.BlockSpec(memory_space=pl.ANY))
        aliases = {len(args): 0}
        args.append(into)
    return pl.pallas_call(
        functools.partial(_tail_kernel, n_in=len(acts), tf=tf, aliased=into is not None),
        grid=grid,
        in_specs=in_specs,
        out_specs=pl.BlockSpec((tm, d), row_map),
        out_shape=jax.ShapeDtypeStruct((m, d), F32),
        scratch_shapes=[pltpu.VMEM((tm, d), F32)],
        input_output_aliases=aliases,
        compiler_params=_cparams(("parallel",) * len(grid)),
        name="layer_tail",
    )(*args)


MLA_Q_SCALE = (MLA_NOPE + MLA_ROPE) ** -0.5 * math.log2(math.e)
ROPE_HALF = MLA_ROPE // 2
NOPE_LO = LANES // 2 - ROPE_HALF


def _rope_group(t, cos_t, sin_t):
    return t * cos_t + pltpu.roll(t, LANES // 2, axis=1) * sin_t


def _mla_proj_kernel(x_ref, wd_ref, qg_ref, kvg_ref, wq_ref, wk_ref, wv_ref,
                     cosq_ref, sinq_ref, cosk_ref, sink_ref, q_ref, k_ref, v_ref):
    xb = x_ref[...].astype(BF16)
    down = _dot(xb, wd_ref[...])
    cq = down[:, 0:MLA_Q_LORA]
    ckv = down[:, MLA_Q_LORA:MLA_Q_LORA + MLA_KV_LORA]
    krg = down[:, MLA_Q_LORA + MLA_KV_LORA:]
    cqn = (cq * lax.rsqrt(jnp.mean(cq * cq, axis=-1, keepdims=True) + RMS_EPS) * qg_ref[...]).astype(BF16)
    ckvn = (ckv * lax.rsqrt(jnp.mean(ckv * ckv, axis=-1, keepdims=True) + RMS_EPS) * kvg_ref[...]).astype(BF16)
    cos_q, sin_q = cosq_ref[...], sinq_ref[...]
    kr = _rope_group(krg, cosk_ref[...], sink_ref[...])
    q = _dot(cqn, wq_ref[...])
    k = _dot(ckvn, wk_ref[...])
    for h in range(MLA_HEADS):
        sl = slice(h * LANES, (h + 1) * LANES)
        q_ref[:, sl] = _rope_group(q[:, sl], cos_q, sin_q).astype(BF16)
        k_ref[:, sl] = (k[:, sl] + kr).astype(BF16)
    v_ref[...] = _dot(ckvn, wv_ref[...]).astype(BF16)


def _mla_proj(x2, wd, qg, kvg, wq, wk, wv, tables, tm, s):
    m, d = x2.shape
    nt = s // tm
    full = lambda a: pl.BlockSpec(a.shape, lambda i: (0, 0))
    tab = lambda: pl.BlockSpec((tm, LANES), lambda i: (i % nt, 0))
    hq = MLA_HEADS * LANES
    hv = MLA_HEADS * MLA_V
    return pl.pallas_call(
        _mla_proj_kernel,
        grid=(m // tm,),
        in_specs=[pl.BlockSpec((tm, d), lambda i: (i, 0)), full(wd), full(qg), full(kvg),
                  full(wq), full(wk), full(wv)] + [tab() for _ in tables],
        out_specs=[pl.BlockSpec((tm, hq), lambda i: (i, 0)),
                   pl.BlockSpec((tm, hq), lambda i: (i, 0)),
                   pl.BlockSpec((tm, hv), lambda i: (i, 0))],
        out_shape=[jax.ShapeDtypeStruct((m, hq), BF16),
                   jax.ShapeDtypeStruct((m, hq), BF16),
                   jax.ShapeDtypeStruct((m, hv), BF16)],
        compiler_params=_cparams(("parallel",)),
        name="mla_proj",
    )(x2, wd, qg, kvg, wq, wk, wv, *tables)


def _mla_attn_kernel(q_ref, k_ref, v_ref, o_ref, ot_ref, *, tq, heads_per_step, key_chunk):
    s = q_ref.shape[1]
    keep_t = (lax.broadcasted_iota(jnp.int32, (tq, tq), 0)
              <= lax.broadcasted_iota(jnp.int32, (tq, tq), 1))
    v_t = jnp.transpose(v_ref[0].astype(F32)).astype(BF16)
    ones_rows = jnp.ones((2 * SUBLANES, s), BF16)
    lhs = [jnp.concatenate([v_t[hh * MLA_V:(hh + 1) * MLA_V, :], ones_rows], axis=0)
           for hh in range(heads_per_step)]
    units = []
    for j in range(s // tq):
        ext = (j + 1) * tq
        starts = list(range(0, ext - tq, key_chunk)) or [0]
        bounds = [(a, b) for a, b in zip(starts, starts[1:] + [ext])]
        for hh in range(heads_per_step):
            units += [(j * tq, hh, k0, k1) for k0, k1 in bounds]

    def logits(r0, hh, k0, k1):
        ql = slice(hh * LANES, (hh + 1) * LANES)
        sc = _dot_nt(k_ref[0, k0:k1, ql], q_ref[0, r0:r0 + tq, ql])
        if k1 == r0 + tq:
            diag = jnp.where(keep_t, sc[k1 - k0 - tq:, :], MASK_NEG)
            sc = diag if k1 - k0 == tq else jnp.concatenate([sc[:k1 - k0 - tq, :], diag], axis=0)
        return sc

    def probs(sc):
        m = _reduce_keys(sc, jnp.max, jnp.maximum)
        return jnp.exp2(sc - m).astype(BF16), m

    running = {}

    def values(r0, hh, k0, k1, p, m):
        acc = _dot(lhs[hh][:, k0:k1], p)
        if k0 > 0:
            m_run, acc_run = running.pop((r0, hh))
            m_new = jnp.maximum(m_run, m)
            acc = acc_run * jnp.exp2(m_run - m_new) + acc * jnp.exp2(m - m_new)
            m = m_new
        if k1 == r0 + tq:
            ot_ref[hh * MLA_V:(hh + 1) * MLA_V, r0:r0 + tq] = acc[0:MLA_V, :] / acc[MLA_V:MLA_V + 1, :]
        else:
            running[(r0, hh)] = (m, acc)

    n = len(units)
    s_cur = logits(*units[0])
    p_prev = None
    for u in range(n + 1):
        s_nxt = logits(*units[u + 1]) if u + 1 < n else None
        p_cur = probs(s_cur) if u < n else None
        if p_prev is not None:
            values(*units[u - 1], *p_prev)
        s_cur, p_prev = s_nxt, p_cur
    o_ref[0] = jnp.transpose(ot_ref[...]).astype(o_ref.dtype)


def _mla_attn(q, k, v, tq):
    bsz, s, _ = q.shape
    hps = 2
    return pl.pallas_call(
        functools.partial(_mla_attn_kernel, tq=tq, heads_per_step=hps, key_chunk=4 * tq),
        grid=(bsz, MLA_HEADS // hps),
        in_specs=[pl.BlockSpec((1, s, hps * LANES), lambda b, h: (b, 0, h)),
                  pl.BlockSpec((1, s, hps * LANES), lambda b, h: (b, 0, h)),
                  pl.BlockSpec((1, s, hps * MLA_V), lambda b, h: (b, 0, h))],
        out_specs=pl.BlockSpec((1, s, hps * MLA_V), lambda b, h: (b, 0, h)),
        out_shape=jax.ShapeDtypeStruct((bsz, s, MLA_HEADS * MLA_V), BF16),
        scratch_shapes=[pltpu.VMEM((hps * MLA_V, s), F32)],
        compiler_params=_cparams(("parallel", "parallel")),
        name="mla_attn",
    )(q, k, v)


def _hy_in_weight(w_in):
    xr, yr, q, k, v, iq, ik, iw = jnp.split(
        w_in, np.cumsum([512, 512, 512, 64, 64, 256, 64, 4])[:-1].tolist(), axis=1)
    pad = jnp.zeros((w_in.shape[0], 256 - 64 * 3 - 4), w_in.dtype)
    return jnp.concatenate([xr, yr, q, iq, k, v, ik, iw, pad], axis=1).astype(BF16)


def _block_diag_groups(w):
    bw = w.shape[-1]
    per = LANES // bw
    ng = w.shape[0] // per
    on_diag = jnp.eye(per, dtype=bool)[None, :, None, :, None]
    out = jnp.where(on_diag, w.reshape(ng, per, bw, 1, bw), 0.0)
    return out.reshape(ng, LANES, LANES).astype(BF16)


def _head_lanes(nope, rope):
    lead = nope.shape[:-1] if nope is not None else rope.shape[:-1]
    dt = nope.dtype if nope is not None else rope.dtype
    z = lambda n: jnp.zeros(lead + (n,), dt)
    r1, r2 = (rope[..., :ROPE_HALF], rope[..., ROPE_HALF:]) if rope is not None else (z(ROPE_HALF), z(ROPE_HALF))
    n1, n2 = (nope[..., :NOPE_LO], nope[..., NOPE_LO:]) if nope is not None else (z(NOPE_LO), z(MLA_NOPE - NOPE_LO))
    return jnp.concatenate([r1, n1, r2, n2, z(LANES - MLA_NOPE - MLA_ROPE)], axis=-1)


def _mla_weights(w_down, w_uq, w_ukv):
    lat = MLA_Q_LORA + MLA_KV_LORA
    wd = jnp.concatenate([w_down[:, :lat], _head_lanes(None, w_down[:, lat:])], axis=1)
    wq = w_uq.reshape(MLA_Q_LORA, MLA_HEADS, MLA_NOPE + MLA_ROPE)
    wq = _head_lanes(wq[:, :, :MLA_NOPE], wq[:, :, MLA_NOPE:]).reshape(MLA_Q_LORA, MLA_HEADS * LANES)
    wkv = w_ukv.reshape(MLA_KV_LORA, MLA_HEADS, MLA_NOPE + MLA_V)
    wk = _head_lanes(wkv[:, :, :MLA_NOPE], None).reshape(MLA_KV_LORA, MLA_HEADS * LANES)
    wv = wkv[:, :, MLA_NOPE:].reshape(MLA_KV_LORA, MLA_HEADS * MLA_V)
    return wd.astype(BF16), wq.astype(BF16), wk.astype(BF16), wv.astype(BF16)


def _rope_tables(s):
    pos = np.arange(s, dtype=np.float64)
    freq = ROPE_BASE ** (-np.arange(0, MLA_ROPE, 2, dtype=np.float64) / MLA_ROPE)
    ang = pos[:, None] * freq[None, :]
    lo = slice(0, ROPE_HALF)
    hi = slice(LANES // 2, LANES // 2 + ROPE_HALF)
    cos_t = np.ones((s, LANES))
    sin_t = np.zeros((s, LANES))
    cos_t[:, lo] = cos_t[:, hi] = np.cos(ang)
    sin_t[:, lo] = -np.sin(ang)
    sin_t[:, hi] = np.sin(ang)
    f32 = lambda a: jnp.asarray(a.astype(np.float32))
    return f32(cos_t * MLA_Q_SCALE), f32(sin_t * MLA_Q_SCALE), f32(cos_t), f32(sin_t)


def _tile_m(m):
    return 512 if m % 512 == 0 else m


def kernel(x, p, ln1_g, ln1_b, ln2_g, ln2_b, mlp_w1, mlp_w2, ple_w_proj, ple_w_gate, hy_w_in, hy_conv_w, hy_conv_b, hy_ga_w, hy_ga_b, hy_gx_w, hy_gx_b, hy_lambda, hy_w_out, mla_w_down, mla_q_norm, mla_kv_norm, mla_w_uq, mla_w_ukv, mla_w_out):
    bsz, s, d = x.shape
    m = bsz * s
    tm = _tile_m(m)
    tf = 1024
    row = lambda a: a.reshape(1, -1)
    x2 = x.reshape(m, d)

    rows3 = lambda a: a.reshape(a.shape[0], 1, -1)
    tail_params = (rows3(ln1_g), rows3(ln1_b), mlp_w1.astype(BF16), mlp_w2.astype(BF16),
                   ple_w_gate.astype(BF16), ple_w_proj.astype(BF16), rows3(ln2_g), rows3(ln2_b))
    p3 = p.reshape(p.shape[0], m, -1)

    xy, q, iq, misc = _inproj(x2, _hy_in_weight(hy_w_in[0]), tm)
    rec = _rglru(xy.reshape(bsz, s, -1), hy_conv_w[0], row(hy_conv_b[0]),
                 _block_diag_groups(hy_ga_w[0]), row(hy_ga_b[0]),
                 _block_diag_groups(hy_gx_w[0]), row(hy_gx_b[0]), row(hy_lambda[0]))
    w_out = hy_w_out[0].astype(BF16)
    x2 = _dsa_layer(q.reshape(bsz, s, -1), iq.reshape(bsz, s, -1), misc.reshape(bsz, s, -1),
                    rec.reshape(m, -1), x2, p3, 0, [w_out[:RNN_WIDTH], w_out[RNN_WIDTH:]], tail_params, tf)

    wd, wq, wk, wv = _mla_weights(mla_w_down[0], mla_w_uq[0], mla_w_ukv[0])
    qp, kp, vp = _mla_proj(x2, wd, row(mla_q_norm[0]), row(mla_kv_norm[0]), wq, wk, wv,
                           _rope_tables(s), min(tm, s), s)
    o = _mla_attn(qp.reshape(bsz, s, -1), kp.reshape(bsz, s, -1), vp.reshape(bsz, s, -1), min(256, s))
    flat = lambda i: (i, 0)
    x2 = _tail([o.reshape(m, -1)], [flat], [mla_w_out[0].astype(BF16)], x2, p3, 1, tail_params,
               tm, tf, (m // tm,), flat)
    return x2.reshape(bsz, s, d)
```

```python
import functools
import math

import numpy as np
import jax
import jax.numpy as jnp
from jax import lax
from jax.experimental import pallas as pl
from jax.experimental.pallas import tpu as pltpu

F32 = jnp.float32
BF16 = jnp.bfloat16

RNN_WIDTH = 512
RNN_BLOCKS = 8
CONV_WIDTH = 4
LRU_C = 8.0
DSA_HEADS = 8
DSA_HEAD_DIM = 64
IDX_HEADS = 4
IDX_DIM = 64
DSA_TOPK_MAX = 256
MLA_HEADS = 16
MLA_Q_LORA = 512
MLA_KV_LORA = 256
MLA_NOPE = 64
MLA_ROPE = 32
MLA_V = 64
ROPE_BASE = 10000.0
DEPTH = 2
DN_ALPHA = (2 * DEPTH) ** 0.25
LN_EPS = 1e-5
RMS_EPS = 1e-6

LANES = 128
SUBLANES = 8
VMEM_LIMIT_BYTES = 56 * 1024 * 1024

MASK_NEG = -3.0e38
INT32_MIN = -(2 ** 31)
STAGE_LAGS = (1, 2, 3)
DSA_Q_SCALE = DSA_HEAD_DIM ** -0.5


def _cparams(sem):
    return pltpu.CompilerParams(dimension_semantics=sem, vmem_limit_bytes=VMEM_LIMIT_BYTES)


def _dot(a, b):
    return jnp.dot(a, b, preferred_element_type=F32)


def _dot_nt(a, b):
    return lax.dot_general(a, b, (((1,), (1,)), ((), ())), preferred_element_type=F32)


def _layer_norm(y, g, b):
    mu = jnp.mean(y, axis=-1, keepdims=True)
    yc = y - mu
    var = jnp.mean(yc * yc, axis=-1, keepdims=True)
    return yc * lax.rsqrt(var + LN_EPS) * g + b


IN_XY = 2 * RNN_WIDTH
IN_Q = DSA_HEADS * DSA_HEAD_DIM
IN_IQ = IDX_HEADS * IDX_DIM
IN_MISC = 2 * LANES
IN_OFFSETS = tuple(int(v) for v in np.cumsum([0, IN_XY, IN_Q, IN_IQ, IN_MISC]))


def _inproj_kernel(x_ref, w_ref, xy_ref, q_ref, iq_ref, misc_ref):
    xb = x_ref[...].astype(BF16)
    cols = [w_ref[:, a:b] for a, b in zip(IN_OFFSETS[:-1], IN_OFFSETS[1:])]
    xy_ref[...] = _dot(xb, cols[0])
    q_ref[...] = (_dot(xb, cols[1]) * DSA_Q_SCALE).astype(BF16)
    iq_ref[...] = _dot(xb, cols[2])
    misc_ref[...] = _dot(xb, cols[3])


def _inproj(x2, w, tm):
    m, d = x2.shape
    n = w.shape[1]
    widths = (IN_XY, IN_Q, IN_IQ, IN_MISC)
    dtypes = (F32, BF16, F32, F32)
    return pl.pallas_call(
        _inproj_kernel,
        grid=(m // tm,),
        in_specs=[pl.BlockSpec((tm, d), lambda i: (i, 0)),
                  pl.BlockSpec((d, n), lambda i: (0, 0))],
        out_specs=[pl.BlockSpec((tm, wd), lambda i: (i, 0)) for wd in widths],
        out_shape=[jax.ShapeDtypeStruct((m, wd), dt) for wd, dt in zip(widths, dtypes)],
        compiler_params=_cparams(("parallel",)),
        name="hy_inproj",
    )(x2, w)


def _rglru_kernel(xr_ref, yr_ref, cw_ref, cb_ref, wa_ref, ba_ref, wx_ref, bx_ref, lam_ref,
                  o_ref, xpad_ref):
    s = xr_ref.shape[1]
    c = xr_ref.shape[2]
    xpad_ref[0:SUBLANES, :] = jnp.zeros((SUBLANES, c), F32)
    xpad_ref[SUBLANES:SUBLANES + s, :] = xr_ref[0]
    xc = None
    for k in range(CONV_WIDTH):
        start = SUBLANES - (CONV_WIDTH - 1) + k
        term = xpad_ref[start:start + s, :] * cw_ref[k:k + 1, :]
        xc = term if xc is None else xc + term
    xc = xc + cb_ref[...]
    xb = xc.astype(BF16)
    r = jax.nn.sigmoid(_dot(xb, wa_ref[0]) + ba_ref[...])
    gi = jax.nn.sigmoid(_dot(xb, wx_ref[0]) + bx_ref[...])
    nl = -lam_ref[...]
    softplus = jnp.maximum(nl, 0.0) + jnp.log1p(jnp.exp(-jnp.abs(nl)))
    log_a = (-LRU_C) * r * softplus
    a = jnp.exp(log_a)
    y = -jnp.tanh(log_a) * (a * a + 1.0)
    h = jnp.where(y > 0.0, y * lax.rsqrt(y), 0.0) * (gi * xc)
    row = lax.broadcasted_iota(jnp.int32, (s, c), 0)
    sh = 1
    while sh < s:
        if sh < SUBLANES:
            valid = row >= sh
            h = a * jnp.where(valid, pltpu.roll(h, sh, axis=0), 0.0) + h
            a = a * jnp.where(valid, pltpu.roll(a, sh, axis=0), 1.0)
        else:
            h = jnp.concatenate([h[:sh], a[sh:] * h[:s - sh] + h[sh:]], axis=0)
            if sh * 2 < s:
                a = jnp.concatenate([a[:sh], a[sh:] * a[:s - sh]], axis=0)
        sh *= 2
    o_ref[0] = (h * jax.nn.gelu(yr_ref[0])).astype(o_ref.dtype)


def _rglru(xy, conv_w, conv_b, wa, ba, wx, bx, lam):
    bsz, s, _ = xy.shape
    c = LANES
    ng = RNN_WIDTH // c
    vec = lambda: pl.BlockSpec((1, c), lambda b, g: (0, g))
    return pl.pallas_call(
        _rglru_kernel,
        grid=(bsz, ng),
        in_specs=[pl.BlockSpec((1, s, c), lambda b, g: (b, 0, g)),
                  pl.BlockSpec((1, s, c), lambda b, g: (b, 0, ng + g)),
                  pl.BlockSpec((CONV_WIDTH, c), lambda b, g: (0, g)),
                  vec(),
                  pl.BlockSpec((1, c, c), lambda b, g: (g, 0, 0)),
                  vec(),
                  pl.BlockSpec((1, c, c), lambda b, g: (g, 0, 0)),
                  vec(),
                  vec()],
        out_specs=pl.BlockSpec((1, s, c), lambda b, g: (b, 0, g)),
        out_shape=jax.ShapeDtypeStruct((bsz, s, RNN_WIDTH), BF16),
        scratch_shapes=[pltpu.VMEM((s + SUBLANES, c), F32)],
        compiler_params=_cparams(("parallel", "parallel")),
        name="hy_rglru",
    )(xy, xy, conv_w, conv_b, wa, ba, wx, bx, lam)


def _ordered_int_to_float(c):
    return lax.bitcast_convert_type(c ^ (lax.shift_right_arithmetic(c, 31) & jnp.int32(0x7FFFFFFF)), F32)


def _reduce_keys(x, reduce_fn, combine_fn, groups=8):
    rows = x.shape[0]
    step = max(rows // groups, SUBLANES)
    parts = [reduce_fn(x[r:r + step, :], axis=0, keepdims=True) for r in range(0, rows, step)]
    while len(parts) > 1:
        parts = [combine_fn(parts[i], parts[i + 1]) if i + 1 < len(parts) else parts[i]
                 for i in range(0, len(parts), 2)]
    return parts[0]


def _dsa_prepare(q_ref, mk_ref):
    tq = q_ref.shape[1]
    ext = mk_ref.shape[1]
    d0 = ext - tq
    keep = (lax.broadcasted_iota(jnp.int32, (tq, tq), 1)
            <= lax.broadcasted_iota(jnp.int32, (tq, tq), 0))

    mk = mk_ref[0]
    vlane = lax.broadcasted_iota(jnp.int32, (ext, DSA_HEAD_DIM), 1)
    v_aug = jnp.concatenate([mk[:, DSA_HEAD_DIM:2 * DSA_HEAD_DIM],
                             jnp.where(vlane == 0, 1.0, 0.0)], axis=1).astype(BF16)
    ik_b = mk[:, 2 * DSA_HEAD_DIM:2 * DSA_HEAD_DIM + IDX_DIM].astype(BF16)
    kcol = lax.broadcasted_iota(jnp.int32, (ext, 1), 0)
    lane = lax.broadcasted_iota(jnp.int32, (1, LANES), 1)
    pos_hi = (kcol & jnp.int32(-LANES)).astype(F32)
    pos_lo = (kcol & jnp.int32(LANES - 1)).astype(F32)
    k_aug = jnp.where(lane < DSA_HEAD_DIM, mk[:, 0:LANES],
                      jnp.where(lane == DSA_HEAD_DIM, pos_hi,
                                jnp.where(lane == DSA_HEAD_DIM + 1, pos_lo, 0.0))).astype(BF16)
    return d0, keep, ik_b, k_aug, v_aug


def _dsa_select(iq_ref, mq_ref, sc_ref, bias_ref, ik_b, keep, *, topk):
    tq, ext = sc_ref.shape
    d0 = ext - tq
    kf = float(topk)
    iw0 = 2 * DSA_HEAD_DIM + IDX_DIM
    iw = mq_ref[0][:, iw0:iw0 + IDX_HEADS] * (IDX_HEADS ** -0.5 * IDX_DIM ** -0.5)
    iq = iq_ref[0].astype(BF16)
    sc = None
    for h in range(IDX_HEADS):
        d = _dot_nt(iq[:, h * IDX_DIM:(h + 1) * IDX_DIM], ik_b)
        term = jnp.maximum(d, 0.0) * iw[:, h:h + 1]
        sc = term if sc is None else sc + term
    if d0 > 0:
        sc_ref[:, 0:d0] = sc[:, 0:d0]
    sc_ref[:, d0:ext] = jnp.where(keep, sc[:, d0:ext], MASK_NEG)

    def search(i, tau):
        cand = tau + lax.shift_left(jnp.int32(1), 31 - i)
        ge = sc_ref[...] >= _ordered_int_to_float(cand)
        cnt = jnp.sum(jnp.where(ge, 1.0, 0.0), axis=-1, keepdims=True)
        return jnp.where(cnt >= kf, cand, tau)

    tau = lax.fori_loop(0, 32, search, jnp.full((tq, 1), INT32_MIN, jnp.int32), unroll=True)
    tau = _ordered_int_to_float(tau)

    sc = sc_ref[...]
    gt = sc > tau
    eq = sc == tau
    need = kf - jnp.sum(jnp.where(gt, 1.0, 0.0), axis=-1, keepdims=True)
    eqf = jnp.where(eq, 1.0, 0.0).astype(BF16)
    tri = (lax.broadcasted_iota(jnp.int32, (LANES, LANES), 0)
           <= lax.broadcasted_iota(jnp.int32, (LANES, LANES), 1)).astype(BF16)
    off = jnp.zeros((tq, 1), F32)
    for c in range(ext // LANES):
        sl = slice(c * LANES, (c + 1) * LANES)
        pc = _dot(eqf[:, sl], tri)
        take = gt[:, sl] | (eq[:, sl] & ((pc + off) <= need))
        if c * LANES >= d0:
            take = take & keep[:, c * LANES - d0:(c + 1) * LANES - d0]
        bias_ref[:, sl] = jnp.where(take, 0.0, MASK_NEG)
        off = off + pc[:, LANES - 1:LANES]


def _dsa_attend(q_ref, o_ref, bias_ref, k_aug, v_aug):
    tq = q_ref.shape[1]
    q = q_ref[0]
    tail_lane = lax.broadcasted_iota(jnp.int32, (tq, LANES - DSA_HEAD_DIM), 1)
    def logits(h):
        slope = 2.0 ** (-8.0 * (h + 1) / DSA_HEADS)
        tail = jnp.where(tail_lane < 2, slope, 0.0).astype(BF16)
        q_aug = jnp.concatenate([q[:, h * DSA_HEAD_DIM:(h + 1) * DSA_HEAD_DIM], tail], axis=1)
        return _dot_nt(q_aug, k_aug) + bias_ref[...]

    def values(h, p):
        o = _dot(p, v_aug)
        o = o[:, 0:DSA_HEAD_DIM] / o[:, DSA_HEAD_DIM:DSA_HEAD_DIM + 1]
        o_ref[0, :, h * DSA_HEAD_DIM:(h + 1) * DSA_HEAD_DIM] = o.astype(o_ref.dtype)

    lg, mx, pr = {}, {}, {}
    d_max, d_exp, d_pv = STAGE_LAGS
    for t in range(DSA_HEADS + d_pv):
        if t < DSA_HEADS:
            lg[t] = logits(t)
        if 0 <= t - d_max < DSA_HEADS:
            mx[t - d_max] = jnp.max(lg[t - d_max], axis=-1, keepdims=True)
        if 0 <= t - d_exp < DSA_HEADS:
            h = t - d_exp
            pr[h] = jnp.exp(lg.pop(h) - mx.pop(h)).astype(BF16)
        if 0 <= t - d_pv < DSA_HEADS:
            values(t - d_pv, pr.pop(t - d_pv))


def _dsa_kernel(q_ref, iq_ref, mq_ref, mk_ref, o_ref, sc_ref, bias_ref, *, topk):
    d0, keep, ik_b, k_aug, v_aug = _dsa_prepare(q_ref, mk_ref)
    if mk_ref.shape[1] > topk:
        _dsa_select(iq_ref, mq_ref, sc_ref, bias_ref, ik_b, keep, topk=topk)
    else:
        if d0 > 0:
            bias_ref[:, 0:d0] = jnp.zeros((keep.shape[0], d0), F32)
        bias_ref[:, d0:] = jnp.where(keep, 0.0, MASK_NEG)
    _dsa_attend(q_ref, o_ref, bias_ref, k_aug, v_aug)


def _dsa(q, iq, misc):
    bsz, s, _ = q.shape
    topk = min(DSA_TOPK_MAX, s // 4)
    tq = min(256, s)
    hd = DSA_HEADS * DSA_HEAD_DIM
    outs = []
    for c in range(s // tq):
        ext = (c + 1) * tq
        outs.append(pl.pallas_call(
            functools.partial(_dsa_kernel, topk=topk),
            grid=(bsz,),
            in_specs=[pl.BlockSpec((1, tq, hd), lambda b, c=c: (b, c, 0)),
                      pl.BlockSpec((1, tq, IDX_HEADS * IDX_DIM), lambda b, c=c: (b, c, 0)),
                      pl.BlockSpec((1, tq, 256), lambda b, c=c: (b, c, 0)),
                      pl.BlockSpec((1, ext, 256), lambda b: (b, 0, 0))],
            out_specs=pl.BlockSpec((1, tq, hd), lambda b: (b, 0, 0)),
            out_shape=jax.ShapeDtypeStruct((bsz, tq, hd), BF16),
            scratch_shapes=[pltpu.VMEM((tq, ext), F32), pltpu.VMEM((tq, ext), F32)],
            compiler_params=_cparams(("parallel",)),
            name=f"hy_dsa_c{c}",
        )(q, iq, misc, misc))
    return outs[0] if len(outs) == 1 else jnp.concatenate(outs, axis=1)


def _tail_kernel(*refs, n_in, tf):
    a_refs = refs[:n_in]
    wo_refs = refs[n_in:2 * n_in]
    (x_ref, p_ref, g1_ref, b1_ref, w1_ref, w2_ref, wg_ref, wp_ref, g2_ref, b2_ref,
     o_ref, acc_ref) = refs[2 * n_in:]
    m = None
    for a_ref, w_ref in zip(a_refs, wo_refs):
        t = _dot(a_ref[...].astype(BF16), w_ref[...])
        m = t if m is None else m + t
    h = _layer_norm(DN_ALPHA * x_ref[...] + m, g1_ref[...], b1_ref[...])
    hb = h.astype(BF16)
    gate = jax.nn.sigmoid(_dot(hb, wg_ref[...]))
    acc_ref[...] = DN_ALPHA * h + gate * _dot(p_ref[...].astype(BF16), wp_ref[...])
    for c in range(w1_ref.shape[1] // tf):
        a = jnp.maximum(_dot(hb, w1_ref[:, c * tf:(c + 1) * tf]), 0.0)
        acc_ref[...] += _dot((a * a).astype(BF16), w2_ref[c * tf:(c + 1) * tf, :])
    o_ref[...] = _layer_norm(acc_ref[...], g2_ref[...], b2_ref[...])


def _tail(acts, wos, x2, p3, layer, g1, b1, w1, w2, wg, wp, g2, b2, tm, tf):
    m, d = x2.shape
    n_in = len(acts)
    rows = lambda a: pl.BlockSpec((tm, a.shape[1]), lambda i: (i, 0))
    resident = lambda a: pl.BlockSpec(a.shape, lambda i: (0, 0), pipeline_mode=pl.Buffered(1))
    of_layer = lambda a: pl.BlockSpec((None,) + a.shape[1:], lambda i: (layer, 0, 0),
                                      pipeline_mode=pl.Buffered(1))
    params = [g1, b1, w1, w2, wg, wp, g2, b2]
    return pl.pallas_call(
        functools.partial(_tail_kernel, n_in=n_in, tf=tf),
        grid=(m // tm,),
        in_specs=([rows(a) for a in acts] + [resident(w) for w in wos]
                  + [rows(x2), pl.BlockSpec((None, tm, p3.shape[2]), lambda i: (layer, i, 0))]
                  + [of_layer(a) for a in params]),
        out_specs=pl.BlockSpec((tm, d), lambda i: (i, 0)),
        out_shape=jax.ShapeDtypeStruct((m, d), F32),
        scratch_shapes=[pltpu.VMEM((tm, d), F32)],
        compiler_params=_cparams(("parallel",)),
        name="layer_tail",
    )(*acts, *wos, x2, p3, *params)


MLA_Q_SCALE = (MLA_NOPE + MLA_ROPE) ** -0.5 * math.log2(math.e)
ROPE_HALF = MLA_ROPE // 2
NOPE_LO = LANES // 2 - ROPE_HALF


def _rope_group(t, cos_t, sin_t):
    return t * cos_t + pltpu.roll(t, LANES // 2, axis=1) * sin_t


def _mla_proj_kernel(x_ref, wd_ref, qg_ref, kvg_ref, wq_ref, wk_ref, wv_ref,
                     cosq_ref, sinq_ref, cosk_ref, sink_ref, q_ref, k_ref, v_ref):
    xb = x_ref[...].astype(BF16)
    down = _dot(xb, wd_ref[...])
    cq = down[:, 0:MLA_Q_LORA]
    ckv = down[:, MLA_Q_LORA:MLA_Q_LORA + MLA_KV_LORA]
    krg = down[:, MLA_Q_LORA + MLA_KV_LORA:]
    cqn = (cq * lax.rsqrt(jnp.mean(cq * cq, axis=-1, keepdims=True) + RMS_EPS) * qg_ref[...]).astype(BF16)
    ckvn = (ckv * lax.rsqrt(jnp.mean(ckv * ckv, axis=-1, keepdims=True) + RMS_EPS) * kvg_ref[...]).astype(BF16)
    cos_q, sin_q = cosq_ref[...], sinq_ref[...]
    kr = _rope_group(krg, cosk_ref[...], sink_ref[...])
    q = _dot(cqn, wq_ref[...])
    k = _dot(ckvn, wk_ref[...])
    for h in range(MLA_HEADS):
        sl = slice(h * LANES, (h + 1) * LANES)
        q_ref[:, sl] = _rope_group(q[:, sl], cos_q, sin_q).astype(BF16)
        k_ref[:, sl] = (k[:, sl] + kr).astype(BF16)
    v_ref[...] = _dot(ckvn, wv_ref[...]).astype(BF16)


def _mla_proj(x2, wd, qg, kvg, wq, wk, wv, tables, tm, s):
    m, d = x2.shape
    nt = s // tm
    full = lambda a: pl.BlockSpec(a.shape, lambda i: (0, 0))
    tab = lambda: pl.BlockSpec((tm, LANES), lambda i: (i % nt, 0))
    hq = MLA_HEADS * LANES
    hv = MLA_HEADS * MLA_V
    return pl.pallas_call(
        _mla_proj_kernel,
        grid=(m // tm,),
        in_specs=[pl.BlockSpec((tm, d), lambda i: (i, 0)), full(wd), full(qg), full(kvg),
                  full(wq), full(wk), full(wv)] + [tab() for _ in tables],
        out_specs=[pl.BlockSpec((tm, hq), lambda i: (i, 0)),
                   pl.BlockSpec((tm, hq), lambda i: (i, 0)),
                   pl.BlockSpec((tm, hv), lambda i: (i, 0))],
        out_shape=[jax.ShapeDtypeStruct((m, hq), BF16),
                   jax.ShapeDtypeStruct((m, hq), BF16),
                   jax.ShapeDtypeStruct((m, hv), BF16)],
        compiler_params=_cparams(("parallel",)),
        name="mla_proj",
    )(x2, wd, qg, kvg, wq, wk, wv, *tables)


def _mla_attn_kernel(q_ref, k_ref, v_ref, o_ref, ot_ref, *, tq, heads_per_step, key_chunk):
    s = q_ref.shape[1]
    keep_t = (lax.broadcasted_iota(jnp.int32, (tq, tq), 0)
              <= lax.broadcasted_iota(jnp.int32, (tq, tq), 1))
    v_t = jnp.transpose(v_ref[0].astype(F32)).astype(BF16)
    ones_rows = jnp.ones((2 * SUBLANES, s), BF16)
    lhs = [jnp.concatenate([v_t[hh * MLA_V:(hh + 1) * MLA_V, :], ones_rows], axis=0)
           for hh in range(heads_per_step)]
    units = []
    for j in range(s // tq):
        ext = (j + 1) * tq
        starts = list(range(0, ext - tq, key_chunk)) or [0]
        bounds = [(a, b) for a, b in zip(starts, starts[1:] + [ext])]
        for hh in range(heads_per_step):
            units += [(j * tq, hh, k0, k1) for k0, k1 in bounds]

    def logits(r0, hh, k0, k1):
        ql = slice(hh * LANES, (hh + 1) * LANES)
        sc = _dot_nt(k_ref[0, k0:k1, ql], q_ref[0, r0:r0 + tq, ql])
        if k1 == r0 + tq:
            diag = jnp.where(keep_t, sc[k1 - k0 - tq:, :], MASK_NEG)
            sc = diag if k1 - k0 == tq else jnp.concatenate([sc[:k1 - k0 - tq, :], diag], axis=0)
        return sc

    def probs(sc):
        m = _reduce_keys(sc, jnp.max, jnp.maximum)
        return jnp.exp2(sc - m).astype(BF16), m

    running = {}

    def values(r0, hh, k0, k1, p, m):
        acc = _dot(lhs[hh][:, k0:k1], p)
        if k0 > 0:
            m_run, acc_run = running.pop((r0, hh))
            m_new = jnp.maximum(m_run, m)
            acc = acc_run * jnp.exp2(m_run - m_new) + acc * jnp.exp2(m - m_new)
            m = m_new
        if k1 == r0 + tq:
            ot_ref[hh * MLA_V:(hh + 1) * MLA_V, r0:r0 + tq] = acc[0:MLA_V, :] / acc[MLA_V:MLA_V + 1, :]
        else:
            running[(r0, hh)] = (m, acc)

    n = len(units)
    stage_s, stage_m, stage_p = {}, {}, {}
    d_max, d_exp, d_pv = STAGE_LAGS
    for t in range(n + d_pv):
        if t < n:
            stage_s[t] = logits(*units[t])
        if 0 <= t - d_max < n:
            u = t - d_max
            stage_m[u] = _reduce_keys(stage_s[u], jnp.max, jnp.maximum)
        if 0 <= t - d_exp < n:
            u = t - d_exp
            stage_p[u] = jnp.exp2(stage_s.pop(u) - stage_m[u]).astype(BF16)
        if 0 <= t - d_pv < n:
            u = t - d_pv
            values(*units[u], stage_p.pop(u), stage_m.pop(u))
    o_ref[0] = jnp.transpose(ot_ref[...]).astype(o_ref.dtype)


def _mla_attn(q, k, v, tq):
    bsz, s, _ = q.shape
    hps = 4
    return pl.pallas_call(
        functools.partial(_mla_attn_kernel, tq=tq, heads_per_step=hps, key_chunk=4 * tq),
        grid=(bsz, MLA_HEADS // hps),
        in_specs=[pl.BlockSpec((1, s, hps * LANES), lambda b, h: (b, 0, h)),
                  pl.BlockSpec((1, s, hps * LANES), lambda b, h: (b, 0, h)),
                  pl.BlockSpec((1, s, hps * MLA_V), lambda b, h: (b, 0, h))],
        out_specs=pl.BlockSpec((1, s, hps * MLA_V), lambda b, h: (b, 0, h)),
        out_shape=jax.ShapeDtypeStruct((bsz, s, MLA_HEADS * MLA_V), BF16),
        scratch_shapes=[pltpu.VMEM((hps * MLA_V, s), F32)],
        compiler_params=_cparams(("parallel", "parallel")),
        name="mla_attn",
    )(q, k, v)


def _hy_in_weight(w_in):
    xr, yr, q, k, v, iq, ik, iw = jnp.split(
        w_in, np.cumsum([512, 512, 512, 64, 64, 256, 64, 4])[:-1].tolist(), axis=1)
    pad = jnp.zeros((w_in.shape[0], 256 - 64 * 3 - 4), w_in.dtype)
    return jnp.concatenate([xr, yr, q, iq, k, v, ik, iw, pad], axis=1).astype(BF16)


def _block_diag_groups(w):
    bw = w.shape[-1]
    per = LANES // bw
    ng = w.shape[0] // per
    on_diag = jnp.eye(per, dtype=bool)[None, :, None, :, None]
    out = jnp.where(on_diag, w.reshape(ng, per, bw, 1, bw), 0.0)
    return out.reshape(ng, LANES, LANES).astype(BF16)


def _head_lanes(nope, rope):
    lead = nope.shape[:-1] if nope is not None else rope.shape[:-1]
    dt = nope.dtype if nope is not None else rope.dtype
    z = lambda n: jnp.zeros(lead + (n,), dt)
    r1, r2 = (rope[..., :ROPE_HALF], rope[..., ROPE_HALF:]) if rope is not None else (z(ROPE_HALF), z(ROPE_HALF))
    n1, n2 = (nope[..., :NOPE_LO], nope[..., NOPE_LO:]) if nope is not None else (z(NOPE_LO), z(MLA_NOPE - NOPE_LO))
    return jnp.concatenate([r1, n1, r2, n2, z(LANES - MLA_NOPE - MLA_ROPE)], axis=-1)


def _mla_weights(w_down, w_uq, w_ukv):
    lat = MLA_Q_LORA + MLA_KV_LORA
    wd = jnp.concatenate([w_down[:, :lat], _head_lanes(None, w_down[:, lat:])], axis=1)
    wq = w_uq.reshape(MLA_Q_LORA, MLA_HEADS, MLA_NOPE + MLA_ROPE)
    wq = _head_lanes(wq[:, :, :MLA_NOPE], wq[:, :, MLA_NOPE:]).reshape(MLA_Q_LORA, MLA_HEADS * LANES)
    wkv = w_ukv.reshape(MLA_KV_LORA, MLA_HEADS, MLA_NOPE + MLA_V)
    wk = _head_lanes(wkv[:, :, :MLA_NOPE], None).reshape(MLA_KV_LORA, MLA_HEADS * LANES)
    wv = wkv[:, :, MLA_NOPE:].reshape(MLA_KV_LORA, MLA_HEADS * MLA_V)
    return wd.astype(BF16), wq.astype(BF16), wk.astype(BF16), wv.astype(BF16)


def _rope_tables(s):
    pos = np.arange(s, dtype=np.float64)
    freq = ROPE_BASE ** (-np.arange(0, MLA_ROPE, 2, dtype=np.float64) / MLA_ROPE)
    ang = pos[:, None] * freq[None, :]
    lo = slice(0, ROPE_HALF)
    hi = slice(LANES // 2, LANES // 2 + ROPE_HALF)
    cos_t = np.ones((s, LANES))
    sin_t = np.zeros((s, LANES))
    cos_t[:, lo] = cos_t[:, hi] = np.cos(ang)
    sin_t[:, lo] = -np.sin(ang)
    sin_t[:, hi] = np.sin(ang)
    f32 = lambda a: jnp.asarray(a.astype(np.float32))
    return f32(cos_t * MLA_Q_SCALE), f32(sin_t * MLA_Q_SCALE), f32(cos_t), f32(sin_t)


def _tile_m(m):
    return 512 if m % 512 == 0 else m


def kernel(x, p, ln1_g, ln1_b, ln2_g, ln2_b, mlp_w1, mlp_w2, ple_w_proj, ple_w_gate, hy_w_in, hy_conv_w, hy_conv_b, hy_ga_w, hy_ga_b, hy_gx_w, hy_gx_b, hy_lambda, hy_w_out, mla_w_down, mla_q_norm, mla_kv_norm, mla_w_uq, mla_w_ukv, mla_w_out):
    bsz, s, d = x.shape
    m = bsz * s
    tm = _tile_m(m)
    tf = 1024
    row = lambda a: a.reshape(1, -1)
    x2 = x.reshape(m, d)

    rows3 = lambda a: a.reshape(a.shape[0], 1, -1)
    tail_params = (rows3(ln1_g), rows3(ln1_b), mlp_w1.astype(BF16), mlp_w2.astype(BF16),
                   ple_w_gate.astype(BF16), ple_w_proj.astype(BF16), rows3(ln2_g), rows3(ln2_b))
    p3 = p.reshape(p.shape[0], m, -1)

    def layer_tail(acts, wos, x2, i):
        return _tail(acts, wos, x2, p3, i, *tail_params, tm, tf)

    xy, q, iq, misc = _inproj(x2, _hy_in_weight(hy_w_in[0]), tm)
    rec = _rglru(xy.reshape(bsz, s, -1), hy_conv_w[0], row(hy_conv_b[0]),
                 _block_diag_groups(hy_ga_w[0]), row(hy_ga_b[0]),
                 _block_diag_groups(hy_gx_w[0]), row(hy_gx_b[0]), row(hy_lambda[0]))
    att = _dsa(q.reshape(bsz, s, -1), iq.reshape(bsz, s, -1), misc.reshape(bsz, s, -1))
    w_out = hy_w_out[0].astype(BF16)
    x2 = layer_tail([rec.reshape(m, -1), att.reshape(m, -1)], [w_out[:RNN_WIDTH], w_out[RNN_WIDTH:]], x2, 0)

    wd, wq, wk, wv = _mla_weights(mla_w_down[0], mla_w_uq[0], mla_w_ukv[0])
    qp, kp, vp = _mla_proj(x2, wd, row(mla_q_norm[0]), row(mla_kv_norm[0]), wq, wk, wv,
                           _rope_tables(s), min(tm, s), s)
    o = _mla_attn(qp.reshape(bsz, s, -1), kp.reshape(bsz, s, -1), vp.reshape(bsz, s, -1), min(256, s))
    x2 = layer_tail([o.reshape(m, -1)], [mla_w_out[0].astype(BF16)], x2, 1)
    return x2.reshape(bsz, s, d)
```

```python
import functools
import math

import numpy as np
import jax
import jax.numpy as jnp
from jax import lax
from jax.experimental import pallas as pl
from jax.experimental.pallas import tpu as pltpu

F32 = jnp.float32
BF16 = jnp.bfloat16

RNN_WIDTH = 512
RNN_BLOCKS = 8
CONV_WIDTH = 4
LRU_C = 8.0
DSA_HEADS = 8
DSA_HEAD_DIM = 64
IDX_HEADS = 4
IDX_DIM = 64
DSA_TOPK_MAX = 256
MLA_HEADS = 16
MLA_Q_LORA = 512
MLA_KV_LORA = 256
MLA_NOPE = 64
MLA_ROPE = 32
MLA_V = 64
ROPE_BASE = 10000.0
DEPTH = 2
DN_ALPHA = (2 * DEPTH) ** 0.25
LN_EPS = 1e-5
RMS_EPS = 1e-6

LANES = 128
SUBLANES = 8
VMEM_LIMIT_BYTES = 56 * 1024 * 1024

MASK_NEG = -3.0e38
INT32_MIN = -(2 ** 31)
STAGE_LAGS = (1, 2, 3)
DSA_Q_SCALE = DSA_HEAD_DIM ** -0.5


def _cparams(sem):
    return pltpu.CompilerParams(dimension_semantics=sem, vmem_limit_bytes=VMEM_LIMIT_BYTES)


def _dot(a, b):
    return jnp.dot(a, b, preferred_element_type=F32)


def _dot_nt(a, b):
    return lax.dot_general(a, b, (((1,), (1,)), ((), ())), preferred_element_type=F32)


def _layer_norm(y, g, b):
    mu = jnp.mean(y, axis=-1, keepdims=True)
    yc = y - mu
    var = jnp.mean(yc * yc, axis=-1, keepdims=True)
    return yc * lax.rsqrt(var + LN_EPS) * g + b


IN_XY = 2 * RNN_WIDTH
IN_Q = DSA_HEADS * DSA_HEAD_DIM
IN_IQ = IDX_HEADS * IDX_DIM
IN_MISC = 2 * LANES
IN_OFFSETS = tuple(int(v) for v in np.cumsum([0, IN_XY, IN_Q, IN_IQ, IN_MISC]))


def _inproj_kernel(x_ref, w_ref, xy_ref, q_ref, iq_ref, misc_ref):
    xb = x_ref[...].astype(BF16)
    cols = [w_ref[:, a:b] for a, b in zip(IN_OFFSETS[:-1], IN_OFFSETS[1:])]
    xy_ref[...] = _dot(xb, cols[0])
    q_ref[...] = (_dot(xb, cols[1]) * DSA_Q_SCALE).astype(BF16)
    iq_ref[...] = _dot(xb, cols[2])
    misc_ref[...] = _dot(xb, cols[3])


def _inproj(x2, w, tm):
    m, d = x2.shape
    n = w.shape[1]
    widths = (IN_XY, IN_Q, IN_IQ, IN_MISC)
    dtypes = (F32, BF16, F32, F32)
    return pl.pallas_call(
        _inproj_kernel,
        grid=(m // tm,),
        in_specs=[pl.BlockSpec((tm, d), lambda i: (i, 0)),
                  pl.BlockSpec((d, n), lambda i: (0, 0))],
        out_specs=[pl.BlockSpec((tm, wd), lambda i: (i, 0)) for wd in widths],
        out_shape=[jax.ShapeDtypeStruct((m, wd), dt) for wd, dt in zip(widths, dtypes)],
        compiler_params=_cparams(("parallel",)),
        name="hy_inproj",
    )(x2, w)


def _rglru_kernel(xr_ref, yr_ref, cw_ref, cb_ref, wa_ref, ba_ref, wx_ref, bx_ref, lam_ref,
                  o_ref, xpad_ref):
    s = xr_ref.shape[1]
    c = xr_ref.shape[2]
    xpad_ref[0:SUBLANES, :] = jnp.zeros((SUBLANES, c), F32)
    xpad_ref[SUBLANES:SUBLANES + s, :] = xr_ref[0]
    xc = None
    for k in range(CONV_WIDTH):
        start = SUBLANES - (CONV_WIDTH - 1) + k
        term = xpad_ref[start:start + s, :] * cw_ref[k:k + 1, :]
        xc = term if xc is None else xc + term
    xc = xc + cb_ref[...]
    xb = xc.astype(BF16)
    r = jax.nn.sigmoid(_dot(xb, wa_ref[0]) + ba_ref[...])
    gi = jax.nn.sigmoid(_dot(xb, wx_ref[0]) + bx_ref[...])
    nl = -lam_ref[...]
    softplus = jnp.maximum(nl, 0.0) + jnp.log1p(jnp.exp(-jnp.abs(nl)))
    log_a = (-LRU_C) * r * softplus
    a = jnp.exp(log_a)
    y = -jnp.tanh(log_a) * (a * a + 1.0)
    h = jnp.where(y > 0.0, y * lax.rsqrt(y), 0.0) * (gi * xc)
    row = lax.broadcasted_iota(jnp.int32, (s, c), 0)
    sh = 1
    while sh < s:
        if sh < SUBLANES:
            valid = row >= sh
            h = a * jnp.where(valid, pltpu.roll(h, sh, axis=0), 0.0) + h
            a = a * jnp.where(valid, pltpu.roll(a, sh, axis=0), 1.0)
        else:
            h = jnp.concatenate([h[:sh], a[sh:] * h[:s - sh] + h[sh:]], axis=0)
            if sh * 2 < s:
                a = jnp.concatenate([a[:sh], a[sh:] * a[:s - sh]], axis=0)
        sh *= 2
    o_ref[0] = (h * jax.nn.gelu(yr_ref[0])).astype(o_ref.dtype)


def _rglru(xy, conv_w, conv_b, wa, ba, wx, bx, lam):
    bsz, s, _ = xy.shape
    c = LANES
    ng = RNN_WIDTH // c
    vec = lambda: pl.BlockSpec((1, c), lambda b, g: (0, g))
    return pl.pallas_call(
        _rglru_kernel,
        grid=(bsz, ng),
        in_specs=[pl.BlockSpec((1, s, c), lambda b, g: (b, 0, g)),
                  pl.BlockSpec((1, s, c), lambda b, g: (b, 0, ng + g)),
                  pl.BlockSpec((CONV_WIDTH, c), lambda b, g: (0, g)),
                  vec(),
                  pl.BlockSpec((1, c, c), lambda b, g: (g, 0, 0)),
                  vec(),
                  pl.BlockSpec((1, c, c), lambda b, g: (g, 0, 0)),
                  vec(),
                  vec()],
        out_specs=pl.BlockSpec((1, s, c), lambda b, g: (b, 0, g)),
        out_shape=jax.ShapeDtypeStruct((bsz, s, RNN_WIDTH), BF16),
        scratch_shapes=[pltpu.VMEM((s + SUBLANES, c), F32)],
        compiler_params=_cparams(("parallel", "parallel")),
        name="hy_rglru",
    )(xy, xy, conv_w, conv_b, wa, ba, wx, bx, lam)


def _ordered_int_to_float(c):
    return lax.bitcast_convert_type(c ^ (lax.shift_right_arithmetic(c, 31) & jnp.int32(0x7FFFFFFF)), F32)


def _reduce_keys(x, reduce_fn, combine_fn, groups=8):
    rows = x.shape[0]
    step = max(rows // groups, SUBLANES)
    parts = [reduce_fn(x[r:r + step, :], axis=0, keepdims=True) for r in range(0, rows, step)]
    while len(parts) > 1:
        parts = [combine_fn(parts[i], parts[i + 1]) if i + 1 < len(parts) else parts[i]
                 for i in range(0, len(parts), 2)]
    return parts[0]


def _dsa_prepare(q_ref, mk_ref):
    tq = q_ref.shape[1]
    ext = mk_ref.shape[1]
    d0 = ext - tq
    keep = (lax.broadcasted_iota(jnp.int32, (tq, tq), 1)
            <= lax.broadcasted_iota(jnp.int32, (tq, tq), 0))

    mk = mk_ref[0]
    vlane = lax.broadcasted_iota(jnp.int32, (ext, DSA_HEAD_DIM), 1)
    v_aug = jnp.concatenate([mk[:, DSA_HEAD_DIM:2 * DSA_HEAD_DIM],
                             jnp.where(vlane == 0, 1.0, 0.0)], axis=1).astype(BF16)
    ik_b = mk[:, 2 * DSA_HEAD_DIM:2 * DSA_HEAD_DIM + IDX_DIM].astype(BF16)
    kcol = lax.broadcasted_iota(jnp.int32, (ext, 1), 0)
    lane = lax.broadcasted_iota(jnp.int32, (1, LANES), 1)
    pos_hi = (kcol & jnp.int32(-LANES)).astype(F32)
    pos_lo = (kcol & jnp.int32(LANES - 1)).astype(F32)
    k_aug = jnp.where(lane < DSA_HEAD_DIM, mk[:, 0:LANES],
                      jnp.where(lane == DSA_HEAD_DIM, pos_hi,
                                jnp.where(lane == DSA_HEAD_DIM + 1, pos_lo, 0.0))).astype(BF16)
    return d0, keep, ik_b, k_aug, v_aug


def _dsa_select(iq_ref, mq_ref, sc_ref, bias_ref, ik_b, keep, *, topk):
    tq, ext = sc_ref.shape
    d0 = ext - tq
    kf = float(topk)
    iw0 = 2 * DSA_HEAD_DIM + IDX_DIM
    iw = mq_ref[0][:, iw0:iw0 + IDX_HEADS] * (IDX_HEADS ** -0.5 * IDX_DIM ** -0.5)
    iq = iq_ref[0].astype(BF16)
    sc = None
    for h in range(IDX_HEADS):
        d = _dot_nt(iq[:, h * IDX_DIM:(h + 1) * IDX_DIM], ik_b)
        term = jnp.maximum(d, 0.0) * iw[:, h:h + 1]
        sc = term if sc is None else sc + term
    if d0 > 0:
        sc_ref[:, 0:d0] = sc[:, 0:d0]
    sc_ref[:, d0:ext] = jnp.where(keep, sc[:, d0:ext], MASK_NEG)

    def search(i, tau):
        cand = tau + lax.shift_left(jnp.int32(1), 31 - i)
        ge = sc_ref[...] >= _ordered_int_to_float(cand)
        cnt = jnp.sum(jnp.where(ge, 1.0, 0.0), axis=-1, keepdims=True)
        return jnp.where(cnt >= kf, cand, tau)

    tau = lax.fori_loop(0, 32, search, jnp.full((tq, 1), INT32_MIN, jnp.int32), unroll=True)
    tau = _ordered_int_to_float(tau)

    sc = sc_ref[...]
    gt = sc > tau
    eq = sc == tau
    need = kf - jnp.sum(jnp.where(gt, 1.0, 0.0), axis=-1, keepdims=True)
    eqf = jnp.where(eq, 1.0, 0.0).astype(BF16)
    tri = (lax.broadcasted_iota(jnp.int32, (LANES, LANES), 0)
           <= lax.broadcasted_iota(jnp.int32, (LANES, LANES), 1)).astype(BF16)
    off = jnp.zeros((tq, 1), F32)
    for c in range(ext // LANES):
        sl = slice(c * LANES, (c + 1) * LANES)
        pc = _dot(eqf[:, sl], tri)
        take = gt[:, sl] | (eq[:, sl] & ((pc + off) <= need))
        if c * LANES >= d0:
            take = take & keep[:, c * LANES - d0:(c + 1) * LANES - d0]
        bias_ref[:, sl] = jnp.where(take, 0.0, MASK_NEG)
        off = off + pc[:, LANES - 1:LANES]


def _dsa_attend(q_ref, o_ref, bias_ref, k_aug, v_aug):
    tq = q_ref.shape[1]
    q = q_ref[0]
    tail_lane = lax.broadcasted_iota(jnp.int32, (tq, LANES - DSA_HEAD_DIM), 1)
    def logits(h):
        slope = 2.0 ** (-8.0 * (h + 1) / DSA_HEADS)
        tail = jnp.where(tail_lane < 2, slope, 0.0).astype(BF16)
        q_aug = jnp.concatenate([q[:, h * DSA_HEAD_DIM:(h + 1) * DSA_HEAD_DIM], tail], axis=1)
        return _dot_nt(q_aug, k_aug) + bias_ref[...]

    def values(h, p):
        o = _dot(p, v_aug)
        o = o[:, 0:DSA_HEAD_DIM] / o[:, DSA_HEAD_DIM:DSA_HEAD_DIM + 1]
        o_ref[0, :, h * DSA_HEAD_DIM:(h + 1) * DSA_HEAD_DIM] = o.astype(o_ref.dtype)

    lg, mx, pr = {}, {}, {}
    d_max, d_exp, d_pv = STAGE_LAGS
    for t in range(DSA_HEADS + d_pv):
        if t < DSA_HEADS:
            lg[t] = logits(t)
        if 0 <= t - d_max < DSA_HEADS:
            mx[t - d_max] = jnp.max(lg[t - d_max], axis=-1, keepdims=True)
        if 0 <= t - d_exp < DSA_HEADS:
            h = t - d_exp
            pr[h] = jnp.exp(lg.pop(h) - mx.pop(h)).astype(BF16)
        if 0 <= t - d_pv < DSA_HEADS:
            values(t - d_pv, pr.pop(t - d_pv))


def _dsa_kernel(q_ref, iq_ref, mq_ref, mk_ref, o_ref, sc_ref, bias_ref, *, topk):
    d0, keep, ik_b, k_aug, v_aug = _dsa_prepare(q_ref, mk_ref)
    if mk_ref.shape[1] > topk:
        _dsa_select(iq_ref, mq_ref, sc_ref, bias_ref, ik_b, keep, topk=topk)
    else:
        if d0 > 0:
            bias_ref[:, 0:d0] = jnp.zeros((keep.shape[0], d0), F32)
        bias_ref[:, d0:] = jnp.where(keep, 0.0, MASK_NEG)
    _dsa_attend(q_ref, o_ref, bias_ref, k_aug, v_aug)


def _dsa(q, iq, misc):
    bsz, s, _ = q.shape
    topk = min(DSA_TOPK_MAX, s // 4)
    tq = min(256, s)
    hd = DSA_HEADS * DSA_HEAD_DIM
    outs = []
    for c in range(s // tq):
        ext = (c + 1) * tq
        outs.append(pl.pallas_call(
            functools.partial(_dsa_kernel, topk=topk),
            grid=(bsz,),
            in_specs=[pl.BlockSpec((1, tq, hd), lambda b, c=c: (b, c, 0)),
                      pl.BlockSpec((1, tq, IDX_HEADS * IDX_DIM), lambda b, c=c: (b, c, 0)),
                      pl.BlockSpec((1, tq, 256), lambda b, c=c: (b, c, 0)),
                      pl.BlockSpec((1, ext, 256), lambda b: (b, 0, 0))],
            out_specs=pl.BlockSpec((1, tq, hd), lambda b: (b, 0, 0)),
            out_shape=jax.ShapeDtypeStruct((bsz, tq, hd), BF16),
            scratch_shapes=[pltpu.VMEM((tq, ext), F32), pltpu.VMEM((tq, ext), F32)],
            compiler_params=_cparams(("parallel",)),
            name=f"hy_dsa_c{c}",
        )(q, iq, misc, misc))
    return outs[0] if len(outs) == 1 else jnp.concatenate(outs, axis=1)


def _tail_kernel(*refs, n_in, tf):
    a_refs = refs[:n_in]
    wo_refs = refs[n_in:2 * n_in]
    (x_ref, p_ref, g1_ref, b1_ref, w1_ref, w2_ref, wg_ref, wp_ref, g2_ref, b2_ref,
     o_ref, acc_ref) = refs[2 * n_in:]
    n_chunks = w1_ref.shape[1] // tf

    def steps_for(rows):
        st = {}

        def out_proj():
            m = None
            for a_ref, w_ref in zip(a_refs, wo_refs):
                t = _dot(a_ref[rows, :].astype(BF16), w_ref[...])
                m = t if m is None else m + t
            st["y"] = DN_ALPHA * x_ref[rows, :] + m

        def norm1():
            st["h"] = _layer_norm(st.pop("y"), g1_ref[...], b1_ref[...])
            st["hb"] = st["h"].astype(BF16)

        def gate():
            g = jax.nn.sigmoid(_dot(st["hb"], wg_ref[...]))
            acc_ref[rows, :] = DN_ALPHA * st.pop("h") + g * _dot(p_ref[rows, :].astype(BF16), wp_ref[...])

        def mlp(c):
            a = jnp.maximum(_dot(st["hb"], w1_ref[:, c * tf:(c + 1) * tf]), 0.0)
            acc_ref[rows, :] += _dot((a * a).astype(BF16), w2_ref[c * tf:(c + 1) * tf, :])

        def norm2():
            o_ref[rows, :] = _layer_norm(acc_ref[rows, :], g2_ref[...], b2_ref[...])

        return [out_proj, norm1, gate] + [functools.partial(mlp, c) for c in range(n_chunks)] + [norm2]

    half = x_ref.shape[0] // 2
    first, second = steps_for(slice(0, half)), steps_for(slice(half, 2 * half))
    lag = 2
    for t in range(len(first) + lag):
        if t < len(first):
            first[t]()
        if 0 <= t - lag < len(second):
            second[t - lag]()


def _tail(acts, wos, x2, p3, layer, g1, b1, w1, w2, wg, wp, g2, b2, tm, tf):
    m, d = x2.shape
    n_in = len(acts)
    rows = lambda a: pl.BlockSpec((tm, a.shape[1]), lambda i: (i, 0))
    resident = lambda a: pl.BlockSpec(a.shape, lambda i: (0, 0), pipeline_mode=pl.Buffered(1))
    of_layer = lambda a: pl.BlockSpec((None,) + a.shape[1:], lambda i: (layer, 0, 0),
                                      pipeline_mode=pl.Buffered(1))
    params = [g1, b1, w1, w2, wg, wp, g2, b2]
    return pl.pallas_call(
        functools.partial(_tail_kernel, n_in=n_in, tf=tf),
        grid=(m // tm,),
        in_specs=([rows(a) for a in acts] + [resident(w) for w in wos]
                  + [rows(x2), pl.BlockSpec((None, tm, p3.shape[2]), lambda i: (layer, i, 0))]
                  + [of_layer(a) for a in params]),
        out_specs=pl.BlockSpec((tm, d), lambda i: (i, 0)),
        out_shape=jax.ShapeDtypeStruct((m, d), F32),
        scratch_shapes=[pltpu.VMEM((tm, d), F32)],
        compiler_params=_cparams(("parallel",)),
        name="layer_tail",
    )(*acts, *wos, x2, p3, *params)


MLA_Q_SCALE = (MLA_NOPE + MLA_ROPE) ** -0.5 * math.log2(math.e)
ROPE_HALF = MLA_ROPE // 2
NOPE_LO = LANES // 2 - ROPE_HALF


def _rope_group(t, cos_t, sin_t):
    return t * cos_t + pltpu.roll(t, LANES // 2, axis=1) * sin_t


def _mla_proj_kernel(x_ref, wd_ref, qg_ref, kvg_ref, wq_ref, wk_ref, wv_ref,
                     cosq_ref, sinq_ref, cosk_ref, sink_ref, q_ref, k_ref, v_ref):
    xb = x_ref[...].astype(BF16)
    down = _dot(xb, wd_ref[...])
    cq = down[:, 0:MLA_Q_LORA]
    ckv = down[:, MLA_Q_LORA:MLA_Q_LORA + MLA_KV_LORA]
    krg = down[:, MLA_Q_LORA + MLA_KV_LORA:]
    cqn = (cq * lax.rsqrt(jnp.mean(cq * cq, axis=-1, keepdims=True) + RMS_EPS) * qg_ref[...]).astype(BF16)
    ckvn = (ckv * lax.rsqrt(jnp.mean(ckv * ckv, axis=-1, keepdims=True) + RMS_EPS) * kvg_ref[...]).astype(BF16)
    cos_q, sin_q = cosq_ref[...], sinq_ref[...]
    kr = _rope_group(krg, cosk_ref[...], sink_ref[...])
    q = _dot(cqn, wq_ref[...])
    k = _dot(ckvn, wk_ref[...])
    for h in range(MLA_HEADS):
        sl = slice(h * LANES, (h + 1) * LANES)
        q_ref[:, sl] = _rope_group(q[:, sl], cos_q, sin_q).astype(BF16)
        k_ref[:, sl] = (k[:, sl] + kr).astype(BF16)
    v_ref[...] = _dot(ckvn, wv_ref[...]).astype(BF16)


def _mla_proj(x2, wd, qg, kvg, wq, wk, wv, tables, tm, s):
    m, d = x2.shape
    nt = s // tm
    full = lambda a: pl.BlockSpec(a.shape, lambda i: (0, 0))
    tab = lambda: pl.BlockSpec((tm, LANES), lambda i: (i % nt, 0))
    hq = MLA_HEADS * LANES
    hv = MLA_HEADS * MLA_V
    return pl.pallas_call(
        _mla_proj_kernel,
        grid=(m // tm,),
        in_specs=[pl.BlockSpec((tm, d), lambda i: (i, 0)), full(wd), full(qg), full(kvg),
                  full(wq), full(wk), full(wv)] + [tab() for _ in tables],
        out_specs=[pl.BlockSpec((tm, hq), lambda i: (i, 0)),
                   pl.BlockSpec((tm, hq), lambda i: (i, 0)),
                   pl.BlockSpec((tm, hv), lambda i: (i, 0))],
        out_shape=[jax.ShapeDtypeStruct((m, hq), BF16),
                   jax.ShapeDtypeStruct((m, hq), BF16),
                   jax.ShapeDtypeStruct((m, hv), BF16)],
        compiler_params=_cparams(("parallel",)),
        name="mla_proj",
    )(x2, wd, qg, kvg, wq, wk, wv, *tables)


def _mla_attn_kernel(q_ref, k_ref, v_ref, o_ref, ot_ref, *, tq, heads_per_step, key_chunk):
    s = q_ref.shape[1]
    keep_t = (lax.broadcasted_iota(jnp.int32, (tq, tq), 0)
              <= lax.broadcasted_iota(jnp.int32, (tq, tq), 1))
    v_t = jnp.transpose(v_ref[0].astype(F32)).astype(BF16)
    ones_rows = jnp.ones((2 * SUBLANES, s), BF16)
    lhs = [jnp.concatenate([v_t[hh * MLA_V:(hh + 1) * MLA_V, :], ones_rows], axis=0)
           for hh in range(heads_per_step)]
    units = []
    for j in range(s // tq):
        ext = (j + 1) * tq
        starts = list(range(0, ext - tq, key_chunk)) or [0]
        bounds = [(a, b) for a, b in zip(starts, starts[1:] + [ext])]
        for hh in range(heads_per_step):
            units += [(j * tq, hh, k0, k1) for k0, k1 in bounds]

    def logits(r0, hh, k0, k1):
        ql = slice(hh * LANES, (hh + 1) * LANES)
        sc = _dot_nt(k_ref[0, k0:k1, ql], q_ref[0, r0:r0 + tq, ql])
        if k1 == r0 + tq:
            diag = jnp.where(keep_t, sc[k1 - k0 - tq:, :], MASK_NEG)
            sc = diag if k1 - k0 == tq else jnp.concatenate([sc[:k1 - k0 - tq, :], diag], axis=0)
        return sc

    def probs(sc):
        m = _reduce_keys(sc, jnp.max, jnp.maximum)
        return jnp.exp2(sc - m).astype(BF16), m

    running = {}

    def values(r0, hh, k0, k1, p, m):
        acc = _dot(lhs[hh][:, k0:k1], p)
        if k0 > 0:
            m_run, acc_run = running.pop((r0, hh))
            m_new = jnp.maximum(m_run, m)
            acc = acc_run * jnp.exp2(m_run - m_new) + acc * jnp.exp2(m - m_new)
            m = m_new
        if k1 == r0 + tq:
            ot_ref[hh * MLA_V:(hh + 1) * MLA_V, r0:r0 + tq] = acc[0:MLA_V, :] / acc[MLA_V:MLA_V + 1, :]
        else:
            running[(r0, hh)] = (m, acc)

    n = len(units)
    stage_s, stage_m, stage_p = {}, {}, {}
    d_max, d_exp, d_pv = STAGE_LAGS
    for t in range(n + d_pv):
        if t < n:
            stage_s[t] = logits(*units[t])
        if 0 <= t - d_max < n:
            u = t - d_max
            stage_m[u] = _reduce_keys(stage_s[u], jnp.max, jnp.maximum)
        if 0 <= t - d_exp < n:
            u = t - d_exp
            stage_p[u] = jnp.exp2(stage_s.pop(u) - stage_m[u]).astype(BF16)
        if 0 <= t - d_pv < n:
            u = t - d_pv
            values(*units[u], stage_p.pop(u), stage_m.pop(u))
    o_ref[0] = jnp.transpose(ot_ref[...]).astype(o_ref.dtype)


def _mla_attn(q, k, v, tq):
    bsz, s, _ = q.shape
    hps = 4
    return pl.pallas_call(
        functools.partial(_mla_attn_kernel, tq=tq, heads_per_step=hps, key_chunk=4 * tq),
        grid=(bsz, MLA_HEADS // hps),
        in_specs=[pl.BlockSpec((1, s, hps * LANES), lambda b, h: (b, 0, h)),
                  pl.BlockSpec((1, s, hps * LANES), lambda b, h: (b, 0, h)),
                  pl.BlockSpec((1, s, hps * MLA_V), lambda b, h: (b, 0, h))],
        out_specs=pl.BlockSpec((1, s, hps * MLA_V), lambda b, h: (b, 0, h)),
        out_shape=jax.ShapeDtypeStruct((bsz, s, MLA_HEADS * MLA_V), BF16),
        scratch_shapes=[pltpu.VMEM((hps * MLA_V, s), F32)],
        compiler_params=_cparams(("parallel", "parallel")),
        name="mla_attn",
    )(q, k, v)


def _hy_in_weight(w_in):
    xr, yr, q, k, v, iq, ik, iw = jnp.split(
        w_in, np.cumsum([512, 512, 512, 64, 64, 256, 64, 4])[:-1].tolist(), axis=1)
    pad = jnp.zeros((w_in.shape[0], 256 - 64 * 3 - 4), w_in.dtype)
    return jnp.concatenate([xr, yr, q, iq, k, v, ik, iw, pad], axis=1).astype(BF16)


def _block_diag_groups(w):
    bw = w.shape[-1]
    per = LANES // bw
    ng = w.shape[0] // per
    on_diag = jnp.eye(per, dtype=bool)[None, :, None, :, None]
    out = jnp.where(on_diag, w.reshape(ng, per, bw, 1, bw), 0.0)
    return out.reshape(ng, LANES, LANES).astype(BF16)


def _head_lanes(nope, rope):
    lead = nope.shape[:-1] if nope is not None else rope.shape[:-1]
    dt = nope.dtype if nope is not None else rope.dtype
    z = lambda n: jnp.zeros(lead + (n,), dt)
    r1, r2 = (rope[..., :ROPE_HALF], rope[..., ROPE_HALF:]) if rope is not None else (z(ROPE_HALF), z(ROPE_HALF))
    n1, n2 = (nope[..., :NOPE_LO], nope[..., NOPE_LO:]) if nope is not None else (z(NOPE_LO), z(MLA_NOPE - NOPE_LO))
    return jnp.concatenate([r1, n1, r2, n2, z(LANES - MLA_NOPE - MLA_ROPE)], axis=-1)


def _mla_weights(w_down, w_uq, w_ukv):
    lat = MLA_Q_LORA + MLA_KV_LORA
    wd = jnp.concatenate([w_down[:, :lat], _head_lanes(None, w_down[:, lat:])], axis=1)
    wq = w_uq.reshape(MLA_Q_LORA, MLA_HEADS, MLA_NOPE + MLA_ROPE)
    wq = _head_lanes(wq[:, :, :MLA_NOPE], wq[:, :, MLA_NOPE:]).reshape(MLA_Q_LORA, MLA_HEADS * LANES)
    wkv = w_ukv.reshape(MLA_KV_LORA, MLA_HEADS, MLA_NOPE + MLA_V)
    wk = _head_lanes(wkv[:, :, :MLA_NOPE], None).reshape(MLA_KV_LORA, MLA_HEADS * LANES)
    wv = wkv[:, :, MLA_NOPE:].reshape(MLA_KV_LORA, MLA_HEADS * MLA_V)
    return wd.astype(BF16), wq.astype(BF16), wk.astype(BF16), wv.astype(BF16)


def _rope_tables(s):
    pos = np.arange(s, dtype=np.float64)
    freq = ROPE_BASE ** (-np.arange(0, MLA_ROPE, 2, dtype=np.float64) / MLA_ROPE)
    ang = pos[:, None] * freq[None, :]
    lo = slice(0, ROPE_HALF)
    hi = slice(LANES // 2, LANES // 2 + ROPE_HALF)
    cos_t = np.ones((s, LANES))
    sin_t = np.zeros((s, LANES))
    cos_t[:, lo] = cos_t[:, hi] = np.cos(ang)
    sin_t[:, lo] = -np.sin(ang)
    sin_t[:, hi] = np.sin(ang)
    f32 = lambda a: jnp.asarray(a.astype(np.float32))
    return f32(cos_t * MLA_Q_SCALE), f32(sin_t * MLA_Q_SCALE), f32(cos_t), f32(sin_t)


def _tile_m(m):
    return 512 if m % 512 == 0 else m


def kernel(x, p, ln1_g, ln1_b, ln2_g, ln2_b, mlp_w1, mlp_w2, ple_w_proj, ple_w_gate, hy_w_in, hy_conv_w, hy_conv_b, hy_ga_w, hy_ga_b, hy_gx_w, hy_gx_b, hy_lambda, hy_w_out, mla_w_down, mla_q_norm, mla_kv_norm, mla_w_uq, mla_w_ukv, mla_w_out):
    bsz, s, d = x.shape
    m = bsz * s
    tm = _tile_m(m)
    tf = 1024
    row = lambda a: a.reshape(1, -1)
    x2 = x.reshape(m, d)

    rows3 = lambda a: a.reshape(a.shape[0], 1, -1)
    tail_params = (rows3(ln1_g), rows3(ln1_b), mlp_w1.astype(BF16), mlp_w2.astype(BF16),
                   ple_w_gate.astype(BF16), ple_w_proj.astype(BF16), rows3(ln2_g), rows3(ln2_b))
    p3 = p.reshape(p.shape[0], m, -1)

    def layer_tail(acts, wos, x2, i):
        return _tail(acts, wos, x2, p3, i, *tail_params, tm, tf)

    xy, q, iq, misc = _inproj(x2, _hy_in_weight(hy_w_in[0]), tm)
    rec = _rglru(xy.reshape(bsz, s, -1), hy_conv_w[0], row(hy_conv_b[0]),
                 _block_diag_groups(hy_ga_w[0]), row(hy_ga_b[0]),
                 _block_diag_groups(hy_gx_w[0]), row(hy_gx_b[0]), row(hy_lambda[0]))
    att = _dsa(q.reshape(bsz, s, -1), iq.reshape(bsz, s, -1), misc.reshape(bsz, s, -1))
    w_out = hy_w_out[0].astype(BF16)
    x2 = layer_tail([rec.reshape(m, -1), att.reshape(m, -1)], [w_out[:RNN_WIDTH], w_out[RNN_WIDTH:]], x2, 0)

    wd, wq, wk, wv = _mla_weights(mla_w_down[0], mla_w_uq[0], mla_w_ukv[0])
    qp, kp, vp = _mla_proj(x2, wd, row(mla_q_norm[0]), row(mla_kv_norm[0]), wq, wk, wv,
                           _rope_tables(s), min(tm, s), s)
    o = _mla_attn(qp.reshape(bsz, s, -1), kp.reshape(bsz, s, -1), vp.reshape(bsz, s, -1), min(256, s))
    x2 = layer_tail([o.reshape(m, -1)], [mla_w_out[0].astype(BF16)], x2, 1)
    return x2.reshape(bsz, s, d)
```

```python
import functools
import math

import numpy as np
import jax
import jax.numpy as jnp
from jax import lax
from jax.experimental import pallas as pl
from jax.experimental.pallas import tpu as pltpu

F32 = jnp.float32
BF16 = jnp.bfloat16

RNN_WIDTH = 512
RNN_BLOCKS = 8
CONV_WIDTH = 4
LRU_C = 8.0
DSA_HEADS = 8
DSA_HEAD_DIM = 64
IDX_HEADS = 4
IDX_DIM = 64
DSA_TOPK_MAX = 256
MLA_HEADS = 16
MLA_Q_LORA = 512
MLA_KV_LORA = 256
MLA_NOPE = 64
MLA_ROPE = 32
MLA_V = 64
ROPE_BASE = 10000.0
DEPTH = 2
DN_ALPHA = (2 * DEPTH) ** 0.25
LN_EPS = 1e-5
RMS_EPS = 1e-6

LANES = 128
SUBLANES = 8
VMEM_LIMIT_BYTES = 56 * 1024 * 1024

MASK_NEG = -3.0e38
INT32_MIN = -(2 ** 31)
STAGE_LAGS = (1, 2, 4)
DSA_Q_SCALE = DSA_HEAD_DIM ** -0.5


def _cparams(sem):
    return pltpu.CompilerParams(dimension_semantics=sem, vmem_limit_bytes=VMEM_LIMIT_BYTES)


def _dot(a, b):
    return jnp.dot(a, b, preferred_element_type=F32)


def _dot_nt(a, b):
    return lax.dot_general(a, b, (((1,), (1,)), ((), ())), preferred_element_type=F32)


def _layer_norm(y, g, b):
    mu = jnp.mean(y, axis=-1, keepdims=True)
    yc = y - mu
    var = jnp.mean(yc * yc, axis=-1, keepdims=True)
    return yc * lax.rsqrt(var + LN_EPS) * g + b


IN_XY = 2 * RNN_WIDTH
IN_Q = DSA_HEADS * DSA_HEAD_DIM
IN_IQ = IDX_HEADS * IDX_DIM
IN_MISC = 2 * LANES
IN_OFFSETS = tuple(int(v) for v in np.cumsum([0, IN_XY, IN_Q, IN_IQ, IN_MISC]))


def _inproj_kernel(x_ref, w_ref, xy_ref, q_ref, iq_ref, misc_ref):
    xb = x_ref[...].astype(BF16)
    cols = [w_ref[:, a:b] for a, b in zip(IN_OFFSETS[:-1], IN_OFFSETS[1:])]
    xy_ref[...] = _dot(xb, cols[0])
    q_ref[...] = (_dot(xb, cols[1]) * DSA_Q_SCALE).astype(BF16)
    iq_ref[...] = _dot(xb, cols[2])
    misc_ref[...] = _dot(xb, cols[3])


def _inproj(x2, w, tm):
    m, d = x2.shape
    n = w.shape[1]
    widths = (IN_XY, IN_Q, IN_IQ, IN_MISC)
    dtypes = (F32, BF16, F32, F32)
    return pl.pallas_call(
        _inproj_kernel,
        grid=(m // tm,),
        in_specs=[pl.BlockSpec((tm, d), lambda i: (i, 0)),
                  pl.BlockSpec((d, n), lambda i: (0, 0))],
        out_specs=[pl.BlockSpec((tm, wd), lambda i: (i, 0)) for wd in widths],
        out_shape=[jax.ShapeDtypeStruct((m, wd), dt) for wd, dt in zip(widths, dtypes)],
        compiler_params=_cparams(("parallel",)),
        name="hy_inproj",
    )(x2, w)


def _rglru_kernel(xr_ref, yr_ref, cw_ref, cb_ref, wa_ref, ba_ref, wx_ref, bx_ref, lam_ref,
                  o_ref, xpad_ref):
    s = xr_ref.shape[1]
    c = xr_ref.shape[2]
    xpad_ref[0:SUBLANES, :] = jnp.zeros((SUBLANES, c), F32)
    xpad_ref[SUBLANES:SUBLANES + s, :] = xr_ref[0]
    xc = None
    for k in range(CONV_WIDTH):
        start = SUBLANES - (CONV_WIDTH - 1) + k
        term = xpad_ref[start:start + s, :] * cw_ref[k:k + 1, :]
        xc = term if xc is None else xc + term
    xc = xc + cb_ref[...]
    xb = xc.astype(BF16)
    r = jax.nn.sigmoid(_dot(xb, wa_ref[0]) + ba_ref[...])
    gi = jax.nn.sigmoid(_dot(xb, wx_ref[0]) + bx_ref[...])
    nl = -lam_ref[...]
    softplus = jnp.maximum(nl, 0.0) + jnp.log1p(jnp.exp(-jnp.abs(nl)))
    log_a = (-LRU_C) * r * softplus
    a = jnp.exp(log_a)
    y = -jnp.tanh(log_a) * (a * a + 1.0)
    h = jnp.where(y > 0.0, y * lax.rsqrt(y), 0.0) * (gi * xc)
    row = lax.broadcasted_iota(jnp.int32, (s, c), 0)
    sh = 1
    while sh < s:
        if sh < SUBLANES:
            valid = row >= sh
            h = a * jnp.where(valid, pltpu.roll(h, sh, axis=0), 0.0) + h
            a = a * jnp.where(valid, pltpu.roll(a, sh, axis=0), 1.0)
        else:
            h = jnp.concatenate([h[:sh], a[sh:] * h[:s - sh] + h[sh:]], axis=0)
            if sh * 2 < s:
                a = jnp.concatenate([a[:sh], a[sh:] * a[:s - sh]], axis=0)
        sh *= 2
    o_ref[0] = (h * jax.nn.gelu(yr_ref[0])).astype(o_ref.dtype)


def _rglru(xy, conv_w, conv_b, wa, ba, wx, bx, lam):
    bsz, s, _ = xy.shape
    c = LANES
    ng = RNN_WIDTH // c
    vec = lambda: pl.BlockSpec((1, c), lambda b, g: (0, g))
    return pl.pallas_call(
        _rglru_kernel,
        grid=(bsz, ng),
        in_specs=[pl.BlockSpec((1, s, c), lambda b, g: (b, 0, g)),
                  pl.BlockSpec((1, s, c), lambda b, g: (b, 0, ng + g)),
                  pl.BlockSpec((CONV_WIDTH, c), lambda b, g: (0, g)),
                  vec(),
                  pl.BlockSpec((1, c, c), lambda b, g: (g, 0, 0)),
                  vec(),
                  pl.BlockSpec((1, c, c), lambda b, g: (g, 0, 0)),
                  vec(),
                  vec()],
        out_specs=pl.BlockSpec((1, s, c), lambda b, g: (b, 0, g)),
        out_shape=jax.ShapeDtypeStruct((bsz, s, RNN_WIDTH), BF16),
        scratch_shapes=[pltpu.VMEM((s + SUBLANES, c), F32)],
        compiler_params=_cparams(("parallel", "parallel")),
        name="hy_rglru",
    )(xy, xy, conv_w, conv_b, wa, ba, wx, bx, lam)


def _ordered_int_to_float(c):
    return lax.bitcast_convert_type(c ^ (lax.shift_right_arithmetic(c, 31) & jnp.int32(0x7FFFFFFF)), F32)


def _reduce_keys(x, reduce_fn, combine_fn, groups=8):
    rows = x.shape[0]
    step = max(rows // groups, SUBLANES)
    parts = [reduce_fn(x[r:r + step, :], axis=0, keepdims=True) for r in range(0, rows, step)]
    while len(parts) > 1:
        parts = [combine_fn(parts[i], parts[i + 1]) if i + 1 < len(parts) else parts[i]
                 for i in range(0, len(parts), 2)]
    return parts[0]


def _dsa_prepare(q_ref, mk_ref):
    tq = q_ref.shape[1]
    ext = mk_ref.shape[1]
    d0 = ext - tq
    keep = (lax.broadcasted_iota(jnp.int32, (tq, tq), 1)
            <= lax.broadcasted_iota(jnp.int32, (tq, tq), 0))

    mk = mk_ref[0]
    vlane = lax.broadcasted_iota(jnp.int32, (ext, DSA_HEAD_DIM), 1)
    v_aug = jnp.concatenate([mk[:, DSA_HEAD_DIM:2 * DSA_HEAD_DIM],
                             jnp.where(vlane == 0, 1.0, 0.0)], axis=1).astype(BF16)
    ik_b = mk[:, 2 * DSA_HEAD_DIM:2 * DSA_HEAD_DIM + IDX_DIM].astype(BF16)
    kcol = lax.broadcasted_iota(jnp.int32, (ext, 1), 0)
    lane = lax.broadcasted_iota(jnp.int32, (1, LANES), 1)
    pos_hi = (kcol & jnp.int32(-LANES)).astype(F32)
    pos_lo = (kcol & jnp.int32(LANES - 1)).astype(F32)
    k_aug = jnp.where(lane < DSA_HEAD_DIM, mk[:, 0:LANES],
                      jnp.where(lane == DSA_HEAD_DIM, pos_hi,
                                jnp.where(lane == DSA_HEAD_DIM + 1, pos_lo, 0.0))).astype(BF16)
    return d0, keep, ik_b, k_aug, v_aug


def _dsa_select(iq_ref, mq_ref, sc_ref, bias_ref, ik_b, keep, *, topk):
    tq, ext = sc_ref.shape
    d0 = ext - tq
    kf = float(topk)
    iw0 = 2 * DSA_HEAD_DIM + IDX_DIM
    iw = mq_ref[0][:, iw0:iw0 + IDX_HEADS] * (IDX_HEADS ** -0.5 * IDX_DIM ** -0.5)
    iq = iq_ref[0].astype(BF16)
    sc = None
    for h in range(IDX_HEADS):
        d = _dot_nt(iq[:, h * IDX_DIM:(h + 1) * IDX_DIM], ik_b)
        term = jnp.maximum(d, 0.0) * iw[:, h:h + 1]
        sc = term if sc is None else sc + term
    if d0 > 0:
        sc_ref[:, 0:d0] = sc[:, 0:d0]
    sc_ref[:, d0:ext] = jnp.where(keep, sc[:, d0:ext], MASK_NEG)

    def search(i, tau):
        cand = tau + lax.shift_left(jnp.int32(1), 31 - i)
        ge = sc_ref[...] >= _ordered_int_to_float(cand)
        cnt = jnp.sum(jnp.where(ge, 1.0, 0.0), axis=-1, keepdims=True)
        return jnp.where(cnt >= kf, cand, tau)

    tau = lax.fori_loop(0, 32, search, jnp.full((tq, 1), INT32_MIN, jnp.int32), unroll=True)
    tau = _ordered_int_to_float(tau)

    sc = sc_ref[...]
    gt = sc > tau
    eq = sc == tau
    need = kf - jnp.sum(jnp.where(gt, 1.0, 0.0), axis=-1, keepdims=True)
    eqf = jnp.where(eq, 1.0, 0.0).astype(BF16)
    tri = (lax.broadcasted_iota(jnp.int32, (LANES, LANES), 0)
           <= lax.broadcasted_iota(jnp.int32, (LANES, LANES), 1)).astype(BF16)
    off = jnp.zeros((tq, 1), F32)
    for c in range(ext // LANES):
        sl = slice(c * LANES, (c + 1) * LANES)
        pc = _dot(eqf[:, sl], tri)
        take = gt[:, sl] | (eq[:, sl] & ((pc + off) <= need))
        if c * LANES >= d0:
            take = take & keep[:, c * LANES - d0:(c + 1) * LANES - d0]
        bias_ref[:, sl] = jnp.where(take, 0.0, MASK_NEG)
        off = off + pc[:, LANES - 1:LANES]


def _dsa_attend(q_ref, o_ref, bias_ref, k_aug, v_aug):
    tq = q_ref.shape[1]
    q = q_ref[0]
    tail_lane = lax.broadcasted_iota(jnp.int32, (tq, LANES - DSA_HEAD_DIM), 1)
    def logits(h):
        slope = 2.0 ** (-8.0 * (h + 1) / DSA_HEADS)
        tail = jnp.where(tail_lane < 2, slope, 0.0).astype(BF16)
        q_aug = jnp.concatenate([q[:, h * DSA_HEAD_DIM:(h + 1) * DSA_HEAD_DIM], tail], axis=1)
        return _dot_nt(q_aug, k_aug) + bias_ref[...]

    def values(h, p):
        o = _dot(p, v_aug)
        o = o[:, 0:DSA_HEAD_DIM] / o[:, DSA_HEAD_DIM:DSA_HEAD_DIM + 1]
        o_ref[0, :, h * DSA_HEAD_DIM:(h + 1) * DSA_HEAD_DIM] = o.astype(o_ref.dtype)

    lg, mx, pr = {}, {}, {}
    d_max, d_exp, d_pv = STAGE_LAGS
    for t in range(DSA_HEADS + d_pv):
        if t < DSA_HEADS:
            lg[t] = logits(t)
        if 0 <= t - d_max < DSA_HEADS:
            mx[t - d_max] = jnp.max(lg[t - d_max], axis=-1, keepdims=True)
        if 0 <= t - d_exp < DSA_HEADS:
            h = t - d_exp
            pr[h] = jnp.exp(lg.pop(h) - mx.pop(h)).astype(BF16)
        if 0 <= t - d_pv < DSA_HEADS:
            values(t - d_pv, pr.pop(t - d_pv))


def _dsa_kernel(q_ref, iq_ref, mq_ref, mk_ref, o_ref, sc_ref, bias_ref, *, topk):
    d0, keep, ik_b, k_aug, v_aug = _dsa_prepare(q_ref, mk_ref)
    if mk_ref.shape[1] > topk:
        _dsa_select(iq_ref, mq_ref, sc_ref, bias_ref, ik_b, keep, topk=topk)
    else:
        if d0 > 0:
            bias_ref[:, 0:d0] = jnp.zeros((keep.shape[0], d0), F32)
        bias_ref[:, d0:] = jnp.where(keep, 0.0, MASK_NEG)
    _dsa_attend(q_ref, o_ref, bias_ref, k_aug, v_aug)


def _dsa(q, iq, misc):
    bsz, s, _ = q.shape
    topk = min(DSA_TOPK_MAX, s // 4)
    tq = min(256, s)
    hd = DSA_HEADS * DSA_HEAD_DIM
    outs = []
    for c in range(s // tq):
        ext = (c + 1) * tq
        outs.append(pl.pallas_call(
            functools.partial(_dsa_kernel, topk=topk),
            grid=(bsz,),
            in_specs=[pl.BlockSpec((1, tq, hd), lambda b, c=c: (b, c, 0)),
                      pl.BlockSpec((1, tq, IDX_HEADS * IDX_DIM), lambda b, c=c: (b, c, 0)),
                      pl.BlockSpec((1, tq, 256), lambda b, c=c: (b, c, 0)),
                      pl.BlockSpec((1, ext, 256), lambda b: (b, 0, 0))],
            out_specs=pl.BlockSpec((1, tq, hd), lambda b: (b, 0, 0)),
            out_shape=jax.ShapeDtypeStruct((bsz, tq, hd), BF16),
            scratch_shapes=[pltpu.VMEM((tq, ext), F32), pltpu.VMEM((tq, ext), F32)],
            compiler_params=_cparams(("parallel",)),
            name=f"hy_dsa_c{c}",
        )(q, iq, misc, misc))
    return outs[0] if len(outs) == 1 else jnp.concatenate(outs, axis=1)


def _tail_kernel(*refs, n_in, tf):
    a_refs = refs[:n_in]
    wo_refs = refs[n_in:2 * n_in]
    (x_ref, p_ref, g1_ref, b1_ref, w1_ref, w2_ref, wg_ref, wp_ref, g2_ref, b2_ref,
     o_ref, acc_ref) = refs[2 * n_in:]
    n_chunks = w1_ref.shape[1] // tf

    def steps_for(rows):
        st = {}

        def out_proj():
            m = None
            for a_ref, w_ref in zip(a_refs, wo_refs):
                t = _dot(a_ref[rows, :].astype(BF16), w_ref[...])
                m = t if m is None else m + t
            st["y"] = DN_ALPHA * x_ref[rows, :] + m

        def norm1():
            st["h"] = _layer_norm(st.pop("y"), g1_ref[...], b1_ref[...])
            st["hb"] = st["h"].astype(BF16)

        def gate():
            g = jax.nn.sigmoid(_dot(st["hb"], wg_ref[...]))
            acc_ref[rows, :] = DN_ALPHA * st.pop("h") + g * _dot(p_ref[rows, :].astype(BF16), wp_ref[...])

        def mlp(c):
            a = jnp.maximum(_dot(st["hb"], w1_ref[:, c * tf:(c + 1) * tf]), 0.0)
            acc_ref[rows, :] += _dot((a * a).astype(BF16), w2_ref[c * tf:(c + 1) * tf, :])

        def norm2():
            o_ref[rows, :] = _layer_norm(acc_ref[rows, :], g2_ref[...], b2_ref[...])

        return [out_proj, norm1, gate] + [functools.partial(mlp, c) for c in range(n_chunks)] + [norm2]

    half = x_ref.shape[0] // 2
    first, second = steps_for(slice(0, half)), steps_for(slice(half, 2 * half))
    lag = 2
    for t in range(len(first) + lag):
        if t < len(first):
            first[t]()
        if 0 <= t - lag < len(second):
            second[t - lag]()


def _tail(acts, wos, x2, p3, layer, g1, b1, w1, w2, wg, wp, g2, b2, tm, tf):
    m, d = x2.shape
    n_in = len(acts)
    rows = lambda a: pl.BlockSpec((tm, a.shape[1]), lambda i: (i, 0))
    resident = lambda a: pl.BlockSpec(a.shape, lambda i: (0, 0), pipeline_mode=pl.Buffered(1))
    of_layer = lambda a: pl.BlockSpec((None,) + a.shape[1:], lambda i: (layer, 0, 0),
                                      pipeline_mode=pl.Buffered(1))
    params = [g1, b1, w1, w2, wg, wp, g2, b2]
    return pl.pallas_call(
        functools.partial(_tail_kernel, n_in=n_in, tf=tf),
        grid=(m // tm,),
        in_specs=([rows(a) for a in acts] + [resident(w) for w in wos]
                  + [rows(x2), pl.BlockSpec((None, tm, p3.shape[2]), lambda i: (layer, i, 0))]
                  + [of_layer(a) for a in params]),
        out_specs=pl.BlockSpec((tm, d), lambda i: (i, 0)),
        out_shape=jax.ShapeDtypeStruct((m, d), F32),
        scratch_shapes=[pltpu.VMEM((tm, d), F32)],
        compiler_params=_cparams(("parallel",)),
        name="layer_tail",
    )(*acts, *wos, x2, p3, *params)


MLA_Q_SCALE = (MLA_NOPE + MLA_ROPE) ** -0.5 * math.log2(math.e)
ROPE_HALF = MLA_ROPE // 2
NOPE_LO = LANES // 2 - ROPE_HALF


def _rope_group(t, cos_t, sin_t):
    return t * cos_t + pltpu.roll(t, LANES // 2, axis=1) * sin_t


def _mla_proj_kernel(x_ref, wd_ref, qg_ref, kvg_ref, wq_ref, wk_ref, wv_ref,
                     cosq_ref, sinq_ref, cosk_ref, sink_ref, q_ref, k_ref, v_ref):
    xb = x_ref[...].astype(BF16)
    down = _dot(xb, wd_ref[...])
    cq = down[:, 0:MLA_Q_LORA]
    ckv = down[:, MLA_Q_LORA:MLA_Q_LORA + MLA_KV_LORA]
    krg = down[:, MLA_Q_LORA + MLA_KV_LORA:]
    cqn = (cq * lax.rsqrt(jnp.mean(cq * cq, axis=-1, keepdims=True) + RMS_EPS) * qg_ref[...]).astype(BF16)
    ckvn = (ckv * lax.rsqrt(jnp.mean(ckv * ckv, axis=-1, keepdims=True) + RMS_EPS) * kvg_ref[...]).astype(BF16)
    cos_q, sin_q = cosq_ref[...], sinq_ref[...]
    kr = _rope_group(krg, cosk_ref[...], sink_ref[...])
    q = _dot(cqn, wq_ref[...])
    k = _dot(ckvn, wk_ref[...])
    for h in range(MLA_HEADS):
        sl = slice(h * LANES, (h + 1) * LANES)
        q_ref[:, sl] = _rope_group(q[:, sl], cos_q, sin_q).astype(BF16)
        k_ref[:, sl] = (k[:, sl] + kr).astype(BF16)
    v_ref[...] = _dot(ckvn, wv_ref[...]).astype(BF16)


def _mla_proj(x2, wd, qg, kvg, wq, wk, wv, tables, tm, s):
    m, d = x2.shape
    nt = s // tm
    full = lambda a: pl.BlockSpec(a.shape, lambda i: (0, 0))
    tab = lambda: pl.BlockSpec((tm, LANES), lambda i: (i % nt, 0))
    hq = MLA_HEADS * LANES
    hv = MLA_HEADS * MLA_V
    return pl.pallas_call(
        _mla_proj_kernel,
        grid=(m // tm,),
        in_specs=[pl.BlockSpec((tm, d), lambda i: (i, 0)), full(wd), full(qg), full(kvg),
                  full(wq), full(wk), full(wv)] + [tab() for _ in tables],
        out_specs=[pl.BlockSpec((tm, hq), lambda i: (i, 0)),
                   pl.BlockSpec((tm, hq), lambda i: (i, 0)),
                   pl.BlockSpec((tm, hv), lambda i: (i, 0))],
        out_shape=[jax.ShapeDtypeStruct((m, hq), BF16),
                   jax.ShapeDtypeStruct((m, hq), BF16),
                   jax.ShapeDtypeStruct((m, hv), BF16)],
        compiler_params=_cparams(("parallel",)),
        name="mla_proj",
    )(x2, wd, qg, kvg, wq, wk, wv, *tables)


def _mla_attn_kernel(q_ref, k_ref, v_ref, o_ref, ot_ref, *, tq, heads_per_step, key_chunk):
    s = q_ref.shape[1]
    keep_t = (lax.broadcasted_iota(jnp.int32, (tq, tq), 0)
              <= lax.broadcasted_iota(jnp.int32, (tq, tq), 1))
    v_t = jnp.transpose(v_ref[0].astype(F32)).astype(BF16)
    ones_rows = jnp.ones((2 * SUBLANES, s), BF16)
    lhs = [jnp.concatenate([v_t[hh * MLA_V:(hh + 1) * MLA_V, :], ones_rows], axis=0)
           for hh in range(heads_per_step)]
    units = []
    for j in range(s // tq):
        ext = (j + 1) * tq
        starts = list(range(0, ext - tq, key_chunk)) or [0]
        bounds = [(a, b) for a, b in zip(starts, starts[1:] + [ext])]
        for hh in range(heads_per_step):
            units += [(j * tq, hh, k0, k1) for k0, k1 in bounds]

    def logits(r0, hh, k0, k1):
        ql = slice(hh * LANES, (hh + 1) * LANES)
        sc = _dot_nt(k_ref[0, k0:k1, ql], q_ref[0, r0:r0 + tq, ql])
        if k1 == r0 + tq:
            diag = jnp.where(keep_t, sc[k1 - k0 - tq:, :], MASK_NEG)
            sc = diag if k1 - k0 == tq else jnp.concatenate([sc[:k1 - k0 - tq, :], diag], axis=0)
        return sc

    def probs(sc):
        m = _reduce_keys(sc, jnp.max, jnp.maximum)
        return jnp.exp2(sc - m).astype(BF16), m

    running = {}

    def values(r0, hh, k0, k1, p, m):
        acc = _dot(lhs[hh][:, k0:k1], p)
        if k0 > 0:
            m_run, acc_run = running.pop((r0, hh))
            m_new = jnp.maximum(m_run, m)
            acc = acc_run * jnp.exp2(m_run - m_new) + acc * jnp.exp2(m - m_new)
            m = m_new
        if k1 == r0 + tq:
            ot_ref[hh * MLA_V:(hh + 1) * MLA_V, r0:r0 + tq] = acc[0:MLA_V, :] / acc[MLA_V:MLA_V + 1, :]
        else:
            running[(r0, hh)] = (m, acc)

    n = len(units)
    stage_s, stage_m, stage_p = {}, {}, {}
    d_max, d_exp, d_pv = STAGE_LAGS
    for t in range(n + d_pv):
        if t < n:
            stage_s[t] = logits(*units[t])
        if 0 <= t - d_max < n:
            u = t - d_max
            stage_m[u] = _reduce_keys(stage_s[u], jnp.max, jnp.maximum)
        if 0 <= t - d_exp < n:
            u = t - d_exp
            stage_p[u] = jnp.exp2(stage_s.pop(u) - stage_m[u]).astype(BF16)
        if 0 <= t - d_pv < n:
            u = t - d_pv
            values(*units[u], stage_p.pop(u), stage_m.pop(u))
    o_ref[0] = jnp.transpose(ot_ref[...]).astype(o_ref.dtype)


def _mla_attn(q, k, v, tq):
    bsz, s, _ = q.shape
    hps = 4
    return pl.pallas_call(
        functools.partial(_mla_attn_kernel, tq=tq, heads_per_step=hps, key_chunk=2 * tq),
        grid=(bsz, MLA_HEADS // hps),
        in_specs=[pl.BlockSpec((1, s, hps * LANES), lambda b, h: (b, 0, h)),
                  pl.BlockSpec((1, s, hps * LANES), lambda b, h: (b, 0, h)),
                  pl.BlockSpec((1, s, hps * MLA_V), lambda b, h: (b, 0, h))],
        out_specs=pl.BlockSpec((1, s, hps * MLA_V), lambda b, h: (b, 0, h)),
        out_shape=jax.ShapeDtypeStruct((bsz, s, MLA_HEADS * MLA_V), BF16),
        scratch_shapes=[pltpu.VMEM((hps * MLA_V, s), F32)],
        compiler_params=_cparams(("parallel", "parallel")),
        name="mla_attn",
    )(q, k, v)


def _hy_in_weight(w_in):
    xr, yr, q, k, v, iq, ik, iw = jnp.split(
        w_in, np.cumsum([512, 512, 512, 64, 64, 256, 64, 4])[:-1].tolist(), axis=1)
    pad = jnp.zeros((w_in.shape[0], 256 - 64 * 3 - 4), w_in.dtype)
    return jnp.concatenate([xr, yr, q, iq, k, v, ik, iw, pad], axis=1).astype(BF16)


def _block_diag_groups(w):
    bw = w.shape[-1]
    per = LANES // bw
    ng = w.shape[0] // per
    on_diag = jnp.eye(per, dtype=bool)[None, :, None, :, None]
    out = jnp.where(on_diag, w.reshape(ng, per, bw, 1, bw), 0.0)
    return out.reshape(ng, LANES, LANES).astype(BF16)


def _head_lanes(nope, rope):
    lead = nope.shape[:-1] if nope is not None else rope.shape[:-1]
    dt = nope.dtype if nope is not None else rope.dtype
    z = lambda n: jnp.zeros(lead + (n,), dt)
    r1, r2 = (rope[..., :ROPE_HALF], rope[..., ROPE_HALF:]) if rope is not None else (z(ROPE_HALF), z(ROPE_HALF))
    n1, n2 = (nope[..., :NOPE_LO], nope[..., NOPE_LO:]) if nope is not None else (z(NOPE_LO), z(MLA_NOPE - NOPE_LO))
    return jnp.concatenate([r1, n1, r2, n2, z(LANES - MLA_NOPE - MLA_ROPE)], axis=-1)


def _mla_weights(w_down, w_uq, w_ukv):
    lat = MLA_Q_LORA + MLA_KV_LORA
    wd = jnp.concatenate([w_down[:, :lat], _head_lanes(None, w_down[:, lat:])], axis=1)
    wq = w_uq.reshape(MLA_Q_LORA, MLA_HEADS, MLA_NOPE + MLA_ROPE)
    wq = _head_lanes(wq[:, :, :MLA_NOPE], wq[:, :, MLA_NOPE:]).reshape(MLA_Q_LORA, MLA_HEADS * LANES)
    wkv = w_ukv.reshape(MLA_KV_LORA, MLA_HEADS, MLA_NOPE + MLA_V)
    wk = _head_lanes(wkv[:, :, :MLA_NOPE], None).reshape(MLA_KV_LORA, MLA_HEADS * LANES)
    wv = wkv[:, :, MLA_NOPE:].reshape(MLA_KV_LORA, MLA_HEADS * MLA_V)
    return wd.astype(BF16), wq.astype(BF16), wk.astype(BF16), wv.astype(BF16)


def _rope_tables(s):
    pos = np.arange(s, dtype=np.float64)
    freq = ROPE_BASE ** (-np.arange(0, MLA_ROPE, 2, dtype=np.float64) / MLA_ROPE)
    ang = pos[:, None] * freq[None, :]
    lo = slice(0, ROPE_HALF)
    hi = slice(LANES // 2, LANES // 2 + ROPE_HALF)
    cos_t = np.ones((s, LANES))
    sin_t = np.zeros((s, LANES))
    cos_t[:, lo] = cos_t[:, hi] = np.cos(ang)
    sin_t[:, lo] = -np.sin(ang)
    sin_t[:, hi] = np.sin(ang)
    f32 = lambda a: jnp.asarray(a.astype(np.float32))
    return f32(cos_t * MLA_Q_SCALE), f32(sin_t * MLA_Q_SCALE), f32(cos_t), f32(sin_t)


def _tile_m(m):
    return 512 if m % 512 == 0 else m


def kernel(x, p, ln1_g, ln1_b, ln2_g, ln2_b, mlp_w1, mlp_w2, ple_w_proj, ple_w_gate, hy_w_in, hy_conv_w, hy_conv_b, hy_ga_w, hy_ga_b, hy_gx_w, hy_gx_b, hy_lambda, hy_w_out, mla_w_down, mla_q_norm, mla_kv_norm, mla_w_uq, mla_w_ukv, mla_w_out):
    bsz, s, d = x.shape
    m = bsz * s
    tm = _tile_m(m)
    tf = 1024
    row = lambda a: a.reshape(1, -1)
    x2 = x.reshape(m, d)

    rows3 = lambda a: a.reshape(a.shape[0], 1, -1)
    tail_params = (rows3(ln1_g), rows3(ln1_b), mlp_w1.astype(BF16), mlp_w2.astype(BF16),
                   ple_w_gate.astype(BF16), ple_w_proj.astype(BF16), rows3(ln2_g), rows3(ln2_b))
    p3 = p.reshape(p.shape[0], m, -1)

    def layer_tail(acts, wos, x2, i):
        return _tail(acts, wos, x2, p3, i, *tail_params, tm, tf)

    xy, q, iq, misc = _inproj(x2, _hy_in_weight(hy_w_in[0]), tm)
    rec = _rglru(xy.reshape(bsz, s, -1), hy_conv_w[0], row(hy_conv_b[0]),
                 _block_diag_groups(hy_ga_w[0]), row(hy_ga_b[0]),
                 _block_diag_groups(hy_gx_w[0]), row(hy_gx_b[0]), row(hy_lambda[0]))
    att = _dsa(q.reshape(bsz, s, -1), iq.reshape(bsz, s, -1), misc.reshape(bsz, s, -1))
    w_out = hy_w_out[0].astype(BF16)
    x2 = layer_tail([rec.reshape(m, -1), att.reshape(m, -1)], [w_out[:RNN_WIDTH], w_out[RNN_WIDTH:]], x2, 0)

    wd, wq, wk, wv = _mla_weights(mla_w_down[0], mla_w_uq[0], mla_w_ukv[0])
    qp, kp, vp = _mla_proj(x2, wd, row(mla_q_norm[0]), row(mla_kv_norm[0]), wq, wk, wv,
                           _rope_tables(s), min(tm, s), s)
    o = _mla_attn(qp.reshape(bsz, s, -1), kp.reshape(bsz, s, -1), vp.reshape(bsz, s, -1), min(256, s))
    x2 = layer_tail([o.reshape(m, -1)], [mla_w_out[0].astype(BF16)], x2, 1)
    return x2.reshape(bsz, s, d)
```

```python
import functools
import math

import numpy as np
import jax
import jax.numpy as jnp
from jax import lax
from jax.experimental import pallas as pl
from jax.experimental.pallas import tpu as pltpu

F32 = jnp.float32
BF16 = jnp.bfloat16

RNN_WIDTH = 512
RNN_BLOCKS = 8
CONV_WIDTH = 4
LRU_C = 8.0
DSA_HEADS = 8
DSA_HEAD_DIM = 64
IDX_HEADS = 4
IDX_DIM = 64
DSA_TOPK_MAX = 256
MLA_HEADS = 16
MLA_Q_LORA = 512
MLA_KV_LORA = 256
MLA_NOPE = 64
MLA_ROPE = 32
MLA_V = 64
ROPE_BASE = 10000.0
DEPTH = 2
DN_ALPHA = (2 * DEPTH) ** 0.25
LN_EPS = 1e-5
RMS_EPS = 1e-6

LANES = 128
SUBLANES = 8
VMEM_LIMIT_BYTES = 56 * 1024 * 1024

MASK_NEG = -3.0e38
INT32_MIN = -(2 ** 31)
STAGE_LAGS = (1, 2, 4)
DSA_Q_SCALE = DSA_HEAD_DIM ** -0.5
DSA_PAIR_MAX_EXTENT = 1024


def _cparams(sem):
    return pltpu.CompilerParams(dimension_semantics=sem, vmem_limit_bytes=VMEM_LIMIT_BYTES)


def _dot(a, b):
    return jnp.dot(a, b, preferred_element_type=F32)


def _dot_nt(a, b):
    return lax.dot_general(a, b, (((1,), (1,)), ((), ())), preferred_element_type=F32)


def _layer_norm(y, g, b):
    mu = jnp.mean(y, axis=-1, keepdims=True)
    yc = y - mu
    var = jnp.mean(yc * yc, axis=-1, keepdims=True)
    return yc * lax.rsqrt(var + LN_EPS) * g + b


IN_XY = 2 * RNN_WIDTH
IN_Q = DSA_HEADS * DSA_HEAD_DIM
IN_IQ = IDX_HEADS * IDX_DIM
IN_MISC = 2 * LANES
IN_OFFSETS = tuple(int(v) for v in np.cumsum([0, IN_XY, IN_Q, IN_IQ, IN_MISC]))


def _inproj_kernel(x_ref, w_ref, xy_ref, q_ref, iq_ref, misc_ref):
    xb = x_ref[...].astype(BF16)
    cols = [w_ref[:, a:b] for a, b in zip(IN_OFFSETS[:-1], IN_OFFSETS[1:])]
    xy_ref[...] = _dot(xb, cols[0])
    q_ref[...] = (_dot(xb, cols[1]) * DSA_Q_SCALE).astype(BF16)
    iq_ref[...] = _dot(xb, cols[2])
    misc_ref[...] = _dot(xb, cols[3])


def _inproj(x2, w, tm):
    m, d = x2.shape
    n = w.shape[1]
    widths = (IN_XY, IN_Q, IN_IQ, IN_MISC)
    dtypes = (F32, BF16, F32, F32)
    return pl.pallas_call(
        _inproj_kernel,
        grid=(m // tm,),
        in_specs=[pl.BlockSpec((tm, d), lambda i: (i, 0)),
                  pl.BlockSpec((d, n), lambda i: (0, 0))],
        out_specs=[pl.BlockSpec((tm, wd), lambda i: (i, 0)) for wd in widths],
        out_shape=[jax.ShapeDtypeStruct((m, wd), dt) for wd, dt in zip(widths, dtypes)],
        compiler_params=_cparams(("parallel",)),
        name="hy_inproj",
    )(x2, w)


def _rglru_kernel(xr_ref, yr_ref, cw_ref, cb_ref, wa_ref, ba_ref, wx_ref, bx_ref, lam_ref,
                  o_ref, xpad_ref):
    s = xr_ref.shape[1]
    c = xr_ref.shape[2]
    xpad_ref[0:SUBLANES, :] = jnp.zeros((SUBLANES, c), F32)
    xpad_ref[SUBLANES:SUBLANES + s, :] = xr_ref[0]
    xc = None
    for k in range(CONV_WIDTH):
        start = SUBLANES - (CONV_WIDTH - 1) + k
        term = xpad_ref[start:start + s, :] * cw_ref[k:k + 1, :]
        xc = term if xc is None else xc + term
    xc = xc + cb_ref[...]
    xb = xc.astype(BF16)
    r = jax.nn.sigmoid(_dot(xb, wa_ref[0]) + ba_ref[...])
    gi = jax.nn.sigmoid(_dot(xb, wx_ref[0]) + bx_ref[...])
    nl = -lam_ref[...]
    softplus = jnp.maximum(nl, 0.0) + jnp.log1p(jnp.exp(-jnp.abs(nl)))
    log_a = (-LRU_C) * r * softplus
    a = jnp.exp(log_a)
    y = -jnp.tanh(log_a) * (a * a + 1.0)
    h = jnp.where(y > 0.0, y * lax.rsqrt(y), 0.0) * (gi * xc)
    row = lax.broadcasted_iota(jnp.int32, (s, c), 0)
    sh = 1
    while sh < s:
        if sh < SUBLANES:
            valid = row >= sh
            h = a * jnp.where(valid, pltpu.roll(h, sh, axis=0), 0.0) + h
            a = a * jnp.where(valid, pltpu.roll(a, sh, axis=0), 1.0)
        else:
            h = jnp.concatenate([h[:sh], a[sh:] * h[:s - sh] + h[sh:]], axis=0)
            if sh * 2 < s:
                a = jnp.concatenate([a[:sh], a[sh:] * a[:s - sh]], axis=0)
        sh *= 2
    o_ref[0] = (h * jax.nn.gelu(yr_ref[0])).astype(o_ref.dtype)


def _rglru(xy, conv_w, conv_b, wa, ba, wx, bx, lam):
    bsz, s, _ = xy.shape
    c = LANES
    ng = RNN_WIDTH // c
    vec = lambda: pl.BlockSpec((1, c), lambda b, g: (0, g))
    return pl.pallas_call(
        _rglru_kernel,
        grid=(bsz, ng),
        in_specs=[pl.BlockSpec((1, s, c), lambda b, g: (b, 0, g)),
                  pl.BlockSpec((1, s, c), lambda b, g: (b, 0, ng + g)),
                  pl.BlockSpec((CONV_WIDTH, c), lambda b, g: (0, g)),
                  vec(),
                  pl.BlockSpec((1, c, c), lambda b, g: (g, 0, 0)),
                  vec(),
                  pl.BlockSpec((1, c, c), lambda b, g: (g, 0, 0)),
                  vec(),
                  vec()],
        out_specs=pl.BlockSpec((1, s, c), lambda b, g: (b, 0, g)),
        out_shape=jax.ShapeDtypeStruct((bsz, s, RNN_WIDTH), BF16),
        scratch_shapes=[pltpu.VMEM((s + SUBLANES, c), F32)],
        compiler_params=_cparams(("parallel", "parallel")),
        name="hy_rglru",
    )(xy, xy, conv_w, conv_b, wa, ba, wx, bx, lam)


def _ordered_int_to_float(c):
    return lax.bitcast_convert_type(c ^ (lax.shift_right_arithmetic(c, 31) & jnp.int32(0x7FFFFFFF)), F32)


def _reduce_keys(x, reduce_fn, combine_fn, groups=8):
    rows = x.shape[0]
    step = max(rows // groups, SUBLANES)
    parts = [reduce_fn(x[r:r + step, :], axis=0, keepdims=True) for r in range(0, rows, step)]
    while len(parts) > 1:
        parts = [combine_fn(parts[i], parts[i + 1]) if i + 1 < len(parts) else parts[i]
                 for i in range(0, len(parts), 2)]
    return parts[0]


def _dsa_prepare(q_ref, mk_ref):
    tq = q_ref.shape[1]
    ext = mk_ref.shape[1]
    d0 = ext - tq
    keep = (lax.broadcasted_iota(jnp.int32, (tq, tq), 1)
            <= lax.broadcasted_iota(jnp.int32, (tq, tq), 0))

    mk = mk_ref[0]
    vlane = lax.broadcasted_iota(jnp.int32, (ext, DSA_HEAD_DIM), 1)
    v_aug = jnp.concatenate([mk[:, DSA_HEAD_DIM:2 * DSA_HEAD_DIM],
                             jnp.where(vlane == 0, 1.0, 0.0)], axis=1).astype(BF16)
    ik_b = mk[:, 2 * DSA_HEAD_DIM:2 * DSA_HEAD_DIM + IDX_DIM].astype(BF16)
    kcol = lax.broadcasted_iota(jnp.int32, (ext, 1), 0)
    lane = lax.broadcasted_iota(jnp.int32, (1, LANES), 1)
    pos_hi = (kcol & jnp.int32(-LANES)).astype(F32)
    pos_lo = (kcol & jnp.int32(LANES - 1)).astype(F32)
    k_aug = jnp.where(lane < DSA_HEAD_DIM, mk[:, 0:LANES],
                      jnp.where(lane == DSA_HEAD_DIM, pos_hi,
                                jnp.where(lane == DSA_HEAD_DIM + 1, pos_lo, 0.0))).astype(BF16)
    return d0, keep, ik_b, k_aug, v_aug


def _dsa_select_steps(rows, iq_ref, mq_ref, sc_ref, bias_ref, ik_b, keep, *, topk):
    tq, ext = sc_ref.shape
    nr = rows.stop - rows.start
    d0 = ext - tq
    kf = float(topk)
    st = {}

    def scores():
        iw0 = 2 * DSA_HEAD_DIM + IDX_DIM
        iw = mq_ref[0, rows, iw0:iw0 + IDX_HEADS] * (IDX_HEADS ** -0.5 * IDX_DIM ** -0.5)
        iq = iq_ref[0, rows, :].astype(BF16)
        sc = None
        for h in range(IDX_HEADS):
            d = _dot_nt(iq[:, h * IDX_DIM:(h + 1) * IDX_DIM], ik_b)
            term = jnp.maximum(d, 0.0) * iw[:, h:h + 1]
            sc = term if sc is None else sc + term
        if d0 > 0:
            sc_ref[rows, 0:d0] = sc[:, 0:d0]
        sc_ref[rows, d0:ext] = jnp.where(keep[rows, :], sc[:, d0:ext], MASK_NEG)
        st["tau"] = jnp.full((nr, 1), INT32_MIN, jnp.int32)

    def search(i):
        cand = st["tau"] + jnp.int32(1 << (31 - i) if i else INT32_MIN)
        ge = sc_ref[rows, :] >= _ordered_int_to_float(cand)
        cnt = jnp.sum(jnp.where(ge, 1.0, 0.0), axis=-1, keepdims=True)
        st["tau"] = jnp.where(cnt >= kf, cand, st["tau"])

    def ties():
        tau = _ordered_int_to_float(st.pop("tau"))
        sc = sc_ref[rows, :]
        gt = sc > tau
        eq = sc == tau
        need = kf - jnp.sum(jnp.where(gt, 1.0, 0.0), axis=-1, keepdims=True)
        eqf = jnp.where(eq, 1.0, 0.0).astype(BF16)
        tri = (lax.broadcasted_iota(jnp.int32, (LANES, LANES), 0)
               <= lax.broadcasted_iota(jnp.int32, (LANES, LANES), 1)).astype(BF16)
        off = jnp.zeros((nr, 1), F32)
        for c in range(ext // LANES):
            sl = slice(c * LANES, (c + 1) * LANES)
            pc = _dot(eqf[:, sl], tri)
            take = gt[:, sl] | (eq[:, sl] & ((pc + off) <= need))
            if c * LANES >= d0:
                take = take & keep[rows, c * LANES - d0:(c + 1) * LANES - d0]
            bias_ref[rows, sl] = jnp.where(take, 0.0, MASK_NEG)
            off = off + pc[:, LANES - 1:LANES]

    return [scores] + [functools.partial(search, i) for i in range(32)] + [ties]


def _dsa_attend_steps(rows, q_ref, o_ref, bias_ref, k_aug, v_aug):
    nr = rows.stop - rows.start
    tail_lane = lax.broadcasted_iota(jnp.int32, (nr, LANES - DSA_HEAD_DIM), 1)

    def logits(h):
        slope = 2.0 ** (-8.0 * (h + 1) / DSA_HEADS)
        tail = jnp.where(tail_lane < 2, slope, 0.0).astype(BF16)
        qh = q_ref[0, rows, h * DSA_HEAD_DIM:(h + 1) * DSA_HEAD_DIM]
        return _dot_nt(jnp.concatenate([qh, tail], axis=1), k_aug) + bias_ref[rows, :]

    def values(h, p):
        o = _dot(p, v_aug)
        o = o[:, 0:DSA_HEAD_DIM] / o[:, DSA_HEAD_DIM:DSA_HEAD_DIM + 1]
        o_ref[0, rows, h * DSA_HEAD_DIM:(h + 1) * DSA_HEAD_DIM] = o.astype(o_ref.dtype)

    lg, mx, pr = {}, {}, {}
    d_max, d_exp, d_pv = STAGE_LAGS

    def step(t):
        if t < DSA_HEADS:
            lg[t] = logits(t)
        if 0 <= t - d_max < DSA_HEADS:
            mx[t - d_max] = jnp.max(lg[t - d_max], axis=-1, keepdims=True)
        if 0 <= t - d_exp < DSA_HEADS:
            h = t - d_exp
            pr[h] = jnp.exp(lg.pop(h) - mx.pop(h)).astype(BF16)
        if 0 <= t - d_pv < DSA_HEADS:
            values(t - d_pv, pr.pop(t - d_pv))

    return [functools.partial(step, t) for t in range(DSA_HEADS + d_pv)]


def _dsa_kernel(q_ref, iq_ref, mq_ref, mk_ref, o_ref, sc_ref, bias_ref, *, topk):
    tq = q_ref.shape[1]
    rows = slice(0, tq)
    select, attend = [], []
    for bi in range(q_ref.shape[0]):
        one = lambda ref: ref.at[bi:bi + 1]
        sc_b, bias_b = sc_ref.at[bi], bias_ref.at[bi]
        d0, keep, ik_b, k_aug, v_aug = _dsa_prepare(one(q_ref), one(mk_ref))
        if mk_ref.shape[1] > topk:
            select.append(_dsa_select_steps(rows, one(iq_ref), one(mq_ref), sc_b, bias_b, ik_b, keep,
                                            topk=topk))
        else:
            if d0 > 0:
                bias_b[:, 0:d0] = jnp.zeros((tq, d0), F32)
            bias_b[:, d0:] = jnp.where(keep, 0.0, MASK_NEG)
        attend.append(_dsa_attend_steps(rows, one(q_ref), one(o_ref), bias_b, k_aug, v_aug))
    for steps in list(zip(*select)) + list(zip(*attend)):
        for step in steps:
            step()


def _dsa(q, iq, misc):
    bsz, s, _ = q.shape
    topk = min(DSA_TOPK_MAX, s // 4)
    tq = min(256, s)
    hd = DSA_HEADS * DSA_HEAD_DIM
    outs = []
    for c in range(s // tq):
        ext = (c + 1) * tq
        bb = 2 if (ext <= DSA_PAIR_MAX_EXTENT and bsz % 2 == 0) else 1
        outs.append(pl.pallas_call(
            functools.partial(_dsa_kernel, topk=topk),
            grid=(bsz // bb,),
            in_specs=[pl.BlockSpec((bb, tq, hd), lambda b, c=c: (b, c, 0)),
                      pl.BlockSpec((bb, tq, IDX_HEADS * IDX_DIM), lambda b, c=c: (b, c, 0)),
                      pl.BlockSpec((bb, tq, 256), lambda b, c=c: (b, c, 0)),
                      pl.BlockSpec((bb, ext, 256), lambda b: (b, 0, 0))],
            out_specs=pl.BlockSpec((bb, tq, hd), lambda b: (b, 0, 0)),
            out_shape=jax.ShapeDtypeStruct((bsz, tq, hd), BF16),
            scratch_shapes=[pltpu.VMEM((bb, tq, ext), F32), pltpu.VMEM((bb, tq, ext), F32)],
            compiler_params=_cparams(("parallel",)),
            name=f"hy_dsa_c{c}",
        )(q, iq, misc, misc))
    return outs[0] if len(outs) == 1 else jnp.concatenate(outs, axis=1)


def _tail_kernel(*refs, n_in, tf):
    a_refs = refs[:n_in]
    wo_refs = refs[n_in:2 * n_in]
    (x_ref, p_ref, g1_ref, b1_ref, w1_ref, w2_ref, wg_ref, wp_ref, g2_ref, b2_ref,
     o_ref, acc_ref) = refs[2 * n_in:]
    n_chunks = w1_ref.shape[1] // tf

    def steps_for(rows):
        st = {}

        def out_proj():
            m = None
            for a_ref, w_ref in zip(a_refs, wo_refs):
                t = _dot(a_ref[rows, :].astype(BF16), w_ref[...])
                m = t if m is None else m + t
            st["y"] = DN_ALPHA * x_ref[rows, :] + m

        def norm1():
            st["h"] = _layer_norm(st.pop("y"), g1_ref[...], b1_ref[...])
            st["hb"] = st["h"].astype(BF16)

        def gate():
            g = jax.nn.sigmoid(_dot(st["hb"], wg_ref[...]))
            acc_ref[rows, :] = DN_ALPHA * st.pop("h") + g * _dot(p_ref[rows, :].astype(BF16), wp_ref[...])

        def mlp(c):
            a = jnp.maximum(_dot(st["hb"], w1_ref[:, c * tf:(c + 1) * tf]), 0.0)
            acc_ref[rows, :] += _dot((a * a).astype(BF16), w2_ref[c * tf:(c + 1) * tf, :])

        def norm2():
            o_ref[rows, :] = _layer_norm(acc_ref[rows, :], g2_ref[...], b2_ref[...])

        return [out_proj, norm1, gate] + [functools.partial(mlp, c) for c in range(n_chunks)] + [norm2]

    half = x_ref.shape[0] // 2
    first, second = steps_for(slice(0, half)), steps_for(slice(half, 2 * half))
    lag = 2
    for t in range(len(first) + lag):
        if t < len(first):
            first[t]()
        if 0 <= t - lag < len(second):
            second[t - lag]()


def _tail(acts, wos, x2, p3, layer, g1, b1, w1, w2, wg, wp, g2, b2, tm, tf):
    m, d = x2.shape
    n_in = len(acts)
    rows = lambda a: pl.BlockSpec((tm, a.shape[1]), lambda i: (i, 0))
    resident = lambda a: pl.BlockSpec(a.shape, lambda i: (0, 0), pipeline_mode=pl.Buffered(1))
    of_layer = lambda a: pl.BlockSpec((None,) + a.shape[1:], lambda i: (layer, 0, 0),
                                      pipeline_mode=pl.Buffered(1))
    params = [g1, b1, w1, w2, wg, wp, g2, b2]
    return pl.pallas_call(
        functools.partial(_tail_kernel, n_in=n_in, tf=tf),
        grid=(m // tm,),
        in_specs=([rows(a) for a in acts] + [resident(w) for w in wos]
                  + [rows(x2), pl.BlockSpec((None, tm, p3.shape[2]), lambda i: (layer, i, 0))]
                  + [of_layer(a) for a in params]),
        out_specs=pl.BlockSpec((tm, d), lambda i: (i, 0)),
        out_shape=jax.ShapeDtypeStruct((m, d), F32),
        scratch_shapes=[pltpu.VMEM((tm, d), F32)],
        compiler_params=_cparams(("parallel",)),
        name="layer_tail",
    )(*acts, *wos, x2, p3, *params)


MLA_Q_SCALE = (MLA_NOPE + MLA_ROPE) ** -0.5 * math.log2(math.e)
ROPE_HALF = MLA_ROPE // 2
NOPE_LO = LANES // 2 - ROPE_HALF


def _rope_group(t, cos_t, sin_t):
    return t * cos_t + pltpu.roll(t, LANES // 2, axis=1) * sin_t


def _mla_proj_kernel(x_ref, wd_ref, qg_ref, kvg_ref, wq_ref, wk_ref, wv_ref,
                     cosq_ref, sinq_ref, cosk_ref, sink_ref, q_ref, k_ref, v_ref):
    xb = x_ref[...].astype(BF16)
    down = _dot(xb, wd_ref[...])
    cq = down[:, 0:MLA_Q_LORA]
    ckv = down[:, MLA_Q_LORA:MLA_Q_LORA + MLA_KV_LORA]
    krg = down[:, MLA_Q_LORA + MLA_KV_LORA:]
    cqn = (cq * lax.rsqrt(jnp.mean(cq * cq, axis=-1, keepdims=True) + RMS_EPS) * qg_ref[...]).astype(BF16)
    ckvn = (ckv * lax.rsqrt(jnp.mean(ckv * ckv, axis=-1, keepdims=True) + RMS_EPS) * kvg_ref[...]).astype(BF16)
    cos_q, sin_q = cosq_ref[...], sinq_ref[...]
    kr = _rope_group(krg, cosk_ref[...], sink_ref[...])
    q = _dot(cqn, wq_ref[...])
    k = _dot(ckvn, wk_ref[...])
    for h in range(MLA_HEADS):
        sl = slice(h * LANES, (h + 1) * LANES)
        q_ref[:, sl] = _rope_group(q[:, sl], cos_q, sin_q).astype(BF16)
        k_ref[:, sl] = (k[:, sl] + kr).astype(BF16)
    v_ref[...] = _dot(ckvn, wv_ref[...]).astype(BF16)


def _mla_proj(x2, wd, qg, kvg, wq, wk, wv, tables, tm, s):
    m, d = x2.shape
    nt = s // tm
    full = lambda a: pl.BlockSpec(a.shape, lambda i: (0, 0))
    tab = lambda: pl.BlockSpec((tm, LANES), lambda i: (i % nt, 0))
    hq = MLA_HEADS * LANES
    hv = MLA_HEADS * MLA_V
    return pl.pallas_call(
        _mla_proj_kernel,
        grid=(m // tm,),
        in_specs=[pl.BlockSpec((tm, d), lambda i: (i, 0)), full(wd), full(qg), full(kvg),
                  full(wq), full(wk), full(wv)] + [tab() for _ in tables],
        out_specs=[pl.BlockSpec((tm, hq), lambda i: (i, 0)),
                   pl.BlockSpec((tm, hq), lambda i: (i, 0)),
                   pl.BlockSpec((tm, hv), lambda i: (i, 0))],
        out_shape=[jax.ShapeDtypeStruct((m, hq), BF16),
                   jax.ShapeDtypeStruct((m, hq), BF16),
                   jax.ShapeDtypeStruct((m, hv), BF16)],
        compiler_params=_cparams(("parallel",)),
        name="mla_proj",
    )(x2, wd, qg, kvg, wq, wk, wv, *tables)


def _mla_attn_kernel(q_ref, k_ref, v_ref, o_ref, ot_ref, *, tq, heads_per_step, key_chunk):
    s = q_ref.shape[1]
    keep_t = (lax.broadcasted_iota(jnp.int32, (tq, tq), 0)
              <= lax.broadcasted_iota(jnp.int32, (tq, tq), 1))
    v_t = jnp.transpose(v_ref[0].astype(F32)).astype(BF16)
    ones_rows = jnp.ones((2 * SUBLANES, s), BF16)
    lhs = [jnp.concatenate([v_t[hh * MLA_V:(hh + 1) * MLA_V, :], ones_rows], axis=0)
           for hh in range(heads_per_step)]
    units = []
    for j in range(s // tq):
        ext = (j + 1) * tq
        starts = list(range(0, ext - tq, key_chunk)) or [0]
        bounds = [(a, b) for a, b in zip(starts, starts[1:] + [ext])]
        for hh in range(heads_per_step):
            units += [(j * tq, hh, k0, k1) for k0, k1 in bounds]

    def logits(r0, hh, k0, k1):
        ql = slice(hh * LANES, (hh + 1) * LANES)
        sc = _dot_nt(k_ref[0, k0:k1, ql], q_ref[0, r0:r0 + tq, ql])
        if k1 == r0 + tq:
            diag = jnp.where(keep_t, sc[k1 - k0 - tq:, :], MASK_NEG)
            sc = diag if k1 - k0 == tq else jnp.concatenate([sc[:k1 - k0 - tq, :], diag], axis=0)
        return sc

    def probs(sc):
        m = _reduce_keys(sc, jnp.max, jnp.maximum)
        return jnp.exp2(sc - m).astype(BF16), m

    running = {}

    def values(r0, hh, k0, k1, p, m):
        acc = _dot(lhs[hh][:, k0:k1], p)
        if k0 > 0:
            m_run, acc_run = running.pop((r0, hh))
            m_new = jnp.maximum(m_run, m)
            acc = acc_run * jnp.exp2(m_run - m_new) + acc * jnp.exp2(m - m_new)
            m = m_new
        if k1 == r0 + tq:
            ot_ref[hh * MLA_V:(hh + 1) * MLA_V, r0:r0 + tq] = acc[0:MLA_V, :] / acc[MLA_V:MLA_V + 1, :]
        else:
            running[(r0, hh)] = (m, acc)

    n = len(units)
    stage_s, stage_m, stage_p = {}, {}, {}
    d_max, d_exp, d_pv = STAGE_LAGS
    for t in range(n + d_pv):
        if t < n:
            stage_s[t] = logits(*units[t])
        if 0 <= t - d_max < n:
            u = t - d_max
            stage_m[u] = _reduce_keys(stage_s[u], jnp.max, jnp.maximum)
        if 0 <= t - d_exp < n:
            u = t - d_exp
            stage_p[u] = jnp.exp2(stage_s.pop(u) - stage_m[u]).astype(BF16)
        if 0 <= t - d_pv < n:
            u = t - d_pv
            values(*units[u], stage_p.pop(u), stage_m.pop(u))
    o_ref[0] = jnp.transpose(ot_ref[...]).astype(o_ref.dtype)


def _mla_attn(q, k, v, tq):
    bsz, s, _ = q.shape
    hps = 4
    return pl.pallas_call(
        functools.partial(_mla_attn_kernel, tq=tq, heads_per_step=hps, key_chunk=2 * tq),
        grid=(bsz, MLA_HEADS // hps),
        in_specs=[pl.BlockSpec((1, s, hps * LANES), lambda b, h: (b, 0, h)),
                  pl.BlockSpec((1, s, hps * LANES), lambda b, h: (b, 0, h)),
                  pl.BlockSpec((1, s, hps * MLA_V), lambda b, h: (b, 0, h))],
        out_specs=pl.BlockSpec((1, s, hps * MLA_V), lambda b, h: (b, 0, h)),
        out_shape=jax.ShapeDtypeStruct((bsz, s, MLA_HEADS * MLA_V), BF16),
        scratch_shapes=[pltpu.VMEM((hps * MLA_V, s), F32)],
        compiler_params=_cparams(("parallel", "parallel")),
        name="mla_attn",
    )(q, k, v)


def _hy_in_weight(w_in):
    xr, yr, q, k, v, iq, ik, iw = jnp.split(
        w_in, np.cumsum([512, 512, 512, 64, 64, 256, 64, 4])[:-1].tolist(), axis=1)
    pad = jnp.zeros((w_in.shape[0], 256 - 64 * 3 - 4), w_in.dtype)
    return jnp.concatenate([xr, yr, q, iq, k, v, ik, iw, pad], axis=1).astype(BF16)


def _block_diag_groups(w):
    bw = w.shape[-1]
    per = LANES // bw
    ng = w.shape[0] // per
    on_diag = jnp.eye(per, dtype=bool)[None, :, None, :, None]
    out = jnp.where(on_diag, w.reshape(ng, per, bw, 1, bw), 0.0)
    return out.reshape(ng, LANES, LANES).astype(BF16)


def _head_lanes(nope, rope):
    lead = nope.shape[:-1] if nope is not None else rope.shape[:-1]
    dt = nope.dtype if nope is not None else rope.dtype
    z = lambda n: jnp.zeros(lead + (n,), dt)
    r1, r2 = (rope[..., :ROPE_HALF], rope[..., ROPE_HALF:]) if rope is not None else (z(ROPE_HALF), z(ROPE_HALF))
    n1, n2 = (nope[..., :NOPE_LO], nope[..., NOPE_LO:]) if nope is not None else (z(NOPE_LO), z(MLA_NOPE - NOPE_LO))
    return jnp.concatenate([r1, n1, r2, n2, z(LANES - MLA_NOPE - MLA_ROPE)], axis=-1)


def _mla_weights(w_down, w_uq, w_ukv):
    lat = MLA_Q_LORA + MLA_KV_LORA
    wd = jnp.concatenate([w_down[:, :lat], _head_lanes(None, w_down[:, lat:])], axis=1)
    wq = w_uq.reshape(MLA_Q_LORA, MLA_HEADS, MLA_NOPE + MLA_ROPE)
    wq = _head_lanes(wq[:, :, :MLA_NOPE], wq[:, :, MLA_NOPE:]).reshape(MLA_Q_LORA, MLA_HEADS * LANES)
    wkv = w_ukv.reshape(MLA_KV_LORA, MLA_HEADS, MLA_NOPE + MLA_V)
    wk = _head_lanes(wkv[:, :, :MLA_NOPE], None).reshape(MLA_KV_LORA, MLA_HEADS * LANES)
    wv = wkv[:, :, MLA_NOPE:].reshape(MLA_KV_LORA, MLA_HEADS * MLA_V)
    return wd.astype(BF16), wq.astype(BF16), wk.astype(BF16), wv.astype(BF16)


def _rope_tables(s):
    pos = np.arange(s, dtype=np.float64)
    freq = ROPE_BASE ** (-np.arange(0, MLA_ROPE, 2, dtype=np.float64) / MLA_ROPE)
    ang = pos[:, None] * freq[None, :]
    lo = slice(0, ROPE_HALF)
    hi = slice(LANES // 2, LANES // 2 + ROPE_HALF)
    cos_t = np.ones((s, LANES))
    sin_t = np.zeros((s, LANES))
    cos_t[:, lo] = cos_t[:, hi] = np.cos(ang)
    sin_t[:, lo] = -np.sin(ang)
    sin_t[:, hi] = np.sin(ang)
    f32 = lambda a: jnp.asarray(a.astype(np.float32))
    return f32(cos_t * MLA_Q_SCALE), f32(sin_t * MLA_Q_SCALE), f32(cos_t), f32(sin_t)


def _tile_m(m):
    return 512 if m % 512 == 0 else m


def kernel(x, p, ln1_g, ln1_b, ln2_g, ln2_b, mlp_w1, mlp_w2, ple_w_proj, ple_w_gate, hy_w_in, hy_conv_w, hy_conv_b, hy_ga_w, hy_ga_b, hy_gx_w, hy_gx_b, hy_lambda, hy_w_out, mla_w_down, mla_q_norm, mla_kv_norm, mla_w_uq, mla_w_ukv, mla_w_out):
    bsz, s, d = x.shape
    m = bsz * s
    tm = _tile_m(m)
    tf = 1024
    row = lambda a: a.reshape(1, -1)
    x2 = x.reshape(m, d)

    rows3 = lambda a: a.reshape(a.shape[0], 1, -1)
    tail_params = (rows3(ln1_g), rows3(ln1_b), mlp_w1.astype(BF16), mlp_w2.astype(BF16),
                   ple_w_gate.astype(BF16), ple_w_proj.astype(BF16), rows3(ln2_g), rows3(ln2_b))
    p3 = p.reshape(p.shape[0], m, -1)

    def layer_tail(acts, wos, x2, i):
        return _tail(acts, wos, x2, p3, i, *tail_params, tm, tf)

    xy, q, iq, misc = _inproj(x2, _hy_in_weight(hy_w_in[0]), tm)
    rec = _rglru(xy.reshape(bsz, s, -1), hy_conv_w[0], row(hy_conv_b[0]),
                 _block_diag_groups(hy_ga_w[0]), row(hy_ga_b[0]),
                 _block_diag_groups(hy_gx_w[0]), row(hy_gx_b[0]), row(hy_lambda[0]))
    att = _dsa(q.reshape(bsz, s, -1), iq.reshape(bsz, s, -1), misc.reshape(bsz, s, -1))
    w_out = hy_w_out[0].astype(BF16)
    x2 = layer_tail([rec.reshape(m, -1), att.reshape(m, -1)], [w_out[:RNN_WIDTH], w_out[RNN_WIDTH:]], x2, 0)

    wd, wq, wk, wv = _mla_weights(mla_w_down[0], mla_w_uq[0], mla_w_ukv[0])
    qp, kp, vp = _mla_proj(x2, wd, row(mla_q_norm[0]), row(mla_kv_norm[0]), wq, wk, wv,
                           _rope_tables(s), min(tm, s), s)
    o = _mla_attn(qp.reshape(bsz, s, -1), kp.reshape(bsz, s, -1), vp.reshape(bsz, s, -1), min(256, s))
    x2 = layer_tail([o.reshape(m, -1)], [mla_w_out[0].astype(BF16)], x2, 1)
    return x2.reshape(bsz, s, d)
```

```python
import functools
import math

import numpy as np
import jax
import jax.numpy as jnp
from jax import lax
from jax.experimental import pallas as pl
from jax.experimental.pallas import tpu as pltpu

F32 = jnp.float32
BF16 = jnp.bfloat16

RNN_WIDTH = 512
RNN_BLOCKS = 8
CONV_WIDTH = 4
LRU_C = 8.0
DSA_HEADS = 8
DSA_HEAD_DIM = 64
IDX_HEADS = 4
IDX_DIM = 64
DSA_TOPK_MAX = 256
MLA_HEADS = 16
MLA_Q_LORA = 512
MLA_KV_LORA = 256
MLA_NOPE = 64
MLA_ROPE = 32
MLA_V = 64
ROPE_BASE = 10000.0
DEPTH = 2
DN_ALPHA = (2 * DEPTH) ** 0.25
LN_EPS = 1e-5
RMS_EPS = 1e-6

LANES = 128
SUBLANES = 8
VMEM_LIMIT_BYTES = 56 * 1024 * 1024

MASK_NEG = -3.0e38
INT32_MIN = -(2 ** 31)
STAGE_LAGS = (1, 2, 4)
DSA_Q_SCALE = DSA_HEAD_DIM ** -0.5
DSA_PAIR_MAX_EXTENT = 768


def _cparams(sem):
    return pltpu.CompilerParams(dimension_semantics=sem, vmem_limit_bytes=VMEM_LIMIT_BYTES)


def _dot(a, b):
    return jnp.dot(a, b, preferred_element_type=F32)


def _dot_nt(a, b):
    return lax.dot_general(a, b, (((1,), (1,)), ((), ())), preferred_element_type=F32)


def _layer_norm(y, g, b):
    mu = jnp.mean(y, axis=-1, keepdims=True)
    yc = y - mu
    var = jnp.mean(yc * yc, axis=-1, keepdims=True)
    return yc * lax.rsqrt(var + LN_EPS) * g + b


IN_XY = 2 * RNN_WIDTH
IN_Q = DSA_HEADS * DSA_HEAD_DIM
IN_IQ = IDX_HEADS * IDX_DIM
IN_MISC = 2 * LANES
IN_OFFSETS = tuple(int(v) for v in np.cumsum([0, IN_XY, IN_Q, IN_IQ, IN_MISC]))


def _inproj_kernel(x_ref, w_ref, xy_ref, q_ref, iq_ref, misc_ref):
    xb = x_ref[...].astype(BF16)
    cols = [w_ref[:, a:b] for a, b in zip(IN_OFFSETS[:-1], IN_OFFSETS[1:])]
    xy_ref[...] = _dot(xb, cols[0])
    q_ref[...] = (_dot(xb, cols[1]) * DSA_Q_SCALE).astype(BF16)
    iq_ref[...] = _dot(xb, cols[2])
    misc_ref[...] = _dot(xb, cols[3])


def _inproj(x2, w, tm):
    m, d = x2.shape
    n = w.shape[1]
    widths = (IN_XY, IN_Q, IN_IQ, IN_MISC)
    dtypes = (F32, BF16, F32, F32)
    return pl.pallas_call(
        _inproj_kernel,
        grid=(m // tm,),
        in_specs=[pl.BlockSpec((tm, d), lambda i: (i, 0)),
                  pl.BlockSpec((d, n), lambda i: (0, 0))],
        out_specs=[pl.BlockSpec((tm, wd), lambda i: (i, 0)) for wd in widths],
        out_shape=[jax.ShapeDtypeStruct((m, wd), dt) for wd, dt in zip(widths, dtypes)],
        compiler_params=_cparams(("parallel",)),
        name="hy_inproj",
    )(x2, w)


def _rglru_kernel(xr_ref, yr_ref, cw_ref, cb_ref, wa_ref, ba_ref, wx_ref, bx_ref, lam_ref,
                  o_ref, xpad_ref):
    s = xr_ref.shape[1]
    c = xr_ref.shape[2]
    xpad_ref[0:SUBLANES, :] = jnp.zeros((SUBLANES, c), F32)
    xpad_ref[SUBLANES:SUBLANES + s, :] = xr_ref[0]
    xc = None
    for k in range(CONV_WIDTH):
        start = SUBLANES - (CONV_WIDTH - 1) + k
        term = xpad_ref[start:start + s, :] * cw_ref[k:k + 1, :]
        xc = term if xc is None else xc + term
    xc = xc + cb_ref[...]
    xb = xc.astype(BF16)
    r = jax.nn.sigmoid(_dot(xb, wa_ref[0]) + ba_ref[...])
    gi = jax.nn.sigmoid(_dot(xb, wx_ref[0]) + bx_ref[...])
    nl = -lam_ref[...]
    softplus = jnp.maximum(nl, 0.0) + jnp.log1p(jnp.exp(-jnp.abs(nl)))
    log_a = (-LRU_C) * r * softplus
    a = jnp.exp(log_a)
    y = -jnp.tanh(log_a) * (a * a + 1.0)
    h = jnp.where(y > 0.0, y * lax.rsqrt(y), 0.0) * (gi * xc)
    row = lax.broadcasted_iota(jnp.int32, (s, c), 0)
    sh = 1
    while sh < s:
        if sh < SUBLANES:
            valid = row >= sh
            h = a * jnp.where(valid, pltpu.roll(h, sh, axis=0), 0.0) + h
            a = a * jnp.where(valid, pltpu.roll(a, sh, axis=0), 1.0)
        else:
            h = jnp.concatenate([h[:sh], a[sh:] * h[:s - sh] + h[sh:]], axis=0)
            if sh * 2 < s:
                a = jnp.concatenate([a[:sh], a[sh:] * a[:s - sh]], axis=0)
        sh *= 2
    o_ref[0] = (h * jax.nn.gelu(yr_ref[0])).astype(o_ref.dtype)


def _rglru(xy, conv_w, conv_b, wa, ba, wx, bx, lam):
    bsz, s, _ = xy.shape
    c = LANES
    ng = RNN_WIDTH // c
    vec = lambda: pl.BlockSpec((1, c), lambda b, g: (0, g))
    return pl.pallas_call(
        _rglru_kernel,
        grid=(bsz, ng),
        in_specs=[pl.BlockSpec((1, s, c), lambda b, g: (b, 0, g)),
                  pl.BlockSpec((1, s, c), lambda b, g: (b, 0, ng + g)),
                  pl.BlockSpec((CONV_WIDTH, c), lambda b, g: (0, g)),
                  vec(),
                  pl.BlockSpec((1, c, c), lambda b, g: (g, 0, 0)),
                  vec(),
                  pl.BlockSpec((1, c, c), lambda b, g: (g, 0, 0)),
                  vec(),
                  vec()],
        out_specs=pl.BlockSpec((1, s, c), lambda b, g: (b, 0, g)),
        out_shape=jax.ShapeDtypeStruct((bsz, s, RNN_WIDTH), BF16),
        scratch_shapes=[pltpu.VMEM((s + SUBLANES, c), F32)],
        compiler_params=_cparams(("parallel", "parallel")),
        name="hy_rglru",
    )(xy, xy, conv_w, conv_b, wa, ba, wx, bx, lam)


def _ordered_int_to_float(c):
    return lax.bitcast_convert_type(c ^ (lax.shift_right_arithmetic(c, 31) & jnp.int32(0x7FFFFFFF)), F32)


def _reduce_keys(x, reduce_fn, combine_fn, groups=8):
    rows = x.shape[0]
    step = max(rows // groups, SUBLANES)
    parts = [reduce_fn(x[r:r + step, :], axis=0, keepdims=True) for r in range(0, rows, step)]
    while len(parts) > 1:
        parts = [combine_fn(parts[i], parts[i + 1]) if i + 1 < len(parts) else parts[i]
                 for i in range(0, len(parts), 2)]
    return parts[0]


def _dsa_prepare(q_ref, mk_ref):
    tq = q_ref.shape[1]
    ext = mk_ref.shape[1]
    d0 = ext - tq
    keep = (lax.broadcasted_iota(jnp.int32, (tq, tq), 1)
            <= lax.broadcasted_iota(jnp.int32, (tq, tq), 0))

    mk = mk_ref[0]
    vlane = lax.broadcasted_iota(jnp.int32, (ext, DSA_HEAD_DIM), 1)
    v_aug = jnp.concatenate([mk[:, DSA_HEAD_DIM:2 * DSA_HEAD_DIM],
                             jnp.where(vlane == 0, 1.0, 0.0)], axis=1).astype(BF16)
    ik_b = mk[:, 2 * DSA_HEAD_DIM:2 * DSA_HEAD_DIM + IDX_DIM].astype(BF16)
    kcol = lax.broadcasted_iota(jnp.int32, (ext, 1), 0)
    lane = lax.broadcasted_iota(jnp.int32, (1, LANES), 1)
    pos_hi = (kcol & jnp.int32(-LANES)).astype(F32)
    pos_lo = (kcol & jnp.int32(LANES - 1)).astype(F32)
    k_aug = jnp.where(lane < DSA_HEAD_DIM, mk[:, 0:LANES],
                      jnp.where(lane == DSA_HEAD_DIM, pos_hi,
                                jnp.where(lane == DSA_HEAD_DIM + 1, pos_lo, 0.0))).astype(BF16)
    return d0, keep, ik_b, k_aug, v_aug


def _dsa_select_steps(rows, iq_ref, mq_ref, sc_ref, bias_ref, ik_b, keep, *, topk):
    tq, ext = sc_ref.shape
    nr = rows.stop - rows.start
    d0 = ext - tq
    kf = float(topk)
    st = {}

    def scores():
        iw0 = 2 * DSA_HEAD_DIM + IDX_DIM
        iw = mq_ref[0, rows, iw0:iw0 + IDX_HEADS] * (IDX_HEADS ** -0.5 * IDX_DIM ** -0.5)
        iq = iq_ref[0, rows, :].astype(BF16)
        sc = None
        for h in range(IDX_HEADS):
            d = _dot_nt(iq[:, h * IDX_DIM:(h + 1) * IDX_DIM], ik_b)
            term = jnp.maximum(d, 0.0) * iw[:, h:h + 1]
            sc = term if sc is None else sc + term
        if d0 > 0:
            sc_ref[rows, 0:d0] = sc[:, 0:d0]
        sc_ref[rows, d0:ext] = jnp.where(keep[rows, :], sc[:, d0:ext], MASK_NEG)
        st["tau"] = jnp.full((nr, 1), INT32_MIN, jnp.int32)

    def search(i):
        cand = st["tau"] + jnp.int32(1 << (31 - i) if i else INT32_MIN)
        ge = sc_ref[rows, :] >= _ordered_int_to_float(cand)
        cnt = jnp.sum(jnp.where(ge, 1.0, 0.0), axis=-1, keepdims=True)
        st["tau"] = jnp.where(cnt >= kf, cand, st["tau"])

    def ties():
        tau = _ordered_int_to_float(st.pop("tau"))
        sc = sc_ref[rows, :]
        gt = sc > tau
        eq = sc == tau
        need = kf - jnp.sum(jnp.where(gt, 1.0, 0.0), axis=-1, keepdims=True)
        eqf = jnp.where(eq, 1.0, 0.0).astype(BF16)
        tri = (lax.broadcasted_iota(jnp.int32, (LANES, LANES), 0)
               <= lax.broadcasted_iota(jnp.int32, (LANES, LANES), 1)).astype(BF16)
        off = jnp.zeros((nr, 1), F32)
        for c in range(ext // LANES):
            sl = slice(c * LANES, (c + 1) * LANES)
            pc = _dot(eqf[:, sl], tri)
            take = gt[:, sl] | (eq[:, sl] & ((pc + off) <= need))
            if c * LANES >= d0:
                take = take & keep[rows, c * LANES - d0:(c + 1) * LANES - d0]
            bias_ref[rows, sl] = jnp.where(take, 0.0, MASK_NEG)
            off = off + pc[:, LANES - 1:LANES]

    return [scores] + [functools.partial(search, i) for i in range(32)] + [ties]


def _dsa_attend_steps(rows, q_ref, o_ref, bias_ref, k_aug, v_aug):
    nr = rows.stop - rows.start
    tail_lane = lax.broadcasted_iota(jnp.int32, (nr, LANES - DSA_HEAD_DIM), 1)

    def logits(h):
        slope = 2.0 ** (-8.0 * (h + 1) / DSA_HEADS)
        tail = jnp.where(tail_lane < 2, slope, 0.0).astype(BF16)
        qh = q_ref[0, rows, h * DSA_HEAD_DIM:(h + 1) * DSA_HEAD_DIM]
        return _dot_nt(jnp.concatenate([qh, tail], axis=1), k_aug) + bias_ref[rows, :]

    def values(h, p):
        o = _dot(p, v_aug)
        o = o[:, 0:DSA_HEAD_DIM] / o[:, DSA_HEAD_DIM:DSA_HEAD_DIM + 1]
        o_ref[0, rows, h * DSA_HEAD_DIM:(h + 1) * DSA_HEAD_DIM] = o.astype(o_ref.dtype)

    lg, mx, pr = {}, {}, {}
    d_max, d_exp, d_pv = STAGE_LAGS

    def step(t):
        if t < DSA_HEADS:
            lg[t] = logits(t)
        if 0 <= t - d_max < DSA_HEADS:
            mx[t - d_max] = jnp.max(lg[t - d_max], axis=-1, keepdims=True)
        if 0 <= t - d_exp < DSA_HEADS:
            h = t - d_exp
            pr[h] = jnp.exp(lg.pop(h) - mx.pop(h)).astype(BF16)
        if 0 <= t - d_pv < DSA_HEADS:
            values(t - d_pv, pr.pop(t - d_pv))

    return [functools.partial(step, t) for t in range(DSA_HEADS + d_pv)]


def _dsa_kernel(q_ref, iq_ref, mq_ref, mk_ref, o_ref, sc_ref, bias_ref, *, topk):
    tq = q_ref.shape[1]
    rows = slice(0, tq)
    select, attend = [], []
    for bi in range(q_ref.shape[0]):
        one = lambda ref: ref.at[bi:bi + 1]
        sc_b, bias_b = sc_ref.at[bi], bias_ref.at[bi]
        d0, keep, ik_b, k_aug, v_aug = _dsa_prepare(one(q_ref), one(mk_ref))
        if mk_ref.shape[1] > topk:
            select.append(_dsa_select_steps(rows, one(iq_ref), one(mq_ref), sc_b, bias_b, ik_b, keep,
                                            topk=topk))
        else:
            if d0 > 0:
                bias_b[:, 0:d0] = jnp.zeros((tq, d0), F32)
            bias_b[:, d0:] = jnp.where(keep, 0.0, MASK_NEG)
        attend.append(_dsa_attend_steps(rows, one(q_ref), one(o_ref), bias_b, k_aug, v_aug))
    for steps in list(zip(*select)) + list(zip(*attend)):
        for step in steps:
            step()


def _dsa(q, iq, misc):
    bsz, s, _ = q.shape
    topk = min(DSA_TOPK_MAX, s // 4)
    tq = min(256, s)
    hd = DSA_HEADS * DSA_HEAD_DIM
    outs = []
    for c in range(s // tq):
        ext = (c + 1) * tq
        bb = 2 if (ext <= DSA_PAIR_MAX_EXTENT and bsz % 2 == 0) else 1
        outs.append(pl.pallas_call(
            functools.partial(_dsa_kernel, topk=topk),
            grid=(bsz // bb,),
            in_specs=[pl.BlockSpec((bb, tq, hd), lambda b, c=c: (b, c, 0)),
                      pl.BlockSpec((bb, tq, IDX_HEADS * IDX_DIM), lambda b, c=c: (b, c, 0)),
                      pl.BlockSpec((bb, tq, 256), lambda b, c=c: (b, c, 0)),
                      pl.BlockSpec((bb, ext, 256), lambda b: (b, 0, 0))],
            out_specs=pl.BlockSpec((bb, tq, hd), lambda b: (b, 0, 0)),
            out_shape=jax.ShapeDtypeStruct((bsz, tq, hd), BF16),
            scratch_shapes=[pltpu.VMEM((bb, tq, ext), F32), pltpu.VMEM((bb, tq, ext), F32)],
            compiler_params=_cparams(("parallel",)),
            name=f"hy_dsa_c{c}",
        )(q, iq, misc, misc))
    return outs[0] if len(outs) == 1 else jnp.concatenate(outs, axis=1)


def _tail_kernel(*refs, n_in, tf):
    a_refs = refs[:n_in]
    wo_refs = refs[n_in:2 * n_in]
    (x_ref, p_ref, g1_ref, b1_ref, w1_ref, w2_ref, wg_ref, wp_ref, g2_ref, b2_ref,
     o_ref, acc_ref) = refs[2 * n_in:]
    n_chunks = w1_ref.shape[1] // tf

    def steps_for(rows):
        st = {}

        def out_proj():
            m = None
            for a_ref, w_ref in zip(a_refs, wo_refs):
                t = _dot(a_ref[rows, :].astype(BF16), w_ref[...])
                m = t if m is None else m + t
            st["y"] = DN_ALPHA * x_ref[rows, :] + m

        def norm1():
            st["h"] = _layer_norm(st.pop("y"), g1_ref[...], b1_ref[...])
            st["hb"] = st["h"].astype(BF16)

        def gate():
            g = jax.nn.sigmoid(_dot(st["hb"], wg_ref[...]))
            acc_ref[rows, :] = DN_ALPHA * st.pop("h") + g * _dot(p_ref[rows, :].astype(BF16), wp_ref[...])

        def mlp(c):
            a = jnp.maximum(_dot(st["hb"], w1_ref[:, c * tf:(c + 1) * tf]), 0.0)
            acc_ref[rows, :] += _dot((a * a).astype(BF16), w2_ref[c * tf:(c + 1) * tf, :])

        def norm2():
            o_ref[rows, :] = _layer_norm(acc_ref[rows, :], g2_ref[...], b2_ref[...])

        return [out_proj, norm1, gate] + [functools.partial(mlp, c) for c in range(n_chunks)] + [norm2]

    half = x_ref.shape[0] // 2
    first, second = steps_for(slice(0, half)), steps_for(slice(half, 2 * half))
    lag = 2
    for t in range(len(first) + lag):
        if t < len(first):
            first[t]()
        if 0 <= t - lag < len(second):
            second[t - lag]()


def _tail(acts, wos, x2, p3, layer, g1, b1, w1, w2, wg, wp, g2, b2, tm, tf):
    m, d = x2.shape
    n_in = len(acts)
    rows = lambda a: pl.BlockSpec((tm, a.shape[1]), lambda i: (i, 0))
    resident = lambda a: pl.BlockSpec(a.shape, lambda i: (0, 0), pipeline_mode=pl.Buffered(1))
    of_layer = lambda a: pl.BlockSpec((None,) + a.shape[1:], lambda i: (layer, 0, 0),
                                      pipeline_mode=pl.Buffered(1))
    params = [g1, b1, w1, w2, wg, wp, g2, b2]
    return pl.pallas_call(
        functools.partial(_tail_kernel, n_in=n_in, tf=tf),
        grid=(m // tm,),
        in_specs=([rows(a) for a in acts] + [resident(w) for w in wos]
                  + [rows(x2), pl.BlockSpec((None, tm, p3.shape[2]), lambda i: (layer, i, 0))]
                  + [of_layer(a) for a in params]),
        out_specs=pl.BlockSpec((tm, d), lambda i: (i, 0)),
        out_shape=jax.ShapeDtypeStruct((m, d), F32),
        scratch_shapes=[pltpu.VMEM((tm, d), F32)],
        compiler_params=_cparams(("parallel",)),
        name="layer_tail",
    )(*acts, *wos, x2, p3, *params)


MLA_Q_SCALE = (MLA_NOPE + MLA_ROPE) ** -0.5 * math.log2(math.e)
ROPE_HALF = MLA_ROPE // 2
NOPE_LO = LANES // 2 - ROPE_HALF


def _rope_group(t, cos_t, sin_t):
    return t * cos_t + pltpu.roll(t, LANES // 2, axis=1) * sin_t


def _mla_proj_kernel(x_ref, wd_ref, qg_ref, kvg_ref, wq_ref, wk_ref, wv_ref,
                     cosq_ref, sinq_ref, cosk_ref, sink_ref, q_ref, k_ref, v_ref):
    xb = x_ref[...].astype(BF16)
    down = _dot(xb, wd_ref[...])
    cq = down[:, 0:MLA_Q_LORA]
    ckv = down[:, MLA_Q_LORA:MLA_Q_LORA + MLA_KV_LORA]
    krg = down[:, MLA_Q_LORA + MLA_KV_LORA:]
    cqn = (cq * lax.rsqrt(jnp.mean(cq * cq, axis=-1, keepdims=True) + RMS_EPS) * qg_ref[...]).astype(BF16)
    ckvn = (ckv * lax.rsqrt(jnp.mean(ckv * ckv, axis=-1, keepdims=True) + RMS_EPS) * kvg_ref[...]).astype(BF16)
    cos_q, sin_q = cosq_ref[...], sinq_ref[...]
    kr = _rope_group(krg, cosk_ref[...], sink_ref[...])
    q = _dot(cqn, wq_ref[...])
    k = _dot(ckvn, wk_ref[...])
    for h in range(MLA_HEADS):
        sl = slice(h * LANES, (h + 1) * LANES)
        q_ref[:, sl] = _rope_group(q[:, sl], cos_q, sin_q).astype(BF16)
        k_ref[:, sl] = (k[:, sl] + kr).astype(BF16)
    v_ref[...] = _dot(ckvn, wv_ref[...]).astype(BF16)


def _mla_proj(x2, wd, qg, kvg, wq, wk, wv, tables, tm, s):
    m, d = x2.shape
    nt = s // tm
    full = lambda a: pl.BlockSpec(a.shape, lambda i: (0, 0))
    tab = lambda: pl.BlockSpec((tm, LANES), lambda i: (i % nt, 0))
    hq = MLA_HEADS * LANES
    hv = MLA_HEADS * MLA_V
    return pl.pallas_call(
        _mla_proj_kernel,
        grid=(m // tm,),
        in_specs=[pl.BlockSpec((tm, d), lambda i: (i, 0)), full(wd), full(qg), full(kvg),
                  full(wq), full(wk), full(wv)] + [tab() for _ in tables],
        out_specs=[pl.BlockSpec((tm, hq), lambda i: (i, 0)),
                   pl.BlockSpec((tm, hq), lambda i: (i, 0)),
                   pl.BlockSpec((tm, hv), lambda i: (i, 0))],
        out_shape=[jax.ShapeDtypeStruct((m, hq), BF16),
                   jax.ShapeDtypeStruct((m, hq), BF16),
                   jax.ShapeDtypeStruct((m, hv), BF16)],
        compiler_params=_cparams(("parallel",)),
        name="mla_proj",
    )(x2, wd, qg, kvg, wq, wk, wv, *tables)


def _mla_attn_kernel(q_ref, k_ref, v_ref, o_ref, ot_ref, *, tq, heads_per_step, key_chunk):
    s = q_ref.shape[1]
    keep_t = (lax.broadcasted_iota(jnp.int32, (tq, tq), 0)
              <= lax.broadcasted_iota(jnp.int32, (tq, tq), 1))
    v_t = jnp.transpose(v_ref[0].astype(F32)).astype(BF16)
    ones_rows = jnp.ones((2 * SUBLANES, s), BF16)
    lhs = [jnp.concatenate([v_t[hh * MLA_V:(hh + 1) * MLA_V, :], ones_rows], axis=0)
           for hh in range(heads_per_step)]
    units = []
    for j in range(s // tq):
        ext = (j + 1) * tq
        starts = list(range(0, ext - tq, key_chunk)) or [0]
        bounds = [(a, b) for a, b in zip(starts, starts[1:] + [ext])]
        for hh in range(heads_per_step):
            units += [(j * tq, hh, k0, k1) for k0, k1 in bounds]

    def logits(r0, hh, k0, k1):
        ql = slice(hh * LANES, (hh + 1) * LANES)
        sc = _dot_nt(k_ref[0, k0:k1, ql], q_ref[0, r0:r0 + tq, ql])
        if k1 == r0 + tq:
            diag = jnp.where(keep_t, sc[k1 - k0 - tq:, :], MASK_NEG)
            sc = diag if k1 - k0 == tq else jnp.concatenate([sc[:k1 - k0 - tq, :], diag], axis=0)
        return sc

    def probs(sc):
        m = _reduce_keys(sc, jnp.max, jnp.maximum)
        return jnp.exp2(sc - m).astype(BF16), m

    running = {}

    def values(r0, hh, k0, k1, p, m):
        acc = _dot(lhs[hh][:, k0:k1], p)
        if k0 > 0:
            m_run, acc_run = running.pop((r0, hh))
            m_new = jnp.maximum(m_run, m)
            acc = acc_run * jnp.exp2(m_run - m_new) + acc * jnp.exp2(m - m_new)
            m = m_new
        if k1 == r0 + tq:
            ot_ref[hh * MLA_V:(hh + 1) * MLA_V, r0:r0 + tq] = acc[0:MLA_V, :] / acc[MLA_V:MLA_V + 1, :]
        else:
            running[(r0, hh)] = (m, acc)

    n = len(units)
    stage_s, stage_m, stage_p = {}, {}, {}
    d_max, d_exp, d_pv = STAGE_LAGS
    for t in range(n + d_pv):
        if t < n:
            stage_s[t] = logits(*units[t])
        if 0 <= t - d_max < n:
            u = t - d_max
            stage_m[u] = _reduce_keys(stage_s[u], jnp.max, jnp.maximum)
        if 0 <= t - d_exp < n:
            u = t - d_exp
            stage_p[u] = jnp.exp2(stage_s.pop(u) - stage_m[u]).astype(BF16)
        if 0 <= t - d_pv < n:
            u = t - d_pv
            values(*units[u], stage_p.pop(u), stage_m.pop(u))
    o_ref[0] = jnp.transpose(ot_ref[...]).astype(o_ref.dtype)


def _mla_attn(q, k, v, tq):
    bsz, s, _ = q.shape
    hps = 4
    return pl.pallas_call(
        functools.partial(_mla_attn_kernel, tq=tq, heads_per_step=hps, key_chunk=2 * tq),
        grid=(bsz, MLA_HEADS // hps),
        in_specs=[pl.BlockSpec((1, s, hps * LANES), lambda b, h: (b, 0, h)),
                  pl.BlockSpec((1, s, hps * LANES), lambda b, h: (b, 0, h)),
                  pl.BlockSpec((1, s, hps * MLA_V), lambda b, h: (b, 0, h))],
        out_specs=pl.BlockSpec((1, s, hps * MLA_V), lambda b, h: (b, 0, h)),
        out_shape=jax.ShapeDtypeStruct((bsz, s, MLA_HEADS * MLA_V), BF16),
        scratch_shapes=[pltpu.VMEM((hps * MLA_V, s), F32)],
        compiler_params=_cparams(("parallel", "parallel")),
        name="mla_attn",
    )(q, k, v)


def _hy_in_weight(w_in):
    xr, yr, q, k, v, iq, ik, iw = jnp.split(
        w_in, np.cumsum([512, 512, 512, 64, 64, 256, 64, 4])[:-1].tolist(), axis=1)
    pad = jnp.zeros((w_in.shape[0], 256 - 64 * 3 - 4), w_in.dtype)
    return jnp.concatenate([xr, yr, q, iq, k, v, ik, iw, pad], axis=1).astype(BF16)


def _block_diag_groups(w):
    bw = w.shape[-1]
    per = LANES // bw
    ng = w.shape[0] // per
    on_diag = jnp.eye(per, dtype=bool)[None, :, None, :, None]
    out = jnp.where(on_diag, w.reshape(ng, per, bw, 1, bw), 0.0)
    return out.reshape(ng, LANES, LANES).astype(BF16)


def _head_lanes(nope, rope):
    lead = nope.shape[:-1] if nope is not None else rope.shape[:-1]
    dt = nope.dtype if nope is not None else rope.dtype
    z = lambda n: jnp.zeros(lead + (n,), dt)
    r1, r2 = (rope[..., :ROPE_HALF], rope[..., ROPE_HALF:]) if rope is not None else (z(ROPE_HALF), z(ROPE_HALF))
    n1, n2 = (nope[..., :NOPE_LO], nope[..., NOPE_LO:]) if nope is not None else (z(NOPE_LO), z(MLA_NOPE - NOPE_LO))
    return jnp.concatenate([r1, n1, r2, n2, z(LANES - MLA_NOPE - MLA_ROPE)], axis=-1)


def _mla_weights(w_down, w_uq, w_ukv):
    lat = MLA_Q_LORA + MLA_KV_LORA
    wd = jnp.concatenate([w_down[:, :lat], _head_lanes(None, w_down[:, lat:])], axis=1)
    wq = w_uq.reshape(MLA_Q_LORA, MLA_HEADS, MLA_NOPE + MLA_ROPE)
    wq = _head_lanes(wq[:, :, :MLA_NOPE], wq[:, :, MLA_NOPE:]).reshape(MLA_Q_LORA, MLA_HEADS * LANES)
    wkv = w_ukv.reshape(MLA_KV_LORA, MLA_HEADS, MLA_NOPE + MLA_V)
    wk = _head_lanes(wkv[:, :, :MLA_NOPE], None).reshape(MLA_KV_LORA, MLA_HEADS * LANES)
    wv = wkv[:, :, MLA_NOPE:].reshape(MLA_KV_LORA, MLA_HEADS * MLA_V)
    return wd.astype(BF16), wq.astype(BF16), wk.astype(BF16), wv.astype(BF16)


def _rope_tables(s):
    pos = np.arange(s, dtype=np.float64)
    freq = ROPE_BASE ** (-np.arange(0, MLA_ROPE, 2, dtype=np.float64) / MLA_ROPE)
    ang = pos[:, None] * freq[None, :]
    lo = slice(0, ROPE_HALF)
    hi = slice(LANES // 2, LANES // 2 + ROPE_HALF)
    cos_t = np.ones((s, LANES))
    sin_t = np.zeros((s, LANES))
    cos_t[:, lo] = cos_t[:, hi] = np.cos(ang)
    sin_t[:, lo] = -np.sin(ang)
    sin_t[:, hi] = np.sin(ang)
    f32 = lambda a: jnp.asarray(a.astype(np.float32))
    return f32(cos_t * MLA_Q_SCALE), f32(sin_t * MLA_Q_SCALE), f32(cos_t), f32(sin_t)


def _tile_m(m):
    return 512 if m % 512 == 0 else m


def kernel(x, p, ln1_g, ln1_b, ln2_g, ln2_b, mlp_w1, mlp_w2, ple_w_proj, ple_w_gate, hy_w_in, hy_conv_w, hy_conv_b, hy_ga_w, hy_ga_b, hy_gx_w, hy_gx_b, hy_lambda, hy_w_out, mla_w_down, mla_q_norm, mla_kv_norm, mla_w_uq, mla_w_ukv, mla_w_out):
    bsz, s, d = x.shape
    m = bsz * s
    tm = _tile_m(m)
    tf = 1024
    row = lambda a: a.reshape(1, -1)
    x2 = x.reshape(m, d)

    rows3 = lambda a: a.reshape(a.shape[0], 1, -1)
    tail_params = (rows3(ln1_g), rows3(ln1_b), mlp_w1.astype(BF16), mlp_w2.astype(BF16),
                   ple_w_gate.astype(BF16), ple_w_proj.astype(BF16), rows3(ln2_g), rows3(ln2_b))
    p3 = p.reshape(p.shape[0], m, -1)

    def layer_tail(acts, wos, x2, i):
        return _tail(acts, wos, x2, p3, i, *tail_params, tm, tf)

    xy, q, iq, misc = _inproj(x2, _hy_in_weight(hy_w_in[0]), tm)
    rec = _rglru(xy.reshape(bsz, s, -1), hy_conv_w[0], row(hy_conv_b[0]),
                 _block_diag_groups(hy_ga_w[0]), row(hy_ga_b[0]),
                 _block_diag_groups(hy_gx_w[0]), row(hy_gx_b[0]), row(hy_lambda[0]))
    att = _dsa(q.reshape(bsz, s, -1), iq.reshape(bsz, s, -1), misc.reshape(bsz, s, -1))
    w_out = hy_w_out[0].astype(BF16)
    x2 = layer_tail([rec.reshape(m, -1), att.reshape(m, -1)], [w_out[:RNN_WIDTH], w_out[RNN_WIDTH:]], x2, 0)

    wd, wq, wk, wv = _mla_weights(mla_w_down[0], mla_w_uq[0], mla_w_ukv[0])
    qp, kp, vp = _mla_proj(x2, wd, row(mla_q_norm[0]), row(mla_kv_norm[0]), wq, wk, wv,
                           _rope_tables(s), min(tm, s), s)
    o = _mla_attn(qp.reshape(bsz, s, -1), kp.reshape(bsz, s, -1), vp.reshape(bsz, s, -1), min(256, s))
    x2 = layer_tail([o.reshape(m, -1)], [mla_w_out[0].astype(BF16)], x2, 1)
    return x2.reshape(bsz, s, d)
```

```python
import functools
import math

import numpy as np
import jax
import jax.numpy as jnp
from jax import lax
from jax.experimental import pallas as pl
from jax.experimental.pallas import tpu as pltpu

F32 = jnp.float32
BF16 = jnp.bfloat16

RNN_WIDTH = 512
RNN_BLOCKS = 8
CONV_WIDTH = 4
LRU_C = 8.0
DSA_HEADS = 8
DSA_HEAD_DIM = 64
IDX_HEADS = 4
IDX_DIM = 64
DSA_TOPK_MAX = 256
MLA_HEADS = 16
MLA_Q_LORA = 512
MLA_KV_LORA = 256
MLA_NOPE = 64
MLA_ROPE = 32
MLA_V = 64
ROPE_BASE = 10000.0
DEPTH = 2
DN_ALPHA = (2 * DEPTH) ** 0.25
LN_EPS = 1e-5
RMS_EPS = 1e-6

LANES = 128
SUBLANES = 8
VMEM_LIMIT_BYTES = 56 * 1024 * 1024

MASK_NEG = -3.0e38
INT32_MIN = -(2 ** 31)
STAGE_LAGS = (1, 2, 4)
DSA_Q_SCALE = DSA_HEAD_DIM ** -0.5
DSA_PAIR_MAX_EXTENT = 768


def _cparams(sem):
    return pltpu.CompilerParams(dimension_semantics=sem, vmem_limit_bytes=VMEM_LIMIT_BYTES)


def _dot(a, b):
    return jnp.dot(a, b, preferred_element_type=F32)


def _dot_nt(a, b):
    return lax.dot_general(a, b, (((1,), (1,)), ((), ())), preferred_element_type=F32)


def _layer_norm(y, g, b):
    mu = jnp.mean(y, axis=-1, keepdims=True)
    yc = y - mu
    var = jnp.mean(yc * yc, axis=-1, keepdims=True)
    return yc * lax.rsqrt(var + LN_EPS) * g + b


IN_XY = 2 * RNN_WIDTH
IN_Q = DSA_HEADS * DSA_HEAD_DIM
IN_IQ = IDX_HEADS * IDX_DIM
IN_MISC = 2 * LANES
IN_OFFSETS = tuple(int(v) for v in np.cumsum([0, IN_XY, IN_Q, IN_IQ, IN_MISC]))


def _inproj_kernel(x_ref, w_ref, xy_ref, q_ref, iq_ref, misc_ref):
    xb = x_ref[...].astype(BF16)
    cols = [w_ref[:, a:b] for a, b in zip(IN_OFFSETS[:-1], IN_OFFSETS[1:])]
    xy_ref[...] = _dot(xb, cols[0])
    q_ref[...] = (_dot(xb, cols[1]) * DSA_Q_SCALE).astype(BF16)
    iq_ref[...] = _dot(xb, cols[2])
    misc_ref[...] = _dot(xb, cols[3])


def _inproj(x2, w, tm):
    m, d = x2.shape
    n = w.shape[1]
    widths = (IN_XY, IN_Q, IN_IQ, IN_MISC)
    dtypes = (F32, BF16, F32, F32)
    return pl.pallas_call(
        _inproj_kernel,
        grid=(m // tm,),
        in_specs=[pl.BlockSpec((tm, d), lambda i: (i, 0)),
                  pl.BlockSpec((d, n), lambda i: (0, 0))],
        out_specs=[pl.BlockSpec((tm, wd), lambda i: (i, 0)) for wd in widths],
        out_shape=[jax.ShapeDtypeStruct((m, wd), dt) for wd, dt in zip(widths, dtypes)],
        compiler_params=_cparams(("parallel",)),
        name="hy_inproj",
    )(x2, w)


def _rglru_kernel(xr_ref, yr_ref, cw_ref, cb_ref, wa_ref, ba_ref, wx_ref, bx_ref, lam_ref,
                  o_ref, xpad_ref):
    s = xr_ref.shape[1]
    c = xr_ref.shape[2]
    xpad_ref[0:SUBLANES, :] = jnp.zeros((SUBLANES, c), F32)
    xpad_ref[SUBLANES:SUBLANES + s, :] = xr_ref[0]
    xc = None
    for k in range(CONV_WIDTH):
        start = SUBLANES - (CONV_WIDTH - 1) + k
        term = xpad_ref[start:start + s, :] * cw_ref[k:k + 1, :]
        xc = term if xc is None else xc + term
    xc = xc + cb_ref[...]
    xb = xc.astype(BF16)
    r = jax.nn.sigmoid(_dot(xb, wa_ref[0]) + ba_ref[...])
    gi = jax.nn.sigmoid(_dot(xb, wx_ref[0]) + bx_ref[...])
    nl = -lam_ref[...]
    softplus = jnp.maximum(nl, 0.0) + jnp.log1p(jnp.exp(-jnp.abs(nl)))
    log_a = (-LRU_C) * r * softplus
    a = jnp.exp(log_a)
    y = -jnp.tanh(log_a) * (a * a + 1.0)
    h = jnp.where(y > 0.0, y * lax.rsqrt(y), 0.0) * (gi * xc)
    row = lax.broadcasted_iota(jnp.int32, (s, c), 0)
    sh = 1
    while sh < s:
        if sh < SUBLANES:
            valid = row >= sh
            h = a * jnp.where(valid, pltpu.roll(h, sh, axis=0), 0.0) + h
            a = a * jnp.where(valid, pltpu.roll(a, sh, axis=0), 1.0)
        else:
            h = jnp.concatenate([h[:sh], a[sh:] * h[:s - sh] + h[sh:]], axis=0)
            if sh * 2 < s:
                a = jnp.concatenate([a[:sh], a[sh:] * a[:s - sh]], axis=0)
        sh *= 2
    o_ref[0] = (h * jax.nn.gelu(yr_ref[0])).astype(o_ref.dtype)


def _rglru(xy, conv_w, conv_b, wa, ba, wx, bx, lam):
    bsz, s, _ = xy.shape
    c = LANES
    ng = RNN_WIDTH // c
    vec = lambda: pl.BlockSpec((1, c), lambda b, g: (0, g))
    return pl.pallas_call(
        _rglru_kernel,
        grid=(bsz, ng),
        in_specs=[pl.BlockSpec((1, s, c), lambda b, g: (b, 0, g)),
                  pl.BlockSpec((1, s, c), lambda b, g: (b, 0, ng + g)),
                  pl.BlockSpec((CONV_WIDTH, c), lambda b, g: (0, g)),
                  vec(),
                  pl.BlockSpec((1, c, c), lambda b, g: (g, 0, 0)),
                  vec(),
                  pl.BlockSpec((1, c, c), lambda b, g: (g, 0, 0)),
                  vec(),
                  vec()],
        out_specs=pl.BlockSpec((1, s, c), lambda b, g: (b, 0, g)),
        out_shape=jax.ShapeDtypeStruct((bsz, s, RNN_WIDTH), BF16),
        scratch_shapes=[pltpu.VMEM((s + SUBLANES, c), F32)],
        compiler_params=_cparams(("parallel", "parallel")),
        name="hy_rglru",
    )(xy, xy, conv_w, conv_b, wa, ba, wx, bx, lam)


def _ordered_int_to_float(c):
    return lax.bitcast_convert_type(c ^ (lax.shift_right_arithmetic(c, 31) & jnp.int32(0x7FFFFFFF)), F32)


def _reduce_keys(x, reduce_fn, combine_fn, groups=8):
    rows = x.shape[0]
    step = max(rows // groups, SUBLANES)
    parts = [reduce_fn(x[r:r + step, :], axis=0, keepdims=True) for r in range(0, rows, step)]
    while len(parts) > 1:
        parts = [combine_fn(parts[i], parts[i + 1]) if i + 1 < len(parts) else parts[i]
                 for i in range(0, len(parts), 2)]
    return parts[0]


def _dsa_prepare(q_ref, mk_ref):
    tq = q_ref.shape[1]
    ext = mk_ref.shape[1]
    d0 = ext - tq
    keep = (lax.broadcasted_iota(jnp.int32, (tq, tq), 1)
            <= lax.broadcasted_iota(jnp.int32, (tq, tq), 0))

    mk = mk_ref[0]
    vlane = lax.broadcasted_iota(jnp.int32, (ext, DSA_HEAD_DIM), 1)
    v_aug = jnp.concatenate([mk[:, DSA_HEAD_DIM:2 * DSA_HEAD_DIM],
                             jnp.where(vlane == 0, 1.0, 0.0)], axis=1).astype(BF16)
    ik_b = mk[:, 2 * DSA_HEAD_DIM:2 * DSA_HEAD_DIM + IDX_DIM].astype(BF16)
    kcol = lax.broadcasted_iota(jnp.int32, (ext, 1), 0)
    lane = lax.broadcasted_iota(jnp.int32, (1, LANES), 1)
    pos_hi = (kcol & jnp.int32(-LANES)).astype(F32)
    pos_lo = (kcol & jnp.int32(LANES - 1)).astype(F32)
    k_aug = jnp.where(lane < DSA_HEAD_DIM, mk[:, 0:LANES],
                      jnp.where(lane == DSA_HEAD_DIM, pos_hi,
                                jnp.where(lane == DSA_HEAD_DIM + 1, pos_lo, 0.0))).astype(BF16)
    return d0, keep, ik_b, k_aug, v_aug


def _dsa_select_steps(rows, iq_ref, mq_ref, sc_ref, bias_ref, ik_b, keep, *, topk):
    tq, ext = sc_ref.shape
    nr = rows.stop - rows.start
    d0 = ext - tq
    kf = float(topk)
    st = {}

    def scores():
        iw0 = 2 * DSA_HEAD_DIM + IDX_DIM
        iw = mq_ref[0, rows, iw0:iw0 + IDX_HEADS] * (IDX_HEADS ** -0.5 * IDX_DIM ** -0.5)
        iq = iq_ref[0, rows, :].astype(BF16)
        sc = None
        for h in range(IDX_HEADS):
            d = _dot_nt(iq[:, h * IDX_DIM:(h + 1) * IDX_DIM], ik_b)
            term = jnp.maximum(d, 0.0) * iw[:, h:h + 1]
            sc = term if sc is None else sc + term
        if d0 > 0:
            sc_ref[rows, 0:d0] = sc[:, 0:d0]
        sc_ref[rows, d0:ext] = jnp.where(keep[rows, :], sc[:, d0:ext], MASK_NEG)
        st["tau"] = jnp.full((nr, 1), INT32_MIN, jnp.int32)

    def search(i):
        cand = st["tau"] + jnp.int32(1 << (31 - i) if i else INT32_MIN)
        ge = sc_ref[rows, :] >= _ordered_int_to_float(cand)
        cnt = jnp.sum(jnp.where(ge, 1.0, 0.0), axis=-1, keepdims=True)
        st["tau"] = jnp.where(cnt >= kf, cand, st["tau"])

    def ties():
        tau = _ordered_int_to_float(st.pop("tau"))
        sc = sc_ref[rows, :]
        gt = sc > tau
        eq = sc == tau
        need = kf - jnp.sum(jnp.where(gt, 1.0, 0.0), axis=-1, keepdims=True)
        eqf = jnp.where(eq, 1.0, 0.0).astype(BF16)
        tri = (lax.broadcasted_iota(jnp.int32, (LANES, LANES), 0)
               <= lax.broadcasted_iota(jnp.int32, (LANES, LANES), 1)).astype(BF16)
        off = jnp.zeros((nr, 1), F32)
        for c in range(ext // LANES):
            sl = slice(c * LANES, (c + 1) * LANES)
            pc = _dot(eqf[:, sl], tri)
            take = gt[:, sl] | (eq[:, sl] & ((pc + off) <= need))
            if c * LANES >= d0:
                take = take & keep[rows, c * LANES - d0:(c + 1) * LANES - d0]
            bias_ref[rows, sl] = jnp.where(take, 0.0, MASK_NEG)
            off = off + pc[:, LANES - 1:LANES]

    return [scores] + [functools.partial(search, i) for i in range(32)] + [ties]


def _dsa_attend_steps(rows, q_ref, o_ref, bias_ref, k_aug, v_aug):
    nr = rows.stop - rows.start
    tail_lane = lax.broadcasted_iota(jnp.int32, (nr, LANES - DSA_HEAD_DIM), 1)

    def logits(h):
        slope = 2.0 ** (-8.0 * (h + 1) / DSA_HEADS)
        tail = jnp.where(tail_lane < 2, slope, 0.0).astype(BF16)
        qh = q_ref[0, rows, h * DSA_HEAD_DIM:(h + 1) * DSA_HEAD_DIM]
        return _dot_nt(jnp.concatenate([qh, tail], axis=1), k_aug) + bias_ref[rows, :]

    def values(h, p):
        o = _dot(p, v_aug)
        o = o[:, 0:DSA_HEAD_DIM] / o[:, DSA_HEAD_DIM:DSA_HEAD_DIM + 1]
        o_ref[0, rows, h * DSA_HEAD_DIM:(h + 1) * DSA_HEAD_DIM] = o.astype(o_ref.dtype)

    lg, mx, pr = {}, {}, {}
    d_max, d_exp, d_pv = STAGE_LAGS

    def step(t):
        if t < DSA_HEADS:
            lg[t] = logits(t)
        if 0 <= t - d_max < DSA_HEADS:
            mx[t - d_max] = jnp.max(lg[t - d_max], axis=-1, keepdims=True)
        if 0 <= t - d_exp < DSA_HEADS:
            h = t - d_exp
            pr[h] = jnp.exp(lg.pop(h) - mx.pop(h)).astype(BF16)
        if 0 <= t - d_pv < DSA_HEADS:
            values(t - d_pv, pr.pop(t - d_pv))

    return [functools.partial(step, t) for t in range(DSA_HEADS + d_pv)]


def _dsa_kernel(q_ref, iq_ref, mq_ref, mk_ref, _aliased_out, o_ref, sc_ref, bias_ref, *, topk):
    tq = q_ref.shape[1]
    rows = slice(0, tq)
    select, attend = [], []
    for bi in range(q_ref.shape[0]):
        one = lambda ref: ref.at[bi:bi + 1]
        sc_b, bias_b = sc_ref.at[bi], bias_ref.at[bi]
        d0, keep, ik_b, k_aug, v_aug = _dsa_prepare(one(q_ref), one(mk_ref))
        if mk_ref.shape[1] > topk:
            select.append(_dsa_select_steps(rows, one(iq_ref), one(mq_ref), sc_b, bias_b, ik_b, keep,
                                            topk=topk))
        else:
            if d0 > 0:
                bias_b[:, 0:d0] = jnp.zeros((tq, d0), F32)
            bias_b[:, d0:] = jnp.where(keep, 0.0, MASK_NEG)
        attend.append(_dsa_attend_steps(rows, one(q_ref), one(o_ref), bias_b, k_aug, v_aug))
    for steps in list(zip(*select)) + list(zip(*attend)):
        for step in steps:
            step()


def _dsa(q, iq, misc):
    bsz, s, _ = q.shape
    topk = min(DSA_TOPK_MAX, s // 4)
    tq = min(256, s)
    hd = DSA_HEADS * DSA_HEAD_DIM
    att = jnp.zeros((bsz, s, hd), BF16)
    for c in range(s // tq):
        ext = (c + 1) * tq
        bb = 2 if (ext <= DSA_PAIR_MAX_EXTENT and bsz % 2 == 0) else 1
        att = pl.pallas_call(
            functools.partial(_dsa_kernel, topk=topk),
            grid=(bsz // bb,),
            in_specs=[pl.BlockSpec((bb, tq, hd), lambda b, c=c: (b, c, 0)),
                      pl.BlockSpec((bb, tq, IDX_HEADS * IDX_DIM), lambda b, c=c: (b, c, 0)),
                      pl.BlockSpec((bb, tq, 256), lambda b, c=c: (b, c, 0)),
                      pl.BlockSpec((bb, ext, 256), lambda b: (b, 0, 0)),
                      pl.BlockSpec(memory_space=pl.ANY)],
            out_specs=pl.BlockSpec((bb, tq, hd), lambda b, c=c: (b, c, 0)),
            out_shape=jax.ShapeDtypeStruct((bsz, s, hd), BF16),
            scratch_shapes=[pltpu.VMEM((bb, tq, ext), F32), pltpu.VMEM((bb, tq, ext), F32)],
            input_output_aliases={4: 0},
            compiler_params=_cparams(("parallel",)),
            name=f"hy_dsa_c{c}",
        )(q, iq, misc, misc, att)
    return att


def _tail_kernel(*refs, n_in, tf):
    a_refs = refs[:n_in]
    wo_refs = refs[n_in:2 * n_in]
    (x_ref, p_ref, g1_ref, b1_ref, w1_ref, w2_ref, wg_ref, wp_ref, g2_ref, b2_ref,
     o_ref, acc_ref) = refs[2 * n_in:]
    n_chunks = w1_ref.shape[1] // tf

    def steps_for(rows):
        st = {}

        def out_proj():
            m = None
            for a_ref, w_ref in zip(a_refs, wo_refs):
                t = _dot(a_ref[rows, :].astype(BF16), w_ref[...])
                m = t if m is None else m + t
            st["y"] = DN_ALPHA * x_ref[rows, :] + m

        def norm1():
            st["h"] = _layer_norm(st.pop("y"), g1_ref[...], b1_ref[...])
            st["hb"] = st["h"].astype(BF16)

        def gate():
            g = jax.nn.sigmoid(_dot(st["hb"], wg_ref[...]))
            acc_ref[rows, :] = DN_ALPHA * st.pop("h") + g * _dot(p_ref[rows, :].astype(BF16), wp_ref[...])

        def mlp(c):
            a = jnp.maximum(_dot(st["hb"], w1_ref[:, c * tf:(c + 1) * tf]), 0.0)
            acc_ref[rows, :] += _dot((a * a).astype(BF16), w2_ref[c * tf:(c + 1) * tf, :])

        def norm2():
            o_ref[rows, :] = _layer_norm(acc_ref[rows, :], g2_ref[...], b2_ref[...])

        return [out_proj, norm1, gate] + [functools.partial(mlp, c) for c in range(n_chunks)] + [norm2]

    half = x_ref.shape[0] // 2
    first, second = steps_for(slice(0, half)), steps_for(slice(half, 2 * half))
    lag = 2
    for t in range(len(first) + lag):
        if t < len(first):
            first[t]()
        if 0 <= t - lag < len(second):
            second[t - lag]()


def _tail(acts, wos, x2, p3, layer, g1, b1, w1, w2, wg, wp, g2, b2, tm, tf):
    m, d = x2.shape
    n_in = len(acts)
    rows = lambda a: pl.BlockSpec((tm, a.shape[1]), lambda i: (i, 0))
    resident = lambda a: pl.BlockSpec(a.shape, lambda i: (0, 0), pipeline_mode=pl.Buffered(1))
    of_layer = lambda a: pl.BlockSpec((None,) + a.shape[1:], lambda i: (layer, 0, 0),
                                      pipeline_mode=pl.Buffered(1))
    params = [g1, b1, w1, w2, wg, wp, g2, b2]
    return pl.pallas_call(
        functools.partial(_tail_kernel, n_in=n_in, tf=tf),
        grid=(m // tm,),
        in_specs=([rows(a) for a in acts] + [resident(w) for w in wos]
                  + [rows(x2), pl.BlockSpec((None, tm, p3.shape[2]), lambda i: (layer, i, 0))]
                  + [of_layer(a) for a in params]),
        out_specs=pl.BlockSpec((tm, d), lambda i: (i, 0)),
        out_shape=jax.ShapeDtypeStruct((m, d), F32),
        scratch_shapes=[pltpu.VMEM((tm, d), F32)],
        compiler_params=_cparams(("parallel",)),
        name="layer_tail",
    )(*acts, *wos, x2, p3, *params)


MLA_Q_SCALE = (MLA_NOPE + MLA_ROPE) ** -0.5 * math.log2(math.e)
ROPE_HALF = MLA_ROPE // 2
NOPE_LO = LANES // 2 - ROPE_HALF


def _rope_group(t, cos_t, sin_t):
    return t * cos_t + pltpu.roll(t, LANES // 2, axis=1) * sin_t


def _mla_proj_kernel(x_ref, wd_ref, qg_ref, kvg_ref, wq_ref, wk_ref, wv_ref,
                     cosq_ref, sinq_ref, cosk_ref, sink_ref, q_ref, k_ref, v_ref):
    xb = x_ref[...].astype(BF16)
    down = _dot(xb, wd_ref[...])
    cq = down[:, 0:MLA_Q_LORA]
    ckv = down[:, MLA_Q_LORA:MLA_Q_LORA + MLA_KV_LORA]
    krg = down[:, MLA_Q_LORA + MLA_KV_LORA:]
    cqn = (cq * lax.rsqrt(jnp.mean(cq * cq, axis=-1, keepdims=True) + RMS_EPS) * qg_ref[...]).astype(BF16)
    ckvn = (ckv * lax.rsqrt(jnp.mean(ckv * ckv, axis=-1, keepdims=True) + RMS_EPS) * kvg_ref[...]).astype(BF16)
    cos_q, sin_q = cosq_ref[...], sinq_ref[...]
    kr = _rope_group(krg, cosk_ref[...], sink_ref[...])
    q = _dot(cqn, wq_ref[...])
    k = _dot(ckvn, wk_ref[...])
    for h in range(MLA_HEADS):
        sl = slice(h * LANES, (h + 1) * LANES)
        q_ref[:, sl] = _rope_group(q[:, sl], cos_q, sin_q).astype(BF16)
        k_ref[:, sl] = (k[:, sl] + kr).astype(BF16)
    v_ref[...] = _dot(ckvn, wv_ref[...]).astype(BF16)


def _mla_proj(x2, wd, qg, kvg, wq, wk, wv, tables, tm, s):
    m, d = x2.shape
    nt = s // tm
    full = lambda a: pl.BlockSpec(a.shape, lambda i: (0, 0))
    tab = lambda: pl.BlockSpec((tm, LANES), lambda i: (i % nt, 0))
    hq = MLA_HEADS * LANES
    hv = MLA_HEADS * MLA_V
    return pl.pallas_call(
        _mla_proj_kernel,
        grid=(m // tm,),
        in_specs=[pl.BlockSpec((tm, d), lambda i: (i, 0)), full(wd), full(qg), full(kvg),
                  full(wq), full(wk), full(wv)] + [tab() for _ in tables],
        out_specs=[pl.BlockSpec((tm, hq), lambda i: (i, 0)),
                   pl.BlockSpec((tm, hq), lambda i: (i, 0)),
                   pl.BlockSpec((tm, hv), lambda i: (i, 0))],
        out_shape=[jax.ShapeDtypeStruct((m, hq), BF16),
                   jax.ShapeDtypeStruct((m, hq), BF16),
                   jax.ShapeDtypeStruct((m, hv), BF16)],
        compiler_params=_cparams(("parallel",)),
        name="mla_proj",
    )(x2, wd, qg, kvg, wq, wk, wv, *tables)


def _mla_attn_kernel(q_ref, k_ref, v_ref, o_ref, ot_ref, *, tq, heads_per_step, key_chunk):
    s = q_ref.shape[1]
    keep_t = (lax.broadcasted_iota(jnp.int32, (tq, tq), 0)
              <= lax.broadcasted_iota(jnp.int32, (tq, tq), 1))
    v_t = jnp.transpose(v_ref[0].astype(F32)).astype(BF16)
    ones_rows = jnp.ones((2 * SUBLANES, s), BF16)
    lhs = [jnp.concatenate([v_t[hh * MLA_V:(hh + 1) * MLA_V, :], ones_rows], axis=0)
           for hh in range(heads_per_step)]
    units = []
    for j in range(s // tq):
        ext = (j + 1) * tq
        starts = list(range(0, ext - tq, key_chunk)) or [0]
        bounds = [(a, b) for a, b in zip(starts, starts[1:] + [ext])]
        for hh in range(heads_per_step):
            units += [(j * tq, hh, k0, k1) for k0, k1 in bounds]

    def logits(r0, hh, k0, k1):
        ql = slice(hh * LANES, (hh + 1) * LANES)
        sc = _dot_nt(k_ref[0, k0:k1, ql], q_ref[0, r0:r0 + tq, ql])
        if k1 == r0 + tq:
            diag = jnp.where(keep_t, sc[k1 - k0 - tq:, :], MASK_NEG)
            sc = diag if k1 - k0 == tq else jnp.concatenate([sc[:k1 - k0 - tq, :], diag], axis=0)
        return sc

    def probs(sc):
        m = _reduce_keys(sc, jnp.max, jnp.maximum)
        return jnp.exp2(sc - m).astype(BF16), m

    running = {}

    def values(r0, hh, k0, k1, p, m):
        acc = _dot(lhs[hh][:, k0:k1], p)
        if k0 > 0:
            m_run, acc_run = running.pop((r0, hh))
            m_new = jnp.maximum(m_run, m)
            acc = acc_run * jnp.exp2(m_run - m_new) + acc * jnp.exp2(m - m_new)
            m = m_new
        if k1 == r0 + tq:
            ot_ref[hh * MLA_V:(hh + 1) * MLA_V, r0:r0 + tq] = acc[0:MLA_V, :] / acc[MLA_V:MLA_V + 1, :]
        else:
            running[(r0, hh)] = (m, acc)

    n = len(units)
    stage_s, stage_m, stage_p = {}, {}, {}
    d_max, d_exp, d_pv = STAGE_LAGS
    for t in range(n + d_pv):
        if t < n:
            stage_s[t] = logits(*units[t])
        if 0 <= t - d_max < n:
            u = t - d_max
            stage_m[u] = _reduce_keys(stage_s[u], jnp.max, jnp.maximum)
        if 0 <= t - d_exp < n:
            u = t - d_exp
            stage_p[u] = jnp.exp2(stage_s.pop(u) - stage_m[u]).astype(BF16)
        if 0 <= t - d_pv < n:
            u = t - d_pv
            values(*units[u], stage_p.pop(u), stage_m.pop(u))
    o_ref[0] = jnp.transpose(ot_ref[...]).astype(o_ref.dtype)


def _mla_attn(q, k, v, tq):
    bsz, s, _ = q.shape
    hps = 4
    return pl.pallas_call(
        functools.partial(_mla_attn_kernel, tq=tq, heads_per_step=hps, key_chunk=2 * tq),
        grid=(bsz, MLA_HEADS // hps),
        in_specs=[pl.BlockSpec((1, s, hps * LANES), lambda b, h: (b, 0, h)),
                  pl.BlockSpec((1, s, hps * LANES), lambda b, h: (b, 0, h)),
                  pl.BlockSpec((1, s, hps * MLA_V), lambda b, h: (b, 0, h))],
        out_specs=pl.BlockSpec((1, s, hps * MLA_V), lambda b, h: (b, 0, h)),
        out_shape=jax.ShapeDtypeStruct((bsz, s, MLA_HEADS * MLA_V), BF16),
        scratch_shapes=[pltpu.VMEM((hps * MLA_V, s), F32)],
        compiler_params=_cparams(("parallel", "parallel")),
        name="mla_attn",
    )(q, k, v)


def _hy_in_weight(w_in):
    xr, yr, q, k, v, iq, ik, iw = jnp.split(
        w_in, np.cumsum([512, 512, 512, 64, 64, 256, 64, 4])[:-1].tolist(), axis=1)
    pad = jnp.zeros((w_in.shape[0], 256 - 64 * 3 - 4), w_in.dtype)
    return jnp.concatenate([xr, yr, q, iq, k, v, ik, iw, pad], axis=1).astype(BF16)


def _block_diag_groups(w):
    bw = w.shape[-1]
    per = LANES // bw
    ng = w.shape[0] // per
    on_diag = jnp.eye(per, dtype=bool)[None, :, None, :, None]
    out = jnp.where(on_diag, w.reshape(ng, per, bw, 1, bw), 0.0)
    return out.reshape(ng, LANES, LANES).astype(BF16)


def _head_lanes(nope, rope):
    lead = nope.shape[:-1] if nope is not None else rope.shape[:-1]
    dt = nope.dtype if nope is not None else rope.dtype
    z = lambda n: jnp.zeros(lead + (n,), dt)
    r1, r2 = (rope[..., :ROPE_HALF], rope[..., ROPE_HALF:]) if rope is not None else (z(ROPE_HALF), z(ROPE_HALF))
    n1, n2 = (nope[..., :NOPE_LO], nope[..., NOPE_LO:]) if nope is not None else (z(NOPE_LO), z(MLA_NOPE - NOPE_LO))
    return jnp.concatenate([r1, n1, r2, n2, z(LANES - MLA_NOPE - MLA_ROPE)], axis=-1)


def _mla_weights(w_down, w_uq, w_ukv):
    lat = MLA_Q_LORA + MLA_KV_LORA
    wd = jnp.concatenate([w_down[:, :lat], _head_lanes(None, w_down[:, lat:])], axis=1)
    wq = w_uq.reshape(MLA_Q_LORA, MLA_HEADS, MLA_NOPE + MLA_ROPE)
    wq = _head_lanes(wq[:, :, :MLA_NOPE], wq[:, :, MLA_NOPE:]).reshape(MLA_Q_LORA, MLA_HEADS * LANES)
    wkv = w_ukv.reshape(MLA_KV_LORA, MLA_HEADS, MLA_NOPE + MLA_V)
    wk = _head_lanes(wkv[:, :, :MLA_NOPE], None).reshape(MLA_KV_LORA, MLA_HEADS * LANES)
    wv = wkv[:, :, MLA_NOPE:].reshape(MLA_KV_LORA, MLA_HEADS * MLA_V)
    return wd.astype(BF16), wq.astype(BF16), wk.astype(BF16), wv.astype(BF16)


def _rope_tables(s):
    pos = np.arange(s, dtype=np.float64)
    freq = ROPE_BASE ** (-np.arange(0, MLA_ROPE, 2, dtype=np.float64) / MLA_ROPE)
    ang = pos[:, None] * freq[None, :]
    lo = slice(0, ROPE_HALF)
    hi = slice(LANES // 2, LANES // 2 + ROPE_HALF)
    cos_t = np.ones((s, LANES))
    sin_t = np.zeros((s, LANES))
    cos_t[:, lo] = cos_t[:, hi] = np.cos(ang)
    sin_t[:, lo] = -np.sin(ang)
    sin_t[:, hi] = np.sin(ang)
    f32 = lambda a: jnp.asarray(a.astype(np.float32))
    return f32(cos_t * MLA_Q_SCALE), f32(sin_t * MLA_Q_SCALE), f32(cos_t), f32(sin_t)


def _tile_m(m):
    return 512 if m % 512 == 0 else m


def kernel(x, p, ln1_g, ln1_b, ln2_g, ln2_b, mlp_w1, mlp_w2, ple_w_proj, ple_w_gate, hy_w_in, hy_conv_w, hy_conv_b, hy_ga_w, hy_ga_b, hy_gx_w, hy_gx_b, hy_lambda, hy_w_out, mla_w_down, mla_q_norm, mla_kv_norm, mla_w_uq, mla_w_ukv, mla_w_out):
    bsz, s, d = x.shape
    m = bsz * s
    tm = _tile_m(m)
    tf = 1024
    row = lambda a: a.reshape(1, -1)
    x2 = x.reshape(m, d)

    rows3 = lambda a: a.reshape(a.shape[0], 1, -1)
    tail_params = (rows3(ln1_g), rows3(ln1_b), mlp_w1.astype(BF16), mlp_w2.astype(BF16),
                   ple_w_gate.astype(BF16), ple_w_proj.astype(BF16), rows3(ln2_g), rows3(ln2_b))
    p3 = p.reshape(p.shape[0], m, -1)

    def layer_tail(acts, wos, x2, i):
        return _tail(acts, wos, x2, p3, i, *tail_params, tm, tf)

    xy, q, iq, misc = _inproj(x2, _hy_in_weight(hy_w_in[0]), tm)
    rec = _rglru(xy.reshape(bsz, s, -1), hy_conv_w[0], row(hy_conv_b[0]),
                 _block_diag_groups(hy_ga_w[0]), row(hy_ga_b[0]),
                 _block_diag_groups(hy_gx_w[0]), row(hy_gx_b[0]), row(hy_lambda[0]))
    att = _dsa(q.reshape(bsz, s, -1), iq.reshape(bsz, s, -1), misc.reshape(bsz, s, -1))
    w_out = hy_w_out[0].astype(BF16)
    x2 = layer_tail([rec.reshape(m, -1), att.reshape(m, -1)], [w_out[:RNN_WIDTH], w_out[RNN_WIDTH:]], x2, 0)

    wd, wq, wk, wv = _mla_weights(mla_w_down[0], mla_w_uq[0], mla_w_ukv[0])
    qp, kp, vp = _mla_proj(x2, wd, row(mla_q_norm[0]), row(mla_kv_norm[0]), wq, wk, wv,
                           _rope_tables(s), min(tm, s), s)
    o = _mla_attn(qp.reshape(bsz, s, -1), kp.reshape(bsz, s, -1), vp.reshape(bsz, s, -1), min(256, s))
    x2 = layer_tail([o.reshape(m, -1)], [mla_w_out[0].astype(BF16)], x2, 1)
    return x2.reshape(bsz, s, d)
```

```python
import functools
import math

import numpy as np
import jax
import jax.numpy as jnp
from jax import lax
from jax.experimental import pallas as pl
from jax.experimental.pallas import tpu as pltpu

F32 = jnp.float32
BF16 = jnp.bfloat16

RNN_WIDTH = 512
RNN_BLOCKS = 8
CONV_WIDTH = 4
LRU_C = 8.0
DSA_HEADS = 8
DSA_HEAD_DIM = 64
IDX_HEADS = 4
IDX_DIM = 64
DSA_TOPK_MAX = 256
MLA_HEADS = 16
MLA_Q_LORA = 512
MLA_KV_LORA = 256
MLA_NOPE = 64
MLA_ROPE = 32
MLA_V = 64
ROPE_BASE = 10000.0
DEPTH = 2
DN_ALPHA = (2 * DEPTH) ** 0.25
LN_EPS = 1e-5
RMS_EPS = 1e-6

LANES = 128
SUBLANES = 8
VMEM_LIMIT_BYTES = 56 * 1024 * 1024

MASK_NEG = -3.0e38
INT32_MIN = -(2 ** 31)
STAGE_LAGS = (1, 2, 4)
DSA_Q_SCALE = DSA_HEAD_DIM ** -0.5
DSA_PAIR_MAX_EXTENT = 768


def _cparams(sem):
    return pltpu.CompilerParams(dimension_semantics=sem, vmem_limit_bytes=VMEM_LIMIT_BYTES)


def _dot(a, b):
    return jnp.dot(a, b, preferred_element_type=F32)


def _dot_nt(a, b):
    return lax.dot_general(a, b, (((1,), (1,)), ((), ())), preferred_element_type=F32)


def _layer_norm(y, g, b):
    mu = jnp.mean(y, axis=-1, keepdims=True)
    yc = y - mu
    var = jnp.mean(yc * yc, axis=-1, keepdims=True)
    return yc * lax.rsqrt(var + LN_EPS) * g + b


IN_XY = 2 * RNN_WIDTH
IN_Q = DSA_HEADS * DSA_HEAD_DIM
IN_IQ = IDX_HEADS * IDX_DIM
IN_MISC = 2 * LANES
IN_OFFSETS = tuple(int(v) for v in np.cumsum([0, IN_XY, IN_Q, IN_IQ, IN_MISC]))


def _inproj_kernel(x_ref, w_ref, xy_ref, q_ref, iq_ref, misc_ref):
    xb = x_ref[...].astype(BF16)
    cols = [w_ref[:, a:b] for a, b in zip(IN_OFFSETS[:-1], IN_OFFSETS[1:])]
    xy_ref[...] = _dot(xb, cols[0])
    q_ref[...] = (_dot(xb, cols[1]) * DSA_Q_SCALE).astype(BF16)
    iq_ref[...] = _dot(xb, cols[2])
    misc_ref[...] = _dot(xb, cols[3])


def _inproj(x2, w, tm):
    m, d = x2.shape
    n = w.shape[1]
    widths = (IN_XY, IN_Q, IN_IQ, IN_MISC)
    dtypes = (F32, BF16, F32, F32)
    return pl.pallas_call(
        _inproj_kernel,
        grid=(m // tm,),
        in_specs=[pl.BlockSpec((tm, d), lambda i: (i, 0)),
                  pl.BlockSpec((d, n), lambda i: (0, 0))],
        out_specs=[pl.BlockSpec((tm, wd), lambda i: (i, 0)) for wd in widths],
        out_shape=[jax.ShapeDtypeStruct((m, wd), dt) for wd, dt in zip(widths, dtypes)],
        compiler_params=_cparams(("parallel",)),
        name="hy_inproj",
    )(x2, w)


def _rglru_kernel(xr_ref, yr_ref, cw_ref, cb_ref, wa_ref, ba_ref, wx_ref, bx_ref, lam_ref,
                  o_ref, xpad_ref):
    s = xr_ref.shape[1]
    c = xr_ref.shape[2]
    xpad_ref[0:SUBLANES, :] = jnp.zeros((SUBLANES, c), F32)
    xpad_ref[SUBLANES:SUBLANES + s, :] = xr_ref[0]
    xc = None
    for k in range(CONV_WIDTH):
        start = SUBLANES - (CONV_WIDTH - 1) + k
        term = xpad_ref[start:start + s, :] * cw_ref[k:k + 1, :]
        xc = term if xc is None else xc + term
    xc = xc + cb_ref[...]
    xb = xc.astype(BF16)
    r = jax.nn.sigmoid(_dot(xb, wa_ref[0]) + ba_ref[...])
    gi = jax.nn.sigmoid(_dot(xb, wx_ref[0]) + bx_ref[...])
    nl = -lam_ref[...]
    softplus = jnp.maximum(nl, 0.0) + jnp.log1p(jnp.exp(-jnp.abs(nl)))
    log_a = (-LRU_C) * r * softplus
    a = jnp.exp(log_a)
    y = -jnp.tanh(log_a) * (a * a + 1.0)
    h = jnp.where(y > 0.0, y * lax.rsqrt(y), 0.0) * (gi * xc)
    row = lax.broadcasted_iota(jnp.int32, (s, c), 0)
    sh = 1
    while sh < s:
        if sh < SUBLANES:
            valid = row >= sh
            h = a * jnp.where(valid, pltpu.roll(h, sh, axis=0), 0.0) + h
            a = a * jnp.where(valid, pltpu.roll(a, sh, axis=0), 1.0)
        else:
            h = jnp.concatenate([h[:sh], a[sh:] * h[:s - sh] + h[sh:]], axis=0)
            if sh * 2 < s:
                a = jnp.concatenate([a[:sh], a[sh:] * a[:s - sh]], axis=0)
        sh *= 2
    o_ref[0] = (h * jax.nn.gelu(yr_ref[0])).astype(o_ref.dtype)


def _rglru(xy, conv_w, conv_b, wa, ba, wx, bx, lam):
    bsz, s, _ = xy.shape
    c = LANES
    ng = RNN_WIDTH // c
    vec = lambda: pl.BlockSpec((1, c), lambda b, g: (0, g))
    return pl.pallas_call(
        _rglru_kernel,
        grid=(bsz, ng),
        in_specs=[pl.BlockSpec((1, s, c), lambda b, g: (b, 0, g)),
                  pl.BlockSpec((1, s, c), lambda b, g: (b, 0, ng + g)),
                  pl.BlockSpec((CONV_WIDTH, c), lambda b, g: (0, g)),
                  vec(),
                  pl.BlockSpec((1, c, c), lambda b, g: (g, 0, 0)),
                  vec(),
                  pl.BlockSpec((1, c, c), lambda b, g: (g, 0, 0)),
                  vec(),
                  vec()],
        out_specs=pl.BlockSpec((1, s, c), lambda b, g: (b, 0, g)),
        out_shape=jax.ShapeDtypeStruct((bsz, s, RNN_WIDTH), BF16),
        scratch_shapes=[pltpu.VMEM((s + SUBLANES, c), F32)],
        compiler_params=_cparams(("parallel", "parallel")),
        name="hy_rglru",
    )(xy, xy, conv_w, conv_b, wa, ba, wx, bx, lam)


def _ordered_int_to_float(c):
    return lax.bitcast_convert_type(c ^ (lax.shift_right_arithmetic(c, 31) & jnp.int32(0x7FFFFFFF)), F32)


def _reduce_keys(x, reduce_fn, combine_fn, groups=8):
    rows = x.shape[0]
    step = max(rows // groups, SUBLANES)
    parts = [reduce_fn(x[r:r + step, :], axis=0, keepdims=True) for r in range(0, rows, step)]
    while len(parts) > 1:
        parts = [combine_fn(parts[i], parts[i + 1]) if i + 1 < len(parts) else parts[i]
                 for i in range(0, len(parts), 2)]
    return parts[0]


def _dsa_prepare(q_ref, mk_ref):
    tq = q_ref.shape[1]
    ext = mk_ref.shape[1]
    d0 = ext - tq
    keep = (lax.broadcasted_iota(jnp.int32, (tq, tq), 1)
            <= lax.broadcasted_iota(jnp.int32, (tq, tq), 0))

    mk = mk_ref[0]
    vlane = lax.broadcasted_iota(jnp.int32, (ext, DSA_HEAD_DIM), 1)
    v_aug = jnp.concatenate([mk[:, DSA_HEAD_DIM:2 * DSA_HEAD_DIM],
                             jnp.where(vlane == 0, 1.0, 0.0)], axis=1).astype(BF16)
    ik_b = mk[:, 2 * DSA_HEAD_DIM:2 * DSA_HEAD_DIM + IDX_DIM].astype(BF16)
    kcol = lax.broadcasted_iota(jnp.int32, (ext, 1), 0)
    lane = lax.broadcasted_iota(jnp.int32, (1, LANES), 1)
    pos_hi = (kcol & jnp.int32(-LANES)).astype(F32)
    pos_lo = (kcol & jnp.int32(LANES - 1)).astype(F32)
    k_aug = jnp.where(lane < DSA_HEAD_DIM, mk[:, 0:LANES],
                      jnp.where(lane == DSA_HEAD_DIM, pos_hi,
                                jnp.where(lane == DSA_HEAD_DIM + 1, pos_lo, 0.0))).astype(BF16)
    return d0, keep, ik_b, k_aug, v_aug


def _dsa_select_steps(rows, iq_ref, mq_ref, sc_ref, bias_ref, ik_b, keep, *, topk):
    tq, ext = sc_ref.shape
    nr = rows.stop - rows.start
    d0 = ext - tq
    kf = float(topk)
    st = {}

    def scores():
        iw0 = 2 * DSA_HEAD_DIM + IDX_DIM
        iw = mq_ref[0, rows, iw0:iw0 + IDX_HEADS] * (IDX_HEADS ** -0.5 * IDX_DIM ** -0.5)
        iq = iq_ref[0, rows, :].astype(BF16)
        sc = None
        for h in range(IDX_HEADS):
            d = _dot_nt(iq[:, h * IDX_DIM:(h + 1) * IDX_DIM], ik_b)
            term = jnp.maximum(d, 0.0) * iw[:, h:h + 1]
            sc = term if sc is None else sc + term
        if d0 > 0:
            sc_ref[rows, 0:d0] = sc[:, 0:d0]
        sc_ref[rows, d0:ext] = jnp.where(keep[rows, :], sc[:, d0:ext], MASK_NEG)
        st["tau"] = jnp.full((nr, 1), INT32_MIN, jnp.int32)

    def search(i):
        cand = st["tau"] + jnp.int32(1 << (31 - i) if i else INT32_MIN)
        ge = sc_ref[rows, :] >= _ordered_int_to_float(cand)
        cnt = jnp.sum(jnp.where(ge, 1.0, 0.0), axis=-1, keepdims=True)
        st["tau"] = jnp.where(cnt >= kf, cand, st["tau"])

    def ties():
        tau = _ordered_int_to_float(st.pop("tau"))
        sc = sc_ref[rows, :]
        gt = sc > tau
        eq = sc == tau
        need = kf - jnp.sum(jnp.where(gt, 1.0, 0.0), axis=-1, keepdims=True)
        eqf = jnp.where(eq, 1.0, 0.0).astype(BF16)
        tri = (lax.broadcasted_iota(jnp.int32, (LANES, LANES), 0)
               <= lax.broadcasted_iota(jnp.int32, (LANES, LANES), 1)).astype(BF16)
        off = jnp.zeros((nr, 1), F32)
        for c in range(ext // LANES):
            sl = slice(c * LANES, (c + 1) * LANES)
            pc = _dot(eqf[:, sl], tri)
            take = gt[:, sl] | (eq[:, sl] & ((pc + off) <= need))
            if c * LANES >= d0:
                take = take & keep[rows, c * LANES - d0:(c + 1) * LANES - d0]
            bias_ref[rows, sl] = jnp.where(take, 0.0, MASK_NEG)
            off = off + pc[:, LANES - 1:LANES]

    return [scores] + [functools.partial(search, i) for i in range(32)] + [ties]


def _dsa_attend_steps(rows, q_ref, o_ref, bias_ref, k_aug, v_aug):
    nr = rows.stop - rows.start
    tail_lane = lax.broadcasted_iota(jnp.int32, (nr, LANES - DSA_HEAD_DIM), 1)

    def logits(h):
        slope = 2.0 ** (-8.0 * (h + 1) / DSA_HEADS)
        tail = jnp.where(tail_lane < 2, slope, 0.0).astype(BF16)
        qh = q_ref[0, rows, h * DSA_HEAD_DIM:(h + 1) * DSA_HEAD_DIM]
        return _dot_nt(jnp.concatenate([qh, tail], axis=1), k_aug) + bias_ref[rows, :]

    def values(h, p):
        o = _dot(p, v_aug)
        o = o[:, 0:DSA_HEAD_DIM] / o[:, DSA_HEAD_DIM:DSA_HEAD_DIM + 1]
        o_ref[0, rows, h * DSA_HEAD_DIM:(h + 1) * DSA_HEAD_DIM] = o.astype(o_ref.dtype)

    lg, mx, pr = {}, {}, {}
    d_max, d_exp, d_pv = STAGE_LAGS

    def step(t):
        if t < DSA_HEADS:
            lg[t] = logits(t)
        if 0 <= t - d_max < DSA_HEADS:
            mx[t - d_max] = jnp.max(lg[t - d_max], axis=-1, keepdims=True)
        if 0 <= t - d_exp < DSA_HEADS:
            h = t - d_exp
            pr[h] = jnp.exp(lg.pop(h) - mx.pop(h)).astype(BF16)
        if 0 <= t - d_pv < DSA_HEADS:
            values(t - d_pv, pr.pop(t - d_pv))

    return [functools.partial(step, t) for t in range(DSA_HEADS + d_pv)]


def _dsa_kernel(q_ref, iq_ref, mq_ref, mk_ref, _aliased_out, o_ref, sc_ref, bias_ref, *, topk):
    tq = q_ref.shape[1]
    rows = slice(0, tq)
    select, attend = [], []
    for bi in range(q_ref.shape[0]):
        one = lambda ref: ref.at[bi:bi + 1]
        sc_b, bias_b = sc_ref.at[bi], bias_ref.at[bi]
        d0, keep, ik_b, k_aug, v_aug = _dsa_prepare(one(q_ref), one(mk_ref))
        if mk_ref.shape[1] > topk:
            select.append(_dsa_select_steps(rows, one(iq_ref), one(mq_ref), sc_b, bias_b, ik_b, keep,
                                            topk=topk))
        else:
            if d0 > 0:
                bias_b[:, 0:d0] = jnp.zeros((tq, d0), F32)
            bias_b[:, d0:] = jnp.where(keep, 0.0, MASK_NEG)
        attend.append(_dsa_attend_steps(rows, one(q_ref), one(o_ref), bias_b, k_aug, v_aug))
    for steps in list(zip(*select)) + list(zip(*attend)):
        for step in steps:
            step()


def _dsa(q, iq, misc):
    bsz, s, _ = q.shape
    topk = min(DSA_TOPK_MAX, s // 4)
    tq = min(256, s)
    hd = DSA_HEADS * DSA_HEAD_DIM
    att = jnp.zeros((bsz, s, hd), BF16)
    for c in range(s // tq):
        ext = (c + 1) * tq
        bb = 2 if (ext <= DSA_PAIR_MAX_EXTENT and bsz % 2 == 0) else 1
        att = pl.pallas_call(
            functools.partial(_dsa_kernel, topk=topk),
            grid=(bsz // bb,),
            in_specs=[pl.BlockSpec((bb, tq, hd), lambda b, c=c: (b, c, 0)),
                      pl.BlockSpec((bb, tq, IDX_HEADS * IDX_DIM), lambda b, c=c: (b, c, 0)),
                      pl.BlockSpec((bb, tq, 256), lambda b, c=c: (b, c, 0)),
                      pl.BlockSpec((bb, ext, 256), lambda b: (b, 0, 0)),
                      pl.BlockSpec(memory_space=pl.ANY)],
            out_specs=pl.BlockSpec((bb, tq, hd), lambda b, c=c: (b, c, 0)),
            out_shape=jax.ShapeDtypeStruct((bsz, s, hd), BF16),
            scratch_shapes=[pltpu.VMEM((bb, tq, ext), F32), pltpu.VMEM((bb, tq, ext), F32)],
            input_output_aliases={4: 0},
            compiler_params=_cparams(("parallel",)),
            name=f"hy_dsa_c{c}",
        )(q, iq, misc, misc, att)
    return att


def _tail_kernel(*refs, n_in, tf):
    a_refs = refs[:n_in]
    wo_refs = refs[n_in:2 * n_in]
    (x_ref, p_ref, g1_ref, b1_ref, w1_ref, w2_ref, wg_ref, wp_ref, g2_ref, b2_ref,
     o_ref, acc_ref) = refs[2 * n_in:]
    n_chunks = w1_ref.shape[1] // tf

    def steps_for(rows):
        st = {}

        def out_proj():
            m = None
            for a_ref, w_ref in zip(a_refs, wo_refs):
                t = _dot(a_ref[rows, :].astype(BF16), w_ref[...])
                m = t if m is None else m + t
            st["y"] = DN_ALPHA * x_ref[rows, :] + m

        def norm1():
            st["h"] = _layer_norm(st.pop("y"), g1_ref[...], b1_ref[...])
            st["hb"] = st["h"].astype(BF16)

        def gate():
            g = jax.nn.sigmoid(_dot(st["hb"], wg_ref[...]))
            acc_ref[rows, :] = DN_ALPHA * st.pop("h") + g * _dot(p_ref[rows, :].astype(BF16), wp_ref[...])

        def mlp(c):
            a = jnp.maximum(_dot(st["hb"], w1_ref[:, c * tf:(c + 1) * tf]), 0.0)
            acc_ref[rows, :] += _dot((a * a).astype(BF16), w2_ref[c * tf:(c + 1) * tf, :])

        def norm2():
            o_ref[rows, :] = _layer_norm(acc_ref[rows, :], g2_ref[...], b2_ref[...])

        return [out_proj, norm1, gate] + [functools.partial(mlp, c) for c in range(n_chunks)] + [norm2]

    half = x_ref.shape[0] // 2
    first, second = steps_for(slice(0, half)), steps_for(slice(half, 2 * half))
    lag = 2
    for t in range(len(first) + lag):
        if t < len(first):
            first[t]()
        if 0 <= t - lag < len(second):
            second[t - lag]()


def _tail(acts, wos, x2, p3, layer, g1, b1, w1, w2, wg, wp, g2, b2, tm, tf):
    m, d = x2.shape
    n_in = len(acts)
    rows = lambda a: pl.BlockSpec((tm, a.shape[1]), lambda i: (i, 0))
    resident = lambda a: pl.BlockSpec(a.shape, lambda i: (0, 0), pipeline_mode=pl.Buffered(1))
    of_layer = lambda a: pl.BlockSpec((None,) + a.shape[1:], lambda i: (layer, 0, 0),
                                      pipeline_mode=pl.Buffered(1))
    params = [g1, b1, w1, w2, wg, wp, g2, b2]
    return pl.pallas_call(
        functools.partial(_tail_kernel, n_in=n_in, tf=tf),
        grid=(m // tm,),
        in_specs=([rows(a) for a in acts] + [resident(w) for w in wos]
                  + [rows(x2), pl.BlockSpec((None, tm, p3.shape[2]), lambda i: (layer, i, 0))]
                  + [of_layer(a) for a in params]),
        out_specs=pl.BlockSpec((tm, d), lambda i: (i, 0)),
        out_shape=jax.ShapeDtypeStruct((m, d), F32),
        scratch_shapes=[pltpu.VMEM((tm, d), F32)],
        compiler_params=_cparams(("parallel",)),
        name="layer_tail",
    )(*acts, *wos, x2, p3, *params)


MLA_Q_SCALE = (MLA_NOPE + MLA_ROPE) ** -0.5 * math.log2(math.e)
ROPE_HALF = MLA_ROPE // 2
NOPE_LO = LANES // 2 - ROPE_HALF


def _rope_group(t, cos_t, sin_t):
    return t * cos_t + pltpu.roll(t, LANES // 2, axis=1) * sin_t


def _mla_proj_kernel(x_ref, wd_ref, qg_ref, kvg_ref, wq_ref, wk_ref, wv_ref,
                     cosq_ref, sinq_ref, cosk_ref, sink_ref, q_ref, k_ref, v_ref):
    def steps_for(rows):
        st = {}

        def down():
            st["down"] = _dot(x_ref[rows, :].astype(BF16), wd_ref[...])

        def norms():
            d = st.pop("down")
            cq = d[:, 0:MLA_Q_LORA]
            ckv = d[:, MLA_Q_LORA:MLA_Q_LORA + MLA_KV_LORA]
            rms = lambda t: t * lax.rsqrt(jnp.mean(t * t, axis=-1, keepdims=True) + RMS_EPS)
            st["cqn"] = (rms(cq) * qg_ref[...]).astype(BF16)
            st["ckvn"] = (rms(ckv) * kvg_ref[...]).astype(BF16)
            st["kr"] = _rope_group(d[:, MLA_Q_LORA + MLA_KV_LORA:], cosk_ref[rows, :], sink_ref[rows, :])

        def q_up():
            st["q"] = _dot(st.pop("cqn"), wq_ref[...])

        def k_up():
            st["k"] = _dot(st["ckvn"], wk_ref[...])

        def q_rope():
            q, cos_q, sin_q = st.pop("q"), cosq_ref[rows, :], sinq_ref[rows, :]
            for h in range(MLA_HEADS):
                sl = slice(h * LANES, (h + 1) * LANES)
                q_ref[rows, sl] = _rope_group(q[:, sl], cos_q, sin_q).astype(BF16)

        def k_add():
            k, kr = st.pop("k"), st.pop("kr")
            for h in range(MLA_HEADS):
                sl = slice(h * LANES, (h + 1) * LANES)
                k_ref[rows, sl] = (k[:, sl] + kr).astype(BF16)

        def v_up():
            v_ref[rows, :] = _dot(st.pop("ckvn"), wv_ref[...]).astype(BF16)

        return [down, norms, q_up, k_up, q_rope, k_add, v_up]

    half = x_ref.shape[0] // 2
    first, second = steps_for(slice(0, half)), steps_for(slice(half, 2 * half))
    lag = 2
    for t in range(len(first) + lag):
        if t < len(first):
            first[t]()
        if 0 <= t - lag < len(second):
            second[t - lag]()


def _mla_proj(x2, wd, qg, kvg, wq, wk, wv, tables, tm, s):
    m, d = x2.shape
    nt = s // tm
    full = lambda a: pl.BlockSpec(a.shape, lambda i: (0, 0))
    tab = lambda: pl.BlockSpec((tm, LANES), lambda i: (i % nt, 0))
    hq = MLA_HEADS * LANES
    hv = MLA_HEADS * MLA_V
    return pl.pallas_call(
        _mla_proj_kernel,
        grid=(m // tm,),
        in_specs=[pl.BlockSpec((tm, d), lambda i: (i, 0)), full(wd), full(qg), full(kvg),
                  full(wq), full(wk), full(wv)] + [tab() for _ in tables],
        out_specs=[pl.BlockSpec((tm, hq), lambda i: (i, 0)),
                   pl.BlockSpec((tm, hq), lambda i: (i, 0)),
                   pl.BlockSpec((tm, hv), lambda i: (i, 0))],
        out_shape=[jax.ShapeDtypeStruct((m, hq), BF16),
                   jax.ShapeDtypeStruct((m, hq), BF16),
                   jax.ShapeDtypeStruct((m, hv), BF16)],
        compiler_params=_cparams(("parallel",)),
        name="mla_proj",
    )(x2, wd, qg, kvg, wq, wk, wv, *tables)


def _mla_attn_kernel(q_ref, k_ref, v_ref, o_ref, ot_ref, *, tq, heads_per_step, key_chunk):
    s = q_ref.shape[1]
    keep_t = (lax.broadcasted_iota(jnp.int32, (tq, tq), 0)
              <= lax.broadcasted_iota(jnp.int32, (tq, tq), 1))
    v_t = jnp.transpose(v_ref[0].astype(F32)).astype(BF16)
    ones_rows = jnp.ones((2 * SUBLANES, s), BF16)
    lhs = [jnp.concatenate([v_t[hh * MLA_V:(hh + 1) * MLA_V, :], ones_rows], axis=0)
           for hh in range(heads_per_step)]
    units = []
    for j in range(s // tq):
        ext = (j + 1) * tq
        starts = list(range(0, ext - tq, key_chunk)) or [0]
        bounds = [(a, b) for a, b in zip(starts, starts[1:] + [ext])]
        for hh in range(heads_per_step):
            units += [(j * tq, hh, k0, k1) for k0, k1 in bounds]

    def logits(r0, hh, k0, k1):
        ql = slice(hh * LANES, (hh + 1) * LANES)
        sc = _dot_nt(k_ref[0, k0:k1, ql], q_ref[0, r0:r0 + tq, ql])
        if k1 == r0 + tq:
            diag = jnp.where(keep_t, sc[k1 - k0 - tq:, :], MASK_NEG)
            sc = diag if k1 - k0 == tq else jnp.concatenate([sc[:k1 - k0 - tq, :], diag], axis=0)
        return sc

    def probs(sc):
        m = _reduce_keys(sc, jnp.max, jnp.maximum)
        return jnp.exp2(sc - m).astype(BF16), m

    running = {}

    def values(r0, hh, k0, k1, p, m):
        acc = _dot(lhs[hh][:, k0:k1], p)
        if k0 > 0:
            m_run, acc_run = running.pop((r0, hh))
            m_new = jnp.maximum(m_run, m)
            acc = acc_run * jnp.exp2(m_run - m_new) + acc * jnp.exp2(m - m_new)
            m = m_new
        if k1 == r0 + tq:
            ot_ref[hh * MLA_V:(hh + 1) * MLA_V, r0:r0 + tq] = acc[0:MLA_V, :] / acc[MLA_V:MLA_V + 1, :]
        else:
            running[(r0, hh)] = (m, acc)

    n = len(units)
    stage_s, stage_m, stage_p = {}, {}, {}
    d_max, d_exp, d_pv = STAGE_LAGS
    for t in range(n + d_pv):
        if t < n:
            stage_s[t] = logits(*units[t])
        if 0 <= t - d_max < n:
            u = t - d_max
            stage_m[u] = _reduce_keys(stage_s[u], jnp.max, jnp.maximum)
        if 0 <= t - d_exp < n:
            u = t - d_exp
            stage_p[u] = jnp.exp2(stage_s.pop(u) - stage_m[u]).astype(BF16)
        if 0 <= t - d_pv < n:
            u = t - d_pv
            values(*units[u], stage_p.pop(u), stage_m.pop(u))
    o_ref[0] = jnp.transpose(ot_ref[...]).astype(o_ref.dtype)


def _mla_attn(q, k, v, tq):
    bsz, s, _ = q.shape
    hps = 4
    return pl.pallas_call(
        functools.partial(_mla_attn_kernel, tq=tq, heads_per_step=hps, key_chunk=2 * tq),
        grid=(bsz, MLA_HEADS // hps),
        in_specs=[pl.BlockSpec((1, s, hps * LANES), lambda b, h: (b, 0, h)),
                  pl.BlockSpec((1, s, hps * LANES), lambda b, h: (b, 0, h)),
                  pl.BlockSpec((1, s, hps * MLA_V), lambda b, h: (b, 0, h))],
        out_specs=pl.BlockSpec((1, s, hps * MLA_V), lambda b, h: (b, 0, h)),
        out_shape=jax.ShapeDtypeStruct((bsz, s, MLA_HEADS * MLA_V), BF16),
        scratch_shapes=[pltpu.VMEM((hps * MLA_V, s), F32)],
        compiler_params=_cparams(("parallel", "parallel")),
        name="mla_attn",
    )(q, k, v)


def _hy_in_weight(w_in):
    xr, yr, q, k, v, iq, ik, iw = jnp.split(
        w_in, np.cumsum([512, 512, 512, 64, 64, 256, 64, 4])[:-1].tolist(), axis=1)
    pad = jnp.zeros((w_in.shape[0], 256 - 64 * 3 - 4), w_in.dtype)
    return jnp.concatenate([xr, yr, q, iq, k, v, ik, iw, pad], axis=1).astype(BF16)


def _block_diag_groups(w):
    bw = w.shape[-1]
    per = LANES // bw
    ng = w.shape[0] // per
    on_diag = jnp.eye(per, dtype=bool)[None, :, None, :, None]
    out = jnp.where(on_diag, w.reshape(ng, per, bw, 1, bw), 0.0)
    return out.reshape(ng, LANES, LANES).astype(BF16)


def _head_lanes(nope, rope):
    lead = nope.shape[:-1] if nope is not None else rope.shape[:-1]
    dt = nope.dtype if nope is not None else rope.dtype
    z = lambda n: jnp.zeros(lead + (n,), dt)
    r1, r2 = (rope[..., :ROPE_HALF], rope[..., ROPE_HALF:]) if rope is not None else (z(ROPE_HALF), z(ROPE_HALF))
    n1, n2 = (nope[..., :NOPE_LO], nope[..., NOPE_LO:]) if nope is not None else (z(NOPE_LO), z(MLA_NOPE - NOPE_LO))
    return jnp.concatenate([r1, n1, r2, n2, z(LANES - MLA_NOPE - MLA_ROPE)], axis=-1)


def _mla_weights(w_down, w_uq, w_ukv):
    lat = MLA_Q_LORA + MLA_KV_LORA
    wd = jnp.concatenate([w_down[:, :lat], _head_lanes(None, w_down[:, lat:])], axis=1)
    wq = w_uq.reshape(MLA_Q_LORA, MLA_HEADS, MLA_NOPE + MLA_ROPE)
    wq = _head_lanes(wq[:, :, :MLA_NOPE], wq[:, :, MLA_NOPE:]).reshape(MLA_Q_LORA, MLA_HEADS * LANES)
    wkv = w_ukv.reshape(MLA_KV_LORA, MLA_HEADS, MLA_NOPE + MLA_V)
    wk = _head_lanes(wkv[:, :, :MLA_NOPE], None).reshape(MLA_KV_LORA, MLA_HEADS * LANES)
    wv = wkv[:, :, MLA_NOPE:].reshape(MLA_KV_LORA, MLA_HEADS * MLA_V)
    return wd.astype(BF16), wq.astype(BF16), wk.astype(BF16), wv.astype(BF16)


def _rope_tables(s):
    pos = np.arange(s, dtype=np.float64)
    freq = ROPE_BASE ** (-np.arange(0, MLA_ROPE, 2, dtype=np.float64) / MLA_ROPE)
    ang = pos[:, None] * freq[None, :]
    lo = slice(0, ROPE_HALF)
    hi = slice(LANES // 2, LANES // 2 + ROPE_HALF)
    cos_t = np.ones((s, LANES))
    sin_t = np.zeros((s, LANES))
    cos_t[:, lo] = cos_t[:, hi] = np.cos(ang)
    sin_t[:, lo] = -np.sin(ang)
    sin_t[:, hi] = np.sin(ang)
    f32 = lambda a: jnp.asarray(a.astype(np.float32))
    return f32(cos_t * MLA_Q_SCALE), f32(sin_t * MLA_Q_SCALE), f32(cos_t), f32(sin_t)


def _tile_m(m):
    return 512 if m % 512 == 0 else m


def kernel(x, p, ln1_g, ln1_b, ln2_g, ln2_b, mlp_w1, mlp_w2, ple_w_proj, ple_w_gate, hy_w_in, hy_conv_w, hy_conv_b, hy_ga_w, hy_ga_b, hy_gx_w, hy_gx_b, hy_lambda, hy_w_out, mla_w_down, mla_q_norm, mla_kv_norm, mla_w_uq, mla_w_ukv, mla_w_out):
    bsz, s, d = x.shape
    m = bsz * s
    tm = _tile_m(m)
    tf = 1024
    row = lambda a: a.reshape(1, -1)
    x2 = x.reshape(m, d)

    rows3 = lambda a: a.reshape(a.shape[0], 1, -1)
    tail_params = (rows3(ln1_g), rows3(ln1_b), mlp_w1.astype(BF16), mlp_w2.astype(BF16),
                   ple_w_gate.astype(BF16), ple_w_proj.astype(BF16), rows3(ln2_g), rows3(ln2_b))
    p3 = p.reshape(p.shape[0], m, -1)

    def layer_tail(acts, wos, x2, i):
        return _tail(acts, wos, x2, p3, i, *tail_params, tm, tf)

    xy, q, iq, misc = _inproj(x2, _hy_in_weight(hy_w_in[0]), tm)
    rec = _rglru(xy.reshape(bsz, s, -1), hy_conv_w[0], row(hy_conv_b[0]),
                 _block_diag_groups(hy_ga_w[0]), row(hy_ga_b[0]),
                 _block_diag_groups(hy_gx_w[0]), row(hy_gx_b[0]), row(hy_lambda[0]))
    att = _dsa(q.reshape(bsz, s, -1), iq.reshape(bsz, s, -1), misc.reshape(bsz, s, -1))
    w_out = hy_w_out[0].astype(BF16)
    x2 = layer_tail([rec.reshape(m, -1), att.reshape(m, -1)], [w_out[:RNN_WIDTH], w_out[RNN_WIDTH:]], x2, 0)

    wd, wq, wk, wv = _mla_weights(mla_w_down[0], mla_w_uq[0], mla_w_ukv[0])
    qp, kp, vp = _mla_proj(x2, wd, row(mla_q_norm[0]), row(mla_kv_norm[0]), wq, wk, wv,
                           _rope_tables(s), min(tm, s), s)
    o = _mla_attn(qp.reshape(bsz, s, -1), kp.reshape(bsz, s, -1), vp.reshape(bsz, s, -1), min(256, s))
    x2 = layer_tail([o.reshape(m, -1)], [mla_w_out[0].astype(BF16)], x2, 1)
    return x2.reshape(bsz, s, d)
```

```python
import functools
import math

import numpy as np
import jax
import jax.numpy as jnp
from jax import lax
from jax.experimental import pallas as pl
from jax.experimental.pallas import tpu as pltpu

F32 = jnp.float32
BF16 = jnp.bfloat16

RNN_WIDTH = 512
CONV_WIDTH = 4
LRU_C = 8.0
DSA_HEADS = 8
DSA_HEAD_DIM = 64
IDX_HEADS = 4
IDX_DIM = 64
DSA_TOPK_MAX = 256
MLA_HEADS = 16
MLA_Q_LORA = 512
MLA_KV_LORA = 256
MLA_NOPE = 64
MLA_ROPE = 32
MLA_V = 64
ROPE_BASE = 10000.0
DEPTH = 2
DN_ALPHA = (2 * DEPTH) ** 0.25
LN_EPS = 1e-5
RMS_EPS = 1e-6

LANES = 128
SUBLANES = 8
VMEM_LIMIT_BYTES = 56 * 1024 * 1024

MASK_NEG = -3.0e38
INT32_MIN = -(2 ** 31)
STAGE_LAGS = (1, 2, 4)
DSA_Q_SCALE = DSA_HEAD_DIM ** -0.5
DSA_PAIR_MAX_EXTENT = 768


def _cparams(sem):
    return pltpu.CompilerParams(dimension_semantics=sem, vmem_limit_bytes=VMEM_LIMIT_BYTES)


def _dot(a, b):
    return jnp.dot(a, b, preferred_element_type=F32)


def _dot_nt(a, b):
    return lax.dot_general(a, b, (((1,), (1,)), ((), ())), preferred_element_type=F32)


def _layer_norm(y, g, b):
    mu = jnp.mean(y, axis=-1, keepdims=True)
    yc = y - mu
    var = jnp.mean(yc * yc, axis=-1, keepdims=True)
    return yc * lax.rsqrt(var + LN_EPS) * g + b


IN_XY = 2 * RNN_WIDTH
IN_Q = DSA_HEADS * DSA_HEAD_DIM
IN_IQ = IDX_HEADS * IDX_DIM
IN_MISC = 2 * LANES
IN_OFFSETS = tuple(int(v) for v in np.cumsum([0, IN_XY, IN_Q, IN_IQ, IN_MISC]))


def _inproj_kernel(x_ref, w_ref, xy_ref, q_ref, iq_ref, misc_ref):
    xb = x_ref[...].astype(BF16)
    cols = [w_ref[:, a:b] for a, b in zip(IN_OFFSETS[:-1], IN_OFFSETS[1:])]
    xy_ref[...] = _dot(xb, cols[0])
    q_ref[...] = (_dot(xb, cols[1]) * DSA_Q_SCALE).astype(BF16)
    iq_ref[...] = _dot(xb, cols[2])
    misc_ref[...] = _dot(xb, cols[3])


def _inproj(x2, w, tm):
    m, d = x2.shape
    n = w.shape[1]
    widths = (IN_XY, IN_Q, IN_IQ, IN_MISC)
    dtypes = (F32, BF16, F32, F32)
    return pl.pallas_call(
        _inproj_kernel,
        grid=(m // tm,),
        in_specs=[pl.BlockSpec((tm, d), lambda i: (i, 0)),
                  pl.BlockSpec((d, n), lambda i: (0, 0))],
        out_specs=[pl.BlockSpec((tm, wd), lambda i: (i, 0)) for wd in widths],
        out_shape=[jax.ShapeDtypeStruct((m, wd), dt) for wd, dt in zip(widths, dtypes)],
        compiler_params=_cparams(("parallel",)),
        name="hy_inproj",
    )(x2, w)


def _rglru_kernel(xr_ref, yr_ref, cw_ref, cb_ref, wa_ref, ba_ref, wx_ref, bx_ref, lam_ref,
                  o_ref, xpad_ref):
    s = xr_ref.shape[1]
    c = xr_ref.shape[2]
    xpad_ref[0:SUBLANES, :] = jnp.zeros((SUBLANES, c), F32)
    xpad_ref[SUBLANES:SUBLANES + s, :] = xr_ref[0]
    xc = None
    for k in range(CONV_WIDTH):
        start = SUBLANES - (CONV_WIDTH - 1) + k
        term = xpad_ref[start:start + s, :] * cw_ref[k:k + 1, :]
        xc = term if xc is None else xc + term
    xc = xc + cb_ref[...]
    xb = xc.astype(BF16)
    r = jax.nn.sigmoid(_dot(xb, wa_ref[0]) + ba_ref[...])
    gi = jax.nn.sigmoid(_dot(xb, wx_ref[0]) + bx_ref[...])
    nl = -lam_ref[...]
    softplus = jnp.maximum(nl, 0.0) + jnp.log1p(jnp.exp(-jnp.abs(nl)))
    log_a = (-LRU_C) * r * softplus
    a = jnp.exp(log_a)
    y = -jnp.tanh(log_a) * (a * a + 1.0)
    h = jnp.where(y > 0.0, y * lax.rsqrt(y), 0.0) * (gi * xc)
    row = lax.broadcasted_iota(jnp.int32, (s, c), 0)
    sh = 1
    while sh < s:
        if sh < SUBLANES:
            valid = row >= sh
            h = a * jnp.where(valid, pltpu.roll(h, sh, axis=0), 0.0) + h
            a = a * jnp.where(valid, pltpu.roll(a, sh, axis=0), 1.0)
        else:
            h = jnp.concatenate([h[:sh], a[sh:] * h[:s - sh] + h[sh:]], axis=0)
            if sh * 2 < s:
                a = jnp.concatenate([a[:sh], a[sh:] * a[:s - sh]], axis=0)
        sh *= 2
    o_ref[0] = (h * jax.nn.gelu(yr_ref[0])).astype(o_ref.dtype)


def _rglru(xy, conv_w, conv_b, wa, ba, wx, bx, lam):
    bsz, s, _ = xy.shape
    c = LANES
    ng = RNN_WIDTH // c
    vec = lambda: pl.BlockSpec((1, c), lambda b, g: (0, g))
    return pl.pallas_call(
        _rglru_kernel,
        grid=(bsz, ng),
        in_specs=[pl.BlockSpec((1, s, c), lambda b, g: (b, 0, g)),
                  pl.BlockSpec((1, s, c), lambda b, g: (b, 0, ng + g)),
                  pl.BlockSpec((CONV_WIDTH, c), lambda b, g: (0, g)),
                  vec(),
                  pl.BlockSpec((1, c, c), lambda b, g: (g, 0, 0)),
                  vec(),
                  pl.BlockSpec((1, c, c), lambda b, g: (g, 0, 0)),
                  vec(),
                  vec()],
        out_specs=pl.BlockSpec((1, s, c), lambda b, g: (b, 0, g)),
        out_shape=jax.ShapeDtypeStruct((bsz, s, RNN_WIDTH), BF16),
        scratch_shapes=[pltpu.VMEM((s + SUBLANES, c), F32)],
        compiler_params=_cparams(("parallel", "parallel")),
        name="hy_rglru",
    )(xy, xy, conv_w, conv_b, wa, ba, wx, bx, lam)


def _ordered_int_to_float(c):
    return lax.bitcast_convert_type(c ^ (lax.shift_right_arithmetic(c, 31) & jnp.int32(0x7FFFFFFF)), F32)


def _reduce_keys(x, reduce_fn, combine_fn, groups=8):
    rows = x.shape[0]
    step = max(rows // groups, SUBLANES)
    parts = [reduce_fn(x[r:r + step, :], axis=0, keepdims=True) for r in range(0, rows, step)]
    while len(parts) > 1:
        parts = [combine_fn(parts[i], parts[i + 1]) if i + 1 < len(parts) else parts[i]
                 for i in range(0, len(parts), 2)]
    return parts[0]


def _dsa_prepare(q_ref, mk_ref):
    tq = q_ref.shape[1]
    ext = mk_ref.shape[1]
    d0 = ext - tq
    keep = (lax.broadcasted_iota(jnp.int32, (tq, tq), 1)
            <= lax.broadcasted_iota(jnp.int32, (tq, tq), 0))

    mk = mk_ref[0]
    vlane = lax.broadcasted_iota(jnp.int32, (ext, DSA_HEAD_DIM), 1)
    v_aug = jnp.concatenate([mk[:, DSA_HEAD_DIM:2 * DSA_HEAD_DIM],
                             jnp.where(vlane == 0, 1.0, 0.0)], axis=1).astype(BF16)
    ik_b = mk[:, 2 * DSA_HEAD_DIM:2 * DSA_HEAD_DIM + IDX_DIM].astype(BF16)
    kcol = lax.broadcasted_iota(jnp.int32, (ext, 1), 0)
    lane = lax.broadcasted_iota(jnp.int32, (1, LANES), 1)
    pos_hi = (kcol & jnp.int32(-LANES)).astype(F32)
    pos_lo = (kcol & jnp.int32(LANES - 1)).astype(F32)
    k_aug = jnp.where(lane < DSA_HEAD_DIM, mk[:, 0:LANES],
                      jnp.where(lane == DSA_HEAD_DIM, pos_hi,
                                jnp.where(lane == DSA_HEAD_DIM + 1, pos_lo, 0.0))).astype(BF16)
    return d0, keep, ik_b, k_aug, v_aug


def _dsa_select_steps(rows, iq_ref, mq_ref, sc_ref, bias_ref, ik_b, keep, *, topk):
    tq, ext = sc_ref.shape
    nr = rows.stop - rows.start
    d0 = ext - tq
    kf = float(topk)
    st = {}

    def scores():
        iw0 = 2 * DSA_HEAD_DIM + IDX_DIM
        iw = mq_ref[0, rows, iw0:iw0 + IDX_HEADS] * (IDX_HEADS ** -0.5 * IDX_DIM ** -0.5)
        iq = iq_ref[0, rows, :].astype(BF16)
        sc = None
        for h in range(IDX_HEADS):
            d = _dot_nt(iq[:, h * IDX_DIM:(h + 1) * IDX_DIM], ik_b)
            term = jnp.maximum(d, 0.0) * iw[:, h:h + 1]
            sc = term if sc is None else sc + term
        if d0 > 0:
            sc_ref[rows, 0:d0] = sc[:, 0:d0]
        sc_ref[rows, d0:ext] = jnp.where(keep[rows, :], sc[:, d0:ext], MASK_NEG)
        st["tau"] = jnp.full((nr, 1), INT32_MIN, jnp.int32)

    def search(i):
        cand = st["tau"] + jnp.int32(1 << (31 - i) if i else INT32_MIN)
        ge = sc_ref[rows, :] >= _ordered_int_to_float(cand)
        cnt = jnp.sum(jnp.where(ge, 1.0, 0.0), axis=-1, keepdims=True)
        st["tau"] = jnp.where(cnt >= kf, cand, st["tau"])

    def ties():
        tau = _ordered_int_to_float(st.pop("tau"))
        sc = sc_ref[rows, :]
        gt = sc > tau
        eq = sc == tau
        need = kf - jnp.sum(jnp.where(gt, 1.0, 0.0), axis=-1, keepdims=True)
        eqf = jnp.where(eq, 1.0, 0.0).astype(BF16)
        tri = (lax.broadcasted_iota(jnp.int32, (LANES, LANES), 0)
               <= lax.broadcasted_iota(jnp.int32, (LANES, LANES), 1)).astype(BF16)
        off = jnp.zeros((nr, 1), F32)
        for c in range(ext // LANES):
            sl = slice(c * LANES, (c + 1) * LANES)
            pc = _dot(eqf[:, sl], tri)
            take = gt[:, sl] | (eq[:, sl] & ((pc + off) <= need))
            if c * LANES >= d0:
                take = take & keep[rows, c * LANES - d0:(c + 1) * LANES - d0]
            bias_ref[rows, sl] = jnp.where(take, 0.0, MASK_NEG)
            off = off + pc[:, LANES - 1:LANES]

    return [scores] + [functools.partial(search, i) for i in range(32)] + [ties]


def _dsa_attend_steps(rows, q_ref, o_ref, bias_ref, k_aug, v_aug):
    nr = rows.stop - rows.start
    tail_lane = lax.broadcasted_iota(jnp.int32, (nr, LANES - DSA_HEAD_DIM), 1)

    def logits(h):
        slope = 2.0 ** (-8.0 * (h + 1) / DSA_HEADS)
        tail = jnp.where(tail_lane < 2, slope, 0.0).astype(BF16)
        qh = q_ref[0, rows, h * DSA_HEAD_DIM:(h + 1) * DSA_HEAD_DIM]
        return _dot_nt(jnp.concatenate([qh, tail], axis=1), k_aug) + bias_ref[rows, :]

    def values(h, p):
        o = _dot(p, v_aug)
        o = o[:, 0:DSA_HEAD_DIM] / o[:, DSA_HEAD_DIM:DSA_HEAD_DIM + 1]
        o_ref[0, rows, h * DSA_HEAD_DIM:(h + 1) * DSA_HEAD_DIM] = o.astype(o_ref.dtype)

    lg, mx, pr = {}, {}, {}
    d_max, d_exp, d_pv = STAGE_LAGS

    def step(t):
        if t < DSA_HEADS:
            lg[t] = logits(t)
        if 0 <= t - d_max < DSA_HEADS:
            mx[t - d_max] = jnp.max(lg[t - d_max], axis=-1, keepdims=True)
        if 0 <= t - d_exp < DSA_HEADS:
            h = t - d_exp
            pr[h] = jnp.exp(lg.pop(h) - mx.pop(h)).astype(BF16)
        if 0 <= t - d_pv < DSA_HEADS:
            values(t - d_pv, pr.pop(t - d_pv))

    return [functools.partial(step, t) for t in range(DSA_HEADS + d_pv)]


def _dsa_kernel(q_ref, iq_ref, mq_ref, mk_ref, _aliased_out, o_ref, sc_ref, bias_ref, *, topk):
    tq = q_ref.shape[1]
    rows = slice(0, tq)
    select, attend = [], []
    for bi in range(q_ref.shape[0]):
        one = lambda ref: ref.at[bi:bi + 1]
        sc_b, bias_b = sc_ref.at[bi], bias_ref.at[bi]
        d0, keep, ik_b, k_aug, v_aug = _dsa_prepare(one(q_ref), one(mk_ref))
        if mk_ref.shape[1] > topk:
            select.append(_dsa_select_steps(rows, one(iq_ref), one(mq_ref), sc_b, bias_b, ik_b, keep,
                                            topk=topk))
        else:
            if d0 > 0:
                bias_b[:, 0:d0] = jnp.zeros((tq, d0), F32)
            bias_b[:, d0:] = jnp.where(keep, 0.0, MASK_NEG)
        attend.append(_dsa_attend_steps(rows, one(q_ref), one(o_ref), bias_b, k_aug, v_aug))
    for steps in list(zip(*select)) + list(zip(*attend)):
        for step in steps:
            step()


def _dsa(q, iq, misc):
    bsz, s, _ = q.shape
    topk = min(DSA_TOPK_MAX, s // 4)
    tq = min(256, s)
    hd = DSA_HEADS * DSA_HEAD_DIM
    att = jnp.zeros((bsz, s, hd), BF16)
    for c in range(s // tq):
        ext = (c + 1) * tq
        bb = 2 if (ext <= DSA_PAIR_MAX_EXTENT and bsz % 2 == 0) else 1
        att = pl.pallas_call(
            functools.partial(_dsa_kernel, topk=topk),
            grid=(bsz // bb,),
            in_specs=[pl.BlockSpec((bb, tq, hd), lambda b, c=c: (b, c, 0)),
                      pl.BlockSpec((bb, tq, IDX_HEADS * IDX_DIM), lambda b, c=c: (b, c, 0)),
                      pl.BlockSpec((bb, tq, IN_MISC), lambda b, c=c: (b, c, 0)),
                      pl.BlockSpec((bb, ext, IN_MISC), lambda b: (b, 0, 0)),
                      pl.BlockSpec(memory_space=pl.ANY)],
            out_specs=pl.BlockSpec((bb, tq, hd), lambda b, c=c: (b, c, 0)),
            out_shape=jax.ShapeDtypeStruct((bsz, s, hd), BF16),
            scratch_shapes=[pltpu.VMEM((bb, tq, ext), F32), pltpu.VMEM((bb, tq, ext), F32)],
            input_output_aliases={4: 0},
            compiler_params=_cparams(("parallel",)),
            name=f"hy_dsa_c{c}",
        )(q, iq, misc, misc, att)
    return att


def _tail_kernel(*refs, n_in, tf):
    a_refs = refs[:n_in]
    wo_refs = refs[n_in:2 * n_in]
    (x_ref, p_ref, g1_ref, b1_ref, w1_ref, w2_ref, wg_ref, wp_ref, g2_ref, b2_ref,
     o_ref, acc_ref) = refs[2 * n_in:]
    n_chunks = w1_ref.shape[1] // tf

    def steps_for(rows):
        st = {}

        def out_proj():
            m = None
            for a_ref, w_ref in zip(a_refs, wo_refs):
                t = _dot(a_ref[rows, :].astype(BF16), w_ref[...])
                m = t if m is None else m + t
            st["y"] = DN_ALPHA * x_ref[rows, :] + m

        def norm1():
            st["h"] = _layer_norm(st.pop("y"), g1_ref[...], b1_ref[...])
            st["hb"] = st["h"].astype(BF16)

        def gate():
            g = jax.nn.sigmoid(_dot(st["hb"], wg_ref[...]))
            acc_ref[rows, :] = DN_ALPHA * st.pop("h") + g * _dot(p_ref[rows, :].astype(BF16), wp_ref[...])

        def mlp(c):
            a = jnp.maximum(_dot(st["hb"], w1_ref[:, c * tf:(c + 1) * tf]), 0.0)
            acc_ref[rows, :] += _dot((a * a).astype(BF16), w2_ref[c * tf:(c + 1) * tf, :])

        def norm2():
            o_ref[rows, :] = _layer_norm(acc_ref[rows, :], g2_ref[...], b2_ref[...])

        return [out_proj, norm1, gate] + [functools.partial(mlp, c) for c in range(n_chunks)] + [norm2]

    half = x_ref.shape[0] // 2
    first, second = steps_for(slice(0, half)), steps_for(slice(half, 2 * half))
    lag = 2
    for t in range(len(first) + lag):
        if t < len(first):
            first[t]()
        if 0 <= t - lag < len(second):
            second[t - lag]()


def _tail(acts, wos, x2, p3, layer, g1, b1, w1, w2, wg, wp, g2, b2, tm, tf):
    m, d = x2.shape
    n_in = len(acts)
    rows = lambda a: pl.BlockSpec((tm, a.shape[1]), lambda i: (i, 0))
    resident = lambda a: pl.BlockSpec(a.shape, lambda i: (0, 0), pipeline_mode=pl.Buffered(1))
    of_layer = lambda a: pl.BlockSpec((None,) + a.shape[1:], lambda i: (layer, 0, 0),
                                      pipeline_mode=pl.Buffered(1))
    params = [g1, b1, w1, w2, wg, wp, g2, b2]
    return pl.pallas_call(
        functools.partial(_tail_kernel, n_in=n_in, tf=tf),
        grid=(m // tm,),
        in_specs=([rows(a) for a in acts] + [resident(w) for w in wos]
                  + [rows(x2), pl.BlockSpec((None, tm, p3.shape[2]), lambda i: (layer, i, 0))]
                  + [of_layer(a) for a in params]),
        out_specs=pl.BlockSpec((tm, d), lambda i: (i, 0)),
        out_shape=jax.ShapeDtypeStruct((m, d), F32),
        scratch_shapes=[pltpu.VMEM((tm, d), F32)],
        compiler_params=_cparams(("parallel",)),
        name="layer_tail",
    )(*acts, *wos, x2, p3, *params)


MLA_Q_SCALE = (MLA_NOPE + MLA_ROPE) ** -0.5 * math.log2(math.e)
ROPE_HALF = MLA_ROPE // 2
NOPE_LO = LANES // 2 - ROPE_HALF


def _rope_group(t, cos_t, sin_t):
    return t * cos_t + pltpu.roll(t, LANES // 2, axis=1) * sin_t


def _mla_proj_kernel(x_ref, wd_ref, qg_ref, kvg_ref, wq_ref, wk_ref, wv_ref,
                     cosq_ref, sinq_ref, cosk_ref, sink_ref, q_ref, k_ref, v_ref):
    xb = x_ref[...].astype(BF16)
    down = _dot(xb, wd_ref[...])
    cq = down[:, 0:MLA_Q_LORA]
    ckv = down[:, MLA_Q_LORA:MLA_Q_LORA + MLA_KV_LORA]
    krg = down[:, MLA_Q_LORA + MLA_KV_LORA:]
    cqn = (cq * lax.rsqrt(jnp.mean(cq * cq, axis=-1, keepdims=True) + RMS_EPS) * qg_ref[...]).astype(BF16)
    ckvn = (ckv * lax.rsqrt(jnp.mean(ckv * ckv, axis=-1, keepdims=True) + RMS_EPS) * kvg_ref[...]).astype(BF16)
    cos_q, sin_q = cosq_ref[...], sinq_ref[...]
    kr = _rope_group(krg, cosk_ref[...], sink_ref[...])
    q = _dot(cqn, wq_ref[...])
    k = _dot(ckvn, wk_ref[...])
    for h in range(MLA_HEADS):
        sl = slice(h * LANES, (h + 1) * LANES)
        q_ref[:, sl] = _rope_group(q[:, sl], cos_q, sin_q).astype(BF16)
        k_ref[:, sl] = (k[:, sl] + kr).astype(BF16)
    v_ref[...] = _dot(ckvn, wv_ref[...]).astype(BF16)


def _mla_proj(x2, wd, qg, kvg, wq, wk, wv, tables, tm, s):
    m, d = x2.shape
    nt = s // tm
    full = lambda a: pl.BlockSpec(a.shape, lambda i: (0, 0))
    tab = lambda: pl.BlockSpec((tm, LANES), lambda i: (i % nt, 0))
    hq = MLA_HEADS * LANES
    hv = MLA_HEADS * MLA_V
    return pl.pallas_call(
        _mla_proj_kernel,
        grid=(m // tm,),
        in_specs=[pl.BlockSpec((tm, d), lambda i: (i, 0)), full(wd), full(qg), full(kvg),
                  full(wq), full(wk), full(wv)] + [tab() for _ in tables],
        out_specs=[pl.BlockSpec((tm, hq), lambda i: (i, 0)),
                   pl.BlockSpec((tm, hq), lambda i: (i, 0)),
                   pl.BlockSpec((tm, hv), lambda i: (i, 0))],
        out_shape=[jax.ShapeDtypeStruct((m, hq), BF16),
                   jax.ShapeDtypeStruct((m, hq), BF16),
                   jax.ShapeDtypeStruct((m, hv), BF16)],
        compiler_params=_cparams(("parallel",)),
        name="mla_proj",
    )(x2, wd, qg, kvg, wq, wk, wv, *tables)


def _mla_attn_kernel(q_ref, k_ref, v_ref, o_ref, ot_ref, *, tq, heads_per_step, key_chunk):
    s = q_ref.shape[1]
    keep_t = (lax.broadcasted_iota(jnp.int32, (tq, tq), 0)
              <= lax.broadcasted_iota(jnp.int32, (tq, tq), 1))
    v_t = jnp.transpose(v_ref[0].astype(F32)).astype(BF16)
    ones_rows = jnp.ones((2 * SUBLANES, s), BF16)
    lhs = [jnp.concatenate([v_t[hh * MLA_V:(hh + 1) * MLA_V, :], ones_rows], axis=0)
           for hh in range(heads_per_step)]
    units = []
    for j in range(s // tq):
        ext = (j + 1) * tq
        starts = list(range(0, ext - tq, key_chunk)) or [0]
        bounds = [(a, b) for a, b in zip(starts, starts[1:] + [ext])]
        for hh in range(heads_per_step):
            units += [(j * tq, hh, k0, k1) for k0, k1 in bounds]

    def logits(r0, hh, k0, k1):
        ql = slice(hh * LANES, (hh + 1) * LANES)
        sc = _dot_nt(k_ref[0, k0:k1, ql], q_ref[0, r0:r0 + tq, ql])
        if k1 == r0 + tq:
            diag = jnp.where(keep_t, sc[k1 - k0 - tq:, :], MASK_NEG)
            sc = diag if k1 - k0 == tq else jnp.concatenate([sc[:k1 - k0 - tq, :], diag], axis=0)
        return sc

    def probs(sc):
        m = _reduce_keys(sc, jnp.max, jnp.maximum)
        return jnp.exp2(sc - m).astype(BF16), m

    running = {}

    def values(r0, hh, k0, k1, p, m):
        acc = _dot(lhs[hh][:, k0:k1], p)
        if k0 > 0:
            m_run, acc_run = running.pop((r0, hh))
            m_new = jnp.maximum(m_run, m)
            acc = acc_run * jnp.exp2(m_run - m_new) + acc * jnp.exp2(m - m_new)
            m = m_new
        if k1 == r0 + tq:
            ot_ref[hh * MLA_V:(hh + 1) * MLA_V, r0:r0 + tq] = acc[0:MLA_V, :] / acc[MLA_V:MLA_V + 1, :]
        else:
            running[(r0, hh)] = (m, acc)

    n = len(units)
    stage_s, stage_m, stage_p = {}, {}, {}
    d_max, d_exp, d_pv = STAGE_LAGS
    for t in range(n + d_pv):
        if t < n:
            stage_s[t] = logits(*units[t])
        if 0 <= t - d_max < n:
            u = t - d_max
            stage_m[u] = _reduce_keys(stage_s[u], jnp.max, jnp.maximum)
        if 0 <= t - d_exp < n:
            u = t - d_exp
            stage_p[u] = jnp.exp2(stage_s.pop(u) - stage_m[u]).astype(BF16)
        if 0 <= t - d_pv < n:
            u = t - d_pv
            values(*units[u], stage_p.pop(u), stage_m.pop(u))
    o_ref[0] = jnp.transpose(ot_ref[...]).astype(o_ref.dtype)


def _mla_attn(q, k, v, tq):
    bsz, s, _ = q.shape
    hps = 4
    return pl.pallas_call(
        functools.partial(_mla_attn_kernel, tq=tq, heads_per_step=hps, key_chunk=2 * tq),
        grid=(bsz, MLA_HEADS // hps),
        in_specs=[pl.BlockSpec((1, s, hps * LANES), lambda b, h: (b, 0, h)),
                  pl.BlockSpec((1, s, hps * LANES), lambda b, h: (b, 0, h)),
                  pl.BlockSpec((1, s, hps * MLA_V), lambda b, h: (b, 0, h))],
        out_specs=pl.BlockSpec((1, s, hps * MLA_V), lambda b, h: (b, 0, h)),
        out_shape=jax.ShapeDtypeStruct((bsz, s, MLA_HEADS * MLA_V), BF16),
        scratch_shapes=[pltpu.VMEM((hps * MLA_V, s), F32)],
        compiler_params=_cparams(("parallel", "parallel")),
        name="mla_attn",
    )(q, k, v)


def _hy_in_weight(w_in):
    widths = [RNN_WIDTH, RNN_WIDTH, IN_Q, DSA_HEAD_DIM, DSA_HEAD_DIM, IN_IQ, IDX_DIM, IDX_HEADS]
    xr, yr, q, k, v, iq, ik, iw = jnp.split(w_in, np.cumsum(widths)[:-1].tolist(), axis=1)
    pad = jnp.zeros((w_in.shape[0], IN_MISC - 2 * DSA_HEAD_DIM - IDX_DIM - IDX_HEADS), w_in.dtype)
    return jnp.concatenate([xr, yr, q, iq, k, v, ik, iw, pad], axis=1).astype(BF16)


def _block_diag_groups(w):
    bw = w.shape[-1]
    per = LANES // bw
    ng = w.shape[0] // per
    on_diag = jnp.eye(per, dtype=bool)[None, :, None, :, None]
    out = jnp.where(on_diag, w.reshape(ng, per, bw, 1, bw), 0.0)
    return out.reshape(ng, LANES, LANES).astype(BF16)


def _head_lanes(nope, rope):
    lead = nope.shape[:-1] if nope is not None else rope.shape[:-1]
    dt = nope.dtype if nope is not None else rope.dtype
    z = lambda n: jnp.zeros(lead + (n,), dt)
    r1, r2 = (rope[..., :ROPE_HALF], rope[..., ROPE_HALF:]) if rope is not None else (z(ROPE_HALF), z(ROPE_HALF))
    n1, n2 = (nope[..., :NOPE_LO], nope[..., NOPE_LO:]) if nope is not None else (z(NOPE_LO), z(MLA_NOPE - NOPE_LO))
    return jnp.concatenate([r1, n1, r2, n2, z(LANES - MLA_NOPE - MLA_ROPE)], axis=-1)


def _mla_weights(w_down, w_uq, w_ukv):
    lat = MLA_Q_LORA + MLA_KV_LORA
    wd = jnp.concatenate([w_down[:, :lat], _head_lanes(None, w_down[:, lat:])], axis=1)
    wq = w_uq.reshape(MLA_Q_LORA, MLA_HEADS, MLA_NOPE + MLA_ROPE)
    wq = _head_lanes(wq[:, :, :MLA_NOPE], wq[:, :, MLA_NOPE:]).reshape(MLA_Q_LORA, MLA_HEADS * LANES)
    wkv = w_ukv.reshape(MLA_KV_LORA, MLA_HEADS, MLA_NOPE + MLA_V)
    wk = _head_lanes(wkv[:, :, :MLA_NOPE], None).reshape(MLA_KV_LORA, MLA_HEADS * LANES)
    wv = wkv[:, :, MLA_NOPE:].reshape(MLA_KV_LORA, MLA_HEADS * MLA_V)
    return wd.astype(BF16), wq.astype(BF16), wk.astype(BF16), wv.astype(BF16)


def _rope_tables(s):
    pos = np.arange(s, dtype=np.float64)
    freq = ROPE_BASE ** (-np.arange(0, MLA_ROPE, 2, dtype=np.float64) / MLA_ROPE)
    ang = pos[:, None] * freq[None, :]
    lo = slice(0, ROPE_HALF)
    hi = slice(LANES // 2, LANES // 2 + ROPE_HALF)
    cos_t = np.ones((s, LANES))
    sin_t = np.zeros((s, LANES))
    cos_t[:, lo] = cos_t[:, hi] = np.cos(ang)
    sin_t[:, lo] = -np.sin(ang)
    sin_t[:, hi] = np.sin(ang)
    f32 = lambda a: jnp.asarray(a.astype(np.float32))
    return f32(cos_t * MLA_Q_SCALE), f32(sin_t * MLA_Q_SCALE), f32(cos_t), f32(sin_t)


def _tile_m(m):
    return 512 if m % 512 == 0 else m


def kernel(x, p, ln1_g, ln1_b, ln2_g, ln2_b, mlp_w1, mlp_w2, ple_w_proj, ple_w_gate, hy_w_in, hy_conv_w, hy_conv_b, hy_ga_w, hy_ga_b, hy_gx_w, hy_gx_b, hy_lambda, hy_w_out, mla_w_down, mla_q_norm, mla_kv_norm, mla_w_uq, mla_w_ukv, mla_w_out):
    bsz, s, d = x.shape
    m = bsz * s
    tm = _tile_m(m)
    tf = 1024
    row = lambda a: a.reshape(1, -1)
    x2 = x.reshape(m, d)

    rows3 = lambda a: a.reshape(a.shape[0], 1, -1)
    tail_params = (rows3(ln1_g), rows3(ln1_b), mlp_w1.astype(BF16), mlp_w2.astype(BF16),
                   ple_w_gate.astype(BF16), ple_w_proj.astype(BF16), rows3(ln2_g), rows3(ln2_b))
    p3 = p.reshape(p.shape[0], m, -1)

    def layer_tail(acts, wos, x2, i):
        return _tail(acts, wos, x2, p3, i, *tail_params, tm, tf)

    xy, q, iq, misc = _inproj(x2, _hy_in_weight(hy_w_in[0]), tm)
    rec = _rglru(xy.reshape(bsz, s, -1), hy_conv_w[0], row(hy_conv_b[0]),
                 _block_diag_groups(hy_ga_w[0]), row(hy_ga_b[0]),
                 _block_diag_groups(hy_gx_w[0]), row(hy_gx_b[0]), row(hy_lambda[0]))
    att = _dsa(q.reshape(bsz, s, -1), iq.reshape(bsz, s, -1), misc.reshape(bsz, s, -1))
    w_out = hy_w_out[0].astype(BF16)
    x2 = layer_tail([rec.reshape(m, -1), att.reshape(m, -1)], [w_out[:RNN_WIDTH], w_out[RNN_WIDTH:]], x2, 0)

    wd, wq, wk, wv = _mla_weights(mla_w_down[0], mla_w_uq[0], mla_w_ukv[0])
    qp, kp, vp = _mla_proj(x2, wd, row(mla_q_norm[0]), row(mla_kv_norm[0]), wq, wk, wv,
                           _rope_tables(s), min(tm, s), s)
    o = _mla_attn(qp.reshape(bsz, s, -1), kp.reshape(bsz, s, -1), vp.reshape(bsz, s, -1), min(256, s))
    x2 = layer_tail([o.reshape(m, -1)], [mla_w_out[0].astype(BF16)], x2, 1)
    return x2.reshape(bsz, s, d)
```

```python
import functools
import math

import numpy as np
import jax
import jax.numpy as jnp
from jax import lax
from jax.experimental import pallas as pl
from jax.experimental.pallas import tpu as pltpu

F32 = jnp.float32
BF16 = jnp.bfloat16

RNN_WIDTH = 512
CONV_WIDTH = 4
LRU_C = 8.0
DSA_HEADS = 8
DSA_HEAD_DIM = 64
IDX_HEADS = 4
IDX_DIM = 64
DSA_TOPK_MAX = 256
MLA_HEADS = 16
MLA_Q_LORA = 512
MLA_KV_LORA = 256
MLA_NOPE = 64
MLA_ROPE = 32
MLA_V = 64
ROPE_BASE = 10000.0
DEPTH = 2
DN_ALPHA = (2 * DEPTH) ** 0.25
LN_EPS = 1e-5
RMS_EPS = 1e-6

LANES = 128
SUBLANES = 8
VMEM_LIMIT_BYTES = 56 * 1024 * 1024

MASK_NEG = -3.0e38
INT32_MIN = -(2 ** 31)
STAGE_LAGS = (1, 2, 4)
DSA_Q_SCALE = DSA_HEAD_DIM ** -0.5
DSA_PAIR_MAX_EXTENT = 768
TAIL_ROWS = 1024
TAIL_SUB_ROWS = 256


def _cparams(sem):
    return pltpu.CompilerParams(dimension_semantics=sem, vmem_limit_bytes=VMEM_LIMIT_BYTES)


def _dot(a, b):
    return jnp.dot(a, b, preferred_element_type=F32)


def _dot_nt(a, b):
    return lax.dot_general(a, b, (((1,), (1,)), ((), ())), preferred_element_type=F32)


def _layer_norm(y, g, b):
    mu = jnp.mean(y, axis=-1, keepdims=True)
    yc = y - mu
    var = jnp.mean(yc * yc, axis=-1, keepdims=True)
    return yc * lax.rsqrt(var + LN_EPS) * g + b


IN_XY = 2 * RNN_WIDTH
IN_Q = DSA_HEADS * DSA_HEAD_DIM
IN_IQ = IDX_HEADS * IDX_DIM
IN_MISC = 2 * LANES
IN_OFFSETS = tuple(int(v) for v in np.cumsum([0, IN_XY, IN_Q, IN_IQ, IN_MISC]))


def _inproj_kernel(x_ref, w_ref, xy_ref, q_ref, iq_ref, misc_ref):
    xb = x_ref[...].astype(BF16)
    cols = [w_ref[:, a:b] for a, b in zip(IN_OFFSETS[:-1], IN_OFFSETS[1:])]
    xy_ref[...] = _dot(xb, cols[0])
    q_ref[...] = (_dot(xb, cols[1]) * DSA_Q_SCALE).astype(BF16)
    iq_ref[...] = _dot(xb, cols[2])
    misc_ref[...] = _dot(xb, cols[3])


def _inproj(x2, w, tm):
    m, d = x2.shape
    n = w.shape[1]
    widths = (IN_XY, IN_Q, IN_IQ, IN_MISC)
    dtypes = (F32, BF16, F32, F32)
    return pl.pallas_call(
        _inproj_kernel,
        grid=(m // tm,),
        in_specs=[pl.BlockSpec((tm, d), lambda i: (i, 0)),
                  pl.BlockSpec((d, n), lambda i: (0, 0))],
        out_specs=[pl.BlockSpec((tm, wd), lambda i: (i, 0)) for wd in widths],
        out_shape=[jax.ShapeDtypeStruct((m, wd), dt) for wd, dt in zip(widths, dtypes)],
        compiler_params=_cparams(("parallel",)),
        name="hy_inproj",
    )(x2, w)


def _rglru_kernel(xr_ref, yr_ref, cw_ref, cb_ref, wa_ref, ba_ref, wx_ref, bx_ref, lam_ref,
                  o_ref, xpad_ref):
    s = xr_ref.shape[1]
    c = xr_ref.shape[2]
    xpad_ref[0:SUBLANES, :] = jnp.zeros((SUBLANES, c), F32)
    xpad_ref[SUBLANES:SUBLANES + s, :] = xr_ref[0]
    xc = None
    for k in range(CONV_WIDTH):
        start = SUBLANES - (CONV_WIDTH - 1) + k
        term = xpad_ref[start:start + s, :] * cw_ref[k:k + 1, :]
        xc = term if xc is None else xc + term
    xc = xc + cb_ref[...]
    xb = xc.astype(BF16)
    r = jax.nn.sigmoid(_dot(xb, wa_ref[0]) + ba_ref[...])
    gi = jax.nn.sigmoid(_dot(xb, wx_ref[0]) + bx_ref[...])
    nl = -lam_ref[...]
    softplus = jnp.maximum(nl, 0.0) + jnp.log1p(jnp.exp(-jnp.abs(nl)))
    log_a = (-LRU_C) * r * softplus
    a = jnp.exp(log_a)
    y = -jnp.tanh(log_a) * (a * a + 1.0)
    h = jnp.where(y > 0.0, y * lax.rsqrt(y), 0.0) * (gi * xc)
    row = lax.broadcasted_iota(jnp.int32, (s, c), 0)
    sh = 1
    while sh < s:
        if sh < SUBLANES:
            valid = row >= sh
            h = a * jnp.where(valid, pltpu.roll(h, sh, axis=0), 0.0) + h
            a = a * jnp.where(valid, pltpu.roll(a, sh, axis=0), 1.0)
        else:
            h = jnp.concatenate([h[:sh], a[sh:] * h[:s - sh] + h[sh:]], axis=0)
            if sh * 2 < s:
                a = jnp.concatenate([a[:sh], a[sh:] * a[:s - sh]], axis=0)
        sh *= 2
    o_ref[0] = (h * jax.nn.gelu(yr_ref[0])).astype(o_ref.dtype)


def _rglru(xy, conv_w, conv_b, wa, ba, wx, bx, lam):
    bsz, s, _ = xy.shape
    c = LANES
    ng = RNN_WIDTH // c
    vec = lambda: pl.BlockSpec((1, c), lambda b, g: (0, g))
    return pl.pallas_call(
        _rglru_kernel,
        grid=(bsz, ng),
        in_specs=[pl.BlockSpec((1, s, c), lambda b, g: (b, 0, g)),
                  pl.BlockSpec((1, s, c), lambda b, g: (b, 0, ng + g)),
                  pl.BlockSpec((CONV_WIDTH, c), lambda b, g: (0, g)),
                  vec(),
                  pl.BlockSpec((1, c, c), lambda b, g: (g, 0, 0)),
                  vec(),
                  pl.BlockSpec((1, c, c), lambda b, g: (g, 0, 0)),
                  vec(),
                  vec()],
        out_specs=pl.BlockSpec((1, s, c), lambda b, g: (b, 0, g)),
        out_shape=jax.ShapeDtypeStruct((bsz, s, RNN_WIDTH), BF16),
        scratch_shapes=[pltpu.VMEM((s + SUBLANES, c), F32)],
        compiler_params=_cparams(("parallel", "parallel")),
        name="hy_rglru",
    )(xy, xy, conv_w, conv_b, wa, ba, wx, bx, lam)


def _ordered_int_to_float(c):
    return lax.bitcast_convert_type(c ^ (lax.shift_right_arithmetic(c, 31) & jnp.int32(0x7FFFFFFF)), F32)


def _reduce_keys(x, reduce_fn, combine_fn, groups=8):
    rows = x.shape[0]
    step = max(rows // groups, SUBLANES)
    parts = [reduce_fn(x[r:r + step, :], axis=0, keepdims=True) for r in range(0, rows, step)]
    while len(parts) > 1:
        parts = [combine_fn(parts[i], parts[i + 1]) if i + 1 < len(parts) else parts[i]
                 for i in range(0, len(parts), 2)]
    return parts[0]


def _dsa_prepare(q_ref, mk_ref):
    tq = q_ref.shape[1]
    ext = mk_ref.shape[1]
    d0 = ext - tq
    keep = (lax.broadcasted_iota(jnp.int32, (tq, tq), 1)
            <= lax.broadcasted_iota(jnp.int32, (tq, tq), 0))

    mk = mk_ref[0]
    vlane = lax.broadcasted_iota(jnp.int32, (ext, DSA_HEAD_DIM), 1)
    v_aug = jnp.concatenate([mk[:, DSA_HEAD_DIM:2 * DSA_HEAD_DIM],
                             jnp.where(vlane == 0, 1.0, 0.0)], axis=1).astype(BF16)
    ik_b = mk[:, 2 * DSA_HEAD_DIM:2 * DSA_HEAD_DIM + IDX_DIM].astype(BF16)
    kcol = lax.broadcasted_iota(jnp.int32, (ext, 1), 0)
    lane = lax.broadcasted_iota(jnp.int32, (1, LANES), 1)
    pos_hi = (kcol & jnp.int32(-LANES)).astype(F32)
    pos_lo = (kcol & jnp.int32(LANES - 1)).astype(F32)
    k_aug = jnp.where(lane < DSA_HEAD_DIM, mk[:, 0:LANES],
                      jnp.where(lane == DSA_HEAD_DIM, pos_hi,
                                jnp.where(lane == DSA_HEAD_DIM + 1, pos_lo, 0.0))).astype(BF16)
    return d0, keep, ik_b, k_aug, v_aug


def _dsa_select_steps(rows, iq_ref, mq_ref, sc_ref, bias_ref, ik_b, keep, *, topk):
    tq, ext = sc_ref.shape
    nr = rows.stop - rows.start
    d0 = ext - tq
    kf = float(topk)
    st = {}

    def scores():
        iw0 = 2 * DSA_HEAD_DIM + IDX_DIM
        iw = mq_ref[0, rows, iw0:iw0 + IDX_HEADS] * (IDX_HEADS ** -0.5 * IDX_DIM ** -0.5)
        iq = iq_ref[0, rows, :].astype(BF16)
        sc = None
        for h in range(IDX_HEADS):
            d = _dot_nt(iq[:, h * IDX_DIM:(h + 1) * IDX_DIM], ik_b)
            term = jnp.maximum(d, 0.0) * iw[:, h:h + 1]
            sc = term if sc is None else sc + term
        if d0 > 0:
            sc_ref[rows, 0:d0] = sc[:, 0:d0]
        sc_ref[rows, d0:ext] = jnp.where(keep[rows, :], sc[:, d0:ext], MASK_NEG)
        st["tau"] = jnp.full((nr, 1), INT32_MIN, jnp.int32)

    def search(i):
        cand = st["tau"] + jnp.int32(1 << (31 - i) if i else INT32_MIN)
        ge = sc_ref[rows, :] >= _ordered_int_to_float(cand)
        cnt = jnp.sum(jnp.where(ge, 1.0, 0.0), axis=-1, keepdims=True)
        st["tau"] = jnp.where(cnt >= kf, cand, st["tau"])

    def ties():
        tau = _ordered_int_to_float(st.pop("tau"))
        sc = sc_ref[rows, :]
        gt = sc > tau
        eq = sc == tau
        need = kf - jnp.sum(jnp.where(gt, 1.0, 0.0), axis=-1, keepdims=True)
        eqf = jnp.where(eq, 1.0, 0.0).astype(BF16)
        tri = (lax.broadcasted_iota(jnp.int32, (LANES, LANES), 0)
               <= lax.broadcasted_iota(jnp.int32, (LANES, LANES), 1)).astype(BF16)
        off = jnp.zeros((nr, 1), F32)
        for c in range(ext // LANES):
            sl = slice(c * LANES, (c + 1) * LANES)
            pc = _dot(eqf[:, sl], tri)
            take = gt[:, sl] | (eq[:, sl] & ((pc + off) <= need))
            if c * LANES >= d0:
                take = take & keep[rows, c * LANES - d0:(c + 1) * LANES - d0]
            bias_ref[rows, sl] = jnp.where(take, 0.0, MASK_NEG)
            off = off + pc[:, LANES - 1:LANES]

    return [scores] + [functools.partial(search, i) for i in range(32)] + [ties]


def _dsa_attend_steps(rows, q_ref, o_ref, bias_ref, k_aug, v_aug):
    nr = rows.stop - rows.start
    tail_lane = lax.broadcasted_iota(jnp.int32, (nr, LANES - DSA_HEAD_DIM), 1)

    def logits(h):
        slope = 2.0 ** (-8.0 * (h + 1) / DSA_HEADS)
        tail = jnp.where(tail_lane < 2, slope, 0.0).astype(BF16)
        qh = q_ref[0, rows, h * DSA_HEAD_DIM:(h + 1) * DSA_HEAD_DIM]
        return _dot_nt(jnp.concatenate([qh, tail], axis=1), k_aug) + bias_ref[rows, :]

    def values(h, p):
        o = _dot(p, v_aug)
        o = o[:, 0:DSA_HEAD_DIM] / o[:, DSA_HEAD_DIM:DSA_HEAD_DIM + 1]
        o_ref[0, rows, h * DSA_HEAD_DIM:(h + 1) * DSA_HEAD_DIM] = o.astype(o_ref.dtype)

    lg, mx, pr = {}, {}, {}
    d_max, d_exp, d_pv = STAGE_LAGS

    def step(t):
        if t < DSA_HEADS:
            lg[t] = logits(t)
        if 0 <= t - d_max < DSA_HEADS:
            mx[t - d_max] = jnp.max(lg[t - d_max], axis=-1, keepdims=True)
        if 0 <= t - d_exp < DSA_HEADS:
            h = t - d_exp
            pr[h] = jnp.exp(lg.pop(h) - mx.pop(h)).astype(BF16)
        if 0 <= t - d_pv < DSA_HEADS:
            values(t - d_pv, pr.pop(t - d_pv))

    return [functools.partial(step, t) for t in range(DSA_HEADS + d_pv)]


def _dsa_kernel(q_ref, iq_ref, mq_ref, mk_ref, _aliased_out, o_ref, sc_ref, bias_ref, *, topk):
    tq = q_ref.shape[1]
    rows = slice(0, tq)
    select, attend = [], []
    for bi in range(q_ref.shape[0]):
        one = lambda ref: ref.at[bi:bi + 1]
        sc_b, bias_b = sc_ref.at[bi], bias_ref.at[bi]
        d0, keep, ik_b, k_aug, v_aug = _dsa_prepare(one(q_ref), one(mk_ref))
        if mk_ref.shape[1] > topk:
            select.append(_dsa_select_steps(rows, one(iq_ref), one(mq_ref), sc_b, bias_b, ik_b, keep,
                                            topk=topk))
        else:
            if d0 > 0:
                bias_b[:, 0:d0] = jnp.zeros((tq, d0), F32)
            bias_b[:, d0:] = jnp.where(keep, 0.0, MASK_NEG)
        attend.append(_dsa_attend_steps(rows, one(q_ref), one(o_ref), bias_b, k_aug, v_aug))
    for steps in list(zip(*select)) + list(zip(*attend)):
        for step in steps:
            step()


def _dsa(q, iq, misc):
    bsz, s, _ = q.shape
    topk = min(DSA_TOPK_MAX, s // 4)
    tq = min(256, s)
    hd = DSA_HEADS * DSA_HEAD_DIM
    att = jnp.zeros((bsz, s, hd), BF16)
    for c in range(s // tq):
        ext = (c + 1) * tq
        bb = 2 if (ext <= DSA_PAIR_MAX_EXTENT and bsz % 2 == 0) else 1
        att = pl.pallas_call(
            functools.partial(_dsa_kernel, topk=topk),
            grid=(bsz // bb,),
            in_specs=[pl.BlockSpec((bb, tq, hd), lambda b, c=c: (b, c, 0)),
                      pl.BlockSpec((bb, tq, IDX_HEADS * IDX_DIM), lambda b, c=c: (b, c, 0)),
                      pl.BlockSpec((bb, tq, IN_MISC), lambda b, c=c: (b, c, 0)),
                      pl.BlockSpec((bb, ext, IN_MISC), lambda b: (b, 0, 0)),
                      pl.BlockSpec(memory_space=pl.ANY)],
            out_specs=pl.BlockSpec((bb, tq, hd), lambda b, c=c: (b, c, 0)),
            out_shape=jax.ShapeDtypeStruct((bsz, s, hd), BF16),
            scratch_shapes=[pltpu.VMEM((bb, tq, ext), F32), pltpu.VMEM((bb, tq, ext), F32)],
            input_output_aliases={4: 0},
            compiler_params=_cparams(("parallel",)),
            name=f"hy_dsa_c{c}",
        )(q, iq, misc, misc, att)
    return att


def _tail_kernel(*refs, n_in, tf):
    a_refs = refs[:n_in]
    wo_refs = refs[n_in:2 * n_in]
    (x_ref, p_ref, g1_ref, b1_ref, w1_ref, w2_ref, wg_ref, wp_ref, g2_ref, b2_ref,
     o_ref, acc_ref) = refs[2 * n_in:]
    n_chunks = w1_ref.shape[1] // tf

    def steps_for(rows):
        st = {}

        def out_proj():
            m = None
            for a_ref, w_ref in zip(a_refs, wo_refs):
                t = _dot(a_ref[rows, :].astype(BF16), w_ref[...])
                m = t if m is None else m + t
            st["y"] = DN_ALPHA * x_ref[rows, :] + m

        def norm1():
            st["h"] = _layer_norm(st.pop("y"), g1_ref[...], b1_ref[...])
            st["hb"] = st["h"].astype(BF16)

        def gate():
            g = jax.nn.sigmoid(_dot(st["hb"], wg_ref[...]))
            acc_ref[rows, :] = DN_ALPHA * st.pop("h") + g * _dot(p_ref[rows, :].astype(BF16), wp_ref[...])

        def mlp(c):
            a = jnp.maximum(_dot(st["hb"], w1_ref[:, c * tf:(c + 1) * tf]), 0.0)
            acc_ref[rows, :] += _dot((a * a).astype(BF16), w2_ref[c * tf:(c + 1) * tf, :])

        def norm2():
            o_ref[rows, :] = _layer_norm(acc_ref[rows, :], g2_ref[...], b2_ref[...])

        return [out_proj, norm1, gate] + [functools.partial(mlp, c) for c in range(n_chunks)] + [norm2]

    n_sub = x_ref.shape[0] // TAIL_SUB_ROWS
    subs = [steps_for(slice(i * TAIL_SUB_ROWS, (i + 1) * TAIL_SUB_ROWS)) for i in range(n_sub)]
    lag = 2
    for t in range(len(subs[0]) + lag * (n_sub - 1)):
        for i, steps in enumerate(subs):
            if 0 <= t - lag * i < len(steps):
                steps[t - lag * i]()


def _tail(acts, wos, x2, p3, layer, g1, b1, w1, w2, wg, wp, g2, b2, tm, tf):
    m, d = x2.shape
    n_in = len(acts)
    rows = lambda a: pl.BlockSpec((tm, a.shape[1]), lambda i: (i, 0))
    resident = lambda a: pl.BlockSpec(a.shape, lambda i: (0, 0), pipeline_mode=pl.Buffered(1))
    of_layer = lambda a: pl.BlockSpec((None,) + a.shape[1:], lambda i: (layer, 0, 0),
                                      pipeline_mode=pl.Buffered(1))
    params = [g1, b1, w1, w2, wg, wp, g2, b2]
    return pl.pallas_call(
        functools.partial(_tail_kernel, n_in=n_in, tf=tf),
        grid=(m // tm,),
        in_specs=([rows(a) for a in acts] + [resident(w) for w in wos]
                  + [rows(x2), pl.BlockSpec((None, tm, p3.shape[2]), lambda i: (layer, i, 0))]
                  + [of_layer(a) for a in params]),
        out_specs=pl.BlockSpec((tm, d), lambda i: (i, 0)),
        out_shape=jax.ShapeDtypeStruct((m, d), F32),
        scratch_shapes=[pltpu.VMEM((tm, d), F32)],
        compiler_params=_cparams(("parallel",)),
        name="layer_tail",
    )(*acts, *wos, x2, p3, *params)


MLA_Q_SCALE = (MLA_NOPE + MLA_ROPE) ** -0.5 * math.log2(math.e)
ROPE_HALF = MLA_ROPE // 2
NOPE_LO = LANES // 2 - ROPE_HALF


def _rope_group(t, cos_t, sin_t):
    return t * cos_t + pltpu.roll(t, LANES // 2, axis=1) * sin_t


def _mla_proj_kernel(x_ref, wd_ref, qg_ref, kvg_ref, wq_ref, wk_ref, wv_ref,
                     cosq_ref, sinq_ref, cosk_ref, sink_ref, q_ref, k_ref, v_ref):
    xb = x_ref[...].astype(BF16)
    down = _dot(xb, wd_ref[...])
    cq = down[:, 0:MLA_Q_LORA]
    ckv = down[:, MLA_Q_LORA:MLA_Q_LORA + MLA_KV_LORA]
    krg = down[:, MLA_Q_LORA + MLA_KV_LORA:]
    cqn = (cq * lax.rsqrt(jnp.mean(cq * cq, axis=-1, keepdims=True) + RMS_EPS) * qg_ref[...]).astype(BF16)
    ckvn = (ckv * lax.rsqrt(jnp.mean(ckv * ckv, axis=-1, keepdims=True) + RMS_EPS) * kvg_ref[...]).astype(BF16)
    cos_q, sin_q = cosq_ref[...], sinq_ref[...]
    kr = _rope_group(krg, cosk_ref[...], sink_ref[...])
    q = _dot(cqn, wq_ref[...])
    k = _dot(ckvn, wk_ref[...])
    for h in range(MLA_HEADS):
        sl = slice(h * LANES, (h + 1) * LANES)
        q_ref[:, sl] = _rope_group(q[:, sl], cos_q, sin_q).astype(BF16)
        k_ref[:, sl] = (k[:, sl] + kr).astype(BF16)
    v_ref[...] = _dot(ckvn, wv_ref[...]).astype(BF16)


def _mla_proj(x2, wd, qg, kvg, wq, wk, wv, tables, tm, s):
    m, d = x2.shape
    nt = s // tm
    full = lambda a: pl.BlockSpec(a.shape, lambda i: (0, 0))
    tab = lambda: pl.BlockSpec((tm, LANES), lambda i: (i % nt, 0))
    hq = MLA_HEADS * LANES
    hv = MLA_HEADS * MLA_V
    return pl.pallas_call(
        _mla_proj_kernel,
        grid=(m // tm,),
        in_specs=[pl.BlockSpec((tm, d), lambda i: (i, 0)), full(wd), full(qg), full(kvg),
                  full(wq), full(wk), full(wv)] + [tab() for _ in tables],
        out_specs=[pl.BlockSpec((tm, hq), lambda i: (i, 0)),
                   pl.BlockSpec((tm, hq), lambda i: (i, 0)),
                   pl.BlockSpec((tm, hv), lambda i: (i, 0))],
        out_shape=[jax.ShapeDtypeStruct((m, hq), BF16),
                   jax.ShapeDtypeStruct((m, hq), BF16),
                   jax.ShapeDtypeStruct((m, hv), BF16)],
        compiler_params=_cparams(("parallel",)),
        name="mla_proj",
    )(x2, wd, qg, kvg, wq, wk, wv, *tables)


def _mla_attn_kernel(q_ref, k_ref, v_ref, o_ref, ot_ref, *, tq, heads_per_step, key_chunk):
    s = q_ref.shape[1]
    keep_t = (lax.broadcasted_iota(jnp.int32, (tq, tq), 0)
              <= lax.broadcasted_iota(jnp.int32, (tq, tq), 1))
    v_t = jnp.transpose(v_ref[0].astype(F32)).astype(BF16)
    ones_rows = jnp.ones((2 * SUBLANES, s), BF16)
    lhs = [jnp.concatenate([v_t[hh * MLA_V:(hh + 1) * MLA_V, :], ones_rows], axis=0)
           for hh in range(heads_per_step)]
    units = []
    for j in range(s // tq):
        ext = (j + 1) * tq
        starts = list(range(0, ext - tq, key_chunk)) or [0]
        bounds = [(a, b) for a, b in zip(starts, starts[1:] + [ext])]
        for hh in range(heads_per_step):
            units += [(j * tq, hh, k0, k1) for k0, k1 in bounds]

    def logits(r0, hh, k0, k1):
        ql = slice(hh * LANES, (hh + 1) * LANES)
        sc = _dot_nt(k_ref[0, k0:k1, ql], q_ref[0, r0:r0 + tq, ql])
        if k1 == r0 + tq:
            diag = jnp.where(keep_t, sc[k1 - k0 - tq:, :], MASK_NEG)
            sc = diag if k1 - k0 == tq else jnp.concatenate([sc[:k1 - k0 - tq, :], diag], axis=0)
        return sc

    def probs(sc):
        m = _reduce_keys(sc, jnp.max, jnp.maximum)
        return jnp.exp2(sc - m).astype(BF16), m

    running = {}

    def values(r0, hh, k0, k1, p, m):
        acc = _dot(lhs[hh][:, k0:k1], p)
        if k0 > 0:
            m_run, acc_run = running.pop((r0, hh))
            m_new = jnp.maximum(m_run, m)
            acc = acc_run * jnp.exp2(m_run - m_new) + acc * jnp.exp2(m - m_new)
            m = m_new
        if k1 == r0 + tq:
            ot_ref[hh * MLA_V:(hh + 1) * MLA_V, r0:r0 + tq] = acc[0:MLA_V, :] / acc[MLA_V:MLA_V + 1, :]
        else:
            running[(r0, hh)] = (m, acc)

    n = len(units)
    stage_s, stage_m, stage_p = {}, {}, {}
    d_max, d_exp, d_pv = STAGE_LAGS
    for t in range(n + d_pv):
        if t < n:
            stage_s[t] = logits(*units[t])
        if 0 <= t - d_max < n:
            u = t - d_max
            stage_m[u] = _reduce_keys(stage_s[u], jnp.max, jnp.maximum)
        if 0 <= t - d_exp < n:
            u = t - d_exp
            stage_p[u] = jnp.exp2(stage_s.pop(u) - stage_m[u]).astype(BF16)
        if 0 <= t - d_pv < n:
            u = t - d_pv
            values(*units[u], stage_p.pop(u), stage_m.pop(u))
    o_ref[0] = jnp.transpose(ot_ref[...]).astype(o_ref.dtype)


def _mla_attn(q, k, v, tq):
    bsz, s, _ = q.shape
    hps = 4
    return pl.pallas_call(
        functools.partial(_mla_attn_kernel, tq=tq, heads_per_step=hps, key_chunk=2 * tq),
        grid=(bsz, MLA_HEADS // hps),
        in_specs=[pl.BlockSpec((1, s, hps * LANES), lambda b, h: (b, 0, h)),
                  pl.BlockSpec((1, s, hps * LANES), lambda b, h: (b, 0, h)),
                  pl.BlockSpec((1, s, hps * MLA_V), lambda b, h: (b, 0, h))],
        out_specs=pl.BlockSpec((1, s, hps * MLA_V), lambda b, h: (b, 0, h)),
        out_shape=jax.ShapeDtypeStruct((bsz, s, MLA_HEADS * MLA_V), BF16),
        scratch_shapes=[pltpu.VMEM((hps * MLA_V, s), F32)],
        compiler_params=_cparams(("parallel", "parallel")),
        name="mla_attn",
    )(q, k, v)


def _hy_in_weight(w_in):
    widths = [RNN_WIDTH, RNN_WIDTH, IN_Q, DSA_HEAD_DIM, DSA_HEAD_DIM, IN_IQ, IDX_DIM, IDX_HEADS]
    xr, yr, q, k, v, iq, ik, iw = jnp.split(w_in, np.cumsum(widths)[:-1].tolist(), axis=1)
    pad = jnp.zeros((w_in.shape[0], IN_MISC - 2 * DSA_HEAD_DIM - IDX_DIM - IDX_HEADS), w_in.dtype)
    return jnp.concatenate([xr, yr, q, iq, k, v, ik, iw, pad], axis=1).astype(BF16)


def _block_diag_groups(w):
    bw = w.shape[-1]
    per = LANES // bw
    ng = w.shape[0] // per
    on_diag = jnp.eye(per, dtype=bool)[None, :, None, :, None]
    out = jnp.where(on_diag, w.reshape(ng, per, bw, 1, bw), 0.0)
    return out.reshape(ng, LANES, LANES).astype(BF16)


def _head_lanes(nope, rope):
    lead = nope.shape[:-1] if nope is not None else rope.shape[:-1]
    dt = nope.dtype if nope is not None else rope.dtype
    z = lambda n: jnp.zeros(lead + (n,), dt)
    r1, r2 = (rope[..., :ROPE_HALF], rope[..., ROPE_HALF:]) if rope is not None else (z(ROPE_HALF), z(ROPE_HALF))
    n1, n2 = (nope[..., :NOPE_LO], nope[..., NOPE_LO:]) if nope is not None else (z(NOPE_LO), z(MLA_NOPE - NOPE_LO))
    return jnp.concatenate([r1, n1, r2, n2, z(LANES - MLA_NOPE - MLA_ROPE)], axis=-1)


def _mla_weights(w_down, w_uq, w_ukv):
    lat = MLA_Q_LORA + MLA_KV_LORA
    wd = jnp.concatenate([w_down[:, :lat], _head_lanes(None, w_down[:, lat:])], axis=1)
    wq = w_uq.reshape(MLA_Q_LORA, MLA_HEADS, MLA_NOPE + MLA_ROPE)
    wq = _head_lanes(wq[:, :, :MLA_NOPE], wq[:, :, MLA_NOPE:]).reshape(MLA_Q_LORA, MLA_HEADS * LANES)
    wkv = w_ukv.reshape(MLA_KV_LORA, MLA_HEADS, MLA_NOPE + MLA_V)
    wk = _head_lanes(wkv[:, :, :MLA_NOPE], None).reshape(MLA_KV_LORA, MLA_HEADS * LANES)
    wv = wkv[:, :, MLA_NOPE:].reshape(MLA_KV_LORA, MLA_HEADS * MLA_V)
    return wd.astype(BF16), wq.astype(BF16), wk.astype(BF16), wv.astype(BF16)


def _rope_tables(s):
    pos = np.arange(s, dtype=np.float64)
    freq = ROPE_BASE ** (-np.arange(0, MLA_ROPE, 2, dtype=np.float64) / MLA_ROPE)
    ang = pos[:, None] * freq[None, :]
    lo = slice(0, ROPE_HALF)
    hi = slice(LANES // 2, LANES // 2 + ROPE_HALF)
    cos_t = np.ones((s, LANES))
    sin_t = np.zeros((s, LANES))
    cos_t[:, lo] = cos_t[:, hi] = np.cos(ang)
    sin_t[:, lo] = -np.sin(ang)
    sin_t[:, hi] = np.sin(ang)
    f32 = lambda a: jnp.asarray(a.astype(np.float32))
    return f32(cos_t * MLA_Q_SCALE), f32(sin_t * MLA_Q_SCALE), f32(cos_t), f32(sin_t)


def _tile_m(m):
    return 512 if m % 512 == 0 else m


def kernel(x, p, ln1_g, ln1_b, ln2_g, ln2_b, mlp_w1, mlp_w2, ple_w_proj, ple_w_gate, hy_w_in, hy_conv_w, hy_conv_b, hy_ga_w, hy_ga_b, hy_gx_w, hy_gx_b, hy_lambda, hy_w_out, mla_w_down, mla_q_norm, mla_kv_norm, mla_w_uq, mla_w_ukv, mla_w_out):
    bsz, s, d = x.shape
    m = bsz * s
    tm = _tile_m(m)
    tf = 1024
    row = lambda a: a.reshape(1, -1)
    x2 = x.reshape(m, d)

    rows3 = lambda a: a.reshape(a.shape[0], 1, -1)
    tail_params = (rows3(ln1_g), rows3(ln1_b), mlp_w1.astype(BF16), mlp_w2.astype(BF16),
                   ple_w_gate.astype(BF16), ple_w_proj.astype(BF16), rows3(ln2_g), rows3(ln2_b))
    p3 = p.reshape(p.shape[0], m, -1)

    def layer_tail(acts, wos, x2, i):
        return _tail(acts, wos, x2, p3, i, *tail_params, TAIL_ROWS if m % TAIL_ROWS == 0 else tm, tf)

    xy, q, iq, misc = _inproj(x2, _hy_in_weight(hy_w_in[0]), tm)
    rec = _rglru(xy.reshape(bsz, s, -1), hy_conv_w[0], row(hy_conv_b[0]),
                 _block_diag_groups(hy_ga_w[0]), row(hy_ga_b[0]),
                 _block_diag_groups(hy_gx_w[0]), row(hy_gx_b[0]), row(hy_lambda[0]))
    att = _dsa(q.reshape(bsz, s, -1), iq.reshape(bsz, s, -1), misc.reshape(bsz, s, -1))
    w_out = hy_w_out[0].astype(BF16)
    x2 = layer_tail([rec.reshape(m, -1), att.reshape(m, -1)], [w_out[:RNN_WIDTH], w_out[RNN_WIDTH:]], x2, 0)

    wd, wq, wk, wv = _mla_weights(mla_w_down[0], mla_w_uq[0], mla_w_ukv[0])
    qp, kp, vp = _mla_proj(x2, wd, row(mla_q_norm[0]), row(mla_kv_norm[0]), wq, wk, wv,
                           _rope_tables(s), min(tm, s), s)
    o = _mla_attn(qp.reshape(bsz, s, -1), kp.reshape(bsz, s, -1), vp.reshape(bsz, s, -1), min(256, s))
    x2 = layer_tail([o.reshape(m, -1)], [mla_w_out[0].astype(BF16)], x2, 1)
    return x2.reshape(bsz, s, d)
```

```python
import functools
import math

import numpy as np
import jax
import jax.numpy as jnp
from jax import lax
from jax.experimental import pallas as pl
from jax.experimental.pallas import tpu as pltpu

F32 = jnp.float32
BF16 = jnp.bfloat16

RNN_WIDTH = 512
CONV_WIDTH = 4
LRU_C = 8.0
DSA_HEADS = 8
DSA_HEAD_DIM = 64
IDX_HEADS = 4
IDX_DIM = 64
DSA_TOPK_MAX = 256
MLA_HEADS = 16
MLA_Q_LORA = 512
MLA_KV_LORA = 256
MLA_NOPE = 64
MLA_ROPE = 32
MLA_V = 64
ROPE_BASE = 10000.0
DEPTH = 2
DN_ALPHA = (2 * DEPTH) ** 0.25
LN_EPS = 1e-5
RMS_EPS = 1e-6

LANES = 128
SUBLANES = 8
VMEM_LIMIT_BYTES = 56 * 1024 * 1024

MASK_NEG = -3.0e38
INT32_MIN = -(2 ** 31)
STAGE_LAGS = (1, 2, 4)
DSA_Q_SCALE = DSA_HEAD_DIM ** -0.5
DSA_PAIR_MAX_EXTENT = 768
TAIL_ROWS = 1024
TAIL_SUB_ROWS = 256


def _cparams(sem):
    return pltpu.CompilerParams(dimension_semantics=sem, vmem_limit_bytes=VMEM_LIMIT_BYTES)


def _dot(a, b):
    return jnp.dot(a, b, preferred_element_type=F32)


def _dot_nt(a, b):
    return lax.dot_general(a, b, (((1,), (1,)), ((), ())), preferred_element_type=F32)


def _layer_norm(y, g, b):
    mu = jnp.mean(y, axis=-1, keepdims=True)
    yc = y - mu
    var = jnp.mean(yc * yc, axis=-1, keepdims=True)
    return yc * lax.rsqrt(var + LN_EPS) * g + b


IN_XY = 2 * RNN_WIDTH
IN_Q = DSA_HEADS * DSA_HEAD_DIM
IN_IQ = IDX_HEADS * IDX_DIM
IN_MISC = 2 * LANES
IN_OFFSETS = tuple(int(v) for v in np.cumsum([0, IN_XY, IN_Q, IN_IQ, IN_MISC]))


def _inproj_kernel(x_ref, w_ref, xy_ref, q_ref, iq_ref, misc_ref):
    xb = x_ref[...].astype(BF16)
    cols = [w_ref[:, a:b] for a, b in zip(IN_OFFSETS[:-1], IN_OFFSETS[1:])]
    xy_ref[...] = _dot(xb, cols[0])
    q_ref[...] = (_dot(xb, cols[1]) * DSA_Q_SCALE).astype(BF16)
    iq_ref[...] = _dot(xb, cols[2])
    misc_ref[...] = _dot(xb, cols[3])


def _inproj(x2, w, tm):
    m, d = x2.shape
    n = w.shape[1]
    widths = (IN_XY, IN_Q, IN_IQ, IN_MISC)
    dtypes = (F32, BF16, F32, F32)
    return pl.pallas_call(
        _inproj_kernel,
        grid=(m // tm,),
        in_specs=[pl.BlockSpec((tm, d), lambda i: (i, 0)),
                  pl.BlockSpec((d, n), lambda i: (0, 0))],
        out_specs=[pl.BlockSpec((tm, wd), lambda i: (i, 0)) for wd in widths],
        out_shape=[jax.ShapeDtypeStruct((m, wd), dt) for wd, dt in zip(widths, dtypes)],
        compiler_params=_cparams(("parallel",)),
        name="hy_inproj",
    )(x2, w)


def _rglru_kernel(xr_ref, yr_ref, cw_ref, cb_ref, wa_ref, ba_ref, wx_ref, bx_ref, lam_ref,
                  o_ref, xpad_ref):
    s = xr_ref.shape[1]
    c = xr_ref.shape[2]
    xpad_ref[0:SUBLANES, :] = jnp.zeros((SUBLANES, c), F32)
    xpad_ref[SUBLANES:SUBLANES + s, :] = xr_ref[0]
    xc = None
    for k in range(CONV_WIDTH):
        start = SUBLANES - (CONV_WIDTH - 1) + k
        term = xpad_ref[start:start + s, :] * cw_ref[k:k + 1, :]
        xc = term if xc is None else xc + term
    xc = xc + cb_ref[...]
    xb = xc.astype(BF16)
    r = jax.nn.sigmoid(_dot(xb, wa_ref[0]) + ba_ref[...])
    gi = jax.nn.sigmoid(_dot(xb, wx_ref[0]) + bx_ref[...])
    nl = -lam_ref[...]
    softplus = jnp.maximum(nl, 0.0) + jnp.log1p(jnp.exp(-jnp.abs(nl)))
    log_a = (-LRU_C) * r * softplus
    a = jnp.exp(log_a)
    y = -jnp.tanh(log_a) * (a * a + 1.0)
    h = jnp.where(y > 0.0, y * lax.rsqrt(y), 0.0) * (gi * xc)
    row = lax.broadcasted_iota(jnp.int32, (s, c), 0)
    sh = 1
    while sh < s:
        if sh < SUBLANES:
            valid = row >= sh
            h = a * jnp.where(valid, pltpu.roll(h, sh, axis=0), 0.0) + h
            a = a * jnp.where(valid, pltpu.roll(a, sh, axis=0), 1.0)
        else:
            h = jnp.concatenate([h[:sh], a[sh:] * h[:s - sh] + h[sh:]], axis=0)
            if sh * 2 < s:
                a = jnp.concatenate([a[:sh], a[sh:] * a[:s - sh]], axis=0)
        sh *= 2
    o_ref[0] = (h * jax.nn.gelu(yr_ref[0])).astype(o_ref.dtype)


def _rglru(xy, conv_w, conv_b, wa, ba, wx, bx, lam):
    bsz, s, _ = xy.shape
    c = LANES
    ng = RNN_WIDTH // c
    vec = lambda: pl.BlockSpec((1, c), lambda b, g: (0, g))
    return pl.pallas_call(
        _rglru_kernel,
        grid=(bsz, ng),
        in_specs=[pl.BlockSpec((1, s, c), lambda b, g: (b, 0, g)),
                  pl.BlockSpec((1, s, c), lambda b, g: (b, 0, ng + g)),
                  pl.BlockSpec((CONV_WIDTH, c), lambda b, g: (0, g)),
                  vec(),
                  pl.BlockSpec((1, c, c), lambda b, g: (g, 0, 0)),
                  vec(),
                  pl.BlockSpec((1, c, c), lambda b, g: (g, 0, 0)),
                  vec(),
                  vec()],
        out_specs=pl.BlockSpec((1, s, c), lambda b, g: (b, 0, g)),
        out_shape=jax.ShapeDtypeStruct((bsz, s, RNN_WIDTH), BF16),
        scratch_shapes=[pltpu.VMEM((s + SUBLANES, c), F32)],
        compiler_params=_cparams(("parallel", "parallel")),
        name="hy_rglru",
    )(xy, xy, conv_w, conv_b, wa, ba, wx, bx, lam)


def _ordered_int_to_float(c):
    return lax.bitcast_convert_type(c ^ (lax.shift_right_arithmetic(c, 31) & jnp.int32(0x7FFFFFFF)), F32)


def _reduce_keys(x, reduce_fn, combine_fn, groups=8):
    rows = x.shape[0]
    step = max(rows // groups, SUBLANES)
    parts = [reduce_fn(x[r:r + step, :], axis=0, keepdims=True) for r in range(0, rows, step)]
    while len(parts) > 1:
        parts = [combine_fn(parts[i], parts[i + 1]) if i + 1 < len(parts) else parts[i]
                 for i in range(0, len(parts), 2)]
    return parts[0]


def _dsa_prepare(q_ref, mk_ref):
    tq = q_ref.shape[1]
    ext = mk_ref.shape[1]
    d0 = ext - tq
    keep = (lax.broadcasted_iota(jnp.int32, (tq, tq), 1)
            <= lax.broadcasted_iota(jnp.int32, (tq, tq), 0))

    mk = mk_ref[0]
    vlane = lax.broadcasted_iota(jnp.int32, (ext, DSA_HEAD_DIM), 1)
    v_aug = jnp.concatenate([mk[:, DSA_HEAD_DIM:2 * DSA_HEAD_DIM],
                             jnp.where(vlane == 0, 1.0, 0.0)], axis=1).astype(BF16)
    ik_b = mk[:, 2 * DSA_HEAD_DIM:2 * DSA_HEAD_DIM + IDX_DIM].astype(BF16)
    kcol = lax.broadcasted_iota(jnp.int32, (ext, 1), 0)
    lane = lax.broadcasted_iota(jnp.int32, (1, LANES), 1)
    pos_hi = (kcol & jnp.int32(-LANES)).astype(F32)
    pos_lo = (kcol & jnp.int32(LANES - 1)).astype(F32)
    k_aug = jnp.where(lane < DSA_HEAD_DIM, mk[:, 0:LANES],
                      jnp.where(lane == DSA_HEAD_DIM, pos_hi,
                                jnp.where(lane == DSA_HEAD_DIM + 1, pos_lo, 0.0))).astype(BF16)
    return d0, keep, ik_b, k_aug, v_aug


def _dsa_select_steps(rows, iq_ref, mq_ref, sc_ref, bias_ref, ik_b, keep, *, topk):
    tq, ext = sc_ref.shape
    nr = rows.stop - rows.start
    d0 = ext - tq
    kf = float(topk)
    st = {}

    def scores():
        iw0 = 2 * DSA_HEAD_DIM + IDX_DIM
        iw = mq_ref[0, rows, iw0:iw0 + IDX_HEADS] * (IDX_HEADS ** -0.5 * IDX_DIM ** -0.5)
        iq = iq_ref[0, rows, :].astype(BF16)
        sc = None
        for h in range(IDX_HEADS):
            d = _dot_nt(iq[:, h * IDX_DIM:(h + 1) * IDX_DIM], ik_b)
            term = jnp.maximum(d, 0.0) * iw[:, h:h + 1]
            sc = term if sc is None else sc + term
        if d0 > 0:
            sc_ref[rows, 0:d0] = sc[:, 0:d0]
        sc_ref[rows, d0:ext] = jnp.where(keep[rows, :], sc[:, d0:ext], MASK_NEG)
        st["tau"] = jnp.full((nr, 1), INT32_MIN, jnp.int32)

    def search(i):
        cand = st["tau"] + jnp.int32(1 << (31 - i) if i else INT32_MIN)
        ge = sc_ref[rows, :] >= _ordered_int_to_float(cand)
        cnt = jnp.sum(jnp.where(ge, 1.0, 0.0), axis=-1, keepdims=True)
        st["tau"] = jnp.where(cnt >= kf, cand, st["tau"])

    def ties():
        tau = _ordered_int_to_float(st.pop("tau"))
        sc = sc_ref[rows, :]
        gt = sc > tau
        eq = sc == tau
        need = kf - jnp.sum(jnp.where(gt, 1.0, 0.0), axis=-1, keepdims=True)
        eqf = jnp.where(eq, 1.0, 0.0).astype(BF16)
        tri = (lax.broadcasted_iota(jnp.int32, (LANES, LANES), 0)
               <= lax.broadcasted_iota(jnp.int32, (LANES, LANES), 1)).astype(BF16)
        off = jnp.zeros((nr, 1), F32)
        for c in range(ext // LANES):
            sl = slice(c * LANES, (c + 1) * LANES)
            pc = _dot(eqf[:, sl], tri)
            take = gt[:, sl] | (eq[:, sl] & ((pc + off) <= need))
            if c * LANES >= d0:
                take = take & keep[rows, c * LANES - d0:(c + 1) * LANES - d0]
            bias_ref[rows, sl] = jnp.where(take, 0.0, MASK_NEG)
            off = off + pc[:, LANES - 1:LANES]

    return [scores] + [functools.partial(search, i) for i in range(32)] + [ties]


def _dsa_attend_steps(rows, q_ref, o_ref, bias_ref, k_aug, v_aug):
    nr = rows.stop - rows.start
    tail_lane = lax.broadcasted_iota(jnp.int32, (nr, LANES - DSA_HEAD_DIM), 1)

    def logits(h):
        slope = 2.0 ** (-8.0 * (h + 1) / DSA_HEADS)
        tail = jnp.where(tail_lane < 2, slope, 0.0).astype(BF16)
        qh = q_ref[0, rows, h * DSA_HEAD_DIM:(h + 1) * DSA_HEAD_DIM]
        return _dot_nt(jnp.concatenate([qh, tail], axis=1), k_aug) + bias_ref[rows, :]

    def values(h, p):
        o = _dot(p, v_aug)
        o = o[:, 0:DSA_HEAD_DIM] / o[:, DSA_HEAD_DIM:DSA_HEAD_DIM + 1]
        o_ref[0, rows, h * DSA_HEAD_DIM:(h + 1) * DSA_HEAD_DIM] = o.astype(o_ref.dtype)

    lg, mx, pr = {}, {}, {}
    d_max, d_exp, d_pv = STAGE_LAGS

    def step(t):
        if t < DSA_HEADS:
            lg[t] = logits(t)
        if 0 <= t - d_max < DSA_HEADS:
            mx[t - d_max] = jnp.max(lg[t - d_max], axis=-1, keepdims=True)
        if 0 <= t - d_exp < DSA_HEADS:
            h = t - d_exp
            pr[h] = jnp.exp(lg.pop(h) - mx.pop(h)).astype(BF16)
        if 0 <= t - d_pv < DSA_HEADS:
            values(t - d_pv, pr.pop(t - d_pv))

    return [functools.partial(step, t) for t in range(DSA_HEADS + d_pv)]


def _dsa_kernel(q_ref, iq_ref, mq_ref, mk_ref, _aliased_out, o_ref, sc_ref, bias_ref, *, topk):
    tq = q_ref.shape[1]
    rows = slice(0, tq)
    select, attend = [], []
    for bi in range(q_ref.shape[0]):
        one = lambda ref: ref.at[bi:bi + 1]
        sc_b, bias_b = sc_ref.at[bi], bias_ref.at[bi]
        d0, keep, ik_b, k_aug, v_aug = _dsa_prepare(one(q_ref), one(mk_ref))
        if mk_ref.shape[1] > topk:
            select.append(_dsa_select_steps(rows, one(iq_ref), one(mq_ref), sc_b, bias_b, ik_b, keep,
                                            topk=topk))
        else:
            if d0 > 0:
                bias_b[:, 0:d0] = jnp.zeros((tq, d0), F32)
            bias_b[:, d0:] = jnp.where(keep, 0.0, MASK_NEG)
        attend.append(_dsa_attend_steps(rows, one(q_ref), one(o_ref), bias_b, k_aug, v_aug))
    for steps in list(zip(*select)) + list(zip(*attend)):
        for step in steps:
            step()


def _dsa(q, iq, misc):
    bsz, s, _ = q.shape
    topk = min(DSA_TOPK_MAX, s // 4)
    tq = min(256, s)
    hd = DSA_HEADS * DSA_HEAD_DIM
    att = jnp.zeros((bsz, s, hd), BF16)
    for c in range(s // tq):
        ext = (c + 1) * tq
        bb = 2 if (ext <= DSA_PAIR_MAX_EXTENT and bsz % 2 == 0) else 1
        att = pl.pallas_call(
            functools.partial(_dsa_kernel, topk=topk),
            grid=(bsz // bb,),
            in_specs=[pl.BlockSpec((bb, tq, hd), lambda b, c=c: (b, c, 0)),
                      pl.BlockSpec((bb, tq, IDX_HEADS * IDX_DIM), lambda b, c=c: (b, c, 0)),
                      pl.BlockSpec((bb, tq, IN_MISC), lambda b, c=c: (b, c, 0)),
                      pl.BlockSpec((bb, ext, IN_MISC), lambda b: (b, 0, 0)),
                      pl.BlockSpec(memory_space=pl.ANY)],
            out_specs=pl.BlockSpec((bb, tq, hd), lambda b, c=c: (b, c, 0)),
            out_shape=jax.ShapeDtypeStruct((bsz, s, hd), BF16),
            scratch_shapes=[pltpu.VMEM((bb, tq, ext), F32), pltpu.VMEM((bb, tq, ext), F32)],
            input_output_aliases={4: 0},
            compiler_params=_cparams(("parallel",)),
            name=f"hy_dsa_c{c}",
        )(q, iq, misc, misc, att)
    return att


def _tail_kernel(*refs, n_in, tf):
    a_refs = refs[:n_in]
    wo_refs = refs[n_in:2 * n_in]
    (x_ref, p_ref, g1_ref, b1_ref, w1_ref, w2_ref, wg_ref, wp_ref, g2_ref, b2_ref,
     o_ref, acc_ref) = refs[2 * n_in:]
    n_chunks = w1_ref.shape[1] // tf

    def steps_for(rows):
        st = {}

        def out_proj():
            m = None
            for a_ref, w_ref in zip(a_refs, wo_refs):
                t = _dot(a_ref[rows, :].astype(BF16), w_ref[...])
                m = t if m is None else m + t
            st["y"] = DN_ALPHA * x_ref[rows, :] + m

        def norm1():
            st["h"] = _layer_norm(st.pop("y"), g1_ref[...], b1_ref[...])
            st["hb"] = st["h"].astype(BF16)

        def gate():
            g = jax.nn.sigmoid(_dot(st["hb"], wg_ref[...]))
            acc_ref[rows, :] = DN_ALPHA * st.pop("h") + g * _dot(p_ref[rows, :].astype(BF16), wp_ref[...])

        def mlp(c):
            a = jnp.maximum(_dot(st["hb"], w1_ref[:, c * tf:(c + 1) * tf]), 0.0)
            acc_ref[rows, :] += _dot((a * a).astype(BF16), w2_ref[c * tf:(c + 1) * tf, :])

        def norm2():
            o_ref[rows, :] = _layer_norm(acc_ref[rows, :], g2_ref[...], b2_ref[...])

        return [out_proj, norm1, gate] + [functools.partial(mlp, c) for c in range(n_chunks)] + [norm2]

    n_sub = x_ref.shape[0] // TAIL_SUB_ROWS
    subs = [steps_for(slice(i * TAIL_SUB_ROWS, (i + 1) * TAIL_SUB_ROWS)) for i in range(n_sub)]
    lag = 1
    for t in range(len(subs[0]) + lag * (n_sub - 1)):
        for i, steps in enumerate(subs):
            if 0 <= t - lag * i < len(steps):
                steps[t - lag * i]()


def _tail(acts, wos, x2, p3, layer, g1, b1, w1, w2, wg, wp, g2, b2, tm, tf):
    m, d = x2.shape
    n_in = len(acts)
    rows = lambda a: pl.BlockSpec((tm, a.shape[1]), lambda i: (i, 0))
    resident = lambda a: pl.BlockSpec(a.shape, lambda i: (0, 0), pipeline_mode=pl.Buffered(1))
    of_layer = lambda a: pl.BlockSpec((None,) + a.shape[1:], lambda i: (layer, 0, 0),
                                      pipeline_mode=pl.Buffered(1))
    params = [g1, b1, w1, w2, wg, wp, g2, b2]
    return pl.pallas_call(
        functools.partial(_tail_kernel, n_in=n_in, tf=tf),
        grid=(m // tm,),
        in_specs=([rows(a) for a in acts] + [resident(w) for w in wos]
                  + [rows(x2), pl.BlockSpec((None, tm, p3.shape[2]), lambda i: (layer, i, 0))]
                  + [of_layer(a) for a in params]),
        out_specs=pl.BlockSpec((tm, d), lambda i: (i, 0)),
        out_shape=jax.ShapeDtypeStruct((m, d), F32),
        scratch_shapes=[pltpu.VMEM((tm, d), F32)],
        compiler_params=_cparams(("parallel",)),
        name="layer_tail",
    )(*acts, *wos, x2, p3, *params)


MLA_Q_SCALE = (MLA_NOPE + MLA_ROPE) ** -0.5 * math.log2(math.e)
ROPE_HALF = MLA_ROPE // 2
NOPE_LO = LANES // 2 - ROPE_HALF


def _rope_group(t, cos_t, sin_t):
    return t * cos_t + pltpu.roll(t, LANES // 2, axis=1) * sin_t


def _mla_proj_kernel(x_ref, wd_ref, qg_ref, kvg_ref, wq_ref, wk_ref, wv_ref,
                     cosq_ref, sinq_ref, cosk_ref, sink_ref, q_ref, k_ref, v_ref):
    xb = x_ref[...].astype(BF16)
    down = _dot(xb, wd_ref[...])
    cq = down[:, 0:MLA_Q_LORA]
    ckv = down[:, MLA_Q_LORA:MLA_Q_LORA + MLA_KV_LORA]
    krg = down[:, MLA_Q_LORA + MLA_KV_LORA:]
    cqn = (cq * lax.rsqrt(jnp.mean(cq * cq, axis=-1, keepdims=True) + RMS_EPS) * qg_ref[...]).astype(BF16)
    ckvn = (ckv * lax.rsqrt(jnp.mean(ckv * ckv, axis=-1, keepdims=True) + RMS_EPS) * kvg_ref[...]).astype(BF16)
    cos_q, sin_q = cosq_ref[...], sinq_ref[...]
    kr = _rope_group(krg, cosk_ref[...], sink_ref[...])
    q = _dot(cqn, wq_ref[...])
    k = _dot(ckvn, wk_ref[...])
    for h in range(MLA_HEADS):
        sl = slice(h * LANES, (h + 1) * LANES)
        q_ref[:, sl] = _rope_group(q[:, sl], cos_q, sin_q).astype(BF16)
        k_ref[:, sl] = (k[:, sl] + kr).astype(BF16)
    v_ref[...] = _dot(ckvn, wv_ref[...]).astype(BF16)


def _mla_proj(x2, wd, qg, kvg, wq, wk, wv, tables, tm, s):
    m, d = x2.shape
    nt = s // tm
    full = lambda a: pl.BlockSpec(a.shape, lambda i: (0, 0))
    tab = lambda: pl.BlockSpec((tm, LANES), lambda i: (i % nt, 0))
    hq = MLA_HEADS * LANES
    hv = MLA_HEADS * MLA_V
    return pl.pallas_call(
        _mla_proj_kernel,
        grid=(m // tm,),
        in_specs=[pl.BlockSpec((tm, d), lambda i: (i, 0)), full(wd), full(qg), full(kvg),
                  full(wq), full(wk), full(wv)] + [tab() for _ in tables],
        out_specs=[pl.BlockSpec((tm, hq), lambda i: (i, 0)),
                   pl.BlockSpec((tm, hq), lambda i: (i, 0)),
                   pl.BlockSpec((tm, hv), lambda i: (i, 0))],
        out_shape=[jax.ShapeDtypeStruct((m, hq), BF16),
                   jax.ShapeDtypeStruct((m, hq), BF16),
                   jax.ShapeDtypeStruct((m, hv), BF16)],
        compiler_params=_cparams(("parallel",)),
        name="mla_proj",
    )(x2, wd, qg, kvg, wq, wk, wv, *tables)


def _mla_attn_kernel(q_ref, k_ref, v_ref, o_ref, ot_ref, *, tq, heads_per_step, key_chunk):
    s = q_ref.shape[1]
    keep_t = (lax.broadcasted_iota(jnp.int32, (tq, tq), 0)
              <= lax.broadcasted_iota(jnp.int32, (tq, tq), 1))
    v_t = jnp.transpose(v_ref[0].astype(F32)).astype(BF16)
    ones_rows = jnp.ones((2 * SUBLANES, s), BF16)
    lhs = [jnp.concatenate([v_t[hh * MLA_V:(hh + 1) * MLA_V, :], ones_rows], axis=0)
           for hh in range(heads_per_step)]
    units = []
    for j in range(s // tq):
        ext = (j + 1) * tq
        starts = list(range(0, ext - tq, key_chunk)) or [0]
        bounds = [(a, b) for a, b in zip(starts, starts[1:] + [ext])]
        for hh in range(heads_per_step):
            units += [(j * tq, hh, k0, k1) for k0, k1 in bounds]

    def logits(r0, hh, k0, k1):
        ql = slice(hh * LANES, (hh + 1) * LANES)
        sc = _dot_nt(k_ref[0, k0:k1, ql], q_ref[0, r0:r0 + tq, ql])
        if k1 == r0 + tq:
            diag = jnp.where(keep_t, sc[k1 - k0 - tq:, :], MASK_NEG)
            sc = diag if k1 - k0 == tq else jnp.concatenate([sc[:k1 - k0 - tq, :], diag], axis=0)
        return sc

    def probs(sc):
        m = _reduce_keys(sc, jnp.max, jnp.maximum)
        return jnp.exp2(sc - m).astype(BF16), m

    running = {}

    def values(r0, hh, k0, k1, p, m):
        acc = _dot(lhs[hh][:, k0:k1], p)
        if k0 > 0:
            m_run, acc_run = running.pop((r0, hh))
            m_new = jnp.maximum(m_run, m)
            acc = acc_run * jnp.exp2(m_run - m_new) + acc * jnp.exp2(m - m_new)
            m = m_new
        if k1 == r0 + tq:
            ot_ref[hh * MLA_V:(hh + 1) * MLA_V, r0:r0 + tq] = acc[0:MLA_V, :] / acc[MLA_V:MLA_V + 1, :]
        else:
            running[(r0, hh)] = (m, acc)

    n = len(units)
    stage_s, stage_m, stage_p = {}, {}, {}
    d_max, d_exp, d_pv = STAGE_LAGS
    for t in range(n + d_pv):
        if t < n:
            stage_s[t] = logits(*units[t])
        if 0 <= t - d_max < n:
            u = t - d_max
            stage_m[u] = _reduce_keys(stage_s[u], jnp.max, jnp.maximum)
        if 0 <= t - d_exp < n:
            u = t - d_exp
            stage_p[u] = jnp.exp2(stage_s.pop(u) - stage_m[u]).astype(BF16)
        if 0 <= t - d_pv < n:
            u = t - d_pv
            values(*units[u], stage_p.pop(u), stage_m.pop(u))
    o_ref[0] = jnp.transpose(ot_ref[...]).astype(o_ref.dtype)


def _mla_attn(q, k, v, tq):
    bsz, s, _ = q.shape
    hps = 4
    return pl.pallas_call(
        functools.partial(_mla_attn_kernel, tq=tq, heads_per_step=hps, key_chunk=2 * tq),
        grid=(bsz, MLA_HEADS // hps),
        in_specs=[pl.BlockSpec((1, s, hps * LANES), lambda b, h: (b, 0, h)),
                  pl.BlockSpec((1, s, hps * LANES), lambda b, h: (b, 0, h)),
                  pl.BlockSpec((1, s, hps * MLA_V), lambda b, h: (b, 0, h))],
        out_specs=pl.BlockSpec((1, s, hps * MLA_V), lambda b, h: (b, 0, h)),
        out_shape=jax.ShapeDtypeStruct((bsz, s, MLA_HEADS * MLA_V), BF16),
        scratch_shapes=[pltpu.VMEM((hps * MLA_V, s), F32)],
        compiler_params=_cparams(("parallel", "parallel")),
        name="mla_attn",
    )(q, k, v)


def _hy_in_weight(w_in):
    widths = [RNN_WIDTH, RNN_WIDTH, IN_Q, DSA_HEAD_DIM, DSA_HEAD_DIM, IN_IQ, IDX_DIM, IDX_HEADS]
    xr, yr, q, k, v, iq, ik, iw = jnp.split(w_in, np.cumsum(widths)[:-1].tolist(), axis=1)
    pad = jnp.zeros((w_in.shape[0], IN_MISC - 2 * DSA_HEAD_DIM - IDX_DIM - IDX_HEADS), w_in.dtype)
    return jnp.concatenate([xr, yr, q, iq, k, v, ik, iw, pad], axis=1).astype(BF16)


def _block_diag_groups(w):
    bw = w.shape[-1]
    per = LANES // bw
    ng = w.shape[0] // per
    on_diag = jnp.eye(per, dtype=bool)[None, :, None, :, None]
    out = jnp.where(on_diag, w.reshape(ng, per, bw, 1, bw), 0.0)
    return out.reshape(ng, LANES, LANES).astype(BF16)


def _head_lanes(nope, rope):
    lead = nope.shape[:-1] if nope is not None else rope.shape[:-1]
    dt = nope.dtype if nope is not None else rope.dtype
    z = lambda n: jnp.zeros(lead + (n,), dt)
    r1, r2 = (rope[..., :ROPE_HALF], rope[..., ROPE_HALF:]) if rope is not None else (z(ROPE_HALF), z(ROPE_HALF))
    n1, n2 = (nope[..., :NOPE_LO], nope[..., NOPE_LO:]) if nope is not None else (z(NOPE_LO), z(MLA_NOPE - NOPE_LO))
    return jnp.concatenate([r1, n1, r2, n2, z(LANES - MLA_NOPE - MLA_ROPE)], axis=-1)


def _mla_weights(w_down, w_uq, w_ukv):
    lat = MLA_Q_LORA + MLA_KV_LORA
    wd = jnp.concatenate([w_down[:, :lat], _head_lanes(None, w_down[:, lat:])], axis=1)
    wq = w_uq.reshape(MLA_Q_LORA, MLA_HEADS, MLA_NOPE + MLA_ROPE)
    wq = _head_lanes(wq[:, :, :MLA_NOPE], wq[:, :, MLA_NOPE:]).reshape(MLA_Q_LORA, MLA_HEADS * LANES)
    wkv = w_ukv.reshape(MLA_KV_LORA, MLA_HEADS, MLA_NOPE + MLA_V)
    wk = _head_lanes(wkv[:, :, :MLA_NOPE], None).reshape(MLA_KV_LORA, MLA_HEADS * LANES)
    wv = wkv[:, :, MLA_NOPE:].reshape(MLA_KV_LORA, MLA_HEADS * MLA_V)
    return wd.astype(BF16), wq.astype(BF16), wk.astype(BF16), wv.astype(BF16)


def _rope_tables(s):
    pos = np.arange(s, dtype=np.float64)
    freq = ROPE_BASE ** (-np.arange(0, MLA_ROPE, 2, dtype=np.float64) / MLA_ROPE)
    ang = pos[:, None] * freq[None, :]
    lo = slice(0, ROPE_HALF)
    hi = slice(LANES // 2, LANES // 2 + ROPE_HALF)
    cos_t = np.ones((s, LANES))
    sin_t = np.zeros((s, LANES))
    cos_t[:, lo] = cos_t[:, hi] = np.cos(ang)
    sin_t[:, lo] = -np.sin(ang)
    sin_t[:, hi] = np.sin(ang)
    f32 = lambda a: jnp.asarray(a.astype(np.float32))
    return f32(cos_t * MLA_Q_SCALE), f32(sin_t * MLA_Q_SCALE), f32(cos_t), f32(sin_t)


def _tile_m(m):
    return 512 if m % 512 == 0 else m


def kernel(x, p, ln1_g, ln1_b, ln2_g, ln2_b, mlp_w1, mlp_w2, ple_w_proj, ple_w_gate, hy_w_in, hy_conv_w, hy_conv_b, hy_ga_w, hy_ga_b, hy_gx_w, hy_gx_b, hy_lambda, hy_w_out, mla_w_down, mla_q_norm, mla_kv_norm, mla_w_uq, mla_w_ukv, mla_w_out):
    bsz, s, d = x.shape
    m = bsz * s
    tm = _tile_m(m)
    tf = 1024
    row = lambda a: a.reshape(1, -1)
    x2 = x.reshape(m, d)

    rows3 = lambda a: a.reshape(a.shape[0], 1, -1)
    tail_params = (rows3(ln1_g), rows3(ln1_b), mlp_w1.astype(BF16), mlp_w2.astype(BF16),
                   ple_w_gate.astype(BF16), ple_w_proj.astype(BF16), rows3(ln2_g), rows3(ln2_b))
    p3 = p.reshape(p.shape[0], m, -1)

    def layer_tail(acts, wos, x2, i):
        return _tail(acts, wos, x2, p3, i, *tail_params, TAIL_ROWS if m % TAIL_ROWS == 0 else tm, tf)

    xy, q, iq, misc = _inproj(x2, _hy_in_weight(hy_w_in[0]), tm)
    rec = _rglru(xy.reshape(bsz, s, -1), hy_conv_w[0], row(hy_conv_b[0]),
                 _block_diag_groups(hy_ga_w[0]), row(hy_ga_b[0]),
                 _block_diag_groups(hy_gx_w[0]), row(hy_gx_b[0]), row(hy_lambda[0]))
    att = _dsa(q.reshape(bsz, s, -1), iq.reshape(bsz, s, -1), misc.reshape(bsz, s, -1))
    w_out = hy_w_out[0].astype(BF16)
    x2 = layer_tail([rec.reshape(m, -1), att.reshape(m, -1)], [w_out[:RNN_WIDTH], w_out[RNN_WIDTH:]], x2, 0)

    wd, wq, wk, wv = _mla_weights(mla_w_down[0], mla_w_uq[0], mla_w_ukv[0])
    qp, kp, vp = _mla_proj(x2, wd, row(mla_q_norm[0]), row(mla_kv_norm[0]), wq, wk, wv,
                           _rope_tables(s), min(tm, s), s)
    o = _mla_attn(qp.reshape(bsz, s, -1), kp.reshape(bsz, s, -1), vp.reshape(bsz, s, -1), min(256, s))
    x2 = layer_tail([o.reshape(m, -1)], [mla_w_out[0].astype(BF16)], x2, 1)
    return x2.reshape(bsz, s, d)
```

```python
import functools
import math

import numpy as np
import jax
import jax.numpy as jnp
from jax import lax
from jax.experimental import pallas as pl
from jax.experimental.pallas import tpu as pltpu

F32 = jnp.float32
BF16 = jnp.bfloat16

RNN_WIDTH = 512
CONV_WIDTH = 4
LRU_C = 8.0
DSA_HEADS = 8
DSA_HEAD_DIM = 64
IDX_HEADS = 4
IDX_DIM = 64
DSA_TOPK_MAX = 256
MLA_HEADS = 16
MLA_Q_LORA = 512
MLA_KV_LORA = 256
MLA_NOPE = 64
MLA_ROPE = 32
MLA_V = 64
ROPE_BASE = 10000.0
DEPTH = 2
DN_ALPHA = (2 * DEPTH) ** 0.25
LN_EPS = 1e-5
RMS_EPS = 1e-6

LANES = 128
SUBLANES = 8
VMEM_LIMIT_BYTES = 56 * 1024 * 1024

MASK_NEG = -3.0e38
INT32_MIN = -(2 ** 31)
STAGE_LAGS = (1, 2, 4)
DSA_Q_SCALE = DSA_HEAD_DIM ** -0.5
DSA_PAIR_MAX_EXTENT = 768
TAIL_ROWS = 1024
TAIL_SUB_ROWS = 256


def _cparams(sem):
    return pltpu.CompilerParams(dimension_semantics=sem, vmem_limit_bytes=VMEM_LIMIT_BYTES)


def _dot(a, b):
    return jnp.dot(a, b, preferred_element_type=F32)


def _dot_nt(a, b):
    return lax.dot_general(a, b, (((1,), (1,)), ((), ())), preferred_element_type=F32)


def _layer_norm(y, g, b):
    mu = jnp.mean(y, axis=-1, keepdims=True)
    yc = y - mu
    var = jnp.mean(yc * yc, axis=-1, keepdims=True)
    return yc * lax.rsqrt(var + LN_EPS) * g + b


IN_XY = 2 * RNN_WIDTH
IN_Q = DSA_HEADS * DSA_HEAD_DIM
IN_IQ = IDX_HEADS * IDX_DIM
IN_MISC = 2 * LANES
IN_OFFSETS = tuple(int(v) for v in np.cumsum([0, IN_XY, IN_Q, IN_IQ, IN_MISC]))


KEYSIDE = 3 * LANES


def _inproj_kernel(x_ref, w_ref, xy_ref, q_ref, iq_ref, misc_ref, ks_ref, *, tiles_per_seq):
    tm = x_ref.shape[0]
    xb = x_ref[...].astype(BF16)
    cols = [w_ref[:, a:b] for a, b in zip(IN_OFFSETS[:-1], IN_OFFSETS[1:])]
    xy_ref[...] = _dot(xb, cols[0])
    q_ref[...] = (_dot(xb, cols[1]) * DSA_Q_SCALE).astype(BF16)
    iq_ref[...] = _dot(xb, cols[2])
    misc = _dot(xb, cols[3])
    misc_ref[...] = misc
    pos = (pl.program_id(0) % tiles_per_seq) * tm + lax.broadcasted_iota(jnp.int32, (tm, 1), 0)
    lane = lax.broadcasted_iota(jnp.int32, (1, LANES), 1)
    pos_hi = (pos & jnp.int32(-LANES)).astype(F32)
    pos_lo = (pos & jnp.int32(LANES - 1)).astype(F32)
    ks_ref[:, 0:LANES] = jnp.where(lane < DSA_HEAD_DIM, misc[:, 0:LANES],
                                   jnp.where(lane == DSA_HEAD_DIM, pos_hi,
                                             jnp.where(lane == DSA_HEAD_DIM + 1, pos_lo, 0.0))).astype(BF16)
    half_lane = lax.broadcasted_iota(jnp.int32, (tm, DSA_HEAD_DIM), 1)
    ks_ref[:, LANES:2 * LANES] = jnp.concatenate(
        [misc[:, DSA_HEAD_DIM:2 * DSA_HEAD_DIM], jnp.where(half_lane == 0, 1.0, 0.0)], axis=1).astype(BF16)
    ks_ref[:, 2 * LANES:3 * LANES] = jnp.concatenate(
        [misc[:, 2 * DSA_HEAD_DIM:2 * DSA_HEAD_DIM + IDX_DIM], jnp.zeros((tm, LANES - IDX_DIM), F32)],
        axis=1).astype(BF16)


def _inproj(x2, w, tm, s):
    m, d = x2.shape
    n = w.shape[1]
    widths = (IN_XY, IN_Q, IN_IQ, IN_MISC, KEYSIDE)
    dtypes = (F32, BF16, F32, F32, BF16)
    return pl.pallas_call(
        functools.partial(_inproj_kernel, tiles_per_seq=s // tm),
        grid=(m // tm,),
        in_specs=[pl.BlockSpec((tm, d), lambda i: (i, 0)),
                  pl.BlockSpec((d, n), lambda i: (0, 0))],
        out_specs=[pl.BlockSpec((tm, wd), lambda i: (i, 0)) for wd in widths],
        out_shape=[jax.ShapeDtypeStruct((m, wd), dt) for wd, dt in zip(widths, dtypes)],
        compiler_params=_cparams(("parallel",)),
        name="hy_inproj",
    )(x2, w)


def _rglru_kernel(xr_ref, yr_ref, cw_ref, cb_ref, wa_ref, ba_ref, wx_ref, bx_ref, lam_ref,
                  o_ref, xpad_ref):
    s = xr_ref.shape[1]
    c = xr_ref.shape[2]
    xpad_ref[0:SUBLANES, :] = jnp.zeros((SUBLANES, c), F32)
    xpad_ref[SUBLANES:SUBLANES + s, :] = xr_ref[0]
    xc = None
    for k in range(CONV_WIDTH):
        start = SUBLANES - (CONV_WIDTH - 1) + k
        term = xpad_ref[start:start + s, :] * cw_ref[k:k + 1, :]
        xc = term if xc is None else xc + term
    xc = xc + cb_ref[...]
    xb = xc.astype(BF16)
    r = jax.nn.sigmoid(_dot(xb, wa_ref[0]) + ba_ref[...])
    gi = jax.nn.sigmoid(_dot(xb, wx_ref[0]) + bx_ref[...])
    nl = -lam_ref[...]
    softplus = jnp.maximum(nl, 0.0) + jnp.log1p(jnp.exp(-jnp.abs(nl)))
    log_a = (-LRU_C) * r * softplus
    a = jnp.exp(log_a)
    y = -jnp.tanh(log_a) * (a * a + 1.0)
    h = jnp.where(y > 0.0, y * lax.rsqrt(y), 0.0) * (gi * xc)
    row = lax.broadcasted_iota(jnp.int32, (s, c), 0)
    sh = 1
    while sh < s:
        if sh < SUBLANES:
            valid = row >= sh
            h = a * jnp.where(valid, pltpu.roll(h, sh, axis=0), 0.0) + h
            a = a * jnp.where(valid, pltpu.roll(a, sh, axis=0), 1.0)
        else:
            h = jnp.concatenate([h[:sh], a[sh:] * h[:s - sh] + h[sh:]], axis=0)
            if sh * 2 < s:
                a = jnp.concatenate([a[:sh], a[sh:] * a[:s - sh]], axis=0)
        sh *= 2
    o_ref[0] = (h * jax.nn.gelu(yr_ref[0])).astype(o_ref.dtype)


def _rglru(xy, conv_w, conv_b, wa, ba, wx, bx, lam):
    bsz, s, _ = xy.shape
    c = LANES
    ng = RNN_WIDTH // c
    vec = lambda: pl.BlockSpec((1, c), lambda b, g: (0, g))
    return pl.pallas_call(
        _rglru_kernel,
        grid=(bsz, ng),
        in_specs=[pl.BlockSpec((1, s, c), lambda b, g: (b, 0, g)),
                  pl.BlockSpec((1, s, c), lambda b, g: (b, 0, ng + g)),
                  pl.BlockSpec((CONV_WIDTH, c), lambda b, g: (0, g)),
                  vec(),
                  pl.BlockSpec((1, c, c), lambda b, g: (g, 0, 0)),
                  vec(),
                  pl.BlockSpec((1, c, c), lambda b, g: (g, 0, 0)),
                  vec(),
                  vec()],
        out_specs=pl.BlockSpec((1, s, c), lambda b, g: (b, 0, g)),
        out_shape=jax.ShapeDtypeStruct((bsz, s, RNN_WIDTH), BF16),
        scratch_shapes=[pltpu.VMEM((s + SUBLANES, c), F32)],
        compiler_params=_cparams(("parallel", "parallel")),
        name="hy_rglru",
    )(xy, xy, conv_w, conv_b, wa, ba, wx, bx, lam)


def _ordered_int_to_float(c):
    return lax.bitcast_convert_type(c ^ (lax.shift_right_arithmetic(c, 31) & jnp.int32(0x7FFFFFFF)), F32)


def _reduce_keys(x, reduce_fn, combine_fn, groups=8):
    rows = x.shape[0]
    step = max(rows // groups, SUBLANES)
    parts = [reduce_fn(x[r:r + step, :], axis=0, keepdims=True) for r in range(0, rows, step)]
    while len(parts) > 1:
        parts = [combine_fn(parts[i], parts[i + 1]) if i + 1 < len(parts) else parts[i]
                 for i in range(0, len(parts), 2)]
    return parts[0]


def _dsa_prepare(q_ref, mk_ref):
    tq = q_ref.shape[1]
    ext = mk_ref.shape[1]
    d0 = ext - tq
    keep = (lax.broadcasted_iota(jnp.int32, (tq, tq), 1)
            <= lax.broadcasted_iota(jnp.int32, (tq, tq), 0))

    k_aug = mk_ref[0, :, 0:LANES]
    v_aug = mk_ref[0, :, LANES:2 * LANES]
    ik_b = mk_ref[0, :, 2 * LANES:2 * LANES + IDX_DIM]
    return d0, keep, ik_b, k_aug, v_aug


def _dsa_select_steps(rows, iq_ref, mq_ref, sc_ref, bias_ref, ik_b, keep, *, topk):
    tq, ext = sc_ref.shape
    nr = rows.stop - rows.start
    d0 = ext - tq
    kf = float(topk)
    st = {}

    def scores():
        iw0 = 2 * DSA_HEAD_DIM + IDX_DIM
        iw = mq_ref[0, rows, iw0:iw0 + IDX_HEADS] * (IDX_HEADS ** -0.5 * IDX_DIM ** -0.5)
        iq = iq_ref[0, rows, :].astype(BF16)
        sc = None
        for h in range(IDX_HEADS):
            d = _dot_nt(iq[:, h * IDX_DIM:(h + 1) * IDX_DIM], ik_b)
            term = jnp.maximum(d, 0.0) * iw[:, h:h + 1]
            sc = term if sc is None else sc + term
        if d0 > 0:
            sc_ref[rows, 0:d0] = sc[:, 0:d0]
        sc_ref[rows, d0:ext] = jnp.where(keep[rows, :], sc[:, d0:ext], MASK_NEG)
        st["tau"] = jnp.full((nr, 1), INT32_MIN, jnp.int32)

    def search(i):
        cand = st["tau"] + jnp.int32(1 << (31 - i) if i else INT32_MIN)
        ge = sc_ref[rows, :] >= _ordered_int_to_float(cand)
        cnt = jnp.sum(jnp.where(ge, 1.0, 0.0), axis=-1, keepdims=True)
        st["tau"] = jnp.where(cnt >= kf, cand, st["tau"])

    def ties():
        tau = _ordered_int_to_float(st.pop("tau"))
        sc = sc_ref[rows, :]
        gt = sc > tau
        eq = sc == tau
        need = kf - jnp.sum(jnp.where(gt, 1.0, 0.0), axis=-1, keepdims=True)
        eqf = jnp.where(eq, 1.0, 0.0).astype(BF16)
        tri = (lax.broadcasted_iota(jnp.int32, (LANES, LANES), 0)
               <= lax.broadcasted_iota(jnp.int32, (LANES, LANES), 1)).astype(BF16)
        off = jnp.zeros((nr, 1), F32)
        for c in range(ext // LANES):
            sl = slice(c * LANES, (c + 1) * LANES)
            pc = _dot(eqf[:, sl], tri)
            take = gt[:, sl] | (eq[:, sl] & ((pc + off) <= need))
            if c * LANES >= d0:
                take = take & keep[rows, c * LANES - d0:(c + 1) * LANES - d0]
            bias_ref[rows, sl] = jnp.where(take, 0.0, MASK_NEG)
            off = off + pc[:, LANES - 1:LANES]

    return [scores] + [functools.partial(search, i) for i in range(32)] + [ties]


def _dsa_attend_steps(rows, q_ref, o_ref, bias_ref, k_aug, v_aug):
    nr = rows.stop - rows.start
    tail_lane = lax.broadcasted_iota(jnp.int32, (nr, LANES - DSA_HEAD_DIM), 1)

    def logits(h):
        slope = 2.0 ** (-8.0 * (h + 1) / DSA_HEADS)
        tail = jnp.where(tail_lane < 2, slope, 0.0).astype(BF16)
        qh = q_ref[0, rows, h * DSA_HEAD_DIM:(h + 1) * DSA_HEAD_DIM]
        return _dot_nt(jnp.concatenate([qh, tail], axis=1), k_aug) + bias_ref[rows, :]

    def values(h, p):
        o = _dot(p, v_aug)
        o = o[:, 0:DSA_HEAD_DIM] / o[:, DSA_HEAD_DIM:DSA_HEAD_DIM + 1]
        o_ref[0, rows, h * DSA_HEAD_DIM:(h + 1) * DSA_HEAD_DIM] = o.astype(o_ref.dtype)

    lg, mx, pr = {}, {}, {}
    d_max, d_exp, d_pv = STAGE_LAGS

    def step(t):
        if t < DSA_HEADS:
            lg[t] = logits(t)
        if 0 <= t - d_max < DSA_HEADS:
            mx[t - d_max] = jnp.max(lg[t - d_max], axis=-1, keepdims=True)
        if 0 <= t - d_exp < DSA_HEADS:
            h = t - d_exp
            pr[h] = jnp.exp(lg.pop(h) - mx.pop(h)).astype(BF16)
        if 0 <= t - d_pv < DSA_HEADS:
            values(t - d_pv, pr.pop(t - d_pv))

    return [functools.partial(step, t) for t in range(DSA_HEADS + d_pv)]


def _dsa_kernel(q_ref, iq_ref, mq_ref, mk_ref, _aliased_out, o_ref, sc_ref, bias_ref, *, topk):
    tq = q_ref.shape[1]
    rows = slice(0, tq)
    select, attend = [], []
    for bi in range(q_ref.shape[0]):
        one = lambda ref: ref.at[bi:bi + 1]
        sc_b, bias_b = sc_ref.at[bi], bias_ref.at[bi]
        d0, keep, ik_b, k_aug, v_aug = _dsa_prepare(one(q_ref), one(mk_ref))
        if mk_ref.shape[1] > topk:
            select.append(_dsa_select_steps(rows, one(iq_ref), one(mq_ref), sc_b, bias_b, ik_b, keep,
                                            topk=topk))
        else:
            if d0 > 0:
                bias_b[:, 0:d0] = jnp.zeros((tq, d0), F32)
            bias_b[:, d0:] = jnp.where(keep, 0.0, MASK_NEG)
        attend.append(_dsa_attend_steps(rows, one(q_ref), one(o_ref), bias_b, k_aug, v_aug))
    for steps in list(zip(*select)) + list(zip(*attend)):
        for step in steps:
            step()


def _dsa(q, iq, misc, keyside):
    bsz, s, _ = q.shape
    topk = min(DSA_TOPK_MAX, s // 4)
    tq = min(256, s)
    hd = DSA_HEADS * DSA_HEAD_DIM
    att = jnp.zeros((bsz, s, hd), BF16)
    for c in range(s // tq):
        ext = (c + 1) * tq
        bb = 2 if (ext <= DSA_PAIR_MAX_EXTENT and bsz % 2 == 0) else 1
        att = pl.pallas_call(
            functools.partial(_dsa_kernel, topk=topk),
            grid=(bsz // bb,),
            in_specs=[pl.BlockSpec((bb, tq, hd), lambda b, c=c: (b, c, 0)),
                      pl.BlockSpec((bb, tq, IDX_HEADS * IDX_DIM), lambda b, c=c: (b, c, 0)),
                      pl.BlockSpec((bb, tq, IN_MISC), lambda b, c=c: (b, c, 0)),
                      pl.BlockSpec((bb, ext, KEYSIDE), lambda b: (b, 0, 0)),
                      pl.BlockSpec(memory_space=pl.ANY)],
            out_specs=pl.BlockSpec((bb, tq, hd), lambda b, c=c: (b, c, 0)),
            out_shape=jax.ShapeDtypeStruct((bsz, s, hd), BF16),
            scratch_shapes=[pltpu.VMEM((bb, tq, ext), F32), pltpu.VMEM((bb, tq, ext), F32)],
            input_output_aliases={4: 0},
            compiler_params=_cparams(("parallel",)),
            name=f"hy_dsa_c{c}",
        )(q, iq, misc, keyside, att)
    return att


def _tail_kernel(*refs, n_in, tf):
    a_refs = refs[:n_in]
    wo_refs = refs[n_in:2 * n_in]
    (x_ref, p_ref, g1_ref, b1_ref, w1_ref, w2_ref, wg_ref, wp_ref, g2_ref, b2_ref,
     o_ref, acc_ref) = refs[2 * n_in:]
    n_chunks = w1_ref.shape[1] // tf

    def steps_for(rows):
        st = {}

        def out_proj():
            m = None
            for a_ref, w_ref in zip(a_refs, wo_refs):
                t = _dot(a_ref[rows, :].astype(BF16), w_ref[...])
                m = t if m is None else m + t
            st["y"] = DN_ALPHA * x_ref[rows, :] + m

        def norm1():
            st["h"] = _layer_norm(st.pop("y"), g1_ref[...], b1_ref[...])
            st["hb"] = st["h"].astype(BF16)

        def gate():
            g = jax.nn.sigmoid(_dot(st["hb"], wg_ref[...]))
            acc_ref[rows, :] = DN_ALPHA * st.pop("h") + g * _dot(p_ref[rows, :].astype(BF16), wp_ref[...])

        def mlp(c):
            a = jnp.maximum(_dot(st["hb"], w1_ref[:, c * tf:(c + 1) * tf]), 0.0)
            acc_ref[rows, :] += _dot((a * a).astype(BF16), w2_ref[c * tf:(c + 1) * tf, :])

        def norm2():
            o_ref[rows, :] = _layer_norm(acc_ref[rows, :], g2_ref[...], b2_ref[...])

        return [out_proj, norm1, gate] + [functools.partial(mlp, c) for c in range(n_chunks)] + [norm2]

    n_sub = x_ref.shape[0] // TAIL_SUB_ROWS
    subs = [steps_for(slice(i * TAIL_SUB_ROWS, (i + 1) * TAIL_SUB_ROWS)) for i in range(n_sub)]
    lag = 1
    for t in range(len(subs[0]) + lag * (n_sub - 1)):
        for i, steps in enumerate(subs):
            if 0 <= t - lag * i < len(steps):
                steps[t - lag * i]()


def _tail(acts, wos, x2, p3, layer, g1, b1, w1, w2, wg, wp, g2, b2, tm, tf):
    m, d = x2.shape
    n_in = len(acts)
    rows = lambda a: pl.BlockSpec((tm, a.shape[1]), lambda i: (i, 0))
    resident = lambda a: pl.BlockSpec(a.shape, lambda i: (0, 0), pipeline_mode=pl.Buffered(1))
    of_layer = lambda a: pl.BlockSpec((None,) + a.shape[1:], lambda i: (layer, 0, 0),
                                      pipeline_mode=pl.Buffered(1))
    params = [g1, b1, w1, w2, wg, wp, g2, b2]
    return pl.pallas_call(
        functools.partial(_tail_kernel, n_in=n_in, tf=tf),
        grid=(m // tm,),
        in_specs=([rows(a) for a in acts] + [resident(w) for w in wos]
                  + [rows(x2), pl.BlockSpec((None, tm, p3.shape[2]), lambda i: (layer, i, 0))]
                  + [of_layer(a) for a in params]),
        out_specs=pl.BlockSpec((tm, d), lambda i: (i, 0)),
        out_shape=jax.ShapeDtypeStruct((m, d), F32),
        scratch_shapes=[pltpu.VMEM((tm, d), F32)],
        compiler_params=_cparams(("parallel",)),
        name="layer_tail",
    )(*acts, *wos, x2, p3, *params)


MLA_Q_SCALE = (MLA_NOPE + MLA_ROPE) ** -0.5 * math.log2(math.e)
ROPE_HALF = MLA_ROPE // 2
NOPE_LO = LANES // 2 - ROPE_HALF


def _rope_group(t, cos_t, sin_t):
    return t * cos_t + pltpu.roll(t, LANES // 2, axis=1) * sin_t


def _mla_proj_kernel(x_ref, wd_ref, qg_ref, kvg_ref, wq_ref, wk_ref, wv_ref,
                     cosq_ref, sinq_ref, cosk_ref, sink_ref, q_ref, k_ref, v_ref):
    xb = x_ref[...].astype(BF16)
    down = _dot(xb, wd_ref[...])
    cq = down[:, 0:MLA_Q_LORA]
    ckv = down[:, MLA_Q_LORA:MLA_Q_LORA + MLA_KV_LORA]
    krg = down[:, MLA_Q_LORA + MLA_KV_LORA:]
    cqn = (cq * lax.rsqrt(jnp.mean(cq * cq, axis=-1, keepdims=True) + RMS_EPS) * qg_ref[...]).astype(BF16)
    ckvn = (ckv * lax.rsqrt(jnp.mean(ckv * ckv, axis=-1, keepdims=True) + RMS_EPS) * kvg_ref[...]).astype(BF16)
    cos_q, sin_q = cosq_ref[...], sinq_ref[...]
    kr = _rope_group(krg, cosk_ref[...], sink_ref[...])
    q = _dot(cqn, wq_ref[...])
    k = _dot(ckvn, wk_ref[...])
    for h in range(MLA_HEADS):
        sl = slice(h * LANES, (h + 1) * LANES)
        q_ref[:, sl] = _rope_group(q[:, sl], cos_q, sin_q).astype(BF16)
        k_ref[:, sl] = (k[:, sl] + kr).astype(BF16)
    v_ref[...] = _dot(ckvn, wv_ref[...]).astype(BF16)


def _mla_proj(x2, wd, qg, kvg, wq, wk, wv, tables, tm, s):
    m, d = x2.shape
    nt = s // tm
    full = lambda a: pl.BlockSpec(a.shape, lambda i: (0, 0))
    tab = lambda: pl.BlockSpec((tm, LANES), lambda i: (i % nt, 0))
    hq = MLA_HEADS * LANES
    hv = MLA_HEADS * MLA_V
    return pl.pallas_call(
        _mla_proj_kernel,
        grid=(m // tm,),
        in_specs=[pl.BlockSpec((tm, d), lambda i: (i, 0)), full(wd), full(qg), full(kvg),
                  full(wq), full(wk), full(wv)] + [tab() for _ in tables],
        out_specs=[pl.BlockSpec((tm, hq), lambda i: (i, 0)),
                   pl.BlockSpec((tm, hq), lambda i: (i, 0)),
                   pl.BlockSpec((tm, hv), lambda i: (i, 0))],
        out_shape=[jax.ShapeDtypeStruct((m, hq), BF16),
                   jax.ShapeDtypeStruct((m, hq), BF16),
                   jax.ShapeDtypeStruct((m, hv), BF16)],
        compiler_params=_cparams(("parallel",)),
        name="mla_proj",
    )(x2, wd, qg, kvg, wq, wk, wv, *tables)


def _mla_attn_kernel(q_ref, k_ref, v_ref, o_ref, ot_ref, *, tq, heads_per_step, key_chunk):
    s = q_ref.shape[1]
    keep_t = (lax.broadcasted_iota(jnp.int32, (tq, tq), 0)
              <= lax.broadcasted_iota(jnp.int32, (tq, tq), 1))
    v_t = jnp.transpose(v_ref[0].astype(F32)).astype(BF16)
    ones_rows = jnp.ones((2 * SUBLANES, s), BF16)
    lhs = [jnp.concatenate([v_t[hh * MLA_V:(hh + 1) * MLA_V, :], ones_rows], axis=0)
           for hh in range(heads_per_step)]
    units = []
    for j in range(s // tq):
        ext = (j + 1) * tq
        starts = list(range(0, ext - tq, key_chunk)) or [0]
        bounds = [(a, b) for a, b in zip(starts, starts[1:] + [ext])]
        for hh in range(heads_per_step):
            units += [(j * tq, hh, k0, k1) for k0, k1 in bounds]

    def logits(r0, hh, k0, k1):
        ql = slice(hh * LANES, (hh + 1) * LANES)
        sc = _dot_nt(k_ref[0, k0:k1, ql], q_ref[0, r0:r0 + tq, ql])
        if k1 == r0 + tq:
            diag = jnp.where(keep_t, sc[k1 - k0 - tq:, :], MASK_NEG)
            sc = diag if k1 - k0 == tq else jnp.concatenate([sc[:k1 - k0 - tq, :], diag], axis=0)
        return sc

    def probs(sc):
        m = _reduce_keys(sc, jnp.max, jnp.maximum)
        return jnp.exp2(sc - m).astype(BF16), m

    running = {}

    def values(r0, hh, k0, k1, p, m):
        acc = _dot(lhs[hh][:, k0:k1], p)
        if k0 > 0:
            m_run, acc_run = running.pop((r0, hh))
            m_new = jnp.maximum(m_run, m)
            acc = acc_run * jnp.exp2(m_run - m_new) + acc * jnp.exp2(m - m_new)
            m = m_new
        if k1 == r0 + tq:
            ot_ref[hh * MLA_V:(hh + 1) * MLA_V, r0:r0 + tq] = acc[0:MLA_V, :] / acc[MLA_V:MLA_V + 1, :]
        else:
            running[(r0, hh)] = (m, acc)

    n = len(units)
    stage_s, stage_m, stage_p = {}, {}, {}
    d_max, d_exp, d_pv = STAGE_LAGS
    for t in range(n + d_pv):
        if t < n:
            stage_s[t] = logits(*units[t])
        if 0 <= t - d_max < n:
            u = t - d_max
            stage_m[u] = _reduce_keys(stage_s[u], jnp.max, jnp.maximum)
        if 0 <= t - d_exp < n:
            u = t - d_exp
            stage_p[u] = jnp.exp2(stage_s.pop(u) - stage_m[u]).astype(BF16)
        if 0 <= t - d_pv < n:
            u = t - d_pv
            values(*units[u], stage_p.pop(u), stage_m.pop(u))
    o_ref[0] = jnp.transpose(ot_ref[...]).astype(o_ref.dtype)


def _mla_attn(q, k, v, tq):
    bsz, s, _ = q.shape
    hps = 4
    return pl.pallas_call(
        functools.partial(_mla_attn_kernel, tq=tq, heads_per_step=hps, key_chunk=2 * tq),
        grid=(bsz, MLA_HEADS // hps),
        in_specs=[pl.BlockSpec((1, s, hps * LANES), lambda b, h: (b, 0, h)),
                  pl.BlockSpec((1, s, hps * LANES), lambda b, h: (b, 0, h)),
                  pl.BlockSpec((1, s, hps * MLA_V), lambda b, h: (b, 0, h))],
        out_specs=pl.BlockSpec((1, s, hps * MLA_V), lambda b, h: (b, 0, h)),
        out_shape=jax.ShapeDtypeStruct((bsz, s, MLA_HEADS * MLA_V), BF16),
        scratch_shapes=[pltpu.VMEM((hps * MLA_V, s), F32)],
        compiler_params=_cparams(("parallel", "parallel")),
        name="mla_attn",
    )(q, k, v)


def _hy_in_weight(w_in):
    widths = [RNN_WIDTH, RNN_WIDTH, IN_Q, DSA_HEAD_DIM, DSA_HEAD_DIM, IN_IQ, IDX_DIM, IDX_HEADS]
    xr, yr, q, k, v, iq, ik, iw = jnp.split(w_in, np.cumsum(widths)[:-1].tolist(), axis=1)
    pad = jnp.zeros((w_in.shape[0], IN_MISC - 2 * DSA_HEAD_DIM - IDX_DIM - IDX_HEADS), w_in.dtype)
    return jnp.concatenate([xr, yr, q, iq, k, v, ik, iw, pad], axis=1).astype(BF16)


def _block_diag_groups(w):
    bw = w.shape[-1]
    per = LANES // bw
    ng = w.shape[0] // per
    on_diag = jnp.eye(per, dtype=bool)[None, :, None, :, None]
    out = jnp.where(on_diag, w.reshape(ng, per, bw, 1, bw), 0.0)
    return out.reshape(ng, LANES, LANES).astype(BF16)


def _head_lanes(nope, rope):
    lead = nope.shape[:-1] if nope is not None else rope.shape[:-1]
    dt = nope.dtype if nope is not None else rope.dtype
    z = lambda n: jnp.zeros(lead + (n,), dt)
    r1, r2 = (rope[..., :ROPE_HALF], rope[..., ROPE_HALF:]) if rope is not None else (z(ROPE_HALF), z(ROPE_HALF))
    n1, n2 = (nope[..., :NOPE_LO], nope[..., NOPE_LO:]) if nope is not None else (z(NOPE_LO), z(MLA_NOPE - NOPE_LO))
    return jnp.concatenate([r1, n1, r2, n2, z(LANES - MLA_NOPE - MLA_ROPE)], axis=-1)


def _mla_weights(w_down, w_uq, w_ukv):
    lat = MLA_Q_LORA + MLA_KV_LORA
    wd = jnp.concatenate([w_down[:, :lat], _head_lanes(None, w_down[:, lat:])], axis=1)
    wq = w_uq.reshape(MLA_Q_LORA, MLA_HEADS, MLA_NOPE + MLA_ROPE)
    wq = _head_lanes(wq[:, :, :MLA_NOPE], wq[:, :, MLA_NOPE:]).reshape(MLA_Q_LORA, MLA_HEADS * LANES)
    wkv = w_ukv.reshape(MLA_KV_LORA, MLA_HEADS, MLA_NOPE + MLA_V)
    wk = _head_lanes(wkv[:, :, :MLA_NOPE], None).reshape(MLA_KV_LORA, MLA_HEADS * LANES)
    wv = wkv[:, :, MLA_NOPE:].reshape(MLA_KV_LORA, MLA_HEADS * MLA_V)
    return wd.astype(BF16), wq.astype(BF16), wk.astype(BF16), wv.astype(BF16)


def _rope_tables(s):
    pos = np.arange(s, dtype=np.float64)
    freq = ROPE_BASE ** (-np.arange(0, MLA_ROPE, 2, dtype=np.float64) / MLA_ROPE)
    ang = pos[:, None] * freq[None, :]
    lo = slice(0, ROPE_HALF)
    hi = slice(LANES // 2, LANES // 2 + ROPE_HALF)
    cos_t = np.ones((s, LANES))
    sin_t = np.zeros((s, LANES))
    cos_t[:, lo] = cos_t[:, hi] = np.cos(ang)
    sin_t[:, lo] = -np.sin(ang)
    sin_t[:, hi] = np.sin(ang)
    f32 = lambda a: jnp.asarray(a.astype(np.float32))
    return f32(cos_t * MLA_Q_SCALE), f32(sin_t * MLA_Q_SCALE), f32(cos_t), f32(sin_t)


def _tile_m(m):
    return 512 if m % 512 == 0 else m


def kernel(x, p, ln1_g, ln1_b, ln2_g, ln2_b, mlp_w1, mlp_w2, ple_w_proj, ple_w_gate, hy_w_in, hy_conv_w, hy_conv_b, hy_ga_w, hy_ga_b, hy_gx_w, hy_gx_b, hy_lambda, hy_w_out, mla_w_down, mla_q_norm, mla_kv_norm, mla_w_uq, mla_w_ukv, mla_w_out):
    bsz, s, d = x.shape
    m = bsz * s
    tm = _tile_m(m)
    tf = 1024
    row = lambda a: a.reshape(1, -1)
    x2 = x.reshape(m, d)

    rows3 = lambda a: a.reshape(a.shape[0], 1, -1)
    tail_params = (rows3(ln1_g), rows3(ln1_b), mlp_w1.astype(BF16), mlp_w2.astype(BF16),
                   ple_w_gate.astype(BF16), ple_w_proj.astype(BF16), rows3(ln2_g), rows3(ln2_b))
    p3 = p.reshape(p.shape[0], m, -1)

    def layer_tail(acts, wos, x2, i):
        return _tail(acts, wos, x2, p3, i, *tail_params, TAIL_ROWS if m % TAIL_ROWS == 0 else tm, tf)

    xy, q, iq, misc, keyside = _inproj(x2, _hy_in_weight(hy_w_in[0]), tm, s)
    rec = _rglru(xy.reshape(bsz, s, -1), hy_conv_w[0], row(hy_conv_b[0]),
                 _block_diag_groups(hy_ga_w[0]), row(hy_ga_b[0]),
                 _block_diag_groups(hy_gx_w[0]), row(hy_gx_b[0]), row(hy_lambda[0]))
    att = _dsa(q.reshape(bsz, s, -1), iq.reshape(bsz, s, -1), misc.reshape(bsz, s, -1),
               keyside.reshape(bsz, s, -1))
    w_out = hy_w_out[0].astype(BF16)
    x2 = layer_tail([rec.reshape(m, -1), att.reshape(m, -1)], [w_out[:RNN_WIDTH], w_out[RNN_WIDTH:]], x2, 0)

    wd, wq, wk, wv = _mla_weights(mla_w_down[0], mla_w_uq[0], mla_w_ukv[0])
    qp, kp, vp = _mla_proj(x2, wd, row(mla_q_norm[0]), row(mla_kv_norm[0]), wq, wk, wv,
                           _rope_tables(s), min(tm, s), s)
    o = _mla_attn(qp.reshape(bsz, s, -1), kp.reshape(bsz, s, -1), vp.reshape(bsz, s, -1), min(256, s))
    x2 = layer_tail([o.reshape(m, -1)], [mla_w_out[0].astype(BF16)], x2, 1)
    return x2.reshape(bsz, s, d)
```

```python
import functools
import math

import numpy as np
import jax
import jax.numpy as jnp
from jax import lax
from jax.experimental import pallas as pl
from jax.experimental.pallas import tpu as pltpu

F32 = jnp.float32
BF16 = jnp.bfloat16

RNN_WIDTH = 512
CONV_WIDTH = 4
LRU_C = 8.0
DSA_HEADS = 8
DSA_HEAD_DIM = 64
IDX_HEADS = 4
IDX_DIM = 64
DSA_TOPK_MAX = 256
MLA_HEADS = 16
MLA_Q_LORA = 512
MLA_KV_LORA = 256
MLA_NOPE = 64
MLA_ROPE = 32
MLA_V = 64
ROPE_BASE = 10000.0
DEPTH = 2
DN_ALPHA = (2 * DEPTH) ** 0.25
LN_EPS = 1e-5
RMS_EPS = 1e-6

LANES = 128
SUBLANES = 8
VMEM_LIMIT_BYTES = 56 * 1024 * 1024

MASK_NEG = -3.0e38
INT32_MIN = -(2 ** 31)
STAGE_LAGS = (1, 2, 4)
DSA_Q_SCALE = DSA_HEAD_DIM ** -0.5
DSA_PAIR_MAX_EXTENT = 768
TAIL_ROWS = 1024
TAIL_SUB_ROWS = 256


def _cparams(sem):
    return pltpu.CompilerParams(dimension_semantics=sem, vmem_limit_bytes=VMEM_LIMIT_BYTES)


def _dot(a, b):
    return jnp.dot(a, b, preferred_element_type=F32)


def _dot_nt(a, b):
    return lax.dot_general(a, b, (((1,), (1,)), ((), ())), preferred_element_type=F32)


def _layer_norm(y, g, b):
    mu = jnp.mean(y, axis=-1, keepdims=True)
    yc = y - mu
    var = jnp.mean(yc * yc, axis=-1, keepdims=True)
    return yc * lax.rsqrt(var + LN_EPS) * g + b


IN_XY = 2 * RNN_WIDTH
IN_Q = DSA_HEADS * DSA_HEAD_DIM
IN_IQ = IDX_HEADS * IDX_DIM
IN_MISC = 2 * LANES
IN_OFFSETS = tuple(int(v) for v in np.cumsum([0, IN_XY, IN_Q, IN_IQ, IN_MISC]))


KEYSIDE = 3 * LANES


def _inproj_kernel(x_ref, w_ref, xy_ref, q_ref, iq_ref, misc_ref, ks_ref, *, tiles_per_seq):
    tm = x_ref.shape[0]
    xb = x_ref[...].astype(BF16)
    cols = [w_ref[:, a:b] for a, b in zip(IN_OFFSETS[:-1], IN_OFFSETS[1:])]
    xy_ref[...] = _dot(xb, cols[0])
    q_ref[...] = (_dot(xb, cols[1]) * DSA_Q_SCALE).astype(BF16)
    iq_ref[...] = _dot(xb, cols[2])
    misc = _dot(xb, cols[3])
    misc_ref[...] = misc[:, LANES:2 * LANES]
    pos = (pl.program_id(0) % tiles_per_seq) * tm + lax.broadcasted_iota(jnp.int32, (tm, 1), 0)
    lane = lax.broadcasted_iota(jnp.int32, (1, LANES), 1)
    pos_hi = (pos & jnp.int32(-LANES)).astype(F32)
    pos_lo = (pos & jnp.int32(LANES - 1)).astype(F32)
    ks_ref[:, 0:LANES] = jnp.where(lane < DSA_HEAD_DIM, misc[:, 0:LANES],
                                   jnp.where(lane == DSA_HEAD_DIM, pos_hi,
                                             jnp.where(lane == DSA_HEAD_DIM + 1, pos_lo, 0.0))).astype(BF16)
    half_lane = lax.broadcasted_iota(jnp.int32, (tm, DSA_HEAD_DIM), 1)
    ks_ref[:, LANES:2 * LANES] = jnp.concatenate(
        [misc[:, DSA_HEAD_DIM:2 * DSA_HEAD_DIM], jnp.where(half_lane == 0, 1.0, 0.0)], axis=1).astype(BF16)
    ks_ref[:, 2 * LANES:3 * LANES] = jnp.concatenate(
        [misc[:, 2 * DSA_HEAD_DIM:2 * DSA_HEAD_DIM + IDX_DIM], jnp.zeros((tm, LANES - IDX_DIM), F32)],
        axis=1).astype(BF16)


def _inproj(x2, w, tm, s):
    m, d = x2.shape
    n = w.shape[1]
    widths = (IN_XY, IN_Q, IN_IQ, LANES, KEYSIDE)
    dtypes = (F32, BF16, F32, F32, BF16)
    return pl.pallas_call(
        functools.partial(_inproj_kernel, tiles_per_seq=s // tm),
        grid=(m // tm,),
        in_specs=[pl.BlockSpec((tm, d), lambda i: (i, 0)),
                  pl.BlockSpec((d, n), lambda i: (0, 0))],
        out_specs=[pl.BlockSpec((tm, wd), lambda i: (i, 0)) for wd in widths],
        out_shape=[jax.ShapeDtypeStruct((m, wd), dt) for wd, dt in zip(widths, dtypes)],
        compiler_params=_cparams(("parallel",)),
        name="hy_inproj",
    )(x2, w)


def _rglru_kernel(xr_ref, yr_ref, cw_ref, cb_ref, wa_ref, ba_ref, wx_ref, bx_ref, lam_ref,
                  o_ref, xpad_ref):
    s = xr_ref.shape[1]
    c = xr_ref.shape[2]
    xpad_ref[0:SUBLANES, :] = jnp.zeros((SUBLANES, c), F32)
    xpad_ref[SUBLANES:SUBLANES + s, :] = xr_ref[0]
    xc = None
    for k in range(CONV_WIDTH):
        start = SUBLANES - (CONV_WIDTH - 1) + k
        term = xpad_ref[start:start + s, :] * cw_ref[k:k + 1, :]
        xc = term if xc is None else xc + term
    xc = xc + cb_ref[...]
    xb = xc.astype(BF16)
    r = jax.nn.sigmoid(_dot(xb, wa_ref[0]) + ba_ref[...])
    gi = jax.nn.sigmoid(_dot(xb, wx_ref[0]) + bx_ref[...])
    nl = -lam_ref[...]
    softplus = jnp.maximum(nl, 0.0) + jnp.log1p(jnp.exp(-jnp.abs(nl)))
    log_a = (-LRU_C) * r * softplus
    a = jnp.exp(log_a)
    y = -jnp.tanh(log_a) * (a * a + 1.0)
    h = jnp.where(y > 0.0, y * lax.rsqrt(y), 0.0) * (gi * xc)
    row = lax.broadcasted_iota(jnp.int32, (s, c), 0)
    sh = 1
    while sh < s:
        if sh < SUBLANES:
            valid = row >= sh
            h = a * jnp.where(valid, pltpu.roll(h, sh, axis=0), 0.0) + h
            a = a * jnp.where(valid, pltpu.roll(a, sh, axis=0), 1.0)
        else:
            h = jnp.concatenate([h[:sh], a[sh:] * h[:s - sh] + h[sh:]], axis=0)
            if sh * 2 < s:
                a = jnp.concatenate([a[:sh], a[sh:] * a[:s - sh]], axis=0)
        sh *= 2
    o_ref[0] = (h * jax.nn.gelu(yr_ref[0])).astype(o_ref.dtype)


def _rglru(xy, conv_w, conv_b, wa, ba, wx, bx, lam):
    bsz, s, _ = xy.shape
    c = LANES
    ng = RNN_WIDTH // c
    vec = lambda: pl.BlockSpec((1, c), lambda b, g: (0, g))
    return pl.pallas_call(
        _rglru_kernel,
        grid=(bsz, ng),
        in_specs=[pl.BlockSpec((1, s, c), lambda b, g: (b, 0, g)),
                  pl.BlockSpec((1, s, c), lambda b, g: (b, 0, ng + g)),
                  pl.BlockSpec((CONV_WIDTH, c), lambda b, g: (0, g)),
                  vec(),
                  pl.BlockSpec((1, c, c), lambda b, g: (g, 0, 0)),
                  vec(),
                  pl.BlockSpec((1, c, c), lambda b, g: (g, 0, 0)),
                  vec(),
                  vec()],
        out_specs=pl.BlockSpec((1, s, c), lambda b, g: (b, 0, g)),
        out_shape=jax.ShapeDtypeStruct((bsz, s, RNN_WIDTH), BF16),
        scratch_shapes=[pltpu.VMEM((s + SUBLANES, c), F32)],
        compiler_params=_cparams(("parallel", "parallel")),
        name="hy_rglru",
    )(xy, xy, conv_w, conv_b, wa, ba, wx, bx, lam)


def _ordered_int_to_float(c):
    return lax.bitcast_convert_type(c ^ (lax.shift_right_arithmetic(c, 31) & jnp.int32(0x7FFFFFFF)), F32)


def _reduce_keys(x, reduce_fn, combine_fn, groups=8):
    rows = x.shape[0]
    step = max(rows // groups, SUBLANES)
    parts = [reduce_fn(x[r:r + step, :], axis=0, keepdims=True) for r in range(0, rows, step)]
    while len(parts) > 1:
        parts = [combine_fn(parts[i], parts[i + 1]) if i + 1 < len(parts) else parts[i]
                 for i in range(0, len(parts), 2)]
    return parts[0]


def _dsa_prepare(q_ref, mk_ref):
    tq = q_ref.shape[1]
    ext = mk_ref.shape[1]
    d0 = ext - tq
    keep = (lax.broadcasted_iota(jnp.int32, (tq, tq), 1)
            <= lax.broadcasted_iota(jnp.int32, (tq, tq), 0))

    k_aug = mk_ref[0, :, 0:LANES]
    v_aug = mk_ref[0, :, LANES:2 * LANES]
    ik_b = mk_ref[0, :, 2 * LANES:2 * LANES + IDX_DIM]
    return d0, keep, ik_b, k_aug, v_aug


def _dsa_select_steps(rows, iq_ref, mq_ref, sc_ref, bias_ref, ik_b, keep, *, topk):
    tq, ext = sc_ref.shape
    nr = rows.stop - rows.start
    d0 = ext - tq
    kf = float(topk)
    st = {}

    def scores():
        iw0 = 2 * DSA_HEAD_DIM + IDX_DIM - LANES
        iw = mq_ref[0, rows, iw0:iw0 + IDX_HEADS] * (IDX_HEADS ** -0.5 * IDX_DIM ** -0.5)
        iq = iq_ref[0, rows, :].astype(BF16)
        sc = None
        for h in range(IDX_HEADS):
            d = _dot_nt(iq[:, h * IDX_DIM:(h + 1) * IDX_DIM], ik_b)
            term = jnp.maximum(d, 0.0) * iw[:, h:h + 1]
            sc = term if sc is None else sc + term
        if d0 > 0:
            sc_ref[rows, 0:d0] = sc[:, 0:d0]
        sc_ref[rows, d0:ext] = jnp.where(keep[rows, :], sc[:, d0:ext], MASK_NEG)
        st["tau"] = jnp.full((nr, 1), INT32_MIN, jnp.int32)

    def search(i):
        cand = st["tau"] + jnp.int32(1 << (31 - i) if i else INT32_MIN)
        ge = sc_ref[rows, :] >= _ordered_int_to_float(cand)
        cnt = jnp.sum(jnp.where(ge, 1.0, 0.0), axis=-1, keepdims=True)
        st["tau"] = jnp.where(cnt >= kf, cand, st["tau"])

    def ties():
        tau = _ordered_int_to_float(st.pop("tau"))
        sc = sc_ref[rows, :]
        gt = sc > tau
        eq = sc == tau
        need = kf - jnp.sum(jnp.where(gt, 1.0, 0.0), axis=-1, keepdims=True)
        eqf = jnp.where(eq, 1.0, 0.0).astype(BF16)
        tri = (lax.broadcasted_iota(jnp.int32, (LANES, LANES), 0)
               <= lax.broadcasted_iota(jnp.int32, (LANES, LANES), 1)).astype(BF16)
        off = jnp.zeros((nr, 1), F32)
        for c in range(ext // LANES):
            sl = slice(c * LANES, (c + 1) * LANES)
            pc = _dot(eqf[:, sl], tri)
            take = gt[:, sl] | (eq[:, sl] & ((pc + off) <= need))
            if c * LANES >= d0:
                take = take & keep[rows, c * LANES - d0:(c + 1) * LANES - d0]
            bias_ref[rows, sl] = jnp.where(take, 0.0, MASK_NEG)
            off = off + pc[:, LANES - 1:LANES]

    return [scores] + [functools.partial(search, i) for i in range(32)] + [ties]


def _dsa_attend_steps(rows, q_ref, o_ref, bias_ref, k_aug, v_aug):
    nr = rows.stop - rows.start
    tail_lane = lax.broadcasted_iota(jnp.int32, (nr, LANES - DSA_HEAD_DIM), 1)

    def logits(h):
        slope = 2.0 ** (-8.0 * (h + 1) / DSA_HEADS)
        tail = jnp.where(tail_lane < 2, slope, 0.0).astype(BF16)
        qh = q_ref[0, rows, h * DSA_HEAD_DIM:(h + 1) * DSA_HEAD_DIM]
        return _dot_nt(jnp.concatenate([qh, tail], axis=1), k_aug) + bias_ref[rows, :]

    def values(h, p):
        o = _dot(p, v_aug)
        o = o[:, 0:DSA_HEAD_DIM] / o[:, DSA_HEAD_DIM:DSA_HEAD_DIM + 1]
        o_ref[0, rows, h * DSA_HEAD_DIM:(h + 1) * DSA_HEAD_DIM] = o.astype(o_ref.dtype)

    lg, mx, pr = {}, {}, {}
    d_max, d_exp, d_pv = STAGE_LAGS

    def step(t):
        if t < DSA_HEADS:
            lg[t] = logits(t)
        if 0 <= t - d_max < DSA_HEADS:
            mx[t - d_max] = jnp.max(lg[t - d_max], axis=-1, keepdims=True)
        if 0 <= t - d_exp < DSA_HEADS:
            h = t - d_exp
            pr[h] = jnp.exp(lg.pop(h) - mx.pop(h)).astype(BF16)
        if 0 <= t - d_pv < DSA_HEADS:
            values(t - d_pv, pr.pop(t - d_pv))

    return [functools.partial(step, t) for t in range(DSA_HEADS + d_pv)]


def _dsa_kernel(q_ref, iq_ref, mq_ref, mk_ref, _aliased_out, o_ref, sc_ref, bias_ref, *, topk):
    tq = q_ref.shape[1]
    rows = slice(0, tq)
    select, attend = [], []
    for bi in range(q_ref.shape[0]):
        one = lambda ref: ref.at[bi:bi + 1]
        sc_b, bias_b = sc_ref.at[bi], bias_ref.at[bi]
        d0, keep, ik_b, k_aug, v_aug = _dsa_prepare(one(q_ref), one(mk_ref))
        if mk_ref.shape[1] > topk:
            select.append(_dsa_select_steps(rows, one(iq_ref), one(mq_ref), sc_b, bias_b, ik_b, keep,
                                            topk=topk))
        else:
            if d0 > 0:
                bias_b[:, 0:d0] = jnp.zeros((tq, d0), F32)
            bias_b[:, d0:] = jnp.where(keep, 0.0, MASK_NEG)
        attend.append(_dsa_attend_steps(rows, one(q_ref), one(o_ref), bias_b, k_aug, v_aug))
    for steps in list(zip(*select)) + list(zip(*attend)):
        for step in steps:
            step()


def _dsa(q, iq, misc, keyside):
    bsz, s, _ = q.shape
    topk = min(DSA_TOPK_MAX, s // 4)
    tq = min(256, s)
    hd = DSA_HEADS * DSA_HEAD_DIM
    att = jnp.zeros((bsz, s, hd), BF16)
    for c in range(s // tq):
        ext = (c + 1) * tq
        bb = 2 if (ext <= DSA_PAIR_MAX_EXTENT and bsz % 2 == 0) else 1
        att = pl.pallas_call(
            functools.partial(_dsa_kernel, topk=topk),
            grid=(bsz // bb,),
            in_specs=[pl.BlockSpec((bb, tq, hd), lambda b, c=c: (b, c, 0)),
                      pl.BlockSpec((bb, tq, IDX_HEADS * IDX_DIM), lambda b, c=c: (b, c, 0)),
                      pl.BlockSpec((bb, tq, LANES), lambda b, c=c: (b, c, 0)),
                      pl.BlockSpec((bb, ext, KEYSIDE), lambda b: (b, 0, 0)),
                      pl.BlockSpec(memory_space=pl.ANY)],
            out_specs=pl.BlockSpec((bb, tq, hd), lambda b, c=c: (b, c, 0)),
            out_shape=jax.ShapeDtypeStruct((bsz, s, hd), BF16),
            scratch_shapes=[pltpu.VMEM((bb, tq, ext), F32), pltpu.VMEM((bb, tq, ext), F32)],
            input_output_aliases={4: 0},
            compiler_params=_cparams(("parallel",)),
            name=f"hy_dsa_c{c}",
        )(q, iq, misc, keyside, att)
    return att


def _tail_kernel(*refs, n_in, tf):
    a_refs = refs[:n_in]
    wo_refs = refs[n_in:2 * n_in]
    (x_ref, p_ref, g1_ref, b1_ref, w1_ref, w2_ref, wg_ref, wp_ref, g2_ref, b2_ref,
     o_ref, acc_ref) = refs[2 * n_in:]
    n_chunks = w1_ref.shape[1] // tf

    def steps_for(rows):
        st = {}

        def out_proj():
            m = None
            for a_ref, w_ref in zip(a_refs, wo_refs):
                t = _dot(a_ref[rows, :].astype(BF16), w_ref[...])
                m = t if m is None else m + t
            st["y"] = DN_ALPHA * x_ref[rows, :] + m

        def norm1():
            st["h"] = _layer_norm(st.pop("y"), g1_ref[...], b1_ref[...])
            st["hb"] = st["h"].astype(BF16)

        def gate():
            g = jax.nn.sigmoid(_dot(st["hb"], wg_ref[...]))
            acc_ref[rows, :] = DN_ALPHA * st.pop("h") + g * _dot(p_ref[rows, :].astype(BF16), wp_ref[...])

        def mlp(c):
            a = jnp.maximum(_dot(st["hb"], w1_ref[:, c * tf:(c + 1) * tf]), 0.0)
            acc_ref[rows, :] += _dot((a * a).astype(BF16), w2_ref[c * tf:(c + 1) * tf, :])

        def norm2():
            o_ref[rows, :] = _layer_norm(acc_ref[rows, :], g2_ref[...], b2_ref[...])

        return [out_proj, norm1, gate] + [functools.partial(mlp, c) for c in range(n_chunks)] + [norm2]

    n_sub = x_ref.shape[0] // TAIL_SUB_ROWS
    subs = [steps_for(slice(i * TAIL_SUB_ROWS, (i + 1) * TAIL_SUB_ROWS)) for i in range(n_sub)]
    lag = 1
    for t in range(len(subs[0]) + lag * (n_sub - 1)):
        for i, steps in enumerate(subs):
            if 0 <= t - lag * i < len(steps):
                steps[t - lag * i]()


def _tail(acts, wos, x2, p3, layer, g1, b1, w1, w2, wg, wp, g2, b2, tm, tf):
    m, d = x2.shape
    n_in = len(acts)
    rows = lambda a: pl.BlockSpec((tm, a.shape[1]), lambda i: (i, 0))
    resident = lambda a: pl.BlockSpec(a.shape, lambda i: (0, 0), pipeline_mode=pl.Buffered(1))
    of_layer = lambda a: pl.BlockSpec((None,) + a.shape[1:], lambda i: (layer, 0, 0),
                                      pipeline_mode=pl.Buffered(1))
    params = [g1, b1, w1, w2, wg, wp, g2, b2]
    return pl.pallas_call(
        functools.partial(_tail_kernel, n_in=n_in, tf=tf),
        grid=(m // tm,),
        in_specs=([rows(a) for a in acts] + [resident(w) for w in wos]
                  + [rows(x2), pl.BlockSpec((None, tm, p3.shape[2]), lambda i: (layer, i, 0))]
                  + [of_layer(a) for a in params]),
        out_specs=pl.BlockSpec((tm, d), lambda i: (i, 0)),
        out_shape=jax.ShapeDtypeStruct((m, d), F32),
        scratch_shapes=[pltpu.VMEM((tm, d), F32)],
        compiler_params=_cparams(("parallel",)),
        name="layer_tail",
    )(*acts, *wos, x2, p3, *params)


MLA_Q_SCALE = (MLA_NOPE + MLA_ROPE) ** -0.5 * math.log2(math.e)
ROPE_HALF = MLA_ROPE // 2
NOPE_LO = LANES // 2 - ROPE_HALF


def _rope_group(t, cos_t, sin_t):
    return t * cos_t + pltpu.roll(t, LANES // 2, axis=1) * sin_t


def _mla_proj_kernel(x_ref, wd_ref, qg_ref, kvg_ref, wq_ref, wk_ref, wv_ref,
                     cosq_ref, sinq_ref, cosk_ref, sink_ref, q_ref, k_ref, v_ref):
    xb = x_ref[...].astype(BF16)
    down = _dot(xb, wd_ref[...])
    cq = down[:, 0:MLA_Q_LORA]
    ckv = down[:, MLA_Q_LORA:MLA_Q_LORA + MLA_KV_LORA]
    krg = down[:, MLA_Q_LORA + MLA_KV_LORA:]
    cqn = (cq * lax.rsqrt(jnp.mean(cq * cq, axis=-1, keepdims=True) + RMS_EPS) * qg_ref[...]).astype(BF16)
    ckvn = (ckv * lax.rsqrt(jnp.mean(ckv * ckv, axis=-1, keepdims=True) + RMS_EPS) * kvg_ref[...]).astype(BF16)
    cos_q, sin_q = cosq_ref[...], sinq_ref[...]
    kr = _rope_group(krg, cosk_ref[...], sink_ref[...])
    q = _dot(cqn, wq_ref[...])
    k = _dot(ckvn, wk_ref[...])
    for h in range(MLA_HEADS):
        sl = slice(h * LANES, (h + 1) * LANES)
        q_ref[:, sl] = _rope_group(q[:, sl], cos_q, sin_q).astype(BF16)
        k_ref[:, sl] = (k[:, sl] + kr).astype(BF16)
    v_ref[...] = _dot(ckvn, wv_ref[...]).astype(BF16)


def _mla_proj(x2, wd, qg, kvg, wq, wk, wv, tables, tm, s):
    m, d = x2.shape
    nt = s // tm
    full = lambda a: pl.BlockSpec(a.shape, lambda i: (0, 0))
    tab = lambda: pl.BlockSpec((tm, LANES), lambda i: (i % nt, 0))
    hq = MLA_HEADS * LANES
    hv = MLA_HEADS * MLA_V
    return pl.pallas_call(
        _mla_proj_kernel,
        grid=(m // tm,),
        in_specs=[pl.BlockSpec((tm, d), lambda i: (i, 0)), full(wd), full(qg), full(kvg),
                  full(wq), full(wk), full(wv)] + [tab() for _ in tables],
        out_specs=[pl.BlockSpec((tm, hq), lambda i: (i, 0)),
                   pl.BlockSpec((tm, hq), lambda i: (i, 0)),
                   pl.BlockSpec((tm, hv), lambda i: (i, 0))],
        out_shape=[jax.ShapeDtypeStruct((m, hq), BF16),
                   jax.ShapeDtypeStruct((m, hq), BF16),
                   jax.ShapeDtypeStruct((m, hv), BF16)],
        compiler_params=_cparams(("parallel",)),
        name="mla_proj",
    )(x2, wd, qg, kvg, wq, wk, wv, *tables)


def _mla_attn_kernel(q_ref, k_ref, v_ref, o_ref, ot_ref, *, tq, heads_per_step, key_chunk):
    s = q_ref.shape[1]
    keep_t = (lax.broadcasted_iota(jnp.int32, (tq, tq), 0)
              <= lax.broadcasted_iota(jnp.int32, (tq, tq), 1))
    v_t = jnp.transpose(v_ref[0].astype(F32)).astype(BF16)
    ones_rows = jnp.ones((2 * SUBLANES, s), BF16)
    lhs = [jnp.concatenate([v_t[hh * MLA_V:(hh + 1) * MLA_V, :], ones_rows], axis=0)
           for hh in range(heads_per_step)]
    units = []
    for j in range(s // tq):
        ext = (j + 1) * tq
        starts = list(range(0, ext - tq, key_chunk)) or [0]
        bounds = [(a, b) for a, b in zip(starts, starts[1:] + [ext])]
        for hh in range(heads_per_step):
            units += [(j * tq, hh, k0, k1) for k0, k1 in bounds]

    def logits(r0, hh, k0, k1):
        ql = slice(hh * LANES, (hh + 1) * LANES)
        sc = _dot_nt(k_ref[0, k0:k1, ql], q_ref[0, r0:r0 + tq, ql])
        if k1 == r0 + tq:
            diag = jnp.where(keep_t, sc[k1 - k0 - tq:, :], MASK_NEG)
            sc = diag if k1 - k0 == tq else jnp.concatenate([sc[:k1 - k0 - tq, :], diag], axis=0)
        return sc

    def probs(sc):
        m = _reduce_keys(sc, jnp.max, jnp.maximum)
        return jnp.exp2(sc - m).astype(BF16), m

    running = {}

    def values(r0, hh, k0, k1, p, m):
        acc = _dot(lhs[hh][:, k0:k1], p)
        if k0 > 0:
            m_run, acc_run = running.pop((r0, hh))
            m_new = jnp.maximum(m_run, m)
            acc = acc_run * jnp.exp2(m_run - m_new) + acc * jnp.exp2(m - m_new)
            m = m_new
        if k1 == r0 + tq:
            ot_ref[hh * MLA_V:(hh + 1) * MLA_V, r0:r0 + tq] = acc[0:MLA_V, :] / acc[MLA_V:MLA_V + 1, :]
        else:
            running[(r0, hh)] = (m, acc)

    n = len(units)
    stage_s, stage_m, stage_p = {}, {}, {}
    d_max, d_exp, d_pv = STAGE_LAGS
    for t in range(n + d_pv):
        if t < n:
            stage_s[t] = logits(*units[t])
        if 0 <= t - d_max < n:
            u = t - d_max
            stage_m[u] = _reduce_keys(stage_s[u], jnp.max, jnp.maximum)
        if 0 <= t - d_exp < n:
            u = t - d_exp
            stage_p[u] = jnp.exp2(stage_s.pop(u) - stage_m[u]).astype(BF16)
        if 0 <= t - d_pv < n:
            u = t - d_pv
            values(*units[u], stage_p.pop(u), stage_m.pop(u))
    o_ref[0] = jnp.transpose(ot_ref[...]).astype(o_ref.dtype)


def _mla_attn(q, k, v, tq):
    bsz, s, _ = q.shape
    hps = 4
    return pl.pallas_call(
        functools.partial(_mla_attn_kernel, tq=tq, heads_per_step=hps, key_chunk=2 * tq),
        grid=(bsz, MLA_HEADS // hps),
        in_specs=[pl.BlockSpec((1, s, hps * LANES), lambda b, h: (b, 0, h)),
                  pl.BlockSpec((1, s, hps * LANES), lambda b, h: (b, 0, h)),
                  pl.BlockSpec((1, s, hps * MLA_V), lambda b, h: (b, 0, h))],
        out_specs=pl.BlockSpec((1, s, hps * MLA_V), lambda b, h: (b, 0, h)),
        out_shape=jax.ShapeDtypeStruct((bsz, s, MLA_HEADS * MLA_V), BF16),
        scratch_shapes=[pltpu.VMEM((hps * MLA_V, s), F32)],
        compiler_params=_cparams(("parallel", "parallel")),
        name="mla_attn",
    )(q, k, v)


def _hy_in_weight(w_in):
    widths = [RNN_WIDTH, RNN_WIDTH, IN_Q, DSA_HEAD_DIM, DSA_HEAD_DIM, IN_IQ, IDX_DIM, IDX_HEADS]
    xr, yr, q, k, v, iq, ik, iw = jnp.split(w_in, np.cumsum(widths)[:-1].tolist(), axis=1)
    pad = jnp.zeros((w_in.shape[0], IN_MISC - 2 * DSA_HEAD_DIM - IDX_DIM - IDX_HEADS), w_in.dtype)
    return jnp.concatenate([xr, yr, q, iq, k, v, ik, iw, pad], axis=1).astype(BF16)


def _block_diag_groups(w):
    bw = w.shape[-1]
    per = LANES // bw
    ng = w.shape[0] // per
    on_diag = jnp.eye(per, dtype=bool)[None, :, None, :, None]
    out = jnp.where(on_diag, w.reshape(ng, per, bw, 1, bw), 0.0)
    return out.reshape(ng, LANES, LANES).astype(BF16)


def _head_lanes(nope, rope):
    lead = nope.shape[:-1] if nope is not None else rope.shape[:-1]
    dt = nope.dtype if nope is not None else rope.dtype
    z = lambda n: jnp.zeros(lead + (n,), dt)
    r1, r2 = (rope[..., :ROPE_HALF], rope[..., ROPE_HALF:]) if rope is not None else (z(ROPE_HALF), z(ROPE_HALF))
    n1, n2 = (nope[..., :NOPE_LO], nope[..., NOPE_LO:]) if nope is not None else (z(NOPE_LO), z(MLA_NOPE - NOPE_LO))
    return jnp.concatenate([r1, n1, r2, n2, z(LANES - MLA_NOPE - MLA_ROPE)], axis=-1)


def _mla_weights(w_down, w_uq, w_ukv):
    lat = MLA_Q_LORA + MLA_KV_LORA
    wd = jnp.concatenate([w_down[:, :lat], _head_lanes(None, w_down[:, lat:])], axis=1)
    wq = w_uq.reshape(MLA_Q_LORA, MLA_HEADS, MLA_NOPE + MLA_ROPE)
    wq = _head_lanes(wq[:, :, :MLA_NOPE], wq[:, :, MLA_NOPE:]).reshape(MLA_Q_LORA, MLA_HEADS * LANES)
    wkv = w_ukv.reshape(MLA_KV_LORA, MLA_HEADS, MLA_NOPE + MLA_V)
    wk = _head_lanes(wkv[:, :, :MLA_NOPE], None).reshape(MLA_KV_LORA, MLA_HEADS * LANES)
    wv = wkv[:, :, MLA_NOPE:].reshape(MLA_KV_LORA, MLA_HEADS * MLA_V)
    return wd.astype(BF16), wq.astype(BF16), wk.astype(BF16), wv.astype(BF16)


def _rope_tables(s):
    pos = np.arange(s, dtype=np.float64)
    freq = ROPE_BASE ** (-np.arange(0, MLA_ROPE, 2, dtype=np.float64) / MLA_ROPE)
    ang = pos[:, None] * freq[None, :]
    lo = slice(0, ROPE_HALF)
    hi = slice(LANES // 2, LANES // 2 + ROPE_HALF)
    cos_t = np.ones((s, LANES))
    sin_t = np.zeros((s, LANES))
    cos_t[:, lo] = cos_t[:, hi] = np.cos(ang)
    sin_t[:, lo] = -np.sin(ang)
    sin_t[:, hi] = np.sin(ang)
    f32 = lambda a: jnp.asarray(a.astype(np.float32))
    return f32(cos_t * MLA_Q_SCALE), f32(sin_t * MLA_Q_SCALE), f32(cos_t), f32(sin_t)


def _tile_m(m):
    return 512 if m % 512 == 0 else m


def kernel(x, p, ln1_g, ln1_b, ln2_g, ln2_b, mlp_w1, mlp_w2, ple_w_proj, ple_w_gate, hy_w_in, hy_conv_w, hy_conv_b, hy_ga_w, hy_ga_b, hy_gx_w, hy_gx_b, hy_lambda, hy_w_out, mla_w_down, mla_q_norm, mla_kv_norm, mla_w_uq, mla_w_ukv, mla_w_out):
    bsz, s, d = x.shape
    m = bsz * s
    tm = _tile_m(m)
    tf = 1024
    row = lambda a: a.reshape(1, -1)
    x2 = x.reshape(m, d)

    rows3 = lambda a: a.reshape(a.shape[0], 1, -1)
    tail_params = (rows3(ln1_g), rows3(ln1_b), mlp_w1.astype(BF16), mlp_w2.astype(BF16),
                   ple_w_gate.astype(BF16), ple_w_proj.astype(BF16), rows3(ln2_g), rows3(ln2_b))
    p3 = p.reshape(p.shape[0], m, -1)

    def layer_tail(acts, wos, x2, i):
        return _tail(acts, wos, x2, p3, i, *tail_params, TAIL_ROWS if m % TAIL_ROWS == 0 else tm, tf)

    xy, q, iq, misc, keyside = _inproj(x2, _hy_in_weight(hy_w_in[0]), tm, s)
    rec = _rglru(xy.reshape(bsz, s, -1), hy_conv_w[0], row(hy_conv_b[0]),
                 _block_diag_groups(hy_ga_w[0]), row(hy_ga_b[0]),
                 _block_diag_groups(hy_gx_w[0]), row(hy_gx_b[0]), row(hy_lambda[0]))
    att = _dsa(q.reshape(bsz, s, -1), iq.reshape(bsz, s, -1), misc.reshape(bsz, s, -1),
               keyside.reshape(bsz, s, -1))
    w_out = hy_w_out[0].astype(BF16)
    x2 = layer_tail([rec.reshape(m, -1), att.reshape(m, -1)], [w_out[:RNN_WIDTH], w_out[RNN_WIDTH:]], x2, 0)

    wd, wq, wk, wv = _mla_weights(mla_w_down[0], mla_w_uq[0], mla_w_ukv[0])
    qp, kp, vp = _mla_proj(x2, wd, row(mla_q_norm[0]), row(mla_kv_norm[0]), wq, wk, wv,
                           _rope_tables(s), min(tm, s), s)
    o = _mla_attn(qp.reshape(bsz, s, -1), kp.reshape(bsz, s, -1), vp.reshape(bsz, s, -1), min(256, s))
    x2 = layer_tail([o.reshape(m, -1)], [mla_w_out[0].astype(BF16)], x2, 1)
    return x2.reshape(bsz, s, d)
```
